```python
import math
import jax, jax.numpy as jnp
from jax import lax
import numpy as np

D_MODEL = 1024
BATCH = 8
SEQ = 2048
DEPTH = 4
DEC_BATCH = 128
DEC_SEQ = 4
PAST_LEN = 16384
PAGE_SIZE = 128

N_MIXERS = 3
N_S5_LAYERS = (DEPTH + 2) // 3
N_RET_LAYERS = (DEPTH + 1) // 3
N_GLA_LAYERS = DEPTH // 3
S5_GROUP = 16
S5_GROUPS = D_MODEL // S5_GROUP
S5_STATE = 64
S5_DT_MIN = 1e-3
S5_DT_MAX = 1e-1
RET_HEADS = 4
RET_DK = D_MODEL // RET_HEADS
RET_DV = 2 * D_MODEL // RET_HEADS
RET_CHUNK = 128
ROPE_BASE = 10000.0
GLA_HEADS = 4
GLA_DK = D_MODEL // 2 // GLA_HEADS
GLA_DV = D_MODEL // GLA_HEADS
GLA_GATE_RANK = 16
GLA_GATE_NORM = 16.0
GLA_CHUNK = 64
D_FF = ((8 * D_MODEL // 3 + 255) // 256) * 256
CONV_W = 3
NORM_EPS = 1e-6

kernel_name = 'hybrid_s5_retention_gla_convffn_step'


def rmsnorm(x, g):
    xf = x.astype(jnp.float32)
    y = xf * lax.rsqrt(jnp.mean(xf * xf, axis=-1, keepdims=True) + NORM_EPS)
    return (y * g.astype(jnp.float32)).astype(x.dtype)


def _linear_combine(e1, e2):
    a1, b1 = e1
    a2, b2 = e2
    return a1 * a2, a2 * b1 + b2


def s5_mixer(u, h0_re, h0_im, lam_re, lam_im, b_re, b_im, c_re, c_im, log_dt, d_skip, w_glu):
    bsz, l, d = u.shape
    f32 = jnp.float32
    uf = u.astype(f32)
    lam = lax.complex(lam_re.astype(f32), lam_im.astype(f32))
    dt = jnp.exp(log_dt.astype(f32))[:, None]
    a_bar = jnp.exp(lam * dt)
    b_bar = ((a_bar - 1.0) / lam)[..., None] * lax.complex(b_re.astype(f32), b_im.astype(f32))
    c_mat = lax.complex(c_re.astype(f32), c_im.astype(f32))
    ug = uf.reshape(bsz, l, S5_GROUPS, S5_GROUP).astype(jnp.complex64)
    bu = jnp.einsum('blgp,gnp->blgn', ug, b_bar)
    h0 = lax.complex(h0_re.astype(f32), h0_im.astype(f32))
    bu = bu.at[:, 0].add(a_bar * h0)
    a = jnp.broadcast_to(a_bar, (1, l) + a_bar.shape)
    _, hs = lax.associative_scan(_linear_combine, (a, bu), axis=1)
    y = jnp.einsum('gpn,blgn->blgp', c_mat, hs).real.reshape(bsz, l, d) + d_skip.astype(f32) * uf
    z = jax.nn.gelu(y) @ w_glu.astype(f32)
    out = z[..., :d] * jax.nn.sigmoid(z[..., d:])
    h_last = hs[:, -1]
    return out.astype(u.dtype), jnp.real(h_last), jnp.imag(h_last)


def rotate_half(t, pos):
    half = t.shape[-1] // 2
    freq = 1.0 / (ROPE_BASE ** jnp.linspace(0.0, 1.0, half, dtype=jnp.float32))
    ang = pos[:, None] * freq[None, :]
    cos = jnp.cos(ang)[None, :, None, :]
    sin = jnp.sin(ang)[None, :, None, :]
    t1, t2 = t[..., :half], t[..., half:]
    return jnp.concatenate([t1 * cos - t2 * sin, t2 * cos + t1 * sin], axis=-1)


def _to_chunks(t, n, c):
    b, l, h, dd = t.shape
    return t.reshape(b, n, c, h, dd).transpose(1, 0, 3, 2, 4)


def _from_chunks(o):
    n, b, h, c, dd = o.shape
    return o.transpose(1, 0, 3, 2, 4).reshape(b, n * c, h, dd)


def retention_chunked(q, k, v, log_gamma, s0):
    b, l, h, dk = q.shape
    c = math.gcd(l, RET_CHUNK)
    n = l // c
    idx = jnp.arange(c, dtype=jnp.float32)
    rel = idx[:, None] - idx[None, :]
    decay = jnp.where(rel >= 0, jnp.exp(jnp.maximum(rel, 0.0)[None] * log_gamma[:, None, None]), 0.0)
    q_decay = jnp.exp((idx + 1.0)[None, :] * log_gamma[:, None])[..., None]
    k_decay = jnp.exp((c - 1.0 - idx)[None, :] * log_gamma[:, None])[..., None]
    chunk_decay = jnp.exp(c * log_gamma)[:, None, None]

    def step(s, inp):
        qc, kc, vc = inp
        scores = jnp.einsum('bhid,bhjd->bhij', qc, kc) * decay
        o = jnp.einsum('bhij,bhjv->bhiv', scores, vc) + jnp.einsum('bhid,bhdv->bhiv', qc, s) * q_decay
        s = s * chunk_decay + jnp.einsum('bhjd,bhjv->bhdv', kc * k_decay, vc)
        return s, o

    s, o = lax.scan(step, s0, (_to_chunks(q, n, c), _to_chunks(k, n, c), _to_chunks(v, n, c)))
    return _from_chunks(o), s


def gla_chunked(q, k, v, log_a, s0):
    b, l, h, dk = q.shape
    c = math.gcd(l, GLA_CHUNK)
    n = l // c
    mask = jnp.tril(jnp.ones((c, c), dtype=bool))[:, :, None]

    def step(s, inp):
        qc, kc, vc, gc = inp
        bc = jnp.cumsum(gc, axis=2)
        diff = bc[:, :, :, None, :] - bc[:, :, None, :, :]
        gate = jnp.exp(jnp.where(mask, diff, -jnp.inf))
        scores = jnp.einsum('bhid,bhjd,bhijd->bhij', qc, kc, gate)
        o = jnp.einsum('bhij,bhjv->bhiv', scores, vc) + jnp.einsum('bhid,bhdv->bhiv', qc * jnp.exp(bc), s)
        b_last = bc[:, :, -1:, :]
        s = s * jnp.exp(b_last[:, :, 0, :])[..., None] + jnp.einsum('bhjd,bhjv->bhdv', kc * jnp.exp(b_last - bc), vc)
        return s, o

    s, o = lax.scan(step, s0, (_to_chunks(q, n, c), _to_chunks(k, n, c), _to_chunks(v, n, c), _to_chunks(log_a, n, c)))
    return _from_chunks(o), s


def retention_mixer(x, pos0, s0, wq, wk, wv, wg, wo):
    b, l, _ = x.shape
    f32 = jnp.float32
    q = (x @ wq).reshape(b, l, RET_HEADS, RET_DK).astype(f32)
    k = (x @ wk).reshape(b, l, RET_HEADS, RET_DK).astype(f32)
    v = (x @ wv).reshape(b, l, RET_HEADS, RET_DV).astype(f32)
    pos = jnp.arange(pos0, pos0 + l, dtype=jnp.int32).astype(f32)
    q = rotate_half(q, pos)
    k = rotate_half(k, pos) * (RET_DK ** -0.5)
    log_gamma = jnp.log1p(-(2.0 ** (-5.0 - jnp.arange(RET_HEADS, dtype=f32))))
    o, s = retention_chunked(q, k, v, log_gamma, s0.astype(f32))
    mu = jnp.mean(o, axis=-1, keepdims=True)
    var = jnp.mean(jnp.square(o - mu), axis=-1, keepdims=True)
    o = ((o - mu) * lax.rsqrt(var + NORM_EPS)).reshape(b, l, RET_HEADS * RET_DV)
    o = o * jax.nn.silu((x @ wg).astype(f32))
    return (o @ wo.astype(f32)).astype(x.dtype), s


def gla_mixer(x, s0, wq, wk, wv, wg, wa1, wa2, ba, norm_g, wo):
    b, l, _ = x.shape
    f32 = jnp.float32
    q = (x @ wq).reshape(b, l, GLA_HEADS, GLA_DK).astype(f32) * (GLA_DK ** -0.5)
    k = (x @ wk).reshape(b, l, GLA_HEADS, GLA_DK).astype(f32)
    v = (x @ wv).reshape(b, l, GLA_HEADS, GLA_DV).astype(f32)
    gate_logit = ((x @ wa1) @ wa2 + ba).astype(f32)
    log_a = (jax.nn.log_sigmoid(gate_logit) / GLA_GATE_NORM).reshape(b, l, GLA_HEADS, GLA_DK)
    o, s = gla_chunked(q, k, v, log_a, s0.astype(f32))
    o = o * lax.rsqrt(jnp.mean(o * o, axis=-1, keepdims=True) + NORM_EPS) * norm_g.astype(f32)
    o = o.reshape(b, l, GLA_HEADS * GLA_DV) * jax.nn.silu((x @ wg).astype(f32))
    return (o @ wo.astype(f32)).astype(x.dtype), s


def conv_ffn(x, buf, w_gate, w_up, conv_w, conv_b, w_down):
    l = x.shape[1]
    gpre = x @ w_gate
    cat = jnp.concatenate([buf.astype(gpre.dtype), gpre], axis=1)
    gconv = conv_b
    for j in range(CONV_W):
        gconv = gconv + conv_w[j] * cat[:, j:j + l]
    hdn = jax.nn.silu(gconv) * (x @ w_up)
    return hdn @ w_down, cat[:, l:]


def trunk(x, pos0, s5_re, s5_im, ret_s, gla_s, conv_buf, p):
    new_re, new_im, new_ret, new_gla, new_conv = [], [], [], [], []
    for layer in range(DEPTH):
        g = p['norm_g'][layer]
        kind = layer % N_MIXERS
        j = layer // N_MIXERS
        h = rmsnorm(x, g[0])
        if kind == 0:
            mix, hr, hi = s5_mixer(h, s5_re[j], s5_im[j], p['s5_lambda_re'][j], p['s5_lambda_im'][j],
                                   p['s5_b_re'][j], p['s5_b_im'][j], p['s5_c_re'][j], p['s5_c_im'][j],
                                   p['s5_log_dt'][j], p['s5_d'][j], p['s5_w_glu'][j])
            new_re.append(hr)
            new_im.append(hi)
        elif kind == 1:
            mix, s = retention_mixer(h, pos0, ret_s[j], p['ret_wq'][j], p['ret_wk'][j], p['ret_wv'][j],
                                     p['ret_wg'][j], p['ret_wo'][j])
            new_ret.append(s)
        else:
            mix, s = gla_mixer(h, gla_s[j], p['gla_wq'][j], p['gla_wk'][j], p['gla_wv'][j], p['gla_wg'][j],
                               p['gla_wa1'][j], p['gla_wa2'][j], p['gla_ba'][j], p['gla_norm_g'][j], p['gla_wo'][j])
            new_gla.append(s)
        x = x + rmsnorm(mix, g[1])
        h = rmsnorm(x, g[2])
        f, buf = conv_ffn(h, conv_buf[layer], p['ffn_w_gate'][layer], p['ffn_w_up'][layer],
                          p['ffn_conv_w'][layer], p['ffn_conv_b'][layer], p['ffn_w_down'][layer])
        new_conv.append(buf)
        x = x + rmsnorm(f, g[3])
    return x, jnp.stack(new_re), jnp.stack(new_im), jnp.stack(new_ret), jnp.stack(new_gla), jnp.stack(new_conv)


def setup_inputs(seed: int = 0) -> dict:
    key = jax.random.key(seed)
    it = iter(list(jax.random.split(key, 40)))

    def nrm(shape, scale):
        return scale * jax.random.normal(next(it), shape, jnp.float32)

    d = D_MODEL
    inp = {}
    inp['x_prompt'] = nrm((BATCH, SEQ, d), 1.0)
    inp['x_sample'] = nrm((DEC_BATCH, DEC_SEQ, d), 1.0)
    inp['state_s5_re'] = nrm((N_S5_LAYERS, DEC_BATCH, S5_GROUPS, S5_STATE), 0.1)
    inp['state_s5_im'] = nrm((N_S5_LAYERS, DEC_BATCH, S5_GROUPS, S5_STATE), 0.1)
    inp['state_ret'] = nrm((N_RET_LAYERS, DEC_BATCH, RET_HEADS, RET_DK, RET_DV), 0.5)
    inp['state_gla'] = nrm((N_GLA_LAYERS, DEC_BATCH, GLA_HEADS, GLA_DK, GLA_DV), 0.5)
    inp['cache_ffn_conv'] = nrm((DEPTH, DEC_BATCH, CONV_W - 1, D_FF), 1.0)
    inp['norm_g'] = 1.0 + nrm((DEPTH, 4, d), 0.01)
    inp['s5_lambda_re'] = -0.5 + nrm((N_S5_LAYERS, S5_GROUPS, S5_STATE), 0.01)
    inp['s5_lambda_im'] = jnp.pi * jnp.arange(S5_STATE, dtype=jnp.float32) + nrm((N_S5_LAYERS, S5_GROUPS, S5_STATE), 0.01)
    inp['s5_b_re'] = nrm((N_S5_LAYERS, S5_GROUPS, S5_STATE, S5_GROUP), (2.0 * S5_GROUP) ** -0.5)
    inp['s5_b_im'] = nrm((N_S5_LAYERS, S5_GROUPS, S5_STATE, S5_GROUP), (2.0 * S5_GROUP) ** -0.5)
    inp['s5_c_re'] = nrm((N_S5_LAYERS, S5_GROUPS, S5_GROUP, S5_STATE), (2.0 * S5_STATE) ** -0.5)
    inp['s5_c_im'] = nrm((N_S5_LAYERS, S5_GROUPS, S5_GROUP, S5_STATE), (2.0 * S5_STATE) ** -0.5)
    inp['s5_log_dt'] = jax.random.uniform(next(it), (N_S5_LAYERS, S5_GROUPS), jnp.float32,
                                          minval=math.log(S5_DT_MIN), maxval=math.log(S5_DT_MAX))
    inp['s5_d'] = nrm((N_S5_LAYERS, d), 1.0)
    inp['s5_w_glu'] = nrm((N_S5_LAYERS, d, 2 * d), d ** -0.5)
    inp['ret_wq'] = nrm((N_RET_LAYERS, d, RET_HEADS * RET_DK), d ** -0.5)
    inp['ret_wk'] = nrm((N_RET_LAYERS, d, RET_HEADS * RET_DK), d ** -0.5)
    inp['ret_wv'] = nrm((N_RET_LAYERS, d, RET_HEADS * RET_DV), d ** -0.5)
    inp['ret_wg'] = nrm((N_RET_LAYERS, d, RET_HEADS * RET_DV), d ** -0.5)
    inp['ret_wo'] = nrm((N_RET_LAYERS, RET_HEADS * RET_DV, d), (RET_HEADS * RET_DV) ** -0.5)
    inp['gla_wq'] = nrm((N_GLA_LAYERS, d, GLA_HEADS * GLA_DK), d ** -0.5)
    inp['gla_wk'] = nrm((N_GLA_LAYERS, d, GLA_HEADS * GLA_DK), d ** -0.5)
    inp['gla_wv'] = nrm((N_GLA_LAYERS, d, GLA_HEADS * GLA_DV), d ** -0.5)
    inp['gla_wg'] = nrm((N_GLA_LAYERS, d, GLA_HEADS * GLA_DV), d ** -0.5)
    inp['gla_wa1'] = nrm((N_GLA_LAYERS, d, GLA_GATE_RANK), d ** -0.5)
    inp['gla_wa2'] = nrm((N_GLA_LAYERS, GLA_GATE_RANK, GLA_HEADS * GLA_DK), GLA_GATE_RANK ** -0.5)
    inp['gla_ba'] = nrm((N_GLA_LAYERS, GLA_HEADS * GLA_DK), 0.01)
    inp['gla_norm_g'] = 1.0 + nrm((N_GLA_LAYERS, GLA_DV), 0.01)
    inp['gla_wo'] = nrm((N_GLA_LAYERS, GLA_HEADS * GLA_DV, d), (GLA_HEADS * GLA_DV) ** -0.5)
    inp['ffn_w_gate'] = nrm((DEPTH, d, D_FF), d ** -0.5)
    inp['ffn_w_up'] = nrm((DEPTH, d, D_FF), d ** -0.5)
    inp['ffn_conv_w'] = nrm((DEPTH, CONV_W, D_FF), CONV_W ** -0.5)
    inp['ffn_conv_b'] = nrm((DEPTH, D_FF), 0.01)
    inp['ffn_w_down'] = nrm((DEPTH, D_FF, d), D_FF ** -0.5)
    return inp


def reference(x_prompt, x_sample, state_s5_re, state_s5_im, state_ret, state_gla, cache_ffn_conv,
              norm_g, s5_lambda_re, s5_lambda_im, s5_b_re, s5_b_im, s5_c_re, s5_c_im, s5_log_dt, s5_d, s5_w_glu,
              ret_wq, ret_wk, ret_wv, ret_wg, ret_wo,
              gla_wq, gla_wk, gla_wv, gla_wg, gla_wa1, gla_wa2, gla_ba, gla_norm_g, gla_wo,
              ffn_w_gate, ffn_w_up, ffn_conv_w, ffn_conv_b, ffn_w_down):
    p = dict(norm_g=norm_g, s5_lambda_re=s5_lambda_re, s5_lambda_im=s5_lambda_im, s5_b_re=s5_b_re,
             s5_b_im=s5_b_im, s5_c_re=s5_c_re, s5_c_im=s5_c_im, s5_log_dt=s5_log_dt, s5_d=s5_d,
             s5_w_glu=s5_w_glu, ret_wq=ret_wq, ret_wk=ret_wk, ret_wv=ret_wv, ret_wg=ret_wg, ret_wo=ret_wo,
             gla_wq=gla_wq, gla_wk=gla_wk, gla_wv=gla_wv, gla_wg=gla_wg, gla_wa1=gla_wa1, gla_wa2=gla_wa2,
             gla_ba=gla_ba, gla_norm_g=gla_norm_g, gla_wo=gla_wo, ffn_w_gate=ffn_w_gate, ffn_w_up=ffn_w_up,
             ffn_conv_w=ffn_conv_w, ffn_conv_b=ffn_conv_b, ffn_w_down=ffn_w_down)
    bp = x_prompt.shape[0]
    f32 = jnp.float32
    z_re = jnp.zeros((N_S5_LAYERS, bp, S5_GROUPS, S5_STATE), f32)
    z_ret = jnp.zeros((N_RET_LAYERS, bp, RET_HEADS, RET_DK, RET_DV), f32)
    z_gla = jnp.zeros((N_GLA_LAYERS, bp, GLA_HEADS, GLA_DK, GLA_DV), f32)
    z_conv = jnp.zeros((DEPTH, bp, CONV_W - 1, D_FF), x_prompt.dtype)
    y_prompt, p_re, p_im, p_ret, p_gla, p_conv = trunk(x_prompt, 0, z_re, z_re, z_ret, z_gla, z_conv, p)
    y_sample, s_re, s_im, s_ret, s_gla, s_conv = trunk(x_sample, PAST_LEN, state_s5_re, state_s5_im,
                                                       state_ret, state_gla, cache_ffn_conv, p)
    return (y_prompt, y_sample, p_re, p_im, p_ret, p_gla, p_conv, s_re, s_im, s_ret, s_gla, s_conv)
```

```python
import functools
import math

import jax
import jax.numpy as jnp
from jax import lax
from jax.experimental import pallas as pl
from jax.experimental.pallas import tpu as pltpu

F32 = jnp.float32
BF16 = jnp.bfloat16

D_MODEL = 1024
PAST_LEN = 16384
N_MIXERS = 3
S5_GROUP = 16
S5_GROUPS = D_MODEL // S5_GROUP
S5_STATE = 64
S5_SUPER = 4
S5_SG_CH = D_MODEL // S5_SUPER
S5_SG_ST = (S5_GROUPS // S5_SUPER) * S5_STATE
RET_HEADS = 4
RET_DK = D_MODEL // RET_HEADS
RET_DV = 2 * D_MODEL // RET_HEADS
RET_CHUNK = 256
ROPE_BASE = 10000.0
GLA_HEADS = 4
GLA_DK = D_MODEL // 2 // GLA_HEADS
GLA_DV = D_MODEL // GLA_HEADS
GLA_GATE_RANK = 16
GLA_GATE_NORM = 16.0
GLA_CHUNK = 64
D_FF = ((8 * D_MODEL // 3 + 255) // 256) * 256
FFN_CHUNKS = 2
FFN_FC = D_FF // FFN_CHUNKS
CONV_W = 3
NORM_EPS = 1e-6
SMALL_SEQ_PAD = 16
ROW_TILE = 512
VMEM_LIMIT_BYTES = 52 * 2 ** 20


def _cparams(*sem):
    return pltpu.CompilerParams(dimension_semantics=sem, vmem_limit_bytes=VMEM_LIMIT_BYTES)


def _const_spec(shape):
    zeros = (0,) * len(shape)
    return pl.BlockSpec(shape, lambda *_: zeros, pipeline_mode=pl.Buffered(1))


def _rms(x, g):
    return x * lax.rsqrt(jnp.mean(x * x, axis=-1, keepdims=True) + NORM_EPS) * g


def _sigmoid(x):
    return 1.0 / (1.0 + jnp.exp(-x))


def _silu(x):
    return x * _sigmoid(x)


def _gelu_tanh(x):
    return 0.5 * x * (1.0 + jnp.tanh(math.sqrt(2.0 / math.pi) * (x + 0.044715 * (x * x * x))))


def _log_sigmoid(x):
    return jnp.minimum(x, 0.0) - jnp.log(1.0 + jnp.exp(-jnp.abs(x)))


def _dot(a, b):
    return jnp.dot(a, b, preferred_element_type=F32)


def _dot_nt(a, b):
    return lax.dot_general(a, b, (((1,), (1,)), ((), ())), preferred_element_type=F32)


def _dot_tn(a, b):
    return lax.dot_general(a, b, (((0,), (0,)), ((), ())), preferred_element_type=F32)


def _s5_kernel(x_ref, h0_ref, g0_ref, g1_ref, are_ref, aim_ref, bbd_ref, cbd_ref, dsk_ref, wglu_ref,
               o_ref, hout_ref, bu_sc, y_sc, *, bn, tt):
    @pl.when(pl.program_id(0) == 0)
    def _():
        hout_ref[...] = h0_ref[...]

    x = x_ref[...]
    u = _rms(x, g0_ref[...])
    ub = u.astype(BF16)
    half = S5_SG_ST // 2
    for sg in range(S5_SUPER):
        bu_sc[...] = _dot(ub[:, S5_SG_CH * sg:S5_SG_CH * (sg + 1)], bbd_ref[sg])
        for hf in range(2):
            c_re = half * hf
            c_im = S5_SG_ST + half * hf
            a_lo = S5_SG_ST * sg + half * hf
            ar = jnp.broadcast_to(are_ref[:, a_lo:a_lo + half], (8, half))
            ai = jnp.broadcast_to(aim_ref[:, a_lo:a_lo + half], (8, half))
            h_re = 2 * S5_SG_ST * sg + c_re
            h_im = 2 * S5_SG_ST * sg + c_im

            def bt_body(bt, carry, c_re=c_re, c_im=c_im, ar=ar, ai=ai, h_re=h_re, h_im=h_im):
                r8 = pl.multiple_of(bt * 8, 8)
                hr0 = hout_ref[pl.ds(r8, 8), h_re:h_re + half]
                hi0 = hout_ref[pl.ds(r8, 8), h_im:h_im + half]

                def t_body(t, hc):
                    hr, hi = hc
                    r0 = pl.multiple_of(t * bn + r8, 8)
                    bur = bu_sc[pl.ds(r0, 8), c_re:c_re + half]
                    bui = bu_sc[pl.ds(r0, 8), c_im:c_im + half]
                    nhr = ar * hr - ai * hi + bur
                    nhi = ar * hi + ai * hr + bui
                    bu_sc[pl.ds(r0, 8), c_re:c_re + half] = nhr
                    bu_sc[pl.ds(r0, 8), c_im:c_im + half] = nhi
                    return nhr, nhi

                hr, hi = lax.fori_loop(0, tt, t_body, (hr0, hi0))
                hout_ref[pl.ds(r8, 8), h_re:h_re + half] = hr
                hout_ref[pl.ds(r8, 8), h_im:h_im + half] = hi
                return carry

            lax.fori_loop(0, bn // 8, bt_body, 0)
        y_sc[:, S5_SG_CH * sg:S5_SG_CH * (sg + 1)] = _dot(bu_sc[...].astype(BF16), cbd_ref[sg])
    y = y_sc[...] + dsk_ref[...] * u
    z = _dot(_gelu_tanh(y).astype(BF16), wglu_ref[...])
    mix = z[:, :D_MODEL] * _sigmoid(z[:, D_MODEL:])
    o_ref[...] = x + _rms(mix, g1_ref[...])


def _s5_layer(x_tm, bn, h0_re, h0_im, g0, g1, lam_re, lam_im, b_re, b_im, c_re, c_im, log_dt, d_skip, w_glu):
    rows = x_tm.shape[0]
    tt = min(rows // bn, max(1, ROW_TILE // bn))
    tile = tt * bn
    ngl = S5_GROUPS // S5_SUPER
    lam = lax.complex(lam_re, lam_im)
    dt = jnp.exp(log_dt)[:, None]
    a_bar = jnp.exp(lam * dt)
    b_bar = ((a_bar - 1.0) / lam)[..., None] * lax.complex(b_re, b_im)
    eye = jnp.eye(ngl, dtype=F32)
    bb = b_bar.reshape(S5_SUPER, ngl, S5_STATE, S5_GROUP).transpose(0, 1, 3, 2)

    def bd_in(part):
        return jnp.einsum('sgpn,gh->sgphn', part, eye).reshape(S5_SUPER, S5_SG_CH, S5_SG_ST)

    bbd = jnp.concatenate([bd_in(jnp.real(bb)), bd_in(jnp.imag(bb))], axis=-1).astype(BF16)
    cc_re = c_re.reshape(S5_SUPER, ngl, S5_GROUP, S5_STATE)
    cc_im = c_im.reshape(S5_SUPER, ngl, S5_GROUP, S5_STATE)

    def bd_out(part):
        return jnp.einsum('sgpn,gh->sgnhp', part, eye).reshape(S5_SUPER, S5_SG_ST, S5_SG_CH)

    cbd = jnp.concatenate([bd_out(cc_re), bd_out(-cc_im)], axis=1).astype(BF16)
    a_re = jnp.real(a_bar).reshape(1, S5_GROUPS * S5_STATE)
    a_im = jnp.imag(a_bar).reshape(1, S5_GROUPS * S5_STATE)
    h0 = jnp.stack([h0_re.reshape(bn, S5_SUPER, S5_SG_ST), h0_im.reshape(bn, S5_SUPER, S5_SG_ST)], axis=2)
    h0 = h0.reshape(bn, 2 * S5_GROUPS * S5_STATE)
    nst = 2 * S5_GROUPS * S5_STATE

    out, hout = pl.pallas_call(
        functools.partial(_s5_kernel, bn=bn, tt=tt),
        grid=(rows // tile,),
        in_specs=[
            pl.BlockSpec((tile, D_MODEL), lambda i: (i, 0)),
            _const_spec((bn, nst)),
            _const_spec((1, D_MODEL)), _const_spec((1, D_MODEL)),
            _const_spec((1, nst // 2)), _const_spec((1, nst // 2)),
            _const_spec((S5_SUPER, S5_SG_CH, 2 * S5_SG_ST)),
            _const_spec((S5_SUPER, 2 * S5_SG_ST, S5_SG_CH)),
            _const_spec((1, D_MODEL)),
            _const_spec((D_MODEL, 2 * D_MODEL)),
        ],
        out_specs=[pl.BlockSpec((tile, D_MODEL), lambda i: (i, 0)), _const_spec((bn, nst))],
        out_shape=[jax.ShapeDtypeStruct((rows, D_MODEL), F32), jax.ShapeDtypeStruct((bn, nst), F32)],
        scratch_shapes=[pltpu.VMEM((tile, 2 * S5_SG_ST), F32), pltpu.VMEM((tile, D_MODEL), F32)],
        compiler_params=_cparams("arbitrary"),
        name="s5_layer",
    )(x_tm, h0, g0.reshape(1, -1), g1.reshape(1, -1), a_re, a_im, bbd, cbd, d_skip.reshape(1, -1),
      w_glu.astype(BF16))
    hout = hout.reshape(bn, S5_SUPER, 2, ngl, S5_STATE)
    new_re = hout[:, :, 0].reshape(bn, S5_GROUPS, S5_STATE)
    new_im = hout[:, :, 1].reshape(bn, S5_GROUPS, S5_STATE)
    return out, new_re, new_im


def _ffn_kernel(x_ref, cin_ref, g2_ref, g3_ref, wg_ref, wu_ref, wd_ref, cw_ref, cb_ref,
                o_ref, cout_ref, h_sc, acc_sc, gext_sc, carry_sc, *, shift, tile):
    i = pl.program_id(1)
    c = pl.program_id(2)
    nchunk = pl.num_programs(2)
    halo = 2 * shift

    @pl.when(c == 0)
    def _():
        h_sc[...] = _rms(x_ref[...], g2_ref[...]).astype(BF16)
        acc_sc[...] = jnp.zeros_like(acc_sc)

    @pl.when(i == 0)
    def _():
        carry_sc[c] = cin_ref[0, 0]

    h = h_sc[...]
    gpre = _dot(h, wg_ref[0])
    base = gext_sc.shape[0] - tile
    gext_sc[base - halo:base, :] = carry_sc[c]
    gext_sc[base:base + tile, :] = gpre
    carry_sc[c] = gpre[tile - halo:tile, :]
    cw = cw_ref[0]
    gconv = cb_ref[0] + cw[0:1, :] * gext_sc[base - halo:base - halo + tile, :]
    gconv = gconv + cw[1:2, :] * gext_sc[base - shift:base - shift + tile, :]
    gconv = gconv + cw[2:3, :] * gpre
    hdn = (_silu(gconv) * _dot(h, wu_ref[0])).astype(BF16)
    acc_sc[...] += _dot(hdn, wd_ref[0])

    @pl.when(c == nchunk - 1)
    def _():
        o_ref[...] = x_ref[...] + _rms(acc_sc[...], g3_ref[...])

    @pl.when(jnp.logical_and(c == nchunk - 1, i == pl.num_programs(1) - 1))
    def _():
        cout_ref[0] = carry_sc[...]


def _ffn_layer(x, nseg, shift, cin, g2, g3, w_gate, w_up, conv_w, conv_b, w_down):
    rows = x.shape[0]
    seg_rows = rows // nseg
    tile = min(ROW_TILE, seg_rows)
    tiles = seg_rows // tile
    halo = 2 * shift
    nf, fc = FFN_CHUNKS, FFN_FC
    wg = w_gate.astype(BF16).reshape(D_MODEL, nf, fc).transpose(1, 0, 2)
    wu = w_up.astype(BF16).reshape(D_MODEL, nf, fc).transpose(1, 0, 2)
    wd = w_down.astype(BF16).reshape(nf, fc, D_MODEL)
    cw = conv_w.reshape(CONV_W, nf, fc).transpose(1, 0, 2)
    cb = conv_b.reshape(nf, 1, fc)
    cin4 = cin.reshape(nseg, halo, nf, fc).transpose(0, 2, 1, 3)
    out, cout = pl.pallas_call(
        functools.partial(_ffn_kernel, shift=shift, tile=tile),
        grid=(nseg, tiles, nf),
        in_specs=[
            pl.BlockSpec((tile, D_MODEL), lambda s, i, c: (s * tiles + i, 0)),
            pl.BlockSpec((1, 1, halo, fc), lambda s, i, c: (s, c, 0, 0)),
            _const_spec((1, D_MODEL)), _const_spec((1, D_MODEL)),
            pl.BlockSpec((1, D_MODEL, fc), lambda s, i, c: (c, 0, 0)),
            pl.BlockSpec((1, D_MODEL, fc), lambda s, i, c: (c, 0, 0)),
            pl.BlockSpec((1, fc, D_MODEL), lambda s, i, c: (c, 0, 0)),
            pl.BlockSpec((1, CONV_W, fc), lambda s, i, c: (c, 0, 0)),
            pl.BlockSpec((1, 1, fc), lambda s, i, c: (c, 0, 0)),
        ],
        out_specs=[
            pl.BlockSpec((tile, D_MODEL), lambda s, i, c: (s * tiles + i, 0)),
            pl.BlockSpec((1, nf, halo, fc), lambda s, i, c: (s, 0, 0, 0)),
        ],
        out_shape=[jax.ShapeDtypeStruct((rows, D_MODEL), F32),
                   jax.ShapeDtypeStruct((nseg, nf, halo, fc), F32)],
        scratch_shapes=[pltpu.VMEM((tile, D_MODEL), BF16), pltpu.VMEM((tile, D_MODEL), F32),
                        pltpu.VMEM((-(-halo // 8) * 8 + tile, fc), F32), pltpu.VMEM((nf, halo, fc), F32)],
        compiler_params=_cparams("arbitrary", "arbitrary", "arbitrary"),
        name="conv_ffn",
    )(x, cin4, g2.reshape(1, -1), g3.reshape(1, -1), wg, wu, wd, cw, cb)
    return out, cout.transpose(0, 2, 1, 3).reshape(nseg, halo, D_FF)


def _ret_proj_kernel(x_ref, cos_ref, sin_ref, g_ref, wq_ref, wk_ref, wv_ref, q_ref, k_ref, v_ref):
    h = _rms(x_ref[...], g_ref[...]).astype(BF16)
    cos = cos_ref[...]
    sin = sin_ref[...]
    hw = RET_DK // 2
    for w_ref, dst, scale in ((wq_ref, q_ref, 1.0), (wk_ref, k_ref, RET_DK ** -0.5)):
        p = _dot(h, w_ref[...])
        for hd in range(RET_HEADS):
            lo = RET_DK * hd
            t1 = p[:, lo:lo + hw]
            t2 = p[:, lo + hw:lo + 2 * hw]
            dst[:, lo:lo + hw] = ((t1 * cos - t2 * sin) * scale).astype(BF16)
            dst[:, lo + hw:lo + 2 * hw] = ((t2 * cos + t1 * sin) * scale).astype(BF16)
    v_ref[...] = _dot(h, wv_ref[...]).astype(BF16)


def _ret_chunk_kernel(q_ref, k_ref, v_ref, s0_ref, o_ref, sout_ref, *, chunk, pad):
    @pl.when(pl.program_id(1) == 0)
    def _():
        sout_ref[...] = s0_ref[...]

    ri = lax.broadcasted_iota(jnp.int32, (chunk, chunk), 0)
    ci = lax.broadcasted_iota(jnp.int32, (chunk, chunk), 1)
    rel = (ri - ci).astype(F32)
    causal = ri >= ci
    idx = lax.broadcasted_iota(jnp.int32, (chunk, 1), 0).astype(F32)
    for hd in range(RET_HEADS):
        lg = math.log1p(-(2.0 ** (-5.0 - hd)))
        qh = q_ref[0, :, RET_DK * hd:RET_DK * (hd + 1)]
        kh = k_ref[0, :, RET_DK * hd:RET_DK * (hd + 1)]
        vh = v_ref[0, :, RET_DV * hd:RET_DV * (hd + 1)]
        s = sout_ref[0, hd]
        decay = jnp.where(causal, jnp.exp(jnp.maximum(rel, 0.0) * lg), 0.0)
        scores = _dot_nt(qh, kh) * decay
        q_decay = jnp.exp((idx + (1.0 - pad)) * lg)
        o = _dot(scores.astype(BF16), vh) + _dot(qh, s.astype(BF16)) * q_decay
        o_ref[0, :, RET_DV * hd:RET_DV * (hd + 1)] = o
        k_decay = jnp.exp((chunk - 1.0 - idx) * lg)
        kd = (kh.astype(F32) * k_decay).astype(BF16)
        sout_ref[0, hd] = s * math.exp((chunk - pad) * lg) + _dot_tn(kd, vh)


def _ret_out_kernel(x_ref, o_ref, g0_ref, g1_ref, wg_ref, wo_ref, y_ref, on_sc):
    x = x_ref[...]
    h = _rms(x, g0_ref[...]).astype(BF16)
    gate = _silu(_dot(h, wg_ref[...]))
    for hd in range(RET_HEADS):
        sl = slice(RET_DV * hd, RET_DV * (hd + 1))
        oh = o_ref[:, sl]
        mu = jnp.mean(oh, axis=-1, keepdims=True)
        ctr = oh - mu
        var = jnp.mean(ctr * ctr, axis=-1, keepdims=True)
        on_sc[:, sl] = (ctr * lax.rsqrt(var + NORM_EPS) * gate[:, sl]).astype(BF16)
    y_ref[...] = x + _rms(_dot(on_sc[...], wo_ref[...]), g1_ref[...])


def _row_grid_call(kernel, row_ins, const_ins, outs, rows, tile, scratch=(), name=None, tab_ins=()):
    nt = rows // tile
    in_specs = [pl.BlockSpec((tile, a.shape[1]), lambda i: (i, 0)) for a in row_ins]
    for a in tab_ins:
        ntab = a.shape[0] // tile
        in_specs.append(pl.BlockSpec((tile, a.shape[1]), lambda i, ntab=ntab: (i % ntab, 0)))
    in_specs += [_const_spec(a.shape) for a in const_ins]
    return pl.pallas_call(
        kernel,
        grid=(nt,),
        in_specs=in_specs,
        out_specs=[pl.BlockSpec((tile, n), lambda i: (i, 0)) for n, _ in outs],
        out_shape=[jax.ShapeDtypeStruct((rows, n), dt) for n, dt in outs],
        scratch_shapes=list(scratch),
        compiler_params=_cparams("arbitrary"),
        name=name,
    )(*row_ins, *tab_ins, *const_ins)


def _to_seq(a, bn, l, time_major, pad):
    n = a.shape[1]
    a = a.reshape(l, bn, n).transpose(1, 0, 2) if time_major else a.reshape(bn, l, n)
    if pad:
        a = jnp.pad(a, ((0, 0), (pad, 0), (0, 0)))
    return a


def _from_seq(a, bn, l, time_major, pad):
    n = a.shape[2]
    a = a[:, pad:, :]
    a = a.transpose(1, 0, 2) if time_major else a
    return a.reshape(bn * l, n)


def _seq_chunking(l, chunk):
    if l % chunk == 0:
        return chunk, 0
    assert l <= SMALL_SEQ_PAD
    return SMALL_SEQ_PAD, SMALL_SEQ_PAD - l


def _ret_layer(x, bn, l, time_major, pos0, s0, g0, g1, wq, wk, wv, wg, wo):
    rows = x.shape[0]
    tile = min(ROW_TILE, rows)
    half = RET_DK // 2
    freq = 1.0 / (ROPE_BASE ** jnp.linspace(0.0, 1.0, half, dtype=F32))
    pos = jnp.arange(pos0, pos0 + l, dtype=jnp.int32).astype(F32)
    ang = pos[:, None] * freq[None, :]
    cos, sin = jnp.cos(ang), jnp.sin(ang)
    if time_major:
        cos, sin = jnp.repeat(cos, bn, axis=0), jnp.repeat(sin, bn, axis=0)
    elif l < tile:
        cos, sin = jnp.tile(cos, (tile // l, 1)), jnp.tile(sin, (tile // l, 1))
    q, k, v = _row_grid_call(
        _ret_proj_kernel, [x], [g0.reshape(1, -1), wq.astype(BF16), wk.astype(BF16), wv.astype(BF16)],
        [(RET_HEADS * RET_DK, BF16), (RET_HEADS * RET_DK, BF16), (RET_HEADS * RET_DV, BF16)],
        rows, tile, name="ret_proj", tab_ins=[cos, sin])
    chunk, pad = _seq_chunking(l, RET_CHUNK)
    lp = l + pad
    q3, k3, v3 = (_to_seq(a, bn, l, time_major, pad) for a in (q, k, v))
    nc = lp // chunk
    o3, s_new = pl.pallas_call(
        functools.partial(_ret_chunk_kernel, chunk=chunk, pad=pad),
        grid=(bn, nc),
        in_specs=[
            pl.BlockSpec((1, chunk, RET_HEADS * RET_DK), lambda b, i: (b, i, 0)),
            pl.BlockSpec((1, chunk, RET_HEADS * RET_DK), lambda b, i: (b, i, 0)),
            pl.BlockSpec((1, chunk, RET_HEADS * RET_DV), lambda b, i: (b, i, 0)),
            pl.BlockSpec((1, RET_HEADS, RET_DK, RET_DV), lambda b, i: (b, 0, 0, 0)),
        ],
        out_specs=[
            pl.BlockSpec((1, chunk, RET_HEADS * RET_DV), lambda b, i: (b, i, 0)),
            pl.BlockSpec((1, RET_HEADS, RET_DK, RET_DV), lambda b, i: (b, 0, 0, 0)),
        ],
        out_shape=[jax.ShapeDtypeStruct((bn, lp, RET_HEADS * RET_DV), F32),
                   jax.ShapeDtypeStruct((bn, RET_HEADS, RET_DK, RET_DV), F32)],
        compiler_params=_cparams("arbitrary", "arbitrary"),
        name="ret_chunk",
    )(q3, k3, v3, s0)
    o = _from_seq(o3, bn, l, time_major, pad)
    (y,) = _row_grid_call(
        _ret_out_kernel, [x, o], [g0.reshape(1, -1), g1.reshape(1, -1), wg.astype(BF16), wo.astype(BF16)],
        [(D_MODEL, F32)], rows, tile, scratch=[pltpu.VMEM((tile, RET_HEADS * RET_DV), BF16)], name="ret_out")
    return y, s_new


def _gla_proj_kernel(x_ref, g_ref, wq_ref, wk_ref, wv_ref, wa1_ref, wa2_ref, ba_ref,
                     q_ref, k_ref, v_ref, la_ref):
    h = _rms(x_ref[...], g_ref[...]).astype(BF16)
    q_ref[...] = (_dot(h, wq_ref[...]) * (GLA_DK ** -0.5)).astype(BF16)
    k_ref[...] = _dot(h, wk_ref[...]).astype(BF16)
    v_ref[...] = _dot(h, wv_ref[...]).astype(BF16)
    low = _dot(h, wa1_ref[...]).astype(BF16)
    logit = _dot(low, wa2_ref[...]) + ba_ref[...]
    la_ref[...] = _log_sigmoid(logit) / GLA_GATE_NORM


def _gla_chunk_kernel(q_ref, k_ref, v_ref, la_ref, s0_ref, o_ref, sout_ref, st_sc, *, chunk):
    i = pl.program_id(1)

    @pl.when(i == 0)
    def _():
        for hd in range(GLA_HEADS):
            st_sc[hd] = s0_ref[0, hd].T

    ri = lax.broadcasted_iota(jnp.int32, (chunk, chunk), 0)
    ci = lax.broadcasted_iota(jnp.int32, (chunk, chunk), 1)
    causal = ri >= ci
    tri = causal.astype(BF16)
    mid = chunk // 2
    for hd in range(GLA_HEADS):
        ksl = slice(GLA_DK * hd, GLA_DK * (hd + 1))
        vsl = slice(GLA_DV * hd, GLA_DV * (hd + 1))
        la = la_ref[0, :, ksl]
        p1 = la.astype(BF16)
        r1 = la - p1.astype(F32)
        p2 = r1.astype(BF16)
        p3 = (r1 - p2.astype(F32)).astype(BF16)
        bc = _dot(tri, p1) + _dot(tri, p2) + _dot(tri, p3)
        ref = bc[mid:mid + 1, :]
        blast = bc[chunk - 1:chunk, :]
        qh = q_ref[0, :, ksl].astype(F32)
        kh = k_ref[0, :, ksl].astype(F32)
        vh = v_ref[0, :, vsl]
        st = st_sc[hd]
        qt = (qh * jnp.exp(bc - ref)).astype(BF16)
        kt = (kh * jnp.exp(ref - bc)).astype(BF16)
        scores = jnp.where(causal, _dot_nt(qt, kt), 0.0)
        qg = (qh * jnp.exp(bc)).astype(BF16)
        o_ref[0, :, vsl] = _dot(scores.astype(BF16), vh) + _dot_nt(qg, st.astype(BF16))
        kg = (kh * jnp.exp(blast - bc)).astype(BF16)
        st_sc[hd] = st * jnp.exp(blast) + _dot_tn(vh, kg)

    @pl.when(i == pl.num_programs(1) - 1)
    def _():
        for hd in range(GLA_HEADS):
            sout_ref[0, hd] = st_sc[hd].T


def _gla_out_kernel(x_ref, o_ref, g0_ref, g1_ref, ng_ref, wg_ref, wo_ref, y_ref, on_sc):
    x = x_ref[...]
    h = _rms(x, g0_ref[...]).astype(BF16)
    gate = _silu(_dot(h, wg_ref[...]))
    for hd in range(GLA_HEADS):
        sl = slice(GLA_DV * hd, GLA_DV * (hd + 1))
        oh = o_ref[:, sl]
        on = oh * lax.rsqrt(jnp.mean(oh * oh, axis=-1, keepdims=True) + NORM_EPS) * ng_ref[...]
        on_sc[:, sl] = (on * gate[:, sl]).astype(BF16)
    y_ref[...] = x + _rms(_dot(on_sc[...], wo_ref[...]), g1_ref[...])


def _gla_layer(x, bn, l, time_major, s0, g0, g1, wq, wk, wv, wg, wa1, wa2, ba, norm_g, wo):
    rows = x.shape[0]
    tile = min(ROW_TILE, rows)
    lanes = 128
    wa1p = jnp.pad(wa1, ((0, 0), (0, lanes - GLA_GATE_RANK))).astype(BF16)
    wa2p = jnp.pad(wa2, ((0, lanes - GLA_GATE_RANK), (0, 0))).astype(BF16)
    q, k, v, la = _row_grid_call(
        _gla_proj_kernel, [x],
        [g0.reshape(1, -1), wq.astype(BF16), wk.astype(BF16), wv.astype(BF16), wa1p, wa2p, ba.reshape(1, -1)],
        [(GLA_HEADS * GLA_DK, BF16), (GLA_HEADS * GLA_DK, BF16), (GLA_HEADS * GLA_DV, BF16),
         (GLA_HEADS * GLA_DK, F32)],
        rows, tile, name="gla_proj")
    chunk, pad = _seq_chunking(l, GLA_CHUNK)
    lp = l + pad
    q3, k3, v3, la3 = (_to_seq(a, bn, l, time_major, pad) for a in (q, k, v, la))
    nc = lp // chunk
    o3, s_new = pl.pallas_call(
        functools.partial(_gla_chunk_kernel, chunk=chunk),
        grid=(bn, nc),
        in_specs=[
            pl.BlockSpec((1, chunk, GLA_HEADS * GLA_DK), lambda b, i: (b, i, 0)),
            pl.BlockSpec((1, chunk, GLA_HEADS * GLA_DK), lambda b, i: (b, i, 0)),
            pl.BlockSpec((1, chunk, GLA_HEADS * GLA_DV), lambda b, i: (b, i, 0)),
            pl.BlockSpec((1, chunk, GLA_HEADS * GLA_DK), lambda b, i: (b, i, 0)),
            pl.BlockSpec((1, GLA_HEADS, GLA_DK, GLA_DV), lambda b, i: (b, 0, 0, 0)),
        ],
        out_specs=[
            pl.BlockSpec((1, chunk, GLA_HEADS * GLA_DV), lambda b, i: (b, i, 0)),
            pl.BlockSpec((1, GLA_HEADS, GLA_DK, GLA_DV), lambda b, i: (b, 0, 0, 0)),
        ],
        out_shape=[jax.ShapeDtypeStruct((bn, lp, GLA_HEADS * GLA_DV), F32),
                   jax.ShapeDtypeStruct((bn, GLA_HEADS, GLA_DK, GLA_DV), F32)],
        scratch_shapes=[pltpu.VMEM((GLA_HEADS, GLA_DV, GLA_DK), F32)],
        compiler_params=_cparams("arbitrary", "arbitrary"),
        name="gla_chunk",
    )(q3, k3, v3, la3, s0)
    o = _from_seq(o3, bn, l, time_major, pad)
    (y,) = _row_grid_call(
        _gla_out_kernel, [x, o],
        [g0.reshape(1, -1), g1.reshape(1, -1), norm_g.reshape(1, -1), wg.astype(BF16), wo.astype(BF16)],
        [(D_MODEL, F32)], rows, tile, scratch=[pltpu.VMEM((tile, GLA_HEADS * GLA_DV), BF16)], name="gla_out")
    return y, s_new


def _transpose_rows(x, a, b):
    return x.reshape(a, b, x.shape[1]).transpose(1, 0, 2).reshape(a * b, x.shape[1])


def _trunk(x, pos0, s5_re, s5_im, ret_s, gla_s, conv_buf, p, always_time_major):
    bn, l, _ = x.shape
    depth = p['norm_g'].shape[0]
    rows = x.reshape(bn * l, D_MODEL)
    time_major = False
    new_re, new_im, new_ret, new_gla, new_conv = [], [], [], [], []
    for layer in range(depth):
        g = p['norm_g'][layer]
        kind = layer % N_MIXERS
        j = layer // N_MIXERS
        want_tm = always_time_major or kind == 0
        if want_tm != time_major:
            rows = _transpose_rows(rows, l, bn) if time_major else _transpose_rows(rows, bn, l)
            time_major = want_tm
        if kind == 0:
            rows, hr, hi = _s5_layer(rows, bn, s5_re[j], s5_im[j], g[0], g[1], p['s5_lambda_re'][j],
                                     p['s5_lambda_im'][j], p['s5_b_re'][j], p['s5_b_im'][j], p['s5_c_re'][j],
                                     p['s5_c_im'][j], p['s5_log_dt'][j], p['s5_d'][j], p['s5_w_glu'][j])
            new_re.append(hr)
            new_im.append(hi)
        elif kind == 1:
            rows, s = _ret_layer(rows, bn, l, time_major, pos0, ret_s[j], g[0], g[1], p['ret_wq'][j],
                                 p['ret_wk'][j], p['ret_wv'][j], p['ret_wg'][j], p['ret_wo'][j])
            new_ret.append(s)
        else:
            rows, s = _gla_layer(rows, bn, l, time_major, gla_s[j], g[0], g[1], p['gla_wq'][j], p['gla_wk'][j],
                                 p['gla_wv'][j], p['gla_wg'][j], p['gla_wa1'][j], p['gla_wa2'][j], p['gla_ba'][j],
                                 p['gla_norm_g'][j], p['gla_wo'][j])
            new_gla.append(s)
        buf = conv_buf[layer]
        if time_major:
            cin = buf.transpose(1, 0, 2).reshape(1, (CONV_W - 1) * bn, D_FF)
            rows, cout = _ffn_layer(rows, 1, bn, cin, g[2], g[3], p['ffn_w_gate'][layer], p['ffn_w_up'][layer],
                                    p['ffn_conv_w'][layer], p['ffn_conv_b'][layer], p['ffn_w_down'][layer])
            cout = cout.reshape(CONV_W - 1, bn, D_FF).transpose(1, 0, 2)
        else:
            rows, cout = _ffn_layer(rows, bn, 1, buf, g[2], g[3], p['ffn_w_gate'][layer], p['ffn_w_up'][layer],
                                    p['ffn_conv_w'][layer], p['ffn_conv_b'][layer], p['ffn_w_down'][layer])
        new_conv.append(cout)
    if time_major:
        rows = _transpose_rows(rows, l, bn)
    y = rows.reshape(bn, l, D_MODEL)
    return (y, jnp.stack(new_re), jnp.stack(new_im), jnp.stack(new_ret), jnp.stack(new_gla),
            jnp.stack(new_conv))


def kernel(x_prompt, x_sample, state_s5_re, state_s5_im, state_ret, state_gla, cache_ffn_conv,
           norm_g, s5_lambda_re, s5_lambda_im, s5_b_re, s5_b_im, s5_c_re, s5_c_im, s5_log_dt, s5_d, s5_w_glu,
           ret_wq, ret_wk, ret_wv, ret_wg, ret_wo,
           gla_wq, gla_wk, gla_wv, gla_wg, gla_wa1, gla_wa2, gla_ba, gla_norm_g, gla_wo,
           ffn_w_gate, ffn_w_up, ffn_conv_w, ffn_conv_b, ffn_w_down):
    p = dict(norm_g=norm_g, s5_lambda_re=s5_lambda_re, s5_lambda_im=s5_lambda_im, s5_b_re=s5_b_re,
             s5_b_im=s5_b_im, s5_c_re=s5_c_re, s5_c_im=s5_c_im, s5_log_dt=s5_log_dt, s5_d=s5_d,
             s5_w_glu=s5_w_glu, ret_wq=ret_wq, ret_wk=ret_wk, ret_wv=ret_wv, ret_wg=ret_wg, ret_wo=ret_wo,
             gla_wq=gla_wq, gla_wk=gla_wk, gla_wv=gla_wv, gla_wg=gla_wg, gla_wa1=gla_wa1, gla_wa2=gla_wa2,
             gla_ba=gla_ba, gla_norm_g=gla_norm_g, gla_wo=gla_wo, ffn_w_gate=ffn_w_gate, ffn_w_up=ffn_w_up,
             ffn_conv_w=ffn_conv_w, ffn_conv_b=ffn_conv_b, ffn_w_down=ffn_w_down)
    bp = x_prompt.shape[0]
    z_s5 = jnp.zeros((state_s5_re.shape[0], bp) + state_s5_re.shape[2:], F32)
    z_ret = jnp.zeros((state_ret.shape[0], bp) + state_ret.shape[2:], F32)
    z_gla = jnp.zeros((state_gla.shape[0], bp) + state_gla.shape[2:], F32)
    z_conv = jnp.zeros((cache_ffn_conv.shape[0], bp) + cache_ffn_conv.shape[2:], x_prompt.dtype)
    outs_p = _trunk(x_prompt, 0, z_s5, z_s5, z_ret, z_gla, z_conv, p, always_time_major=False)
    outs_s = _trunk(x_sample, PAST_LEN, state_s5_re, state_s5_im, state_ret, state_gla, cache_ffn_conv, p,
                    always_time_major=True)
    return (outs_p[0], outs_s[0]) + tuple(outs_p[1:]) + tuple(outs_s[1:])
```

```python
import functools
import math

import jax
import jax.numpy as jnp
from jax import lax
from jax.experimental import pallas as pl
from jax.experimental.pallas import tpu as pltpu

F32 = jnp.float32
BF16 = jnp.bfloat16

D_MODEL = 1024
PAST_LEN = 16384
N_MIXERS = 3
S5_GROUP = 16
S5_GROUPS = D_MODEL // S5_GROUP
S5_STATE = 64
S5_SUPER = 4
S5_SG_CH = D_MODEL // S5_SUPER
S5_SG_ST = (S5_GROUPS // S5_SUPER) * S5_STATE
RET_HEADS = 4
RET_DK = D_MODEL // RET_HEADS
RET_DV = 2 * D_MODEL // RET_HEADS
RET_CHUNK = 256
ROPE_BASE = 10000.0
GLA_HEADS = 4
GLA_DK = D_MODEL // 2 // GLA_HEADS
GLA_DV = D_MODEL // GLA_HEADS
GLA_GATE_RANK = 16
GLA_GATE_NORM = 16.0
GLA_CHUNK = 64
GLA_GROUP = 8
D_FF = ((8 * D_MODEL // 3 + 255) // 256) * 256
FFN_SUB = 256
FFN_ROW_TILE = 1024
FFN_VMEM_MB = 58
S5_SCAN_UNROLL = 4
SEQ_GROUP_BYTES = 8 * 2 ** 20
CONV_W = 3
NORM_EPS = 1e-6
SMALL_SEQ_PAD = 16
ROW_TILE = 512
VMEM_MB = 52


def _cparams(*sem, vmem_mb=None):
    return pltpu.CompilerParams(dimension_semantics=sem, vmem_limit_bytes=(vmem_mb or VMEM_MB) * 2 ** 20)


def _const_spec(shape):
    zeros = (0,) * len(shape)
    return pl.BlockSpec(shape, lambda *_: zeros, pipeline_mode=pl.Buffered(1))


def _rms(x, g):
    return x * lax.rsqrt(jnp.mean(x * x, axis=-1, keepdims=True) + NORM_EPS) * g


def _sigmoid(x):
    return 1.0 / (1.0 + jnp.exp(-x))


def _silu(x):
    return x * _sigmoid(x)


def _gelu_tanh(x):
    return 0.5 * x * (1.0 + jnp.tanh(math.sqrt(2.0 / math.pi) * (x + 0.044715 * (x * x * x))))


def _log_sigmoid(x):
    return jnp.minimum(x, 0.0) - jnp.log(1.0 + jnp.exp(-jnp.abs(x)))


def _dot(a, b):
    return jnp.dot(a, b, preferred_element_type=F32)


def _dot_nt(a, b):
    return lax.dot_general(a, b, (((1,), (1,)), ((), ())), preferred_element_type=F32)


def _dot_tn(a, b):
    return lax.dot_general(a, b, (((0,), (0,)), ((), ())), preferred_element_type=F32)


def _s5_kernel(x_ref, h0_ref, g0_ref, g1_ref, are_ref, aim_ref, bbd_ref, cbd_ref, dsk_ref, wglu_ref,
               o_ref, hout_ref, bu_sc, y_sc, *, bn, tt):
    @pl.when(pl.program_id(0) == 0)
    def _():
        hout_ref[...] = h0_ref[...]

    x = x_ref[...]
    u = _rms(x, g0_ref[...])
    ub = u.astype(BF16)
    half = S5_SG_ST // 2
    for sg in range(S5_SUPER):
        bu_sc[...] = _dot(ub[:, S5_SG_CH * sg:S5_SG_CH * (sg + 1)], bbd_ref[sg])
        for hf in range(2):
            c_re = half * hf
            c_im = S5_SG_ST + half * hf
            a_lo = S5_SG_ST * sg + half * hf
            ar = jnp.broadcast_to(are_ref[:, a_lo:a_lo + half], (8, half))
            ai = jnp.broadcast_to(aim_ref[:, a_lo:a_lo + half], (8, half))
            h_re = 2 * S5_SG_ST * sg + c_re
            h_im = 2 * S5_SG_ST * sg + c_im

            def bt_body(bt, carry, c_re=c_re, c_im=c_im, ar=ar, ai=ai, h_re=h_re, h_im=h_im):
                r8 = pl.multiple_of(bt * 8, 8)
                hr0 = hout_ref[pl.ds(r8, 8), h_re:h_re + half]
                hi0 = hout_ref[pl.ds(r8, 8), h_im:h_im + half]

                def t_body(t, hc):
                    hr, hi = hc
                    r0 = pl.multiple_of(t * bn + r8, 8)
                    bur = bu_sc[pl.ds(r0, 8), c_re:c_re + half]
                    bui = bu_sc[pl.ds(r0, 8), c_im:c_im + half]
                    nhr = ar * hr - ai * hi + bur
                    nhi = ar * hi + ai * hr + bui
                    bu_sc[pl.ds(r0, 8), c_re:c_re + half] = nhr
                    bu_sc[pl.ds(r0, 8), c_im:c_im + half] = nhi
                    return nhr, nhi

                hr, hi = lax.fori_loop(0, tt, t_body, (hr0, hi0), unroll=S5_SCAN_UNROLL)
                hout_ref[pl.ds(r8, 8), h_re:h_re + half] = hr
                hout_ref[pl.ds(r8, 8), h_im:h_im + half] = hi
                return carry

            lax.fori_loop(0, bn // 8, bt_body, 0)
        y_sc[:, S5_SG_CH * sg:S5_SG_CH * (sg + 1)] = _dot(bu_sc[...].astype(BF16), cbd_ref[sg])
    y = y_sc[...] + dsk_ref[...] * u
    z = _dot(_gelu_tanh(y).astype(BF16), wglu_ref[...])
    mix = z[:, :D_MODEL] * _sigmoid(z[:, D_MODEL:])
    o_ref[...] = x + _rms(mix, g1_ref[...])


def _s5_layer(x, bn, h0_re, h0_im, g0, g1, lam_re, lam_im, b_re, b_im, c_re, c_im, log_dt, d_skip, w_glu):
    rows = x.shape[0]
    tt = min(rows // bn, max(1, ROW_TILE // bn))
    tile = tt * bn
    x_spec = pl.BlockSpec((tile, D_MODEL), lambda i: (i, 0))
    ngl = S5_GROUPS // S5_SUPER
    lam = lax.complex(lam_re, lam_im)
    dt = jnp.exp(log_dt)[:, None]
    a_bar = jnp.exp(lam * dt)
    b_bar = ((a_bar - 1.0) / lam)[..., None] * lax.complex(b_re, b_im)
    eye = jnp.eye(ngl, dtype=F32)
    bb = b_bar.reshape(S5_SUPER, ngl, S5_STATE, S5_GROUP).transpose(0, 1, 3, 2)

    def bd_in(part):
        return jnp.einsum('sgpn,gh->sgphn', part, eye).reshape(S5_SUPER, S5_SG_CH, S5_SG_ST)

    bbd = jnp.concatenate([bd_in(jnp.real(bb)), bd_in(jnp.imag(bb))], axis=-1).astype(BF16)
    cc_re = c_re.reshape(S5_SUPER, ngl, S5_GROUP, S5_STATE)
    cc_im = c_im.reshape(S5_SUPER, ngl, S5_GROUP, S5_STATE)

    def bd_out(part):
        return jnp.einsum('sgpn,gh->sgnhp', part, eye).reshape(S5_SUPER, S5_SG_ST, S5_SG_CH)

    cbd = jnp.concatenate([bd_out(cc_re), bd_out(-cc_im)], axis=1).astype(BF16)
    a_re = jnp.real(a_bar).reshape(1, S5_GROUPS * S5_STATE)
    a_im = jnp.imag(a_bar).reshape(1, S5_GROUPS * S5_STATE)
    h0 = jnp.stack([h0_re.reshape(bn, S5_SUPER, S5_SG_ST), h0_im.reshape(bn, S5_SUPER, S5_SG_ST)], axis=2)
    h0 = h0.reshape(bn, 2 * S5_GROUPS * S5_STATE)
    nst = 2 * S5_GROUPS * S5_STATE

    out, hout = pl.pallas_call(
        functools.partial(_s5_kernel, bn=bn, tt=tt),
        grid=(rows // tile,),
        in_specs=[
            x_spec,
            _const_spec((bn, nst)),
            _const_spec((1, D_MODEL)), _const_spec((1, D_MODEL)),
            _const_spec((1, nst // 2)), _const_spec((1, nst // 2)),
            _const_spec((S5_SUPER, S5_SG_CH, 2 * S5_SG_ST)),
            _const_spec((S5_SUPER, 2 * S5_SG_ST, S5_SG_CH)),
            _const_spec((1, D_MODEL)),
            _const_spec((D_MODEL, 2 * D_MODEL)),
        ],
        out_specs=[x_spec, _const_spec((bn, nst))],
        out_shape=[jax.ShapeDtypeStruct(x.shape, F32), jax.ShapeDtypeStruct((bn, nst), F32)],
        scratch_shapes=[pltpu.VMEM((tile, 2 * S5_SG_ST), F32), pltpu.VMEM((tile, D_MODEL), F32)],
        compiler_params=_cparams("arbitrary"),
        name="s5_layer",
    )(x, h0, g0.reshape(1, -1), g1.reshape(1, -1), a_re, a_im, bbd, cbd, d_skip.reshape(1, -1),
      w_glu.astype(BF16))
    out = out.reshape(rows, D_MODEL)
    hout = hout.reshape(bn, S5_SUPER, 2, ngl, S5_STATE)
    new_re = hout[:, :, 0].reshape(bn, S5_GROUPS, S5_STATE)
    new_im = hout[:, :, 1].reshape(bn, S5_GROUPS, S5_STATE)
    return out, new_re, new_im


def _ffn_kernel(x_ref, cin_ref, g2_ref, g3_ref, wg_ref, wu_ref, wd_ref, cw_ref, cb_ref,
                o_ref, cout_ref, hdn_sc, gext_sc, carry_sc, *, shift, tile):
    i = pl.program_id(1)
    halo = 2 * shift
    base = gext_sc.shape[1] - tile

    @pl.when(i == 0)
    def _():
        carry_sc[...] = cin_ref[0]

    x = x_ref[...]
    h = _rms(x, g2_ref[...]).astype(BF16)
    for j in range(D_FF // FFN_SUB):
        sl = slice(FFN_SUB * j, FFN_SUB * (j + 1))
        buf = gext_sc.at[j % 2]
        gpre = _dot(h, wg_ref[:, sl])
        buf[base - halo:base, :] = carry_sc[:, sl]
        buf[base:base + tile, :] = gpre
        carry_sc[:, sl] = gpre[tile - halo:tile, :]
        gconv = cb_ref[:, sl] + cw_ref[0:1, sl] * buf[base - halo:base - halo + tile, :]
        gconv = gconv + cw_ref[1:2, sl] * buf[base - shift:base - shift + tile, :]
        gconv = gconv + cw_ref[2:3, sl] * gpre
        hdn_sc[:, sl] = (_silu(gconv) * _dot(h, wu_ref[:, sl])).astype(BF16)
    o_ref[...] = x + _rms(_dot(hdn_sc[...], wd_ref[...]), g3_ref[...])

    @pl.when(i == pl.num_programs(1) - 1)
    def _():
        cout_ref[0] = carry_sc[...]


def _ffn_layer(x, nseg, shift, cin, g2, g3, w_gate, w_up, conv_w, conv_b, w_down):
    rows = x.shape[0]
    seg_rows = rows // nseg
    tile = min(FFN_ROW_TILE, seg_rows)
    tiles = seg_rows // tile
    halo = 2 * shift
    out, cout = pl.pallas_call(
        functools.partial(_ffn_kernel, shift=shift, tile=tile),
        grid=(nseg, tiles),
        in_specs=[
            pl.BlockSpec((tile, D_MODEL), lambda s, i: (s * tiles + i, 0)),
            pl.BlockSpec((1, halo, D_FF), lambda s, i: (s, 0, 0)),
            _const_spec((1, D_MODEL)), _const_spec((1, D_MODEL)),
            _const_spec((D_MODEL, D_FF)), _const_spec((D_MODEL, D_FF)), _const_spec((D_FF, D_MODEL)),
            _const_spec((CONV_W, D_FF)), _const_spec((1, D_FF)),
        ],
        out_specs=[
            pl.BlockSpec((tile, D_MODEL), lambda s, i: (s * tiles + i, 0)),
            pl.BlockSpec((1, halo, D_FF), lambda s, i: (s, 0, 0)),
        ],
        out_shape=[jax.ShapeDtypeStruct((rows, D_MODEL), F32),
                   jax.ShapeDtypeStruct((nseg, halo, D_FF), F32)],
        scratch_shapes=[pltpu.VMEM((tile, D_FF), BF16),
                        pltpu.VMEM((2, -(-halo // 8) * 8 + tile, FFN_SUB), F32),
                        pltpu.VMEM((halo, D_FF), F32)],
        compiler_params=_cparams("arbitrary", "arbitrary", vmem_mb=FFN_VMEM_MB),
        name="conv_ffn",
    )(x, cin, g2.reshape(1, -1), g3.reshape(1, -1), w_gate.astype(BF16), w_up.astype(BF16),
      w_down.astype(BF16), conv_w, conv_b.reshape(1, -1))
    return out, cout


def _ret_proj_kernel(x_ref, cos_ref, sin_ref, g_ref, wq_ref, wk_ref, wv_ref, q_ref, k_ref, v_ref):
    h = _rms(x_ref[...], g_ref[...]).astype(BF16)
    cos = cos_ref[...]
    sin = sin_ref[...]
    hw = RET_DK // 2
    for w_ref, dst, scale in ((wq_ref, q_ref, 1.0), (wk_ref, k_ref, RET_DK ** -0.5)):
        p = _dot(h, w_ref[...])
        for hd in range(RET_HEADS):
            lo = RET_DK * hd
            t1 = p[:, lo:lo + hw]
            t2 = p[:, lo + hw:lo + 2 * hw]
            dst[:, lo:lo + hw] = ((t1 * cos - t2 * sin) * scale).astype(BF16)
            dst[:, lo + hw:lo + 2 * hw] = ((t2 * cos + t1 * sin) * scale).astype(BF16)
    v_ref[...] = _dot(h, wv_ref[...]).astype(BF16)


def _ret_chunk_kernel(q_ref, k_ref, v_ref, s0_ref, o_ref, sout_ref, *, chunk, pad, group):
    @pl.when(pl.program_id(1) == 0)
    def _():
        sout_ref[...] = s0_ref[...]

    ri = lax.broadcasted_iota(jnp.int32, (chunk, chunk), 0)
    ci = lax.broadcasted_iota(jnp.int32, (chunk, chunk), 1)
    rel = (ri - ci).astype(F32)
    causal = ri >= ci
    idx = lax.broadcasted_iota(jnp.int32, (chunk, 1), 0).astype(F32)
    for hd in range(RET_HEADS):
        lg = math.log1p(-(2.0 ** (-5.0 - hd)))
        decay = jnp.where(causal, jnp.exp(jnp.maximum(rel, 0.0) * lg), 0.0)
        q_decay = jnp.exp((idx + (1.0 - pad)) * lg)
        k_decay = jnp.exp((chunk - 1.0 - idx) * lg)
        for g in range(group):
            qh = q_ref[g, :, RET_DK * hd:RET_DK * (hd + 1)]
            kh = k_ref[g, :, RET_DK * hd:RET_DK * (hd + 1)]
            vh = v_ref[g, :, RET_DV * hd:RET_DV * (hd + 1)]
            s = sout_ref[g, hd]
            scores = _dot_nt(qh, kh) * decay
            o = _dot(scores.astype(BF16), vh) + _dot(qh, s.astype(BF16)) * q_decay
            o_ref[g, :, RET_DV * hd:RET_DV * (hd + 1)] = o
            kd = (kh.astype(F32) * k_decay).astype(BF16)
            sout_ref[g, hd] = s * math.exp((chunk - pad) * lg) + _dot_tn(kd, vh)


def _ret_out_kernel(x_ref, o_ref, g0_ref, g1_ref, wg_ref, wo_ref, y_ref, on_sc):
    x = x_ref[...]
    h = _rms(x, g0_ref[...]).astype(BF16)
    gate = _silu(_dot(h, wg_ref[...]))
    for hd in range(RET_HEADS):
        sl = slice(RET_DV * hd, RET_DV * (hd + 1))
        oh = o_ref[:, sl]
        mu = jnp.mean(oh, axis=-1, keepdims=True)
        ctr = oh - mu
        var = jnp.mean(ctr * ctr, axis=-1, keepdims=True)
        on_sc[:, sl] = (ctr * lax.rsqrt(var + NORM_EPS) * gate[:, sl]).astype(BF16)
    y_ref[...] = x + _rms(_dot(on_sc[...], wo_ref[...]), g1_ref[...])


def _row_grid_call(kernel, row_ins, const_ins, outs, rows, tile, scratch=(), name=None, tab_ins=()):
    nt = rows // tile
    in_specs = [pl.BlockSpec((tile, a.shape[1]), lambda i: (i, 0)) for a in row_ins]
    for a in tab_ins:
        ntab = a.shape[0] // tile
        in_specs.append(pl.BlockSpec((tile, a.shape[1]), lambda i, ntab=ntab: (i % ntab, 0)))
    in_specs += [_const_spec(a.shape) for a in const_ins]
    return pl.pallas_call(
        kernel,
        grid=(nt,),
        in_specs=in_specs,
        out_specs=[pl.BlockSpec((tile, n), lambda i: (i, 0)) for n, _ in outs],
        out_shape=[jax.ShapeDtypeStruct((rows, n), dt) for n, dt in outs],
        scratch_shapes=list(scratch),
        compiler_params=_cparams("arbitrary"),
        name=name,
    )(*row_ins, *tab_ins, *const_ins)


def _to_seq(a, bn, l, time_major, pad):
    n = a.shape[1]
    a = a.reshape(l, bn, n).transpose(1, 0, 2) if time_major else a.reshape(bn, l, n)
    if pad:
        a = jnp.pad(a, ((0, 0), (pad, 0), (0, 0)))
    return a


def _from_seq(a, bn, l, time_major, pad):
    n = a.shape[2]
    a = a[:, pad:, :]
    a = a.transpose(1, 0, 2) if time_major else a
    return a.reshape(bn * l, n)


def _seq_group(bn, chunk, state_bytes):
    if chunk > SMALL_SEQ_PAD:
        return 1
    grp = max(1, min(bn, SEQ_GROUP_BYTES // state_bytes))
    while bn % grp:
        grp -= 1
    return grp


def _seq_chunking(l, chunk):
    if l % chunk == 0:
        return chunk, 0
    assert l <= SMALL_SEQ_PAD
    return SMALL_SEQ_PAD, SMALL_SEQ_PAD - l


def _ret_layer(x, bn, l, time_major, pos0, s0, g0, g1, wq, wk, wv, wg, wo):
    rows = x.shape[0]
    tile = min(ROW_TILE, rows)
    half = RET_DK // 2
    freq = 1.0 / (ROPE_BASE ** jnp.linspace(0.0, 1.0, half, dtype=F32))
    pos = jnp.arange(pos0, pos0 + l, dtype=jnp.int32).astype(F32)
    ang = pos[:, None] * freq[None, :]
    cos, sin = jnp.cos(ang), jnp.sin(ang)
    if time_major:
        cos, sin = jnp.repeat(cos, bn, axis=0), jnp.repeat(sin, bn, axis=0)
    elif l < tile:
        cos, sin = jnp.tile(cos, (tile // l, 1)), jnp.tile(sin, (tile // l, 1))
    q, k, v = _row_grid_call(
        _ret_proj_kernel, [x], [g0.reshape(1, -1), wq.astype(BF16), wk.astype(BF16), wv.astype(BF16)],
        [(RET_HEADS * RET_DK, BF16), (RET_HEADS * RET_DK, BF16), (RET_HEADS * RET_DV, BF16)],
        rows, tile, name="ret_proj", tab_ins=[cos, sin])
    chunk, pad = _seq_chunking(l, RET_CHUNK)
    lp = l + pad
    q3, k3, v3 = (_to_seq(a, bn, l, time_major, pad) for a in (q, k, v))
    nc = lp // chunk
    grp = _seq_group(bn, chunk, RET_HEADS * RET_DK * RET_DV * 4)
    o3, s_new = pl.pallas_call(
        functools.partial(_ret_chunk_kernel, chunk=chunk, pad=pad, group=grp),
        grid=(bn // grp, nc),
        in_specs=[
            pl.BlockSpec((grp, chunk, RET_HEADS * RET_DK), lambda b, i: (b, i, 0)),
            pl.BlockSpec((grp, chunk, RET_HEADS * RET_DK), lambda b, i: (b, i, 0)),
            pl.BlockSpec((grp, chunk, RET_HEADS * RET_DV), lambda b, i: (b, i, 0)),
            pl.BlockSpec((grp, RET_HEADS, RET_DK, RET_DV), lambda b, i: (b, 0, 0, 0)),
        ],
        out_specs=[
            pl.BlockSpec((grp, chunk, RET_HEADS * RET_DV), lambda b, i: (b, i, 0)),
            pl.BlockSpec((grp, RET_HEADS, RET_DK, RET_DV), lambda b, i: (b, 0, 0, 0)),
        ],
        out_shape=[jax.ShapeDtypeStruct((bn, lp, RET_HEADS * RET_DV), F32),
                   jax.ShapeDtypeStruct((bn, RET_HEADS, RET_DK, RET_DV), F32)],
        compiler_params=_cparams("arbitrary", "arbitrary"),
        name="ret_chunk",
    )(q3, k3, v3, s0)
    o = _from_seq(o3, bn, l, time_major, pad)
    (y,) = _row_grid_call(
        _ret_out_kernel, [x, o], [g0.reshape(1, -1), g1.reshape(1, -1), wg.astype(BF16), wo.astype(BF16)],
        [(D_MODEL, F32)], rows, tile, scratch=[pltpu.VMEM((tile, RET_HEADS * RET_DV), BF16)], name="ret_out")
    return y, s_new


def _gla_proj_kernel(x_ref, g_ref, wq_ref, wk_ref, wv_ref, wa1_ref, wa2_ref, ba_ref,
                     q_ref, k_ref, v_ref, la_ref):
    h = _rms(x_ref[...], g_ref[...]).astype(BF16)
    q_ref[...] = (_dot(h, wq_ref[...]) * (GLA_DK ** -0.5)).astype(BF16)
    k_ref[...] = _dot(h, wk_ref[...]).astype(BF16)
    v_ref[...] = _dot(h, wv_ref[...]).astype(BF16)
    low = _dot(h, wa1_ref[...]).astype(BF16)
    logit = _dot(low, wa2_ref[...]) + ba_ref[...]
    la_ref[...] = _log_sigmoid(logit) / GLA_GATE_NORM


def _gla_chunk_kernel(q_ref, k_ref, v_ref, la_ref, s0_ref, o_ref, sout_ref, st_sc, *, chunk, group):
    i = pl.program_id(1)

    @pl.when(i == 0)
    def _():
        for g in range(group):
            for hd in range(GLA_HEADS):
                st_sc[g, hd] = s0_ref[g, hd].T

    ri = lax.broadcasted_iota(jnp.int32, (chunk, chunk), 0)
    ci = lax.broadcasted_iota(jnp.int32, (chunk, chunk), 1)
    causal = ri >= ci
    tri = causal.astype(BF16)
    mid = chunk // 2
    nk = GLA_HEADS * GLA_DK
    for g in range(group):
        la = la_ref[g]
        p1 = la.astype(BF16)
        r1 = la - p1.astype(F32)
        p2 = r1.astype(BF16)
        p3 = (r1 - p2.astype(F32)).astype(BF16)
        acc = _dot(tri, jnp.concatenate([p1, p2, p3], axis=1))
        bc_all = acc[:, :nk] + acc[:, nk:2 * nk] + acc[:, 2 * nk:]
        for hd in range(GLA_HEADS):
            ksl = slice(GLA_DK * hd, GLA_DK * (hd + 1))
            vsl = slice(GLA_DV * hd, GLA_DV * (hd + 1))
            bc = bc_all[:, ksl]
            ref = bc[mid:mid + 1, :]
            blast = bc[chunk - 1:chunk, :]
            qh = q_ref[g, :, ksl].astype(F32)
            kh = k_ref[g, :, ksl].astype(F32)
            vh = v_ref[g, :, vsl]
            st = st_sc[g, hd]
            qt = (qh * jnp.exp(bc - ref)).astype(BF16)
            kt = (kh * jnp.exp(ref - bc)).astype(BF16)
            scores = jnp.where(causal, _dot_nt(qt, kt), 0.0)
            qg = (qh * jnp.exp(bc)).astype(BF16)
            o_ref[g, :, vsl] = _dot(scores.astype(BF16), vh) + _dot_nt(qg, st.astype(BF16))
            kg = (kh * jnp.exp(blast - bc)).astype(BF16)
            st_sc[g, hd] = st * jnp.exp(blast) + _dot_tn(vh, kg)

    @pl.when(i == pl.num_programs(1) - 1)
    def _():
        for g in range(group):
            for hd in range(GLA_HEADS):
                sout_ref[g, hd] = st_sc[g, hd].T


def _gla_out_kernel(x_ref, o_ref, g0_ref, g1_ref, ng_ref, wg_ref, wo_ref, y_ref, on_sc):
    x = x_ref[...]
    h = _rms(x, g0_ref[...]).astype(BF16)
    gate = _silu(_dot(h, wg_ref[...]))
    for hd in range(GLA_HEADS):
        sl = slice(GLA_DV * hd, GLA_DV * (hd + 1))
        oh = o_ref[:, sl]
        on = oh * lax.rsqrt(jnp.mean(oh * oh, axis=-1, keepdims=True) + NORM_EPS) * ng_ref[...]
        on_sc[:, sl] = (on * gate[:, sl]).astype(BF16)
    y_ref[...] = x + _rms(_dot(on_sc[...], wo_ref[...]), g1_ref[...])


def _gla_layer(x, bn, l, time_major, s0, g0, g1, wq, wk, wv, wg, wa1, wa2, ba, norm_g, wo):
    rows = x.shape[0]
    tile = min(ROW_TILE, rows)
    lanes = 128
    wa1p = jnp.pad(wa1, ((0, 0), (0, lanes - GLA_GATE_RANK))).astype(BF16)
    wa2p = jnp.pad(wa2, ((0, lanes - GLA_GATE_RANK), (0, 0))).astype(BF16)
    q, k, v, la = _row_grid_call(
        _gla_proj_kernel, [x],
        [g0.reshape(1, -1), wq.astype(BF16), wk.astype(BF16), wv.astype(BF16), wa1p, wa2p, ba.reshape(1, -1)],
        [(GLA_HEADS * GLA_DK, BF16), (GLA_HEADS * GLA_DK, BF16), (GLA_HEADS * GLA_DV, BF16),
         (GLA_HEADS * GLA_DK, F32)],
        rows, tile, name="gla_proj")
    chunk, pad = _seq_chunking(l, GLA_CHUNK)
    lp = l + pad
    q3, k3, v3, la3 = (_to_seq(a, bn, l, time_major, pad) for a in (q, k, v, la))
    nc = lp // chunk
    grp = GLA_GROUP if bn % GLA_GROUP == 0 else 1
    o3, s_new = pl.pallas_call(
        functools.partial(_gla_chunk_kernel, chunk=chunk, group=grp),
        grid=(bn // grp, nc),
        in_specs=[
            pl.BlockSpec((grp, chunk, GLA_HEADS * GLA_DK), lambda b, i: (b, i, 0)),
            pl.BlockSpec((grp, chunk, GLA_HEADS * GLA_DK), lambda b, i: (b, i, 0)),
            pl.BlockSpec((grp, chunk, GLA_HEADS * GLA_DV), lambda b, i: (b, i, 0)),
            pl.BlockSpec((grp, chunk, GLA_HEADS * GLA_DK), lambda b, i: (b, i, 0)),
            pl.BlockSpec((grp, GLA_HEADS, GLA_DK, GLA_DV), lambda b, i: (b, 0, 0, 0)),
        ],
        out_specs=[
            pl.BlockSpec((grp, chunk, GLA_HEADS * GLA_DV), lambda b, i: (b, i, 0)),
            pl.BlockSpec((grp, GLA_HEADS, GLA_DK, GLA_DV), lambda b, i: (b, 0, 0, 0)),
        ],
        out_shape=[jax.ShapeDtypeStruct((bn, lp, GLA_HEADS * GLA_DV), F32),
                   jax.ShapeDtypeStruct((bn, GLA_HEADS, GLA_DK, GLA_DV), F32)],
        scratch_shapes=[pltpu.VMEM((grp, GLA_HEADS, GLA_DV, GLA_DK), F32)],
        compiler_params=_cparams("arbitrary", "arbitrary"),
        name="gla_chunk",
    )(q3, k3, v3, la3, s0)
    o = _from_seq(o3, bn, l, time_major, pad)
    (y,) = _row_grid_call(
        _gla_out_kernel, [x, o],
        [g0.reshape(1, -1), g1.reshape(1, -1), norm_g.reshape(1, -1), wg.astype(BF16), wo.astype(BF16)],
        [(D_MODEL, F32)], rows, tile, scratch=[pltpu.VMEM((tile, GLA_HEADS * GLA_DV), BF16)], name="gla_out")
    return y, s_new


def _transpose_rows(x, a, b):
    return x.reshape(a, b, x.shape[1]).transpose(1, 0, 2).reshape(a * b, x.shape[1])


def _trunk(x, pos0, s5_re, s5_im, ret_s, gla_s, conv_buf, p, always_time_major):
    bn, l, _ = x.shape
    depth = p['norm_g'].shape[0]
    rows = x.reshape(bn * l, D_MODEL)
    time_major = False
    new_re, new_im, new_ret, new_gla, new_conv = [], [], [], [], []
    for layer in range(depth):
        g = p['norm_g'][layer]
        kind = layer % N_MIXERS
        j = layer // N_MIXERS
        want_tm = always_time_major or kind == 0
        if want_tm != time_major:
            rows = _transpose_rows(rows, l, bn) if time_major else _transpose_rows(rows, bn, l)
            time_major = want_tm
        if kind == 0:
            rows, hr, hi = _s5_layer(rows, bn, s5_re[j], s5_im[j], g[0], g[1],
                                     p['s5_lambda_re'][j], p['s5_lambda_im'][j], p['s5_b_re'][j], p['s5_b_im'][j],
                                     p['s5_c_re'][j], p['s5_c_im'][j], p['s5_log_dt'][j], p['s5_d'][j],
                                     p['s5_w_glu'][j])
            new_re.append(hr)
            new_im.append(hi)
        elif kind == 1:
            rows, s = _ret_layer(rows, bn, l, time_major, pos0, ret_s[j], g[0], g[1], p['ret_wq'][j],
                                 p['ret_wk'][j], p['ret_wv'][j], p['ret_wg'][j], p['ret_wo'][j])
            new_ret.append(s)
        else:
            rows, s = _gla_layer(rows, bn, l, time_major, gla_s[j], g[0], g[1], p['gla_wq'][j], p['gla_wk'][j],
                                 p['gla_wv'][j], p['gla_wg'][j], p['gla_wa1'][j], p['gla_wa2'][j], p['gla_ba'][j],
                                 p['gla_norm_g'][j], p['gla_wo'][j])
            new_gla.append(s)
        buf = conv_buf[layer]
        if time_major:
            cin = buf.transpose(1, 0, 2).reshape(1, (CONV_W - 1) * bn, D_FF)
            rows, cout = _ffn_layer(rows, 1, bn, cin, g[2], g[3], p['ffn_w_gate'][layer], p['ffn_w_up'][layer],
                                    p['ffn_conv_w'][layer], p['ffn_conv_b'][layer], p['ffn_w_down'][layer])
            cout = cout.reshape(CONV_W - 1, bn, D_FF).transpose(1, 0, 2)
        else:
            rows, cout = _ffn_layer(rows, bn, 1, buf, g[2], g[3], p['ffn_w_gate'][layer], p['ffn_w_up'][layer],
                                    p['ffn_conv_w'][layer], p['ffn_conv_b'][layer], p['ffn_w_down'][layer])
        new_conv.append(cout)
    if time_major:
        rows = _transpose_rows(rows, l, bn)
    y = rows.reshape(bn, l, D_MODEL)
    return (y, jnp.stack(new_re), jnp.stack(new_im), jnp.stack(new_ret), jnp.stack(new_gla),
            jnp.stack(new_conv))


def kernel(x_prompt, x_sample, state_s5_re, state_s5_im, state_ret, state_gla, cache_ffn_conv,
           norm_g, s5_lambda_re, s5_lambda_im, s5_b_re, s5_b_im, s5_c_re, s5_c_im, s5_log_dt, s5_d, s5_w_glu,
           ret_wq, ret_wk, ret_wv, ret_wg, ret_wo,
           gla_wq, gla_wk, gla_wv, gla_wg, gla_wa1, gla_wa2, gla_ba, gla_norm_g, gla_wo,
           ffn_w_gate, ffn_w_up, ffn_conv_w, ffn_conv_b, ffn_w_down):
    p = dict(norm_g=norm_g, s5_lambda_re=s5_lambda_re, s5_lambda_im=s5_lambda_im, s5_b_re=s5_b_re,
             s5_b_im=s5_b_im, s5_c_re=s5_c_re, s5_c_im=s5_c_im, s5_log_dt=s5_log_dt, s5_d=s5_d,
             s5_w_glu=s5_w_glu, ret_wq=ret_wq, ret_wk=ret_wk, ret_wv=ret_wv, ret_wg=ret_wg, ret_wo=ret_wo,
             gla_wq=gla_wq, gla_wk=gla_wk, gla_wv=gla_wv, gla_wg=gla_wg, gla_wa1=gla_wa1, gla_wa2=gla_wa2,
             gla_ba=gla_ba, gla_norm_g=gla_norm_g, gla_wo=gla_wo, ffn_w_gate=ffn_w_gate, ffn_w_up=ffn_w_up,
             ffn_conv_w=ffn_conv_w, ffn_conv_b=ffn_conv_b, ffn_w_down=ffn_w_down)
    bp = x_prompt.shape[0]
    z_s5 = jnp.zeros((state_s5_re.shape[0], bp) + state_s5_re.shape[2:], F32)
    z_ret = jnp.zeros((state_ret.shape[0], bp) + state_ret.shape[2:], F32)
    z_gla = jnp.zeros((state_gla.shape[0], bp) + state_gla.shape[2:], F32)
    z_conv = jnp.zeros((cache_ffn_conv.shape[0], bp) + cache_ffn_conv.shape[2:], x_prompt.dtype)
    outs_p = _trunk(x_prompt, 0, z_s5, z_s5, z_ret, z_gla, z_conv, p, always_time_major=False)
    outs_s = _trunk(x_sample, PAST_LEN, state_s5_re, state_s5_im, state_ret, state_gla, cache_ffn_conv, p,
                    always_time_major=True)
    return (outs_p[0], outs_s[0]) + tuple(outs_p[1:]) + tuple(outs_s[1:])
```

```python
import functools
import math

import jax
import jax.numpy as jnp
from jax import lax
from jax.experimental import pallas as pl
from jax.experimental.pallas import tpu as pltpu

F32 = jnp.float32
BF16 = jnp.bfloat16

D_MODEL = 1024
PAST_LEN = 16384
N_MIXERS = 3
S5_GROUP = 16
S5_GROUPS = D_MODEL // S5_GROUP
S5_STATE = 64
S5_SUPER = 4
S5_SG_CH = D_MODEL // S5_SUPER
S5_SG_ST = (S5_GROUPS // S5_SUPER) * S5_STATE
RET_HEADS = 4
RET_DK = D_MODEL // RET_HEADS
RET_DV = 2 * D_MODEL // RET_HEADS
RET_CHUNK = 256
ROPE_BASE = 10000.0
GLA_HEADS = 4
GLA_DK = D_MODEL // 2 // GLA_HEADS
GLA_DV = D_MODEL // GLA_HEADS
GLA_GATE_RANK = 16
GLA_GATE_NORM = 16.0
GLA_CHUNK = 64
GLA_GROUP = 8
D_FF = ((8 * D_MODEL // 3 + 255) // 256) * 256
FFN_SUB = 256
FFN_ROW_TILE = 1024
FFN_VMEM_MB = 58
SEQ_GROUP_BYTES = 8 * 2 ** 20
CONV_W = 3
NORM_EPS = 1e-6
SMALL_SEQ_PAD = 16
ROW_TILE = 512
VMEM_MB = 52


def _cparams(*sem, vmem_mb=None):
    return pltpu.CompilerParams(dimension_semantics=sem, vmem_limit_bytes=(vmem_mb or VMEM_MB) * 2 ** 20)


def _const_spec(shape):
    zeros = (0,) * len(shape)
    return pl.BlockSpec(shape, lambda *_: zeros, pipeline_mode=pl.Buffered(1))


def _rms(x, g):
    return x * lax.rsqrt(jnp.mean(x * x, axis=-1, keepdims=True) + NORM_EPS) * g


def _sigmoid(x):
    return 1.0 / (1.0 + jnp.exp(-x))


def _silu(x):
    return x * _sigmoid(x)


def _gelu_tanh(x):
    return 0.5 * x * (1.0 + jnp.tanh(math.sqrt(2.0 / math.pi) * (x + 0.044715 * (x * x * x))))


def _log_sigmoid(x):
    return jnp.minimum(x, 0.0) - jnp.log(1.0 + jnp.exp(-jnp.abs(x)))


def _dot(a, b):
    return jnp.dot(a, b, preferred_element_type=F32)


def _dot_nt(a, b):
    return lax.dot_general(a, b, (((1,), (1,)), ((), ())), preferred_element_type=F32)


def _dot_tn(a, b):
    return lax.dot_general(a, b, (((0,), (0,)), ((), ())), preferred_element_type=F32)


def _s5_kernel(x_ref, h0_ref, g0_ref, g1_ref, are_ref, aim_ref, bbd_ref, cbd_ref, dsk_ref, wglu_ref,
               o_ref, hout_ref, bu_sc, y_sc, *, bn, tt):
    @pl.when(pl.program_id(0) == 0)
    def _():
        hout_ref[...] = h0_ref[...]

    x = x_ref[...]
    u = _rms(x, g0_ref[...])
    ub = u.astype(BF16)
    half = S5_SG_ST // 2
    for sg in range(S5_SUPER):
        buf = bu_sc.at[sg % 2]
        buf[...] = _dot(ub[:, S5_SG_CH * sg:S5_SG_CH * (sg + 1)], bbd_ref[sg])
        for hf in range(2):
            c_re = half * hf
            c_im = S5_SG_ST + half * hf
            a_lo = S5_SG_ST * sg + half * hf
            ar = jnp.broadcast_to(are_ref[:, a_lo:a_lo + half], (8, half))
            ai = jnp.broadcast_to(aim_ref[:, a_lo:a_lo + half], (8, half))
            h_re = 2 * S5_SG_ST * sg + c_re
            h_im = 2 * S5_SG_ST * sg + c_im
            for bt in range(bn // 8):
                r8 = 8 * bt
                hr = hout_ref[r8:r8 + 8, h_re:h_re + half]
                hi = hout_ref[r8:r8 + 8, h_im:h_im + half]
                for t in range(tt):
                    r0 = t * bn + r8
                    bur = buf[r0:r0 + 8, c_re:c_re + half]
                    bui = buf[r0:r0 + 8, c_im:c_im + half]
                    hr, hi = ar * hr - ai * hi + bur, ar * hi + ai * hr + bui
                    buf[r0:r0 + 8, c_re:c_re + half] = hr
                    buf[r0:r0 + 8, c_im:c_im + half] = hi
                hout_ref[r8:r8 + 8, h_re:h_re + half] = hr
                hout_ref[r8:r8 + 8, h_im:h_im + half] = hi
        y_sc[:, S5_SG_CH * sg:S5_SG_CH * (sg + 1)] = _dot(buf[...].astype(BF16), cbd_ref[sg])
    y = y_sc[...] + dsk_ref[...] * u
    z = _dot(_gelu_tanh(y).astype(BF16), wglu_ref[...])
    mix = z[:, :D_MODEL] * _sigmoid(z[:, D_MODEL:])
    o_ref[...] = x + _rms(mix, g1_ref[...])


def _s5_layer(x, bn, h0_re, h0_im, g0, g1, lam_re, lam_im, b_re, b_im, c_re, c_im, log_dt, d_skip, w_glu):
    rows = x.shape[0]
    tt = min(rows // bn, max(1, ROW_TILE // bn))
    tile = tt * bn
    x_spec = pl.BlockSpec((tile, D_MODEL), lambda i: (i, 0))
    ngl = S5_GROUPS // S5_SUPER
    lam = lax.complex(lam_re, lam_im)
    dt = jnp.exp(log_dt)[:, None]
    a_bar = jnp.exp(lam * dt)
    b_bar = ((a_bar - 1.0) / lam)[..., None] * lax.complex(b_re, b_im)
    eye = jnp.eye(ngl, dtype=F32)
    bb = b_bar.reshape(S5_SUPER, ngl, S5_STATE, S5_GROUP).transpose(0, 1, 3, 2)

    def bd_in(part):
        return jnp.einsum('sgpn,gh->sgphn', part, eye).reshape(S5_SUPER, S5_SG_CH, S5_SG_ST)

    bbd = jnp.concatenate([bd_in(jnp.real(bb)), bd_in(jnp.imag(bb))], axis=-1).astype(BF16)
    cc_re = c_re.reshape(S5_SUPER, ngl, S5_GROUP, S5_STATE)
    cc_im = c_im.reshape(S5_SUPER, ngl, S5_GROUP, S5_STATE)

    def bd_out(part):
        return jnp.einsum('sgpn,gh->sgnhp', part, eye).reshape(S5_SUPER, S5_SG_ST, S5_SG_CH)

    cbd = jnp.concatenate([bd_out(cc_re), bd_out(-cc_im)], axis=1).astype(BF16)
    a_re = jnp.real(a_bar).reshape(1, S5_GROUPS * S5_STATE)
    a_im = jnp.imag(a_bar).reshape(1, S5_GROUPS * S5_STATE)
    h0 = jnp.stack([h0_re.reshape(bn, S5_SUPER, S5_SG_ST), h0_im.reshape(bn, S5_SUPER, S5_SG_ST)], axis=2)
    h0 = h0.reshape(bn, 2 * S5_GROUPS * S5_STATE)
    nst = 2 * S5_GROUPS * S5_STATE

    out, hout = pl.pallas_call(
        functools.partial(_s5_kernel, bn=bn, tt=tt),
        grid=(rows // tile,),
        in_specs=[
            x_spec,
            _const_spec((bn, nst)),
            _const_spec((1, D_MODEL)), _const_spec((1, D_MODEL)),
            _const_spec((1, nst // 2)), _const_spec((1, nst // 2)),
            _const_spec((S5_SUPER, S5_SG_CH, 2 * S5_SG_ST)),
            _const_spec((S5_SUPER, 2 * S5_SG_ST, S5_SG_CH)),
            _const_spec((1, D_MODEL)),
            _const_spec((D_MODEL, 2 * D_MODEL)),
        ],
        out_specs=[x_spec, _const_spec((bn, nst))],
        out_shape=[jax.ShapeDtypeStruct(x.shape, F32), jax.ShapeDtypeStruct((bn, nst), F32)],
        scratch_shapes=[pltpu.VMEM((2, tile, 2 * S5_SG_ST), F32), pltpu.VMEM((tile, D_MODEL), F32)],
        compiler_params=_cparams("arbitrary"),
        name="s5_layer",
    )(x, h0, g0.reshape(1, -1), g1.reshape(1, -1), a_re, a_im, bbd, cbd, d_skip.reshape(1, -1),
      w_glu.astype(BF16))
    out = out.reshape(rows, D_MODEL)
    hout = hout.reshape(bn, S5_SUPER, 2, ngl, S5_STATE)
    new_re = hout[:, :, 0].reshape(bn, S5_GROUPS, S5_STATE)
    new_im = hout[:, :, 1].reshape(bn, S5_GROUPS, S5_STATE)
    return out, new_re, new_im


def _ffn_kernel(x_ref, cin_ref, g2_ref, g3_ref, wg_ref, wu_ref, wd_ref, cw_ref, cb_ref,
                o_ref, cout_ref, hdn_sc, gext_sc, carry_sc, *, shift, tile):
    i = pl.program_id(1)
    halo = 2 * shift
    base = gext_sc.shape[1] - tile

    @pl.when(i == 0)
    def _():
        carry_sc[...] = cin_ref[0]

    x = x_ref[...]
    h = _rms(x, g2_ref[...]).astype(BF16)
    for j in range(D_FF // FFN_SUB):
        sl = slice(FFN_SUB * j, FFN_SUB * (j + 1))
        buf = gext_sc.at[j % 2]
        gpre = _dot(h, wg_ref[:, sl])
        buf[base - halo:base, :] = carry_sc[:, sl]
        buf[base:base + tile, :] = gpre
        carry_sc[:, sl] = gpre[tile - halo:tile, :]
        gconv = cb_ref[:, sl] + cw_ref[0:1, sl] * buf[base - halo:base - halo + tile, :]
        gconv = gconv + cw_ref[1:2, sl] * buf[base - shift:base - shift + tile, :]
        gconv = gconv + cw_ref[2:3, sl] * gpre
        hdn_sc[:, sl] = (_silu(gconv) * _dot(h, wu_ref[:, sl])).astype(BF16)
    o_ref[...] = x + _rms(_dot(hdn_sc[...], wd_ref[...]), g3_ref[...])

    @pl.when(i == pl.num_programs(1) - 1)
    def _():
        cout_ref[0] = carry_sc[...]


def _ffn_layer(x, nseg, shift, cin, g2, g3, w_gate, w_up, conv_w, conv_b, w_down):
    rows = x.shape[0]
    seg_rows = rows // nseg
    tile = min(FFN_ROW_TILE, seg_rows)
    tiles = seg_rows // tile
    halo = 2 * shift
    out, cout = pl.pallas_call(
        functools.partial(_ffn_kernel, shift=shift, tile=tile),
        grid=(nseg, tiles),
        in_specs=[
            pl.BlockSpec((tile, D_MODEL), lambda s, i: (s * tiles + i, 0)),
            pl.BlockSpec((1, halo, D_FF), lambda s, i: (s, 0, 0)),
            _const_spec((1, D_MODEL)), _const_spec((1, D_MODEL)),
            _const_spec((D_MODEL, D_FF)), _const_spec((D_MODEL, D_FF)), _const_spec((D_FF, D_MODEL)),
            _const_spec((CONV_W, D_FF)), _const_spec((1, D_FF)),
        ],
        out_specs=[
            pl.BlockSpec((tile, D_MODEL), lambda s, i: (s * tiles + i, 0)),
            pl.BlockSpec((1, halo, D_FF), lambda s, i: (s, 0, 0)),
        ],
        out_shape=[jax.ShapeDtypeStruct((rows, D_MODEL), F32),
                   jax.ShapeDtypeStruct((nseg, halo, D_FF), F32)],
        scratch_shapes=[pltpu.VMEM((tile, D_FF), BF16),
                        pltpu.VMEM((2, -(-halo // 8) * 8 + tile, FFN_SUB), F32),
                        pltpu.VMEM((halo, D_FF), F32)],
        compiler_params=_cparams("arbitrary", "arbitrary", vmem_mb=FFN_VMEM_MB),
        name="conv_ffn",
    )(x, cin, g2.reshape(1, -1), g3.reshape(1, -1), w_gate.astype(BF16), w_up.astype(BF16),
      w_down.astype(BF16), conv_w, conv_b.reshape(1, -1))
    return out, cout


def _ret_proj_kernel(x_ref, cos_ref, sin_ref, g_ref, wq_ref, wk_ref, wv_ref, q_ref, k_ref, v_ref):
    h = _rms(x_ref[...], g_ref[...]).astype(BF16)
    cos = cos_ref[...]
    sin = sin_ref[...]
    hw = RET_DK // 2
    for w_ref, dst, scale in ((wq_ref, q_ref, 1.0), (wk_ref, k_ref, RET_DK ** -0.5)):
        p = _dot(h, w_ref[...])
        for hd in range(RET_HEADS):
            lo = RET_DK * hd
            t1 = p[:, lo:lo + hw]
            t2 = p[:, lo + hw:lo + 2 * hw]
            dst[:, lo:lo + hw] = ((t1 * cos - t2 * sin) * scale).astype(BF16)
            dst[:, lo + hw:lo + 2 * hw] = ((t2 * cos + t1 * sin) * scale).astype(BF16)
    v_ref[...] = _dot(h, wv_ref[...]).astype(BF16)


def _ret_chunk_kernel(q_ref, k_ref, v_ref, s0_ref, o_ref, sout_ref, *, chunk, pad, group):
    @pl.when(pl.program_id(1) == 0)
    def _():
        sout_ref[...] = s0_ref[...]

    ri = lax.broadcasted_iota(jnp.int32, (chunk, chunk), 0)
    ci = lax.broadcasted_iota(jnp.int32, (chunk, chunk), 1)
    rel = (ri - ci).astype(F32)
    causal = ri >= ci
    idx = lax.broadcasted_iota(jnp.int32, (chunk, 1), 0).astype(F32)
    for hd in range(RET_HEADS):
        lg = math.log1p(-(2.0 ** (-5.0 - hd)))
        decay = jnp.where(causal, jnp.exp(jnp.maximum(rel, 0.0) * lg), 0.0)
        q_decay = jnp.exp((idx + (1.0 - pad)) * lg)
        k_decay = jnp.exp((chunk - 1.0 - idx) * lg)
        for g in range(group):
            qk0 = RET_DK * (RET_HEADS * g + hd)
            v0 = RET_DV * (RET_HEADS * g + hd)
            qh = q_ref[:, qk0:qk0 + RET_DK]
            kh = k_ref[:, qk0:qk0 + RET_DK]
            vh = v_ref[:, v0:v0 + RET_DV]
            s = sout_ref[g, hd]
            scores = _dot_nt(qh, kh) * decay
            o = _dot(scores.astype(BF16), vh) + _dot(qh, s.astype(BF16)) * q_decay
            o_ref[:, v0:v0 + RET_DV] = o
            kd = (kh.astype(F32) * k_decay).astype(BF16)
            sout_ref[g, hd] = s * math.exp((chunk - pad) * lg) + _dot_tn(kd, vh)


def _ret_out_kernel(x_ref, o_ref, g0_ref, g1_ref, wg_ref, wo_ref, y_ref, on_sc):
    x = x_ref[...]
    h = _rms(x, g0_ref[...]).astype(BF16)
    gate = _silu(_dot(h, wg_ref[...]))
    for hd in range(RET_HEADS):
        sl = slice(RET_DV * hd, RET_DV * (hd + 1))
        oh = o_ref[:, sl]
        mu = jnp.mean(oh, axis=-1, keepdims=True)
        ctr = oh - mu
        var = jnp.mean(ctr * ctr, axis=-1, keepdims=True)
        on_sc[:, sl] = (ctr * lax.rsqrt(var + NORM_EPS) * gate[:, sl]).astype(BF16)
    y_ref[...] = x + _rms(_dot(on_sc[...], wo_ref[...]), g1_ref[...])


def _row_grid_call(kernel, row_ins, const_ins, outs, rows, tile, scratch=(), name=None, tab_ins=()):
    nt = rows // tile
    in_specs = [pl.BlockSpec((tile, a.shape[1]), lambda i: (i, 0)) for a in row_ins]
    for a in tab_ins:
        ntab = a.shape[0] // tile
        in_specs.append(pl.BlockSpec((tile, a.shape[1]), lambda i, ntab=ntab: (i % ntab, 0)))
    in_specs += [_const_spec(a.shape) for a in const_ins]
    return pl.pallas_call(
        kernel,
        grid=(nt,),
        in_specs=in_specs,
        out_specs=[pl.BlockSpec((tile, n), lambda i: (i, 0)) for n, _ in outs],
        out_shape=[jax.ShapeDtypeStruct((rows, n), dt) for n, dt in outs],
        scratch_shapes=list(scratch),
        compiler_params=_cparams("arbitrary"),
        name=name,
    )(*row_ins, *tab_ins, *const_ins)


def _to_seq(a, bn, l, pad):
    a = a.reshape(l, bn * a.shape[1])
    return jnp.pad(a, ((pad, 0), (0, 0))) if pad else a


def _from_seq(a, bn, l, pad):
    return a[pad:].reshape(l * bn, a.shape[1] // bn)


def _seq_group(bn, chunk, state_bytes):
    if chunk > SMALL_SEQ_PAD:
        return 1
    grp = max(1, min(bn, SEQ_GROUP_BYTES // state_bytes))
    while bn % grp:
        grp -= 1
    return grp


def _seq_chunking(l, chunk):
    if l % chunk == 0:
        return chunk, 0
    assert l <= SMALL_SEQ_PAD
    return SMALL_SEQ_PAD, SMALL_SEQ_PAD - l


def _ret_layer(x, bn, l, pos0, s0, g0, g1, wq, wk, wv, wg, wo):
    rows = x.shape[0]
    tile = min(ROW_TILE, rows)
    half = RET_DK // 2
    freq = 1.0 / (ROPE_BASE ** jnp.linspace(0.0, 1.0, half, dtype=F32))
    pos = jnp.arange(pos0, pos0 + l, dtype=jnp.int32).astype(F32)
    ang = pos[:, None] * freq[None, :]
    cos, sin = jnp.cos(ang), jnp.sin(ang)
    cos, sin = jnp.repeat(cos, bn, axis=0), jnp.repeat(sin, bn, axis=0)
    q, k, v = _row_grid_call(
        _ret_proj_kernel, [x], [g0.reshape(1, -1), wq.astype(BF16), wk.astype(BF16), wv.astype(BF16)],
        [(RET_HEADS * RET_DK, BF16), (RET_HEADS * RET_DK, BF16), (RET_HEADS * RET_DV, BF16)],
        rows, tile, name="ret_proj", tab_ins=[cos, sin])
    chunk, pad = _seq_chunking(l, RET_CHUNK)
    lp = l + pad
    q3, k3, v3 = (_to_seq(a, bn, l, pad) for a in (q, k, v))
    nc = lp // chunk
    grp = _seq_group(bn, chunk, RET_HEADS * RET_DK * RET_DV * 4)
    o3, s_new = pl.pallas_call(
        functools.partial(_ret_chunk_kernel, chunk=chunk, pad=pad, group=grp),
        grid=(bn // grp, nc),
        in_specs=[
            pl.BlockSpec((chunk, grp * RET_HEADS * RET_DK), lambda b, i: (i, b)),
            pl.BlockSpec((chunk, grp * RET_HEADS * RET_DK), lambda b, i: (i, b)),
            pl.BlockSpec((chunk, grp * RET_HEADS * RET_DV), lambda b, i: (i, b)),
            pl.BlockSpec((grp, RET_HEADS, RET_DK, RET_DV), lambda b, i: (b, 0, 0, 0)),
        ],
        out_specs=[
            pl.BlockSpec((chunk, grp * RET_HEADS * RET_DV), lambda b, i: (i, b)),
            pl.BlockSpec((grp, RET_HEADS, RET_DK, RET_DV), lambda b, i: (b, 0, 0, 0)),
        ],
        out_shape=[jax.ShapeDtypeStruct((lp, bn * RET_HEADS * RET_DV), F32),
                   jax.ShapeDtypeStruct((bn, RET_HEADS, RET_DK, RET_DV), F32)],
        compiler_params=_cparams("arbitrary", "arbitrary"),
        name="ret_chunk",
    )(q3, k3, v3, s0)
    o = _from_seq(o3, bn, l, pad)
    (y,) = _row_grid_call(
        _ret_out_kernel, [x, o], [g0.reshape(1, -1), g1.reshape(1, -1), wg.astype(BF16), wo.astype(BF16)],
        [(D_MODEL, F32)], rows, tile, scratch=[pltpu.VMEM((tile, RET_HEADS * RET_DV), BF16)], name="ret_out")
    return y, s_new


def _gla_proj_kernel(x_ref, g_ref, wq_ref, wk_ref, wv_ref, wa1_ref, wa2_ref, ba_ref,
                     q_ref, k_ref, v_ref, la_ref):
    h = _rms(x_ref[...], g_ref[...]).astype(BF16)
    q_ref[...] = (_dot(h, wq_ref[...]) * (GLA_DK ** -0.5)).astype(BF16)
    k_ref[...] = _dot(h, wk_ref[...]).astype(BF16)
    v_ref[...] = _dot(h, wv_ref[...]).astype(BF16)
    low = _dot(h, wa1_ref[...]).astype(BF16)
    logit = _dot(low, wa2_ref[...]) + ba_ref[...]
    la_ref[...] = _log_sigmoid(logit) / GLA_GATE_NORM


def _gla_chunk_kernel(q_ref, k_ref, v_ref, la_ref, s0_ref, o_ref, sout_ref, st_sc, *, chunk, group):
    i = pl.program_id(1)

    @pl.when(i == 0)
    def _():
        for g in range(group):
            for hd in range(GLA_HEADS):
                st_sc[g, hd] = s0_ref[g, hd].T

    ri = lax.broadcasted_iota(jnp.int32, (chunk, chunk), 0)
    ci = lax.broadcasted_iota(jnp.int32, (chunk, chunk), 1)
    causal = ri >= ci
    tri = causal.astype(BF16)
    mid = chunk // 2
    nk = GLA_HEADS * GLA_DK
    for g in range(group):
        la = la_ref[:, nk * g:nk * (g + 1)]
        p1 = la.astype(BF16)
        r1 = la - p1.astype(F32)
        p2 = r1.astype(BF16)
        p3 = (r1 - p2.astype(F32)).astype(BF16)
        acc = _dot(tri, jnp.concatenate([p1, p2, p3], axis=1))
        bc_all = acc[:, :nk] + acc[:, nk:2 * nk] + acc[:, 2 * nk:]
        for hd in range(GLA_HEADS):
            ksl = slice(GLA_DK * hd, GLA_DK * (hd + 1))
            bc = bc_all[:, ksl]
            ref = bc[mid:mid + 1, :]
            blast = bc[chunk - 1:chunk, :]
            gk = slice(nk * g + GLA_DK * hd, nk * g + GLA_DK * (hd + 1))
            gv = slice(GLA_DV * (GLA_HEADS * g + hd), GLA_DV * (GLA_HEADS * g + hd + 1))
            qh = q_ref[:, gk].astype(F32)
            kh = k_ref[:, gk].astype(F32)
            vh = v_ref[:, gv]
            st = st_sc[g, hd]
            qt = (qh * jnp.exp(bc - ref)).astype(BF16)
            kt = (kh * jnp.exp(ref - bc)).astype(BF16)
            scores = jnp.where(causal, _dot_nt(qt, kt), 0.0)
            qg = (qh * jnp.exp(bc)).astype(BF16)
            o_ref[:, gv] = _dot(scores.astype(BF16), vh) + _dot_nt(qg, st.astype(BF16))
            kg = (kh * jnp.exp(blast - bc)).astype(BF16)
            st_sc[g, hd] = st * jnp.exp(blast) + _dot_tn(vh, kg)

    @pl.when(i == pl.num_programs(1) - 1)
    def _():
        for g in range(group):
            for hd in range(GLA_HEADS):
                sout_ref[g, hd] = st_sc[g, hd].T


def _gla_out_kernel(x_ref, o_ref, g0_ref, g1_ref, ng_ref, wg_ref, wo_ref, y_ref, on_sc):
    x = x_ref[...]
    h = _rms(x, g0_ref[...]).astype(BF16)
    gate = _silu(_dot(h, wg_ref[...]))
    for hd in range(GLA_HEADS):
        sl = slice(GLA_DV * hd, GLA_DV * (hd + 1))
        oh = o_ref[:, sl]
        on = oh * lax.rsqrt(jnp.mean(oh * oh, axis=-1, keepdims=True) + NORM_EPS) * ng_ref[...]
        on_sc[:, sl] = (on * gate[:, sl]).astype(BF16)
    y_ref[...] = x + _rms(_dot(on_sc[...], wo_ref[...]), g1_ref[...])


def _gla_layer(x, bn, l, s0, g0, g1, wq, wk, wv, wg, wa1, wa2, ba, norm_g, wo):
    rows = x.shape[0]
    tile = min(ROW_TILE, rows)
    lanes = 128
    wa1p = jnp.pad(wa1, ((0, 0), (0, lanes - GLA_GATE_RANK))).astype(BF16)
    wa2p = jnp.pad(wa2, ((0, lanes - GLA_GATE_RANK), (0, 0))).astype(BF16)
    q, k, v, la = _row_grid_call(
        _gla_proj_kernel, [x],
        [g0.reshape(1, -1), wq.astype(BF16), wk.astype(BF16), wv.astype(BF16), wa1p, wa2p, ba.reshape(1, -1)],
        [(GLA_HEADS * GLA_DK, BF16), (GLA_HEADS * GLA_DK, BF16), (GLA_HEADS * GLA_DV, BF16),
         (GLA_HEADS * GLA_DK, F32)],
        rows, tile, name="gla_proj")
    chunk, pad = _seq_chunking(l, GLA_CHUNK)
    lp = l + pad
    q3, k3, v3, la3 = (_to_seq(a, bn, l, pad) for a in (q, k, v, la))
    nc = lp // chunk
    grp = GLA_GROUP if bn % GLA_GROUP == 0 else 1
    o3, s_new = pl.pallas_call(
        functools.partial(_gla_chunk_kernel, chunk=chunk, group=grp),
        grid=(bn // grp, nc),
        in_specs=[
            pl.BlockSpec((chunk, grp * GLA_HEADS * GLA_DK), lambda b, i: (i, b)),
            pl.BlockSpec((chunk, grp * GLA_HEADS * GLA_DK), lambda b, i: (i, b)),
            pl.BlockSpec((chunk, grp * GLA_HEADS * GLA_DV), lambda b, i: (i, b)),
            pl.BlockSpec((chunk, grp * GLA_HEADS * GLA_DK), lambda b, i: (i, b)),
            pl.BlockSpec((grp, GLA_HEADS, GLA_DK, GLA_DV), lambda b, i: (b, 0, 0, 0)),
        ],
        out_specs=[
            pl.BlockSpec((chunk, grp * GLA_HEADS * GLA_DV), lambda b, i: (i, b)),
            pl.BlockSpec((grp, GLA_HEADS, GLA_DK, GLA_DV), lambda b, i: (b, 0, 0, 0)),
        ],
        out_shape=[jax.ShapeDtypeStruct((lp, bn * GLA_HEADS * GLA_DV), F32),
                   jax.ShapeDtypeStruct((bn, GLA_HEADS, GLA_DK, GLA_DV), F32)],
        scratch_shapes=[pltpu.VMEM((grp, GLA_HEADS, GLA_DV, GLA_DK), F32)],
        compiler_params=_cparams("arbitrary", "arbitrary"),
        name="gla_chunk",
    )(q3, k3, v3, la3, s0)
    o = _from_seq(o3, bn, l, pad)
    (y,) = _row_grid_call(
        _gla_out_kernel, [x, o],
        [g0.reshape(1, -1), g1.reshape(1, -1), norm_g.reshape(1, -1), wg.astype(BF16), wo.astype(BF16)],
        [(D_MODEL, F32)], rows, tile, scratch=[pltpu.VMEM((tile, GLA_HEADS * GLA_DV), BF16)], name="gla_out")
    return y, s_new


def _trunk(x, pos0, s5_re, s5_im, ret_s, gla_s, conv_buf, p):
    bn, l, _ = x.shape
    depth = p['norm_g'].shape[0]
    rows = x.transpose(1, 0, 2).reshape(l * bn, D_MODEL)
    new_re, new_im, new_ret, new_gla, new_conv = [], [], [], [], []
    for layer in range(depth):
        g = p['norm_g'][layer]
        kind = layer % N_MIXERS
        j = layer // N_MIXERS
        if kind == 0:
            rows, hr, hi = _s5_layer(rows, bn, s5_re[j], s5_im[j], g[0], g[1],
                                     p['s5_lambda_re'][j], p['s5_lambda_im'][j], p['s5_b_re'][j], p['s5_b_im'][j],
                                     p['s5_c_re'][j], p['s5_c_im'][j], p['s5_log_dt'][j], p['s5_d'][j],
                                     p['s5_w_glu'][j])
            new_re.append(hr)
            new_im.append(hi)
        elif kind == 1:
            rows, s = _ret_layer(rows, bn, l, pos0, ret_s[j], g[0], g[1], p['ret_wq'][j],
                                 p['ret_wk'][j], p['ret_wv'][j], p['ret_wg'][j], p['ret_wo'][j])
            new_ret.append(s)
        else:
            rows, s = _gla_layer(rows, bn, l, gla_s[j], g[0], g[1], p['gla_wq'][j], p['gla_wk'][j],
                                 p['gla_wv'][j], p['gla_wg'][j], p['gla_wa1'][j], p['gla_wa2'][j], p['gla_ba'][j],
                                 p['gla_norm_g'][j], p['gla_wo'][j])
            new_gla.append(s)
        cin = conv_buf[layer].transpose(1, 0, 2).reshape(1, (CONV_W - 1) * bn, D_FF)
        rows, cout = _ffn_layer(rows, 1, bn, cin, g[2], g[3], p['ffn_w_gate'][layer], p['ffn_w_up'][layer],
                                p['ffn_conv_w'][layer], p['ffn_conv_b'][layer], p['ffn_w_down'][layer])
        new_conv.append(cout.reshape(CONV_W - 1, bn, D_FF).transpose(1, 0, 2))
    y = rows.reshape(l, bn, D_MODEL).transpose(1, 0, 2)
    return (y, jnp.stack(new_re), jnp.stack(new_im), jnp.stack(new_ret), jnp.stack(new_gla),
            jnp.stack(new_conv))


def kernel(x_prompt, x_sample, state_s5_re, state_s5_im, state_ret, state_gla, cache_ffn_conv,
           norm_g, s5_lambda_re, s5_lambda_im, s5_b_re, s5_b_im, s5_c_re, s5_c_im, s5_log_dt, s5_d, s5_w_glu,
           ret_wq, ret_wk, ret_wv, ret_wg, ret_wo,
           gla_wq, gla_wk, gla_wv, gla_wg, gla_wa1, gla_wa2, gla_ba, gla_norm_g, gla_wo,
           ffn_w_gate, ffn_w_up, ffn_conv_w, ffn_conv_b, ffn_w_down):
    p = dict(norm_g=norm_g, s5_lambda_re=s5_lambda_re, s5_lambda_im=s5_lambda_im, s5_b_re=s5_b_re,
             s5_b_im=s5_b_im, s5_c_re=s5_c_re, s5_c_im=s5_c_im, s5_log_dt=s5_log_dt, s5_d=s5_d,
             s5_w_glu=s5_w_glu, ret_wq=ret_wq, ret_wk=ret_wk, ret_wv=ret_wv, ret_wg=ret_wg, ret_wo=ret_wo,
             gla_wq=gla_wq, gla_wk=gla_wk, gla_wv=gla_wv, gla_wg=gla_wg, gla_wa1=gla_wa1, gla_wa2=gla_wa2,
             gla_ba=gla_ba, gla_norm_g=gla_norm_g, gla_wo=gla_wo, ffn_w_gate=ffn_w_gate, ffn_w_up=ffn_w_up,
             ffn_conv_w=ffn_conv_w, ffn_conv_b=ffn_conv_b, ffn_w_down=ffn_w_down)
    bp = x_prompt.shape[0]
    z_s5 = jnp.zeros((state_s5_re.shape[0], bp) + state_s5_re.shape[2:], F32)
    z_ret = jnp.zeros((state_ret.shape[0], bp) + state_ret.shape[2:], F32)
    z_gla = jnp.zeros((state_gla.shape[0], bp) + state_gla.shape[2:], F32)
    z_conv = jnp.zeros((cache_ffn_conv.shape[0], bp) + cache_ffn_conv.shape[2:], x_prompt.dtype)
    outs_p = _trunk(x_prompt, 0, z_s5, z_s5, z_ret, z_gla, z_conv, p)
    outs_s = _trunk(x_sample, PAST_LEN, state_s5_re, state_s5_im, state_ret, state_gla, cache_ffn_conv, p)
    return (outs_p[0], outs_s[0]) + tuple(outs_p[1:]) + tuple(outs_s[1:])
```

```python
import functools
import math

import jax
import jax.numpy as jnp
from jax import lax
from jax.experimental import pallas as pl
from jax.experimental.pallas import tpu as pltpu

F32 = jnp.float32
BF16 = jnp.bfloat16

D_MODEL = 1024
PAST_LEN = 16384
N_MIXERS = 3
S5_GROUP = 16
S5_GROUPS = D_MODEL // S5_GROUP
S5_STATE = 64
S5_SUPER = 4
S5_SG_CH = D_MODEL // S5_SUPER
S5_SG_ST = (S5_GROUPS // S5_SUPER) * S5_STATE
RET_HEADS = 4
RET_DK = D_MODEL // RET_HEADS
RET_DV = 2 * D_MODEL // RET_HEADS
RET_CHUNK = 256
ROPE_BASE = 10000.0
GLA_HEADS = 4
GLA_DK = D_MODEL // 2 // GLA_HEADS
GLA_DV = D_MODEL // GLA_HEADS
GLA_GATE_RANK = 16
GLA_GATE_NORM = 16.0
GLA_CHUNK = 64
GLA_GROUP = 8
D_FF = ((8 * D_MODEL // 3 + 255) // 256) * 256
FFN_SUB = 256
FFN_ROW_TILE = 1024
FFN_VMEM_MB = 58
SEQ_GROUP_BYTES = 8 * 2 ** 20
CONV_W = 3
NORM_EPS = 1e-6
SMALL_SEQ_PAD = 16
ROW_TILE = 512
PROJ_ROW_TILE = 1024
VMEM_MB = 52


def _cparams(*sem, vmem_mb=None):
    return pltpu.CompilerParams(dimension_semantics=sem, vmem_limit_bytes=(vmem_mb or VMEM_MB) * 2 ** 20)


def _const_spec(shape):
    zeros = (0,) * len(shape)
    return pl.BlockSpec(shape, lambda *_: zeros, pipeline_mode=pl.Buffered(1))


def _rms(x, g):
    return x * lax.rsqrt(jnp.mean(x * x, axis=-1, keepdims=True) + NORM_EPS) * g


def _sigmoid(x):
    return 1.0 / (1.0 + jnp.exp(-x))


def _silu(x):
    return x * _sigmoid(x)


def _gelu_tanh(x):
    return 0.5 * x * (1.0 + jnp.tanh(math.sqrt(2.0 / math.pi) * (x + 0.044715 * (x * x * x))))


def _log_sigmoid(x):
    return jnp.minimum(x, 0.0) - jnp.log(1.0 + jnp.exp(-jnp.abs(x)))


def _dot(a, b):
    return jnp.dot(a, b, preferred_element_type=F32)


def _dot_nt(a, b):
    return lax.dot_general(a, b, (((1,), (1,)), ((), ())), preferred_element_type=F32)


def _dot_tn(a, b):
    return lax.dot_general(a, b, (((0,), (0,)), ((), ())), preferred_element_type=F32)


def _s5_kernel(x_ref, h0_ref, g0_ref, g1_ref, are_ref, aim_ref, bbd_ref, cbd_ref, dsk_ref, wglu_ref,
               o_ref, hout_ref, bu_sc, y_sc, *, bn, tt):
    @pl.when(pl.program_id(0) == 0)
    def _():
        hout_ref[...] = h0_ref[...]

    x = x_ref[...]
    u = _rms(x, g0_ref[...])
    ub = u.astype(BF16)
    half = S5_SG_ST // 2
    for sg in range(S5_SUPER):
        buf = bu_sc.at[sg % 2]
        buf[...] = _dot(ub[:, S5_SG_CH * sg:S5_SG_CH * (sg + 1)], bbd_ref[sg])
        for hf in range(2):
            c_re = half * hf
            c_im = S5_SG_ST + half * hf
            a_lo = S5_SG_ST * sg + half * hf
            ar = jnp.broadcast_to(are_ref[:, a_lo:a_lo + half], (8, half))
            ai = jnp.broadcast_to(aim_ref[:, a_lo:a_lo + half], (8, half))
            h_re = 2 * S5_SG_ST * sg + c_re
            h_im = 2 * S5_SG_ST * sg + c_im
            for bt in range(bn // 8):
                r8 = 8 * bt
                hr = hout_ref[r8:r8 + 8, h_re:h_re + half]
                hi = hout_ref[r8:r8 + 8, h_im:h_im + half]
                for t in range(tt):
                    r0 = t * bn + r8
                    bur = buf[r0:r0 + 8, c_re:c_re + half]
                    bui = buf[r0:r0 + 8, c_im:c_im + half]
                    hr, hi = ar * hr - ai * hi + bur, ar * hi + ai * hr + bui
                    buf[r0:r0 + 8, c_re:c_re + half] = hr
                    buf[r0:r0 + 8, c_im:c_im + half] = hi
                hout_ref[r8:r8 + 8, h_re:h_re + half] = hr
                hout_ref[r8:r8 + 8, h_im:h_im + half] = hi
        y_sc[:, S5_SG_CH * sg:S5_SG_CH * (sg + 1)] = _dot(buf[...].astype(BF16), cbd_ref[sg])
    y = y_sc[...] + dsk_ref[...] * u
    z = _dot(_gelu_tanh(y).astype(BF16), wglu_ref[...])
    mix = z[:, :D_MODEL] * _sigmoid(z[:, D_MODEL:])
    o_ref[...] = x + _rms(mix, g1_ref[...])


def _s5_layer(x, bn, h0_re, h0_im, g0, g1, lam_re, lam_im, b_re, b_im, c_re, c_im, log_dt, d_skip, w_glu):
    rows = x.shape[0]
    tt = min(rows // bn, max(1, ROW_TILE // bn))
    tile = tt * bn
    x_spec = pl.BlockSpec((tile, D_MODEL), lambda i: (i, 0))
    ngl = S5_GROUPS // S5_SUPER
    lam = lax.complex(lam_re, lam_im)
    dt = jnp.exp(log_dt)[:, None]
    a_bar = jnp.exp(lam * dt)
    b_bar = ((a_bar - 1.0) / lam)[..., None] * lax.complex(b_re, b_im)
    eye = jnp.eye(ngl, dtype=F32)
    bb = b_bar.reshape(S5_SUPER, ngl, S5_STATE, S5_GROUP).transpose(0, 1, 3, 2)

    def bd_in(part):
        return jnp.einsum('sgpn,gh->sgphn', part, eye).reshape(S5_SUPER, S5_SG_CH, S5_SG_ST)

    bbd = jnp.concatenate([bd_in(jnp.real(bb)), bd_in(jnp.imag(bb))], axis=-1).astype(BF16)
    cc_re = c_re.reshape(S5_SUPER, ngl, S5_GROUP, S5_STATE)
    cc_im = c_im.reshape(S5_SUPER, ngl, S5_GROUP, S5_STATE)

    def bd_out(part):
        return jnp.einsum('sgpn,gh->sgnhp', part, eye).reshape(S5_SUPER, S5_SG_ST, S5_SG_CH)

    cbd = jnp.concatenate([bd_out(cc_re), bd_out(-cc_im)], axis=1).astype(BF16)
    a_re = jnp.real(a_bar).reshape(1, S5_GROUPS * S5_STATE)
    a_im = jnp.imag(a_bar).reshape(1, S5_GROUPS * S5_STATE)
    h0 = jnp.stack([h0_re.reshape(bn, S5_SUPER, S5_SG_ST), h0_im.reshape(bn, S5_SUPER, S5_SG_ST)], axis=2)
    h0 = h0.reshape(bn, 2 * S5_GROUPS * S5_STATE)
    nst = 2 * S5_GROUPS * S5_STATE

    out, hout = pl.pallas_call(
        functools.partial(_s5_kernel, bn=bn, tt=tt),
        grid=(rows // tile,),
        in_specs=[
            x_spec,
            _const_spec((bn, nst)),
            _const_spec((1, D_MODEL)), _const_spec((1, D_MODEL)),
            _const_spec((1, nst // 2)), _const_spec((1, nst // 2)),
            _const_spec((S5_SUPER, S5_SG_CH, 2 * S5_SG_ST)),
            _const_spec((S5_SUPER, 2 * S5_SG_ST, S5_SG_CH)),
            _const_spec((1, D_MODEL)),
            _const_spec((D_MODEL, 2 * D_MODEL)),
        ],
        out_specs=[x_spec, _const_spec((bn, nst))],
        out_shape=[jax.ShapeDtypeStruct(x.shape, F32), jax.ShapeDtypeStruct((bn, nst), F32)],
        scratch_shapes=[pltpu.VMEM((2, tile, 2 * S5_SG_ST), F32), pltpu.VMEM((tile, D_MODEL), F32)],
        compiler_params=_cparams("arbitrary"),
        name="s5_layer",
    )(x, h0, g0.reshape(1, -1), g1.reshape(1, -1), a_re, a_im, bbd, cbd, d_skip.reshape(1, -1),
      w_glu.astype(BF16))
    out = out.reshape(rows, D_MODEL)
    hout = hout.reshape(bn, S5_SUPER, 2, ngl, S5_STATE)
    new_re = hout[:, :, 0].reshape(bn, S5_GROUPS, S5_STATE)
    new_im = hout[:, :, 1].reshape(bn, S5_GROUPS, S5_STATE)
    return out, new_re, new_im


def _ffn_kernel(x_ref, cin_ref, g2_ref, g3_ref, wg_ref, wu_ref, wd_ref, cw_ref, cb_ref,
                o_ref, cout_ref, hdn_sc, gext_sc, carry_sc, *, shift, tile):
    i = pl.program_id(1)
    halo = 2 * shift
    base = gext_sc.shape[1] - tile

    @pl.when(i == 0)
    def _():
        carry_sc[...] = cin_ref[0]

    x = x_ref[...]
    h = _rms(x, g2_ref[...]).astype(BF16)
    for j in range(D_FF // FFN_SUB):
        sl = slice(FFN_SUB * j, FFN_SUB * (j + 1))
        buf = gext_sc.at[j % 2]
        gpre = _dot(h, wg_ref[:, sl])
        buf[base - halo:base, :] = carry_sc[:, sl]
        buf[base:base + tile, :] = gpre
        carry_sc[:, sl] = gpre[tile - halo:tile, :]
        gconv = cb_ref[:, sl] + cw_ref[0:1, sl] * buf[base - halo:base - halo + tile, :]
        gconv = gconv + cw_ref[1:2, sl] * buf[base - shift:base - shift + tile, :]
        gconv = gconv + cw_ref[2:3, sl] * gpre
        hdn_sc[:, sl] = (_silu(gconv) * _dot(h, wu_ref[:, sl])).astype(BF16)
    o_ref[...] = x + _rms(_dot(hdn_sc[...], wd_ref[...]), g3_ref[...])

    @pl.when(i == pl.num_programs(1) - 1)
    def _():
        cout_ref[0] = carry_sc[...]


def _ffn_layer(x, nseg, shift, cin, g2, g3, w_gate, w_up, conv_w, conv_b, w_down):
    rows = x.shape[0]
    seg_rows = rows // nseg
    tile = min(FFN_ROW_TILE, seg_rows)
    tiles = seg_rows // tile
    halo = 2 * shift
    out, cout = pl.pallas_call(
        functools.partial(_ffn_kernel, shift=shift, tile=tile),
        grid=(nseg, tiles),
        in_specs=[
            pl.BlockSpec((tile, D_MODEL), lambda s, i: (s * tiles + i, 0)),
            pl.BlockSpec((1, halo, D_FF), lambda s, i: (s, 0, 0)),
            _const_spec((1, D_MODEL)), _const_spec((1, D_MODEL)),
            _const_spec((D_MODEL, D_FF)), _const_spec((D_MODEL, D_FF)), _const_spec((D_FF, D_MODEL)),
            _const_spec((CONV_W, D_FF)), _const_spec((1, D_FF)),
        ],
        out_specs=[
            pl.BlockSpec((tile, D_MODEL), lambda s, i: (s * tiles + i, 0)),
            pl.BlockSpec((1, halo, D_FF), lambda s, i: (s, 0, 0)),
        ],
        out_shape=[jax.ShapeDtypeStruct((rows, D_MODEL), F32),
                   jax.ShapeDtypeStruct((nseg, halo, D_FF), F32)],
        scratch_shapes=[pltpu.VMEM((tile, D_FF), BF16),
                        pltpu.VMEM((2, -(-halo // 8) * 8 + tile, FFN_SUB), F32),
                        pltpu.VMEM((halo, D_FF), F32)],
        compiler_params=_cparams("arbitrary", "arbitrary", vmem_mb=FFN_VMEM_MB),
        name="conv_ffn",
    )(x, cin, g2.reshape(1, -1), g3.reshape(1, -1), w_gate.astype(BF16), w_up.astype(BF16),
      w_down.astype(BF16), conv_w, conv_b.reshape(1, -1))
    return out, cout


def _ret_proj_kernel(x_ref, cos_ref, sin_ref, g_ref, wq_ref, wk_ref, wv_ref, q_ref, k_ref, v_ref):
    h = _rms(x_ref[...], g_ref[...]).astype(BF16)
    cos = cos_ref[...]
    sin = sin_ref[...]
    hw = RET_DK // 2
    for w_ref, dst, scale in ((wq_ref, q_ref, 1.0), (wk_ref, k_ref, RET_DK ** -0.5)):
        p = _dot(h, w_ref[...])
        for hd in range(RET_HEADS):
            lo = RET_DK * hd
            t1 = p[:, lo:lo + hw]
            t2 = p[:, lo + hw:lo + 2 * hw]
            dst[:, lo:lo + hw] = ((t1 * cos - t2 * sin) * scale).astype(BF16)
            dst[:, lo + hw:lo + 2 * hw] = ((t2 * cos + t1 * sin) * scale).astype(BF16)
    v_ref[...] = _dot(h, wv_ref[...]).astype(BF16)


def _ret_chunk_kernel(q_ref, k_ref, v_ref, s0_ref, o_ref, sout_ref, *, chunk, pad, group):
    @pl.when(pl.program_id(1) == 0)
    def _():
        sout_ref[...] = s0_ref[...]

    ri = lax.broadcasted_iota(jnp.int32, (chunk, chunk), 0)
    ci = lax.broadcasted_iota(jnp.int32, (chunk, chunk), 1)
    rel = (ri - ci).astype(F32)
    causal = ri >= ci
    idx = lax.broadcasted_iota(jnp.int32, (chunk, 1), 0).astype(F32)
    for hd in range(RET_HEADS):
        lg = math.log1p(-(2.0 ** (-5.0 - hd)))
        decay = jnp.where(causal, jnp.exp(jnp.maximum(rel, 0.0) * lg), 0.0)
        q_decay = jnp.exp((idx + (1.0 - pad)) * lg)
        k_decay = jnp.exp((chunk - 1.0 - idx) * lg)
        for g in range(group):
            qh = q_ref[g, :, RET_DK * hd:RET_DK * (hd + 1)]
            kh = k_ref[g, :, RET_DK * hd:RET_DK * (hd + 1)]
            vh = v_ref[g, :, RET_DV * hd:RET_DV * (hd + 1)]
            s = sout_ref[g, hd]
            scores = _dot_nt(qh, kh) * decay
            o = _dot(scores.astype(BF16), vh) + _dot(qh, s.astype(BF16)) * q_decay
            o_ref[g, :, RET_DV * hd:RET_DV * (hd + 1)] = o
            kd = (kh.astype(F32) * k_decay).astype(BF16)
            sout_ref[g, hd] = s * math.exp((chunk - pad) * lg) + _dot_tn(kd, vh)


def _ret_out_kernel(x_ref, o_ref, g0_ref, g1_ref, wg_ref, wo_ref, y_ref, on_sc):
    x = x_ref[...]
    h = _rms(x, g0_ref[...]).astype(BF16)
    gate = _silu(_dot(h, wg_ref[...]))
    for hd in range(RET_HEADS):
        sl = slice(RET_DV * hd, RET_DV * (hd + 1))
        oh = o_ref[:, sl]
        mu = jnp.mean(oh, axis=-1, keepdims=True)
        ctr = oh - mu
        var = jnp.mean(ctr * ctr, axis=-1, keepdims=True)
        on_sc[:, sl] = (ctr * lax.rsqrt(var + NORM_EPS) * gate[:, sl]).astype(BF16)
    y_ref[...] = x + _rms(_dot(on_sc[...], wo_ref[...]), g1_ref[...])


def _row_grid_call(kernel, row_ins, const_ins, outs, rows, tile, scratch=(), name=None, tab_ins=(),
                   vmem_mb=None):
    nt = rows // tile
    in_specs = [pl.BlockSpec((tile, a.shape[1]), lambda i: (i, 0)) for a in row_ins]
    for a in tab_ins:
        ntab = a.shape[0] // tile
        in_specs.append(pl.BlockSpec((tile, a.shape[1]), lambda i, ntab=ntab: (i % ntab, 0)))
    in_specs += [_const_spec(a.shape) for a in const_ins]
    return pl.pallas_call(
        kernel,
        grid=(nt,),
        in_specs=in_specs,
        out_specs=[pl.BlockSpec((tile, n), lambda i: (i, 0)) for n, _ in outs],
        out_shape=[jax.ShapeDtypeStruct((rows, n), dt) for n, dt in outs],
        scratch_shapes=list(scratch),
        compiler_params=_cparams("arbitrary", vmem_mb=vmem_mb),
        name=name,
    )(*row_ins, *tab_ins, *const_ins)


def _to_seq(a, bn, l, time_major, pad):
    n = a.shape[1]
    a = a.reshape(l, bn, n).transpose(1, 0, 2) if time_major else a.reshape(bn, l, n)
    if pad:
        a = jnp.pad(a, ((0, 0), (pad, 0), (0, 0)))
    return a


def _from_seq(a, bn, l, time_major, pad):
    n = a.shape[2]
    a = a[:, pad:, :]
    a = a.transpose(1, 0, 2) if time_major else a
    return a.reshape(bn * l, n)


def _seq_group(bn, chunk, state_bytes):
    if chunk > SMALL_SEQ_PAD:
        return 1
    grp = max(1, min(bn, SEQ_GROUP_BYTES // state_bytes))
    while bn % grp:
        grp -= 1
    return grp


def _seq_chunking(l, chunk):
    if l % chunk == 0:
        return chunk, 0
    assert l <= SMALL_SEQ_PAD
    return SMALL_SEQ_PAD, SMALL_SEQ_PAD - l


def _ret_layer(x, bn, l, time_major, pos0, s0, g0, g1, wq, wk, wv, wg, wo):
    rows = x.shape[0]
    tile = min(ROW_TILE, rows)
    ptile = min(PROJ_ROW_TILE, rows)
    half = RET_DK // 2
    freq = 1.0 / (ROPE_BASE ** jnp.linspace(0.0, 1.0, half, dtype=F32))
    pos = jnp.arange(pos0, pos0 + l, dtype=jnp.int32).astype(F32)
    ang = pos[:, None] * freq[None, :]
    cos, sin = jnp.cos(ang), jnp.sin(ang)
    if time_major:
        cos, sin = jnp.repeat(cos, bn, axis=0), jnp.repeat(sin, bn, axis=0)
    elif l < ptile:
        cos, sin = jnp.tile(cos, (ptile // l, 1)), jnp.tile(sin, (ptile // l, 1))
    q, k, v = _row_grid_call(
        _ret_proj_kernel, [x], [g0.reshape(1, -1), wq.astype(BF16), wk.astype(BF16), wv.astype(BF16)],
        [(RET_HEADS * RET_DK, BF16), (RET_HEADS * RET_DK, BF16), (RET_HEADS * RET_DV, BF16)],
        rows, ptile, name="ret_proj", tab_ins=[cos, sin], vmem_mb=FFN_VMEM_MB)
    chunk, pad = _seq_chunking(l, RET_CHUNK)
    lp = l + pad
    q3, k3, v3 = (_to_seq(a, bn, l, time_major, pad) for a in (q, k, v))
    nc = lp // chunk
    grp = _seq_group(bn, chunk, RET_HEADS * RET_DK * RET_DV * 4)
    o3, s_new = pl.pallas_call(
        functools.partial(_ret_chunk_kernel, chunk=chunk, pad=pad, group=grp),
        grid=(bn // grp, nc),
        in_specs=[
            pl.BlockSpec((grp, chunk, RET_HEADS * RET_DK), lambda b, i: (b, i, 0)),
            pl.BlockSpec((grp, chunk, RET_HEADS * RET_DK), lambda b, i: (b, i, 0)),
            pl.BlockSpec((grp, chunk, RET_HEADS * RET_DV), lambda b, i: (b, i, 0)),
            pl.BlockSpec((grp, RET_HEADS, RET_DK, RET_DV), lambda b, i: (b, 0, 0, 0)),
        ],
        out_specs=[
            pl.BlockSpec((grp, chunk, RET_HEADS * RET_DV), lambda b, i: (b, i, 0)),
            pl.BlockSpec((grp, RET_HEADS, RET_DK, RET_DV), lambda b, i: (b, 0, 0, 0)),
        ],
        out_shape=[jax.ShapeDtypeStruct((bn, lp, RET_HEADS * RET_DV), F32),
                   jax.ShapeDtypeStruct((bn, RET_HEADS, RET_DK, RET_DV), F32)],
        compiler_params=_cparams("arbitrary", "arbitrary"),
        name="ret_chunk",
    )(q3, k3, v3, s0)
    o = _from_seq(o3, bn, l, time_major, pad)
    (y,) = _row_grid_call(
        _ret_out_kernel, [x, o], [g0.reshape(1, -1), g1.reshape(1, -1), wg.astype(BF16), wo.astype(BF16)],
        [(D_MODEL, F32)], rows, tile, scratch=[pltpu.VMEM((tile, RET_HEADS * RET_DV), BF16)], name="ret_out")
    return y, s_new


def _gla_proj_kernel(x_ref, g_ref, wq_ref, wk_ref, wv_ref, wa1_ref, wa2_ref, ba_ref,
                     q_ref, k_ref, v_ref, la_ref):
    h = _rms(x_ref[...], g_ref[...]).astype(BF16)
    q_ref[...] = (_dot(h, wq_ref[...]) * (GLA_DK ** -0.5)).astype(BF16)
    k_ref[...] = _dot(h, wk_ref[...]).astype(BF16)
    v_ref[...] = _dot(h, wv_ref[...]).astype(BF16)
    low = _dot(h, wa1_ref[...]).astype(BF16)
    logit = _dot(low, wa2_ref[...]) + ba_ref[...]
    la_ref[...] = _log_sigmoid(logit) / GLA_GATE_NORM


def _gla_chunk_kernel(q_ref, k_ref, v_ref, la_ref, s0_ref, o_ref, sout_ref, st_sc, *, chunk, group):
    i = pl.program_id(1)

    @pl.when(i == 0)
    def _():
        for g in range(group):
            for hd in range(GLA_HEADS):
                st_sc[g, hd] = s0_ref[g, hd].T

    ri = lax.broadcasted_iota(jnp.int32, (chunk, chunk), 0)
    ci = lax.broadcasted_iota(jnp.int32, (chunk, chunk), 1)
    causal = ri >= ci
    tri = causal.astype(BF16)
    mid = chunk // 2
    nk = GLA_HEADS * GLA_DK
    for g in range(group):
        la = la_ref[g]
        p1 = la.astype(BF16)
        r1 = la - p1.astype(F32)
        p2 = r1.astype(BF16)
        p3 = (r1 - p2.astype(F32)).astype(BF16)
        acc = _dot(tri, jnp.concatenate([p1, p2, p3], axis=1))
        bc_all = acc[:, :nk] + acc[:, nk:2 * nk] + acc[:, 2 * nk:]
        for hd in range(GLA_HEADS):
            ksl = slice(GLA_DK * hd, GLA_DK * (hd + 1))
            vsl = slice(GLA_DV * hd, GLA_DV * (hd + 1))
            bc = bc_all[:, ksl]
            ref = bc[mid:mid + 1, :]
            blast = bc[chunk - 1:chunk, :]
            qh = q_ref[g, :, ksl].astype(F32)
            kh = k_ref[g, :, ksl].astype(F32)
            vh = v_ref[g, :, vsl]
            st = st_sc[g, hd]
            qt = (qh * jnp.exp(bc - ref)).astype(BF16)
            kt = (kh * jnp.exp(ref - bc)).astype(BF16)
            scores = jnp.where(causal, _dot_nt(qt, kt), 0.0)
            qg = (qh * jnp.exp(bc)).astype(BF16)
            o_ref[g, :, vsl] = _dot(scores.astype(BF16), vh) + _dot_nt(qg, st.astype(BF16))
            kg = (kh * jnp.exp(blast - bc)).astype(BF16)
            st_sc[g, hd] = st * jnp.exp(blast) + _dot_tn(vh, kg)

    @pl.when(i == pl.num_programs(1) - 1)
    def _():
        for g in range(group):
            for hd in range(GLA_HEADS):
                sout_ref[g, hd] = st_sc[g, hd].T


def _gla_out_kernel(x_ref, o_ref, g0_ref, g1_ref, ng_ref, wg_ref, wo_ref, y_ref, on_sc):
    x = x_ref[...]
    h = _rms(x, g0_ref[...]).astype(BF16)
    gate = _silu(_dot(h, wg_ref[...]))
    for hd in range(GLA_HEADS):
        sl = slice(GLA_DV * hd, GLA_DV * (hd + 1))
        oh = o_ref[:, sl]
        on = oh * lax.rsqrt(jnp.mean(oh * oh, axis=-1, keepdims=True) + NORM_EPS) * ng_ref[...]
        on_sc[:, sl] = (on * gate[:, sl]).astype(BF16)
    y_ref[...] = x + _rms(_dot(on_sc[...], wo_ref[...]), g1_ref[...])


def _gla_layer(x, bn, l, time_major, s0, g0, g1, wq, wk, wv, wg, wa1, wa2, ba, norm_g, wo):
    rows = x.shape[0]
    tile = min(ROW_TILE, rows)
    lanes = 128
    wa1p = jnp.pad(wa1, ((0, 0), (0, lanes - GLA_GATE_RANK))).astype(BF16)
    wa2p = jnp.pad(wa2, ((0, lanes - GLA_GATE_RANK), (0, 0))).astype(BF16)
    q, k, v, la = _row_grid_call(
        _gla_proj_kernel, [x],
        [g0.reshape(1, -1), wq.astype(BF16), wk.astype(BF16), wv.astype(BF16), wa1p, wa2p, ba.reshape(1, -1)],
        [(GLA_HEADS * GLA_DK, BF16), (GLA_HEADS * GLA_DK, BF16), (GLA_HEADS * GLA_DV, BF16),
         (GLA_HEADS * GLA_DK, F32)],
        rows, min(PROJ_ROW_TILE, rows), name="gla_proj", vmem_mb=FFN_VMEM_MB)
    chunk, pad = _seq_chunking(l, GLA_CHUNK)
    lp = l + pad
    q3, k3, v3, la3 = (_to_seq(a, bn, l, time_major, pad) for a in (q, k, v, la))
    nc = lp // chunk
    grp = GLA_GROUP if bn % GLA_GROUP == 0 else 1
    o3, s_new = pl.pallas_call(
        functools.partial(_gla_chunk_kernel, chunk=chunk, group=grp),
        grid=(bn // grp, nc),
        in_specs=[
            pl.BlockSpec((grp, chunk, GLA_HEADS * GLA_DK), lambda b, i: (b, i, 0)),
            pl.BlockSpec((grp, chunk, GLA_HEADS * GLA_DK), lambda b, i: (b, i, 0)),
            pl.BlockSpec((grp, chunk, GLA_HEADS * GLA_DV), lambda b, i: (b, i, 0)),
            pl.BlockSpec((grp, chunk, GLA_HEADS * GLA_DK), lambda b, i: (b, i, 0)),
            pl.BlockSpec((grp, GLA_HEADS, GLA_DK, GLA_DV), lambda b, i: (b, 0, 0, 0)),
        ],
        out_specs=[
            pl.BlockSpec((grp, chunk, GLA_HEADS * GLA_DV), lambda b, i: (b, i, 0)),
            pl.BlockSpec((grp, GLA_HEADS, GLA_DK, GLA_DV), lambda b, i: (b, 0, 0, 0)),
        ],
        out_shape=[jax.ShapeDtypeStruct((bn, lp, GLA_HEADS * GLA_DV), F32),
                   jax.ShapeDtypeStruct((bn, GLA_HEADS, GLA_DK, GLA_DV), F32)],
        scratch_shapes=[pltpu.VMEM((grp, GLA_HEADS, GLA_DV, GLA_DK), F32)],
        compiler_params=_cparams("arbitrary", "arbitrary"),
        name="gla_chunk",
    )(q3, k3, v3, la3, s0)
    o = _from_seq(o3, bn, l, time_major, pad)
    (y,) = _row_grid_call(
        _gla_out_kernel, [x, o],
        [g0.reshape(1, -1), g1.reshape(1, -1), norm_g.reshape(1, -1), wg.astype(BF16), wo.astype(BF16)],
        [(D_MODEL, F32)], rows, tile, scratch=[pltpu.VMEM((tile, GLA_HEADS * GLA_DV), BF16)], name="gla_out")
    return y, s_new


def _transpose_rows(x, a, b):
    return x.reshape(a, b, x.shape[1]).transpose(1, 0, 2).reshape(a * b, x.shape[1])


def _trunk(x, pos0, s5_re, s5_im, ret_s, gla_s, conv_buf, p, always_time_major):
    bn, l, _ = x.shape
    depth = p['norm_g'].shape[0]
    rows = x.reshape(bn * l, D_MODEL)
    time_major = False
    new_re, new_im, new_ret, new_gla, new_conv = [], [], [], [], []
    for layer in range(depth):
        g = p['norm_g'][layer]
        kind = layer % N_MIXERS
        j = layer // N_MIXERS
        want_tm = always_time_major or kind == 0
        if want_tm != time_major:
            rows = _transpose_rows(rows, l, bn) if time_major else _transpose_rows(rows, bn, l)
            time_major = want_tm
        if kind == 0:
            rows, hr, hi = _s5_layer(rows, bn, s5_re[j], s5_im[j], g[0], g[1],
                                     p['s5_lambda_re'][j], p['s5_lambda_im'][j], p['s5_b_re'][j], p['s5_b_im'][j],
                                     p['s5_c_re'][j], p['s5_c_im'][j], p['s5_log_dt'][j], p['s5_d'][j],
                                     p['s5_w_glu'][j])
            new_re.append(hr)
            new_im.append(hi)
        elif kind == 1:
            rows, s = _ret_layer(rows, bn, l, time_major, pos0, ret_s[j], g[0], g[1], p['ret_wq'][j],
                                 p['ret_wk'][j], p['ret_wv'][j], p['ret_wg'][j], p['ret_wo'][j])
            new_ret.append(s)
        else:
            rows, s = _gla_layer(rows, bn, l, time_major, gla_s[j], g[0], g[1], p['gla_wq'][j], p['gla_wk'][j],
                                 p['gla_wv'][j], p['gla_wg'][j], p['gla_wa1'][j], p['gla_wa2'][j], p['gla_ba'][j],
                                 p['gla_norm_g'][j], p['gla_wo'][j])
            new_gla.append(s)
        buf = conv_buf[layer]
        if time_major:
            cin = buf.transpose(1, 0, 2).reshape(1, (CONV_W - 1) * bn, D_FF)
            rows, cout = _ffn_layer(rows, 1, bn, cin, g[2], g[3], p['ffn_w_gate'][layer], p['ffn_w_up'][layer],
                                    p['ffn_conv_w'][layer], p['ffn_conv_b'][layer], p['ffn_w_down'][layer])
            cout = cout.reshape(CONV_W - 1, bn, D_FF).transpose(1, 0, 2)
        else:
            rows, cout = _ffn_layer(rows, bn, 1, buf, g[2], g[3], p['ffn_w_gate'][layer], p['ffn_w_up'][layer],
                                    p['ffn_conv_w'][layer], p['ffn_conv_b'][layer], p['ffn_w_down'][layer])
        new_conv.append(cout)
    if time_major:
        rows = _transpose_rows(rows, l, bn)
    y = rows.reshape(bn, l, D_MODEL)
    return (y, jnp.stack(new_re), jnp.stack(new_im), jnp.stack(new_ret), jnp.stack(new_gla),
            jnp.stack(new_conv))


def kernel(x_prompt, x_sample, state_s5_re, state_s5_im, state_ret, state_gla, cache_ffn_conv,
           norm_g, s5_lambda_re, s5_lambda_im, s5_b_re, s5_b_im, s5_c_re, s5_c_im, s5_log_dt, s5_d, s5_w_glu,
           ret_wq, ret_wk, ret_wv, ret_wg, ret_wo,
           gla_wq, gla_wk, gla_wv, gla_wg, gla_wa1, gla_wa2, gla_ba, gla_norm_g, gla_wo,
           ffn_w_gate, ffn_w_up, ffn_conv_w, ffn_conv_b, ffn_w_down):
    p = dict(norm_g=norm_g, s5_lambda_re=s5_lambda_re, s5_lambda_im=s5_lambda_im, s5_b_re=s5_b_re,
             s5_b_im=s5_b_im, s5_c_re=s5_c_re, s5_c_im=s5_c_im, s5_log_dt=s5_log_dt, s5_d=s5_d,
             s5_w_glu=s5_w_glu, ret_wq=ret_wq, ret_wk=ret_wk, ret_wv=ret_wv, ret_wg=ret_wg, ret_wo=ret_wo,
             gla_wq=gla_wq, gla_wk=gla_wk, gla_wv=gla_wv, gla_wg=gla_wg, gla_wa1=gla_wa1, gla_wa2=gla_wa2,
             gla_ba=gla_ba, gla_norm_g=gla_norm_g, gla_wo=gla_wo, ffn_w_gate=ffn_w_gate, ffn_w_up=ffn_w_up,
             ffn_conv_w=ffn_conv_w, ffn_conv_b=ffn_conv_b, ffn_w_down=ffn_w_down)
    bp = x_prompt.shape[0]
    z_s5 = jnp.zeros((state_s5_re.shape[0], bp) + state_s5_re.shape[2:], F32)
    z_ret = jnp.zeros((state_ret.shape[0], bp) + state_ret.shape[2:], F32)
    z_gla = jnp.zeros((state_gla.shape[0], bp) + state_gla.shape[2:], F32)
    z_conv = jnp.zeros((cache_ffn_conv.shape[0], bp) + cache_ffn_conv.shape[2:], x_prompt.dtype)
    outs_p = _trunk(x_prompt, 0, z_s5, z_s5, z_ret, z_gla, z_conv, p, always_time_major=False)
    outs_s = _trunk(x_sample, PAST_LEN, state_s5_re, state_s5_im, state_ret, state_gla, cache_ffn_conv, p,
                    always_time_major=True)
    return (outs_p[0], outs_s[0]) + tuple(outs_p[1:]) + tuple(outs_s[1:])
```

```python
import functools
import math

import jax
import jax.numpy as jnp
from jax import lax
from jax.experimental import pallas as pl
from jax.experimental.pallas import tpu as pltpu

F32 = jnp.float32
BF16 = jnp.bfloat16

D_MODEL = 1024
PAST_LEN = 16384
N_MIXERS = 3
S5_GROUP = 16
S5_GROUPS = D_MODEL // S5_GROUP
S5_STATE = 64
S5_SUPER = 4
S5_SG_CH = D_MODEL // S5_SUPER
S5_SG_ST = (S5_GROUPS // S5_SUPER) * S5_STATE
RET_HEADS = 4
RET_DK = D_MODEL // RET_HEADS
RET_DV = 2 * D_MODEL // RET_HEADS
RET_CHUNK = 256
ROPE_BASE = 10000.0
GLA_HEADS = 4
GLA_DK = D_MODEL // 2 // GLA_HEADS
GLA_DV = D_MODEL // GLA_HEADS
GLA_GATE_RANK = 16
GLA_GATE_NORM = 16.0
GLA_CHUNK = 64
GLA_GROUP = 8
D_FF = ((8 * D_MODEL // 3 + 255) // 256) * 256
FFN_SUB = 256
FFN_ROW_TILE = 1024
FFN_VMEM_MB = 58
SEQ_GROUP_BYTES = 8 * 2 ** 20
CONV_W = 3
NORM_EPS = 1e-6
SMALL_SEQ_PAD = 16
ROW_TILE = 512
SUBLANES = 8
PROJ_ROW_TILE = 1024
VMEM_MB = 52


def _cparams(*sem, vmem_mb=None):
    return pltpu.CompilerParams(dimension_semantics=sem, vmem_limit_bytes=(vmem_mb or VMEM_MB) * 2 ** 20)


def _const_spec(shape):
    zeros = (0,) * len(shape)
    return pl.BlockSpec(shape, lambda *_: zeros, pipeline_mode=pl.Buffered(1))


def _stacked_spec(shape, index):
    where = (index,) + (0,) * len(shape)
    return pl.BlockSpec((None,) + tuple(shape), lambda *_: where, pipeline_mode=pl.Buffered(1))


def _rms(x, g):
    return x * lax.rsqrt(jnp.mean(x * x, axis=-1, keepdims=True) + NORM_EPS) * g


def _sigmoid(x):
    return 1.0 / (1.0 + jnp.exp(-x))


def _silu(x):
    return x * _sigmoid(x)


def _gelu_tanh(x):
    return 0.5 * x * (1.0 + jnp.tanh(math.sqrt(2.0 / math.pi) * (x + 0.044715 * (x * x * x))))


def _log_sigmoid(x):
    return jnp.minimum(x, 0.0) - jnp.log(1.0 + jnp.exp(-jnp.abs(x)))


def _dot(a, b):
    return jnp.dot(a, b, preferred_element_type=F32)


def _dot_nt(a, b):
    return lax.dot_general(a, b, (((1,), (1,)), ((), ())), preferred_element_type=F32)


def _dot_tn(a, b):
    return lax.dot_general(a, b, (((0,), (0,)), ((), ())), preferred_element_type=F32)


def _s5_kernel(x_ref, h0_ref, g0_ref, g1_ref, are_ref, aim_ref, bre_ref, bim_ref, cc_ref, dsk_ref, wglu_ref,
               o_ref, hout_ref, bu_sc, y_sc, bbd_ref, cbd_ref, *, bn, tt, x_bm):
    @pl.when(pl.program_id(0) == 0)
    def _():
        hout_ref[...] = h0_ref[...]
        ngl = S5_GROUPS // S5_SUPER

        def lane_tiler(k, n):
            r = lax.broadcasted_iota(jnp.int32, (k, n), 0)
            c = lax.broadcasted_iota(jnp.int32, (k, n), 1)
            return ((c & (k - 1)) == r).astype(BF16)

        tile_n = lane_tiler(S5_STATE, S5_SG_ST)
        tile_p = lane_tiler(S5_GROUP, S5_SG_CH)
        sh_p = S5_GROUP.bit_length() - 1
        sh_n = S5_STATE.bit_length() - 1
        rb = lax.broadcasted_iota(jnp.int32, (S5_SG_CH, S5_SG_ST), 0) >> sh_p
        cb = lax.broadcasted_iota(jnp.int32, (S5_SG_CH, S5_SG_ST), 1) >> sh_n
        rc = (lax.broadcasted_iota(jnp.int32, (2 * S5_SG_ST, S5_SG_CH), 0) >> sh_n) & (ngl - 1)
        cc = lax.broadcasted_iota(jnp.int32, (2 * S5_SG_ST, S5_SG_CH), 1) >> sh_p
        for sg in range(S5_SUPER):
            for part, ref in ((0, bre_ref), (1, bim_ref)):
                blk = _dot(ref[sg].astype(BF16), tile_n)
                bbd_ref[sg, :, S5_SG_ST * part:S5_SG_ST * (part + 1)] = jnp.where(rb == cb, blk, 0.0).astype(BF16)
            blk = _dot(cc_ref[sg].astype(BF16), tile_p)
            cbd_ref[sg] = jnp.where(rc == cc, blk, 0.0).astype(BF16)

    x = x_ref[...]
    if x_bm:
        x = jnp.swapaxes(x, 0, 1).reshape(tt * bn, D_MODEL)
    u = _rms(x, g0_ref[...])
    ub = u.astype(BF16)
    half = S5_SG_ST // 2
    for sg in range(S5_SUPER):
        buf = bu_sc.at[sg % 2]
        buf[...] = _dot(ub[:, S5_SG_CH * sg:S5_SG_CH * (sg + 1)], bbd_ref[sg])
        for hf in range(2):
            c_re = half * hf
            c_im = S5_SG_ST + half * hf
            a_lo = S5_SG_ST * sg + half * hf
            ar = jnp.broadcast_to(are_ref[:, a_lo:a_lo + half], (8, half))
            ai = jnp.broadcast_to(aim_ref[:, a_lo:a_lo + half], (8, half))
            h_re = 2 * S5_SG_ST * sg + c_re
            h_im = 2 * S5_SG_ST * sg + c_im
            for bt in range(bn // 8):
                r8 = 8 * bt
                hr = hout_ref[r8:r8 + 8, h_re:h_re + half]
                hi = hout_ref[r8:r8 + 8, h_im:h_im + half]
                for t in range(tt):
                    r0 = t * bn + r8
                    bur = buf[r0:r0 + 8, c_re:c_re + half]
                    bui = buf[r0:r0 + 8, c_im:c_im + half]
                    hr, hi = ar * hr - ai * hi + bur, ar * hi + ai * hr + bui
                    buf[r0:r0 + 8, c_re:c_re + half] = hr
                    buf[r0:r0 + 8, c_im:c_im + half] = hi
                hout_ref[r8:r8 + 8, h_re:h_re + half] = hr
                hout_ref[r8:r8 + 8, h_im:h_im + half] = hi
        y_sc[:, S5_SG_CH * sg:S5_SG_CH * (sg + 1)] = _dot(buf[...].astype(BF16), cbd_ref[sg])
    y = y_sc[...] + dsk_ref[...] * u
    z = _dot(_gelu_tanh(y).astype(BF16), wglu_ref[...])
    mix = z[:, :D_MODEL] * _sigmoid(z[:, D_MODEL:])
    o_ref[...] = x + _rms(mix, g1_ref[...])


def _s5_layer(x, bn, h0_re, h0_im, g0, g1, lam_re, lam_im, b_re, b_im, c_re, c_im, log_dt, d_skip, w_glu,
              j, x_bm=False):
    rows = x.shape[0]
    tt = min(rows // bn, max(1, ROW_TILE // bn))
    tile = tt * bn
    o_spec = pl.BlockSpec((tile, D_MODEL), lambda i: (i, 0))
    x_spec = o_spec
    if x_bm:
        assert bn == SUBLANES
        x = x.reshape(bn, rows // bn, D_MODEL)
        x_spec = pl.BlockSpec((bn, tt, D_MODEL), lambda i: (0, i, 0))
    ngl = S5_GROUPS // S5_SUPER
    lam = lax.complex(lam_re, lam_im)
    dt = jnp.exp(log_dt)[:, None]
    a_bar = jnp.exp(lam * dt)
    b_bar = ((a_bar - 1.0) / lam)[..., None] * lax.complex(b_re, b_im)
    bt = b_bar.transpose(0, 2, 1).reshape(S5_SUPER, S5_SG_CH, S5_STATE)
    bt_re, bt_im = jnp.real(bt), jnp.imag(bt)
    ct_re = c_re.transpose(0, 2, 1).reshape(S5_SUPER, S5_SG_ST, S5_GROUP)
    ct_im = c_im.transpose(0, 2, 1).reshape(S5_SUPER, S5_SG_ST, S5_GROUP)
    ct = jnp.concatenate([ct_re, -ct_im], axis=1)
    a_re = jnp.real(a_bar).reshape(1, S5_GROUPS * S5_STATE)
    a_im = jnp.imag(a_bar).reshape(1, S5_GROUPS * S5_STATE)
    h0 = jnp.stack([h0_re.reshape(bn, S5_SUPER, S5_SG_ST), h0_im.reshape(bn, S5_SUPER, S5_SG_ST)], axis=2)
    h0 = h0.reshape(bn, 2 * S5_GROUPS * S5_STATE)
    nst = 2 * S5_GROUPS * S5_STATE

    out, hout = pl.pallas_call(
        functools.partial(_s5_kernel, bn=bn, tt=tt, x_bm=x_bm),
        grid=(rows // tile,),
        in_specs=[
            x_spec,
            _const_spec((bn, nst)),
            _const_spec((1, D_MODEL)), _const_spec((1, D_MODEL)),
            _const_spec((1, nst // 2)), _const_spec((1, nst // 2)),
            _const_spec((S5_SUPER, S5_SG_CH, S5_STATE)), _const_spec((S5_SUPER, S5_SG_CH, S5_STATE)),
            _const_spec((S5_SUPER, 2 * S5_SG_ST, S5_GROUP)),
            _const_spec((1, D_MODEL)),
            _stacked_spec((D_MODEL, 2 * D_MODEL), j),
        ],
        out_specs=[o_spec, _const_spec((bn, nst))],
        out_shape=[jax.ShapeDtypeStruct((rows, D_MODEL), F32), jax.ShapeDtypeStruct((bn, nst), F32)],
        scratch_shapes=[pltpu.VMEM((2, tile, 2 * S5_SG_ST), F32), pltpu.VMEM((tile, D_MODEL), F32),
                        pltpu.VMEM((S5_SUPER, S5_SG_CH, 2 * S5_SG_ST), BF16),
                        pltpu.VMEM((S5_SUPER, 2 * S5_SG_ST, S5_SG_CH), BF16)],
        compiler_params=_cparams("arbitrary"),
        name="s5_layer",
    )(x, h0, g0.reshape(1, -1), g1.reshape(1, -1), a_re, a_im, bt_re, bt_im, ct, d_skip.reshape(1, -1), w_glu)
    hout = hout.reshape(bn, S5_SUPER, 2, ngl, S5_STATE)
    new_re = hout[:, :, 0].reshape(bn, S5_GROUPS, S5_STATE)
    new_im = hout[:, :, 1].reshape(bn, S5_GROUPS, S5_STATE)
    return out, new_re, new_im


def _ffn_kernel(x_ref, cin_ref, g2_ref, g3_ref, wg_ref, wu_ref, wd_ref, cw_ref, cb_ref,
                o_ref, cout_ref, hdn_sc, gext_sc, carry_sc, *, bn, tile, x_bm, out_bm):
    i = pl.program_id(0)
    halo = 2 * bn
    base = gext_sc.shape[1] - tile

    @pl.when(i == 0)
    def _():
        carry_sc[...] = cin_ref[...]

    x = x_ref[...]
    if x_bm:
        x = jnp.swapaxes(x, 0, 1).reshape(tile, D_MODEL)
    h = _rms(x, g2_ref[...]).astype(BF16)
    for j in range(D_FF // FFN_SUB):
        sl = slice(FFN_SUB * j, FFN_SUB * (j + 1))
        buf = gext_sc.at[j % 2]
        gpre = _dot(h, wg_ref[:, sl])
        buf[base - halo:base, :] = carry_sc[:, sl]
        buf[base:base + tile, :] = gpre
        carry_sc[:, sl] = gpre[tile - halo:tile, :]
        gconv = cb_ref[:, sl] + cw_ref[0:1, sl] * buf[base - halo:base - halo + tile, :]
        gconv = gconv + cw_ref[1:2, sl] * buf[base - bn:base - bn + tile, :]
        gconv = gconv + cw_ref[2:3, sl] * gpre
        hdn_sc[:, sl] = (_silu(gconv) * _dot(h, wu_ref[:, sl])).astype(BF16)
    y = x + _rms(_dot(hdn_sc[...], wd_ref[...]), g3_ref[...])
    if out_bm:
        y = jnp.swapaxes(y.reshape(tile // bn, bn, D_MODEL), 0, 1)
    o_ref[...] = y

    @pl.when(i == pl.num_programs(0) - 1)
    def _():
        cout_ref[...] = carry_sc[...]


def _ffn_layer(x, bn, cin, g2, g3, w_gate, w_up, conv_w, conv_b, w_down, layer, x_bm=False, out_bm=False):
    rows = x.shape[0]
    tile = min(FFN_ROW_TILE, rows)
    tt = tile // bn
    halo = 2 * bn
    assert not (x_bm or out_bm) or bn == SUBLANES
    tm_spec = pl.BlockSpec((tile, D_MODEL), lambda i: (i, 0))
    bm_spec = pl.BlockSpec((bn, tt, D_MODEL), lambda i: (0, i, 0))
    if x_bm:
        x = x.reshape(bn, rows // bn, D_MODEL)
    out_shape = (bn, rows // bn, D_MODEL) if out_bm else (rows, D_MODEL)
    out, cout = pl.pallas_call(
        functools.partial(_ffn_kernel, bn=bn, tile=tile, x_bm=x_bm, out_bm=out_bm),
        grid=(rows // tile,),
        in_specs=[
            bm_spec if x_bm else tm_spec,
            _const_spec((halo, D_FF)),
            _const_spec((1, D_MODEL)), _const_spec((1, D_MODEL)),
            _stacked_spec((D_MODEL, D_FF), layer), _stacked_spec((D_MODEL, D_FF), layer),
            _stacked_spec((D_FF, D_MODEL), layer),
            _const_spec((CONV_W, D_FF)), _const_spec((1, D_FF)),
        ],
        out_specs=[bm_spec if out_bm else tm_spec, _const_spec((halo, D_FF))],
        out_shape=[jax.ShapeDtypeStruct(out_shape, F32), jax.ShapeDtypeStruct((halo, D_FF), F32)],
        scratch_shapes=[pltpu.VMEM((tile, D_FF), BF16),
                        pltpu.VMEM((2, halo + tile, FFN_SUB), F32),
                        pltpu.VMEM((halo, D_FF), F32)],
        compiler_params=_cparams("arbitrary", vmem_mb=FFN_VMEM_MB),
        name="conv_ffn",
    )(x, cin, g2.reshape(1, -1), g3.reshape(1, -1), w_gate, w_up, w_down, conv_w, conv_b.reshape(1, -1))
    return out.reshape(rows, D_MODEL), cout


def _ret_proj_kernel(x_ref, cos_ref, sin_ref, g_ref, wq_ref, wk_ref, wv_ref, q_ref, k_ref, v_ref):
    h = _rms(x_ref[...], g_ref[...]).astype(BF16)
    cos = cos_ref[...]
    sin = sin_ref[...]
    hw = RET_DK // 2
    for w_ref, dst, scale in ((wq_ref, q_ref, 1.0), (wk_ref, k_ref, RET_DK ** -0.5)):
        p = _dot(h, w_ref[...])
        for hd in range(RET_HEADS):
            lo = RET_DK * hd
            t1 = p[:, lo:lo + hw]
            t2 = p[:, lo + hw:lo + 2 * hw]
            dst[:, lo:lo + hw] = ((t1 * cos - t2 * sin) * scale).astype(BF16)
            dst[:, lo + hw:lo + 2 * hw] = ((t2 * cos + t1 * sin) * scale).astype(BF16)
    v_ref[...] = _dot(h, wv_ref[...]).astype(BF16)


def _ret_chunk_kernel(q_ref, k_ref, v_ref, s0_ref, o_ref, sout_ref, *, chunk, pad, group):
    @pl.when(pl.program_id(1) == 0)
    def _():
        sout_ref[...] = s0_ref[...]

    ri = lax.broadcasted_iota(jnp.int32, (chunk, chunk), 0)
    ci = lax.broadcasted_iota(jnp.int32, (chunk, chunk), 1)
    rel = (ri - ci).astype(F32)
    causal = ri >= ci
    idx = lax.broadcasted_iota(jnp.int32, (chunk, 1), 0).astype(F32)
    for hd in range(RET_HEADS):
        lg = math.log1p(-(2.0 ** (-5.0 - hd)))
        decay = jnp.where(causal, jnp.exp(jnp.maximum(rel, 0.0) * lg), 0.0)
        q_decay = jnp.exp((idx + (1.0 - pad)) * lg)
        k_decay = jnp.exp((chunk - 1.0 - idx) * lg)
        for g in range(group):
            qh = q_ref[g, :, RET_DK * hd:RET_DK * (hd + 1)]
            kh = k_ref[g, :, RET_DK * hd:RET_DK * (hd + 1)]
            vh = v_ref[g, :, RET_DV * hd:RET_DV * (hd + 1)]
            s = sout_ref[g, hd]
            scores = _dot_nt(qh, kh) * decay
            o = _dot(scores.astype(BF16), vh) + _dot(qh, s.astype(BF16)) * q_decay
            o_ref[g, :, RET_DV * hd:RET_DV * (hd + 1)] = o
            kd = (kh.astype(F32) * k_decay).astype(BF16)
            sout_ref[g, hd] = s * math.exp((chunk - pad) * lg) + _dot_tn(kd, vh)


def _ret_out_kernel(x_ref, o_ref, g0_ref, g1_ref, wg_ref, wo_ref, y_ref, on_sc):
    x = x_ref[...]
    h = _rms(x, g0_ref[...]).astype(BF16)
    gate = _silu(_dot(h, wg_ref[...]))
    for hd in range(RET_HEADS):
        sl = slice(RET_DV * hd, RET_DV * (hd + 1))
        oh = o_ref[:, sl]
        mu = jnp.mean(oh, axis=-1, keepdims=True)
        ctr = oh - mu
        var = jnp.mean(ctr * ctr, axis=-1, keepdims=True)
        on_sc[:, sl] = (ctr * lax.rsqrt(var + NORM_EPS) * gate[:, sl]).astype(BF16)
    y_ref[...] = x + _rms(_dot(on_sc[...], wo_ref[...]), g1_ref[...])


def _row_grid_call(kernel, row_ins, const_ins, outs, rows, tile, scratch=(), name=None, tab_ins=(),
                   vmem_mb=None):
    nt = rows // tile
    in_specs = [pl.BlockSpec((tile, a.shape[1]), lambda i: (i, 0)) for a in row_ins]
    for a in tab_ins:
        ntab = a.shape[0] // tile
        in_specs.append(pl.BlockSpec((tile, a.shape[1]), lambda i, ntab=ntab: (i % ntab, 0)))
    in_specs += [_const_spec(a.shape) for a in const_ins]
    return pl.pallas_call(
        kernel,
        grid=(nt,),
        in_specs=in_specs,
        out_specs=[pl.BlockSpec((tile, n), lambda i: (i, 0)) for n, _ in outs],
        out_shape=[jax.ShapeDtypeStruct((rows, n), dt) for n, dt in outs],
        scratch_shapes=list(scratch),
        compiler_params=_cparams("arbitrary", vmem_mb=vmem_mb),
        name=name,
    )(*row_ins, *tab_ins, *const_ins)


def _to_seq(a, bn, l, time_major, pad):
    n = a.shape[1]
    a = a.reshape(l, bn, n).transpose(1, 0, 2) if time_major else a.reshape(bn, l, n)
    if pad:
        a = jnp.pad(a, ((0, 0), (pad, 0), (0, 0)))
    return a


def _from_seq(a, bn, l, time_major, pad):
    n = a.shape[2]
    a = a[:, pad:, :]
    a = a.transpose(1, 0, 2) if time_major else a
    return a.reshape(bn * l, n)


def _seq_group(bn, chunk, state_bytes):
    if chunk > SMALL_SEQ_PAD:
        return 1
    grp = max(1, min(bn, SEQ_GROUP_BYTES // state_bytes))
    while bn % grp:
        grp -= 1
    return grp


def _seq_chunking(l, chunk):
    if l % chunk == 0:
        return chunk, 0
    assert l <= SMALL_SEQ_PAD
    return SMALL_SEQ_PAD, SMALL_SEQ_PAD - l


def _ret_layer(x, bn, l, time_major, pos0, s0, g0, g1, wq, wk, wv, wg, wo):
    rows = x.shape[0]
    tile = min(ROW_TILE, rows)
    ptile = min(PROJ_ROW_TILE, rows)
    half = RET_DK // 2
    freq = 1.0 / (ROPE_BASE ** jnp.linspace(0.0, 1.0, half, dtype=F32))
    pos = jnp.arange(pos0, pos0 + l, dtype=jnp.int32).astype(F32)
    ang = pos[:, None] * freq[None, :]
    cos, sin = jnp.cos(ang), jnp.sin(ang)
    if time_major:
        cos, sin = jnp.repeat(cos, bn, axis=0), jnp.repeat(sin, bn, axis=0)
    elif l < ptile:
        cos, sin = jnp.tile(cos, (ptile // l, 1)), jnp.tile(sin, (ptile // l, 1))
    q, k, v = _row_grid_call(
        _ret_proj_kernel, [x], [g0.reshape(1, -1), wq.astype(BF16), wk.astype(BF16), wv.astype(BF16)],
        [(RET_HEADS * RET_DK, BF16), (RET_HEADS * RET_DK, BF16), (RET_HEADS * RET_DV, BF16)],
        rows, ptile, name="ret_proj", tab_ins=[cos, sin], vmem_mb=FFN_VMEM_MB)
    chunk, pad = _seq_chunking(l, RET_CHUNK)
    lp = l + pad
    q3, k3, v3 = (_to_seq(a, bn, l, time_major, pad) for a in (q, k, v))
    nc = lp // chunk
    grp = _seq_group(bn, chunk, RET_HEADS * RET_DK * RET_DV * 4)
    o3, s_new = pl.pallas_call(
        functools.partial(_ret_chunk_kernel, chunk=chunk, pad=pad, group=grp),
        grid=(bn // grp, nc),
        in_specs=[
            pl.BlockSpec((grp, chunk, RET_HEADS * RET_DK), lambda b, i: (b, i, 0)),
            pl.BlockSpec((grp, chunk, RET_HEADS * RET_DK), lambda b, i: (b, i, 0)),
            pl.BlockSpec((grp, chunk, RET_HEADS * RET_DV), lambda b, i: (b, i, 0)),
            pl.BlockSpec((grp, RET_HEADS, RET_DK, RET_DV), lambda b, i: (b, 0, 0, 0)),
        ],
        out_specs=[
            pl.BlockSpec((grp, chunk, RET_HEADS * RET_DV), lambda b, i: (b, i, 0)),
            pl.BlockSpec((grp, RET_HEADS, RET_DK, RET_DV), lambda b, i: (b, 0, 0, 0)),
        ],
        out_shape=[jax.ShapeDtypeStruct((bn, lp, RET_HEADS * RET_DV), F32),
                   jax.ShapeDtypeStruct((bn, RET_HEADS, RET_DK, RET_DV), F32)],
        compiler_params=_cparams("arbitrary", "arbitrary"),
        name="ret_chunk",
    )(q3, k3, v3, s0)
    o = _from_seq(o3, bn, l, time_major, pad)
    (y,) = _row_grid_call(
        _ret_out_kernel, [x, o], [g0.reshape(1, -1), g1.reshape(1, -1), wg.astype(BF16), wo.astype(BF16)],
        [(D_MODEL, F32)], rows, tile, scratch=[pltpu.VMEM((tile, RET_HEADS * RET_DV), BF16)], name="ret_out")
    return y, s_new


def _gla_proj_kernel(x_ref, g_ref, wq_ref, wk_ref, wv_ref, wa1_ref, wa2_ref, ba_ref,
                     q_ref, k_ref, v_ref, la_ref):
    h = _rms(x_ref[...], g_ref[...]).astype(BF16)
    q_ref[...] = (_dot(h, wq_ref[...]) * (GLA_DK ** -0.5)).astype(BF16)
    k_ref[...] = _dot(h, wk_ref[...]).astype(BF16)
    v_ref[...] = _dot(h, wv_ref[...]).astype(BF16)
    low = _dot(h, wa1_ref[...]).astype(BF16)
    logit = _dot(low, wa2_ref[...]) + ba_ref[...]
    la_ref[...] = _log_sigmoid(logit) / GLA_GATE_NORM


def _gla_chunk_kernel(q_ref, k_ref, v_ref, la_ref, s0_ref, o_ref, sout_ref, st_sc, *, chunk, group):
    i = pl.program_id(1)

    @pl.when(i == 0)
    def _():
        for g in range(group):
            for hd in range(GLA_HEADS):
                st_sc[g, hd] = s0_ref[g, hd].T

    ri = lax.broadcasted_iota(jnp.int32, (chunk, chunk), 0)
    ci = lax.broadcasted_iota(jnp.int32, (chunk, chunk), 1)
    causal = ri >= ci
    tri = causal.astype(BF16)
    mid = chunk // 2
    nk = GLA_HEADS * GLA_DK
    for g in range(group):
        la = la_ref[g]
        p1 = la.astype(BF16)
        r1 = la - p1.astype(F32)
        p2 = r1.astype(BF16)
        p3 = (r1 - p2.astype(F32)).astype(BF16)
        acc = _dot(tri, jnp.concatenate([p1, p2, p3], axis=1))
        bc_all = acc[:, :nk] + acc[:, nk:2 * nk] + acc[:, 2 * nk:]
        for hd in range(GLA_HEADS):
            ksl = slice(GLA_DK * hd, GLA_DK * (hd + 1))
            vsl = slice(GLA_DV * hd, GLA_DV * (hd + 1))
            bc = bc_all[:, ksl]
            ref = bc[mid:mid + 1, :]
            blast = bc[chunk - 1:chunk, :]
            qh = q_ref[g, :, ksl].astype(F32)
            kh = k_ref[g, :, ksl].astype(F32)
            vh = v_ref[g, :, vsl]
            st = st_sc[g, hd]
            qt = (qh * jnp.exp(bc - ref)).astype(BF16)
            kt = (kh * jnp.exp(ref - bc)).astype(BF16)
            scores = jnp.where(causal, _dot_nt(qt, kt), 0.0)
            qg = (qh * jnp.exp(bc)).astype(BF16)
            o_ref[g, :, vsl] = _dot(scores.astype(BF16), vh) + _dot_nt(qg, st.astype(BF16))
            kg = (kh * jnp.exp(blast - bc)).astype(BF16)
            st_sc[g, hd] = st * jnp.exp(blast) + _dot_tn(vh, kg)

    @pl.when(i == pl.num_programs(1) - 1)
    def _():
        for g in range(group):
            for hd in range(GLA_HEADS):
                sout_ref[g, hd] = st_sc[g, hd].T


def _gla_out_kernel(x_ref, o_ref, g0_ref, g1_ref, ng_ref, wg_ref, wo_ref, y_ref, on_sc):
    x = x_ref[...]
    h = _rms(x, g0_ref[...]).astype(BF16)
    gate = _silu(_dot(h, wg_ref[...]))
    for hd in range(GLA_HEADS):
        sl = slice(GLA_DV * hd, GLA_DV * (hd + 1))
        oh = o_ref[:, sl]
        on = oh * lax.rsqrt(jnp.mean(oh * oh, axis=-1, keepdims=True) + NORM_EPS) * ng_ref[...]
        on_sc[:, sl] = (on * gate[:, sl]).astype(BF16)
    y_ref[...] = x + _rms(_dot(on_sc[...], wo_ref[...]), g1_ref[...])


def _gla_layer(x, bn, l, time_major, s0, g0, g1, wq, wk, wv, wg, wa1, wa2, ba, norm_g, wo):
    rows = x.shape[0]
    tile = min(ROW_TILE, rows)
    lanes = 128
    wa1p = jnp.pad(wa1, ((0, 0), (0, lanes - GLA_GATE_RANK))).astype(BF16)
    wa2p = jnp.pad(wa2, ((0, lanes - GLA_GATE_RANK), (0, 0))).astype(BF16)
    q, k, v, la = _row_grid_call(
        _gla_proj_kernel, [x],
        [g0.reshape(1, -1), wq.astype(BF16), wk.astype(BF16), wv.astype(BF16), wa1p, wa2p, ba.reshape(1, -1)],
        [(GLA_HEADS * GLA_DK, BF16), (GLA_HEADS * GLA_DK, BF16), (GLA_HEADS * GLA_DV, BF16),
         (GLA_HEADS * GLA_DK, F32)],
        rows, min(PROJ_ROW_TILE, rows), name="gla_proj", vmem_mb=FFN_VMEM_MB)
    chunk, pad = _seq_chunking(l, GLA_CHUNK)
    lp = l + pad
    q3, k3, v3, la3 = (_to_seq(a, bn, l, time_major, pad) for a in (q, k, v, la))
    nc = lp // chunk
    grp = GLA_GROUP if bn % GLA_GROUP == 0 else 1
    o3, s_new = pl.pallas_call(
        functools.partial(_gla_chunk_kernel, chunk=chunk, group=grp),
        grid=(bn // grp, nc),
        in_specs=[
            pl.BlockSpec((grp, chunk, GLA_HEADS * GLA_DK), lambda b, i: (b, i, 0)),
            pl.BlockSpec((grp, chunk, GLA_HEADS * GLA_DK), lambda b, i: (b, i, 0)),
            pl.BlockSpec((grp, chunk, GLA_HEADS * GLA_DV), lambda b, i: (b, i, 0)),
            pl.BlockSpec((grp, chunk, GLA_HEADS * GLA_DK), lambda b, i: (b, i, 0)),
            pl.BlockSpec((grp, GLA_HEADS, GLA_DK, GLA_DV), lambda b, i: (b, 0, 0, 0)),
        ],
        out_specs=[
            pl.BlockSpec((grp, chunk, GLA_HEADS * GLA_DV), lambda b, i: (b, i, 0)),
            pl.BlockSpec((grp, GLA_HEADS, GLA_DK, GLA_DV), lambda b, i: (b, 0, 0, 0)),
        ],
        out_shape=[jax.ShapeDtypeStruct((bn, lp, GLA_HEADS * GLA_DV), F32),
                   jax.ShapeDtypeStruct((bn, GLA_HEADS, GLA_DK, GLA_DV), F32)],
        scratch_shapes=[pltpu.VMEM((grp, GLA_HEADS, GLA_DV, GLA_DK), F32)],
        compiler_params=_cparams("arbitrary", "arbitrary"),
        name="gla_chunk",
    )(q3, k3, v3, la3, s0)
    o = _from_seq(o3, bn, l, time_major, pad)
    (y,) = _row_grid_call(
        _gla_out_kernel, [x, o],
        [g0.reshape(1, -1), g1.reshape(1, -1), norm_g.reshape(1, -1), wg.astype(BF16), wo.astype(BF16)],
        [(D_MODEL, F32)], rows, tile, scratch=[pltpu.VMEM((tile, GLA_HEADS * GLA_DV), BF16)], name="gla_out")
    return y, s_new


def _transpose_rows(x, a, b):
    return x.reshape(a, b, x.shape[1]).transpose(1, 0, 2).reshape(a * b, x.shape[1])


def _trunk(x, pos0, s5_re, s5_im, ret_s, gla_s, conv_buf, p, batch_major_mixers):
    bn, l, _ = x.shape
    depth = p['norm_g'].shape[0]
    assert not batch_major_mixers or bn == SUBLANES
    rows = x.reshape(bn * l, D_MODEL)
    time_major = False
    if not batch_major_mixers:
        rows, time_major = _transpose_rows(rows, bn, l), True
    new_re, new_im, new_ret, new_gla, new_conv = [], [], [], [], []
    for layer in range(depth):
        g = p['norm_g'][layer]
        kind = layer % N_MIXERS
        j = layer // N_MIXERS
        if kind == 0:
            rows, hr, hi = _s5_layer(rows, bn, s5_re[j], s5_im[j], g[0], g[1],
                                     p['s5_lambda_re'][j], p['s5_lambda_im'][j], p['s5_b_re'][j], p['s5_b_im'][j],
                                     p['s5_c_re'][j], p['s5_c_im'][j], p['s5_log_dt'][j], p['s5_d'][j],
                                     p['s5_w_glu'], j, x_bm=not time_major)
            time_major = True
            new_re.append(hr)
            new_im.append(hi)
        elif kind == 1:
            rows, s = _ret_layer(rows, bn, l, time_major, pos0, ret_s[j], g[0], g[1], p['ret_wq'][j],
                                 p['ret_wk'][j], p['ret_wv'][j], p['ret_wg'][j], p['ret_wo'][j])
            new_ret.append(s)
        else:
            rows, s = _gla_layer(rows, bn, l, time_major, gla_s[j], g[0], g[1], p['gla_wq'][j], p['gla_wk'][j],
                                 p['gla_wv'][j], p['gla_wg'][j], p['gla_wa1'][j], p['gla_wa2'][j], p['gla_ba'][j],
                                 p['gla_norm_g'][j], p['gla_wo'][j])
            new_gla.append(s)
        next_is_s5 = layer + 1 < depth and (layer + 1) % N_MIXERS == 0
        out_bm = batch_major_mixers and not next_is_s5
        cin = conv_buf[layer].transpose(1, 0, 2).reshape((CONV_W - 1) * bn, D_FF)
        rows, cout = _ffn_layer(rows, bn, cin, g[2], g[3], p['ffn_w_gate'], p['ffn_w_up'],
                                p['ffn_conv_w'][layer], p['ffn_conv_b'][layer], p['ffn_w_down'], layer,
                                x_bm=not time_major, out_bm=out_bm)
        time_major = not out_bm
        new_conv.append(cout.reshape(CONV_W - 1, bn, D_FF).transpose(1, 0, 2))
    if time_major:
        rows = _transpose_rows(rows, l, bn)
    y = rows.reshape(bn, l, D_MODEL)
    return (y, jnp.stack(new_re), jnp.stack(new_im), jnp.stack(new_ret), jnp.stack(new_gla),
            jnp.stack(new_conv))


def kernel(x_prompt, x_sample, state_s5_re, state_s5_im, state_ret, state_gla, cache_ffn_conv,
           norm_g, s5_lambda_re, s5_lambda_im, s5_b_re, s5_b_im, s5_c_re, s5_c_im, s5_log_dt, s5_d, s5_w_glu,
           ret_wq, ret_wk, ret_wv, ret_wg, ret_wo,
           gla_wq, gla_wk, gla_wv, gla_wg, gla_wa1, gla_wa2, gla_ba, gla_norm_g, gla_wo,
           ffn_w_gate, ffn_w_up, ffn_conv_w, ffn_conv_b, ffn_w_down):
    p = dict(norm_g=norm_g, s5_lambda_re=s5_lambda_re, s5_lambda_im=s5_lambda_im, s5_b_re=s5_b_re,
             s5_b_im=s5_b_im, s5_c_re=s5_c_re, s5_c_im=s5_c_im, s5_log_dt=s5_log_dt, s5_d=s5_d,
             s5_w_glu=s5_w_glu.astype(BF16), ret_wq=ret_wq, ret_wk=ret_wk, ret_wv=ret_wv, ret_wg=ret_wg, ret_wo=ret_wo,
             gla_wq=gla_wq, gla_wk=gla_wk, gla_wv=gla_wv, gla_wg=gla_wg, gla_wa1=gla_wa1, gla_wa2=gla_wa2,
             gla_ba=gla_ba, gla_norm_g=gla_norm_g, gla_wo=gla_wo, ffn_w_gate=ffn_w_gate.astype(BF16),
             ffn_w_up=ffn_w_up.astype(BF16), ffn_conv_w=ffn_conv_w, ffn_conv_b=ffn_conv_b,
             ffn_w_down=ffn_w_down.astype(BF16))
    bp = x_prompt.shape[0]
    z_s5 = jnp.zeros((state_s5_re.shape[0], bp) + state_s5_re.shape[2:], F32)
    z_ret = jnp.zeros((state_ret.shape[0], bp) + state_ret.shape[2:], F32)
    z_gla = jnp.zeros((state_gla.shape[0], bp) + state_gla.shape[2:], F32)
    z_conv = jnp.zeros((cache_ffn_conv.shape[0], bp) + cache_ffn_conv.shape[2:], x_prompt.dtype)
    outs_p = _trunk(x_prompt, 0, z_s5, z_s5, z_ret, z_gla, z_conv, p, batch_major_mixers=True)
    outs_s = _trunk(x_sample, PAST_LEN, state_s5_re, state_s5_im, state_ret, state_gla, cache_ffn_conv, p,
                    batch_major_mixers=False)
    return (outs_p[0], outs_s[0]) + tuple(outs_p[1:]) + tuple(outs_s[1:])
```

```python
import functools
import math

import jax
import jax.numpy as jnp
from jax import lax
from jax.experimental import pallas as pl
from jax.experimental.pallas import tpu as pltpu

F32 = jnp.float32
BF16 = jnp.bfloat16

D_MODEL = 1024
PAST_LEN = 16384
N_MIXERS = 3
S5_GROUP = 16
S5_GROUPS = D_MODEL // S5_GROUP
S5_STATE = 64
S5_SUPER = 4
S5_SG_CH = D_MODEL // S5_SUPER
S5_SG_ST = (S5_GROUPS // S5_SUPER) * S5_STATE
RET_HEADS = 4
RET_DK = D_MODEL // RET_HEADS
RET_DV = 2 * D_MODEL // RET_HEADS
RET_CHUNK = 256
ROPE_BASE = 10000.0
GLA_HEADS = 4
GLA_DK = D_MODEL // 2 // GLA_HEADS
GLA_DV = D_MODEL // GLA_HEADS
GLA_GATE_RANK = 16
GLA_GATE_NORM = 16.0
GLA_CHUNK = 64
GLA_GROUP = 8
D_FF = ((8 * D_MODEL // 3 + 255) // 256) * 256
FFN_SUB = 256
FFN_ROW_TILE = 1024
FFN_VMEM_MB = 58
S5_BU_BUFFERS = 4
SEQ_GROUP_BYTES = 8 * 2 ** 20
CONV_W = 3
NORM_EPS = 1e-6
SMALL_SEQ_PAD = 16
ROW_TILE = 512
SUBLANES = 8
PROJ_ROW_TILE = 1024
VMEM_MB = 52


def _cparams(*sem, vmem_mb=None):
    return pltpu.CompilerParams(dimension_semantics=sem, vmem_limit_bytes=(vmem_mb or VMEM_MB) * 2 ** 20)


def _const_spec(shape):
    zeros = (0,) * len(shape)
    return pl.BlockSpec(shape, lambda *_: zeros, pipeline_mode=pl.Buffered(1))


def _stacked_spec(shape, index):
    where = (index,) + (0,) * len(shape)
    return pl.BlockSpec((None,) + tuple(shape), lambda *_: where, pipeline_mode=pl.Buffered(1))


def _rms(x, g):
    return x * lax.rsqrt(jnp.mean(x * x, axis=-1, keepdims=True) + NORM_EPS) * g


def _sigmoid(x):
    return 1.0 / (1.0 + jnp.exp(-x))


def _silu(x):
    return x * _sigmoid(x)


def _gelu_tanh(x):
    return 0.5 * x * (1.0 + jnp.tanh(math.sqrt(2.0 / math.pi) * (x + 0.044715 * (x * x * x))))


def _log_sigmoid(x):
    return jnp.minimum(x, 0.0) - jnp.log(1.0 + jnp.exp(-jnp.abs(x)))


def _dot(a, b):
    return jnp.dot(a, b, preferred_element_type=F32)


def _dot_nt(a, b):
    return lax.dot_general(a, b, (((1,), (1,)), ((), ())), preferred_element_type=F32)


def _dot_tn(a, b):
    return lax.dot_general(a, b, (((0,), (0,)), ((), ())), preferred_element_type=F32)


def _s5_kernel(x_ref, h0re_ref, h0im_ref, g0_ref, g1_ref, are_ref, aim_ref, bre_ref, bim_ref, cc_ref, dsk_ref,
               wglu_ref, o_ref, hre_ref, him_ref, bu_sc, y_sc, bbd_ref, cbd_ref, *, bn, tt, x_bm):
    @pl.when(pl.program_id(0) == 0)
    def _():
        hre_ref[...] = h0re_ref[...]
        him_ref[...] = h0im_ref[...]
        ngl = S5_GROUPS // S5_SUPER

        def lane_tiler(k, n):
            r = lax.broadcasted_iota(jnp.int32, (k, n), 0)
            c = lax.broadcasted_iota(jnp.int32, (k, n), 1)
            return ((c & (k - 1)) == r).astype(BF16)

        tile_n = lane_tiler(S5_STATE, S5_SG_ST)
        tile_p = lane_tiler(S5_GROUP, S5_SG_CH)
        sh_p = S5_GROUP.bit_length() - 1
        sh_n = S5_STATE.bit_length() - 1
        rb = lax.broadcasted_iota(jnp.int32, (S5_SG_CH, S5_SG_ST), 0) >> sh_p
        cb = lax.broadcasted_iota(jnp.int32, (S5_SG_CH, S5_SG_ST), 1) >> sh_n
        rc = (lax.broadcasted_iota(jnp.int32, (2 * S5_SG_ST, S5_SG_CH), 0) >> sh_n) & (ngl - 1)
        cc = lax.broadcasted_iota(jnp.int32, (2 * S5_SG_ST, S5_SG_CH), 1) >> sh_p
        for sg in range(S5_SUPER):
            for part, ref in ((0, bre_ref), (1, bim_ref)):
                blk = _dot(ref[sg].astype(BF16), tile_n)
                bbd_ref[sg, :, S5_SG_ST * part:S5_SG_ST * (part + 1)] = jnp.where(rb == cb, blk, 0.0).astype(BF16)
            blk = _dot(cc_ref[sg].astype(BF16), tile_p)
            cbd_ref[sg] = jnp.where(rc == cc, blk, 0.0).astype(BF16)

    x = x_ref[...]
    if x_bm:
        x = jnp.swapaxes(x, 0, 1).reshape(tt * bn, D_MODEL)
    u = _rms(x, g0_ref[...])
    ub = u.astype(BF16)
    half = S5_SG_ST // 2
    for sg in range(S5_SUPER):
        buf = bu_sc.at[sg % S5_BU_BUFFERS]
        buf[...] = _dot(ub[:, S5_SG_CH * sg:S5_SG_CH * (sg + 1)], bbd_ref[sg])
        for hf in range(2):
            c_re = half * hf
            c_im = S5_SG_ST + half * hf
            a_lo = S5_SG_ST * sg + half * hf
            ar = jnp.broadcast_to(are_ref[:, a_lo:a_lo + half], (8, half))
            ai = jnp.broadcast_to(aim_ref[:, a_lo:a_lo + half], (8, half))
            for bt in range(bn // 8):
                r8 = 8 * bt
                hr = hre_ref[r8:r8 + 8, a_lo:a_lo + half]
                hi = him_ref[r8:r8 + 8, a_lo:a_lo + half]
                for t in range(tt):
                    r0 = t * bn + r8
                    bur = buf[r0:r0 + 8, c_re:c_re + half]
                    bui = buf[r0:r0 + 8, c_im:c_im + half]
                    hr, hi = ar * hr - ai * hi + bur, ar * hi + ai * hr + bui
                    buf[r0:r0 + 8, c_re:c_re + half] = hr
                    buf[r0:r0 + 8, c_im:c_im + half] = hi
                hre_ref[r8:r8 + 8, a_lo:a_lo + half] = hr
                him_ref[r8:r8 + 8, a_lo:a_lo + half] = hi
        y_sc[:, S5_SG_CH * sg:S5_SG_CH * (sg + 1)] = _dot(buf[...].astype(BF16), cbd_ref[sg])
    y = y_sc[...] + dsk_ref[...] * u
    z = _dot(_gelu_tanh(y).astype(BF16), wglu_ref[...])
    mix = z[:, :D_MODEL] * _sigmoid(z[:, D_MODEL:])
    o_ref[...] = x + _rms(mix, g1_ref[...])


def _s5_layer(x, bn, h0_re, h0_im, g0, g1, lam_re, lam_im, b_re, b_im, c_re, c_im, log_dt, d_skip, w_glu,
              j, x_bm=False):
    rows = x.shape[0]
    tt = min(rows // bn, max(1, ROW_TILE // bn))
    tile = tt * bn
    o_spec = pl.BlockSpec((tile, D_MODEL), lambda i: (i, 0))
    x_spec = o_spec
    if x_bm:
        assert bn == SUBLANES
        x = x.reshape(bn, rows // bn, D_MODEL)
        x_spec = pl.BlockSpec((bn, tt, D_MODEL), lambda i: (0, i, 0))
    lam = lax.complex(lam_re, lam_im)
    dt = jnp.exp(log_dt)[:, None]
    a_bar = jnp.exp(lam * dt)
    b_bar = ((a_bar - 1.0) / lam)[..., None] * lax.complex(b_re, b_im)
    bt = b_bar.transpose(0, 2, 1).reshape(S5_SUPER, S5_SG_CH, S5_STATE)
    bt_re, bt_im = jnp.real(bt), jnp.imag(bt)
    ct_re = c_re.transpose(0, 2, 1).reshape(S5_SUPER, S5_SG_ST, S5_GROUP)
    ct_im = c_im.transpose(0, 2, 1).reshape(S5_SUPER, S5_SG_ST, S5_GROUP)
    ct = jnp.concatenate([ct_re, -ct_im], axis=1)
    a_re = jnp.real(a_bar).reshape(1, S5_GROUPS * S5_STATE)
    a_im = jnp.imag(a_bar).reshape(1, S5_GROUPS * S5_STATE)
    nst = S5_GROUPS * S5_STATE

    out, new_re, new_im = pl.pallas_call(
        functools.partial(_s5_kernel, bn=bn, tt=tt, x_bm=x_bm),
        grid=(rows // tile,),
        in_specs=[
            x_spec,
            _const_spec((bn, nst)), _const_spec((bn, nst)),
            _const_spec((1, D_MODEL)), _const_spec((1, D_MODEL)),
            _const_spec((1, nst)), _const_spec((1, nst)),
            _const_spec((S5_SUPER, S5_SG_CH, S5_STATE)), _const_spec((S5_SUPER, S5_SG_CH, S5_STATE)),
            _const_spec((S5_SUPER, 2 * S5_SG_ST, S5_GROUP)),
            _const_spec((1, D_MODEL)),
            _stacked_spec((D_MODEL, 2 * D_MODEL), j),
        ],
        out_specs=[o_spec, _const_spec((bn, nst)), _const_spec((bn, nst))],
        out_shape=[jax.ShapeDtypeStruct((rows, D_MODEL), F32), jax.ShapeDtypeStruct((bn, nst), F32),
                   jax.ShapeDtypeStruct((bn, nst), F32)],
        scratch_shapes=[pltpu.VMEM((S5_BU_BUFFERS, tile, 2 * S5_SG_ST), F32), pltpu.VMEM((tile, D_MODEL), F32),
                        pltpu.VMEM((S5_SUPER, S5_SG_CH, 2 * S5_SG_ST), BF16),
                        pltpu.VMEM((S5_SUPER, 2 * S5_SG_ST, S5_SG_CH), BF16)],
        compiler_params=_cparams("arbitrary"),
        name="s5_layer",
    )(x, h0_re.reshape(bn, nst), h0_im.reshape(bn, nst), g0.reshape(1, -1), g1.reshape(1, -1), a_re, a_im,
      bt_re, bt_im, ct, d_skip.reshape(1, -1), w_glu)
    return out, new_re.reshape(bn, S5_GROUPS, S5_STATE), new_im.reshape(bn, S5_GROUPS, S5_STATE)


def _ffn_kernel(x_ref, cin_ref, g2_ref, g3_ref, wg_ref, wu_ref, wd_ref, cw_ref, cb_ref,
                o_ref, cout_ref, hdn_sc, gext_sc, carry_sc, *, bn, tile, x_bm, out_bm):
    i = pl.program_id(0)
    halo = 2 * bn
    base = gext_sc.shape[1] - tile

    @pl.when(i == 0)
    def _():
        carry_sc[...] = cin_ref[...]

    x = x_ref[...]
    if x_bm:
        x = jnp.swapaxes(x, 0, 1).reshape(tile, D_MODEL)
    h = _rms(x, g2_ref[...]).astype(BF16)
    for j in range(D_FF // FFN_SUB):
        sl = slice(FFN_SUB * j, FFN_SUB * (j + 1))
        buf = gext_sc.at[j % 2]
        gpre = _dot(h, wg_ref[:, sl])
        buf[base - halo:base, :] = carry_sc[:, sl]
        buf[base:base + tile, :] = gpre
        carry_sc[:, sl] = gpre[tile - halo:tile, :]
        gconv = cb_ref[:, sl] + cw_ref[0:1, sl] * buf[base - halo:base - halo + tile, :]
        gconv = gconv + cw_ref[1:2, sl] * buf[base - bn:base - bn + tile, :]
        gconv = gconv + cw_ref[2:3, sl] * gpre
        hdn_sc[:, sl] = (_silu(gconv) * _dot(h, wu_ref[:, sl])).astype(BF16)
    y = x + _rms(_dot(hdn_sc[...], wd_ref[...]), g3_ref[...])
    if out_bm:
        y = jnp.swapaxes(y.reshape(tile // bn, bn, D_MODEL), 0, 1)
    o_ref[...] = y

    @pl.when(i == pl.num_programs(0) - 1)
    def _():
        cout_ref[...] = carry_sc[...]


def _ffn_layer(x, bn, cin, g2, g3, w_gate, w_up, conv_w, conv_b, w_down, layer, x_bm=False, out_bm=False):
    rows = x.shape[0]
    tile = min(FFN_ROW_TILE, rows)
    tt = tile // bn
    halo = 2 * bn
    assert not (x_bm or out_bm) or bn == SUBLANES
    tm_spec = pl.BlockSpec((tile, D_MODEL), lambda i: (i, 0))
    bm_spec = pl.BlockSpec((bn, tt, D_MODEL), lambda i: (0, i, 0))
    if x_bm:
        x = x.reshape(bn, rows // bn, D_MODEL)
    out_shape = (bn, rows // bn, D_MODEL) if out_bm else (rows, D_MODEL)
    out, cout = pl.pallas_call(
        functools.partial(_ffn_kernel, bn=bn, tile=tile, x_bm=x_bm, out_bm=out_bm),
        grid=(rows // tile,),
        in_specs=[
            bm_spec if x_bm else tm_spec,
            _const_spec((halo, D_FF)),
            _const_spec((1, D_MODEL)), _const_spec((1, D_MODEL)),
            _stacked_spec((D_MODEL, D_FF), layer), _stacked_spec((D_MODEL, D_FF), layer),
            _stacked_spec((D_FF, D_MODEL), layer),
            _const_spec((CONV_W, D_FF)), _const_spec((1, D_FF)),
        ],
        out_specs=[bm_spec if out_bm else tm_spec, _const_spec((halo, D_FF))],
        out_shape=[jax.ShapeDtypeStruct(out_shape, F32), jax.ShapeDtypeStruct((halo, D_FF), F32)],
        scratch_shapes=[pltpu.VMEM((tile, D_FF), BF16),
                        pltpu.VMEM((2, halo + tile, FFN_SUB), F32),
                        pltpu.VMEM((halo, D_FF), F32)],
        compiler_params=_cparams("arbitrary", vmem_mb=FFN_VMEM_MB),
        name="conv_ffn",
    )(x, cin, g2.reshape(1, -1), g3.reshape(1, -1), w_gate, w_up, w_down, conv_w, conv_b.reshape(1, -1))
    return out.reshape(rows, D_MODEL), cout


def _ret_proj_kernel(x_ref, cos_ref, sin_ref, g_ref, wq_ref, wk_ref, wv_ref, q_ref, k_ref, v_ref):
    h = _rms(x_ref[...], g_ref[...]).astype(BF16)
    cos = cos_ref[...]
    sin = sin_ref[...]
    hw = RET_DK // 2
    for w_ref, dst, scale in ((wq_ref, q_ref, 1.0), (wk_ref, k_ref, RET_DK ** -0.5)):
        p = _dot(h, w_ref[...])
        for hd in range(RET_HEADS):
            lo = RET_DK * hd
            t1 = p[:, lo:lo + hw]
            t2 = p[:, lo + hw:lo + 2 * hw]
            dst[:, lo:lo + hw] = ((t1 * cos - t2 * sin) * scale).astype(BF16)
            dst[:, lo + hw:lo + 2 * hw] = ((t2 * cos + t1 * sin) * scale).astype(BF16)
    v_ref[...] = _dot(h, wv_ref[...]).astype(BF16)


def _ret_chunk_kernel(q_ref, k_ref, v_ref, s0_ref, o_ref, sout_ref, *, chunk, pad, group):
    @pl.when(pl.program_id(1) == 0)
    def _():
        sout_ref[...] = s0_ref[...]

    ri = lax.broadcasted_iota(jnp.int32, (chunk, chunk), 0)
    ci = lax.broadcasted_iota(jnp.int32, (chunk, chunk), 1)
    rel = (ri - ci).astype(F32)
    causal = ri >= ci
    idx = lax.broadcasted_iota(jnp.int32, (chunk, 1), 0).astype(F32)
    for hd in range(RET_HEADS):
        lg = math.log1p(-(2.0 ** (-5.0 - hd)))
        decay = jnp.where(causal, jnp.exp(jnp.maximum(rel, 0.0) * lg), 0.0)
        q_decay = jnp.exp((idx + (1.0 - pad)) * lg)
        k_decay = jnp.exp((chunk - 1.0 - idx) * lg)
        for g in range(group):
            qh = q_ref[g, :, RET_DK * hd:RET_DK * (hd + 1)]
            kh = k_ref[g, :, RET_DK * hd:RET_DK * (hd + 1)]
            vh = v_ref[g, :, RET_DV * hd:RET_DV * (hd + 1)]
            s = sout_ref[g, hd]
            scores = _dot_nt(qh, kh) * decay
            o = _dot(scores.astype(BF16), vh) + _dot(qh, s.astype(BF16)) * q_decay
            o_ref[g, :, RET_DV * hd:RET_DV * (hd + 1)] = o
            kd = (kh.astype(F32) * k_decay).astype(BF16)
            sout_ref[g, hd] = s * math.exp((chunk - pad) * lg) + _dot_tn(kd, vh)


def _ret_out_kernel(x_ref, o_ref, g0_ref, g1_ref, wg_ref, wo_ref, y_ref, on_sc):
    x = x_ref[...]
    h = _rms(x, g0_ref[...]).astype(BF16)
    gate = _silu(_dot(h, wg_ref[...]))
    for hd in range(RET_HEADS):
        sl = slice(RET_DV * hd, RET_DV * (hd + 1))
        oh = o_ref[:, sl]
        mu = jnp.mean(oh, axis=-1, keepdims=True)
        ctr = oh - mu
        var = jnp.mean(ctr * ctr, axis=-1, keepdims=True)
        on_sc[:, sl] = (ctr * lax.rsqrt(var + NORM_EPS) * gate[:, sl]).astype(BF16)
    y_ref[...] = x + _rms(_dot(on_sc[...], wo_ref[...]), g1_ref[...])


def _row_grid_call(kernel, row_ins, const_ins, outs, rows, tile, scratch=(), name=None, tab_ins=(),
                   vmem_mb=None):
    nt = rows // tile
    in_specs = [pl.BlockSpec((tile, a.shape[1]), lambda i: (i, 0)) for a in row_ins]
    for a in tab_ins:
        ntab = a.shape[0] // tile
        in_specs.append(pl.BlockSpec((tile, a.shape[1]), lambda i, ntab=ntab: (i % ntab, 0)))
    in_specs += [_const_spec(a.shape) for a in const_ins]
    return pl.pallas_call(
        kernel,
        grid=(nt,),
        in_specs=in_specs,
        out_specs=[pl.BlockSpec((tile, n), lambda i: (i, 0)) for n, _ in outs],
        out_shape=[jax.ShapeDtypeStruct((rows, n), dt) for n, dt in outs],
        scratch_shapes=list(scratch),
        compiler_params=_cparams("arbitrary", vmem_mb=vmem_mb),
        name=name,
    )(*row_ins, *tab_ins, *const_ins)


def _to_seq(a, bn, l, time_major, pad):
    n = a.shape[1]
    a = a.reshape(l, bn, n).transpose(1, 0, 2) if time_major else a.reshape(bn, l, n)
    if pad:
        a = jnp.pad(a, ((0, 0), (pad, 0), (0, 0)))
    return a


def _from_seq(a, bn, l, time_major, pad):
    n = a.shape[2]
    a = a[:, pad:, :]
    a = a.transpose(1, 0, 2) if time_major else a
    return a.reshape(bn * l, n)


def _seq_group(bn, chunk, state_bytes):
    if chunk > SMALL_SEQ_PAD:
        return 1
    grp = max(1, min(bn, SEQ_GROUP_BYTES // state_bytes))
    while bn % grp:
        grp -= 1
    return grp


def _seq_chunking(l, chunk):
    if l % chunk == 0:
        return chunk, 0
    assert l <= SMALL_SEQ_PAD
    return SMALL_SEQ_PAD, SMALL_SEQ_PAD - l


def _ret_layer(x, bn, l, time_major, pos0, s0, g0, g1, wq, wk, wv, wg, wo):
    rows = x.shape[0]
    tile = min(ROW_TILE, rows)
    ptile = min(PROJ_ROW_TILE, rows)
    half = RET_DK // 2
    freq = 1.0 / (ROPE_BASE ** jnp.linspace(0.0, 1.0, half, dtype=F32))
    pos = jnp.arange(pos0, pos0 + l, dtype=jnp.int32).astype(F32)
    ang = pos[:, None] * freq[None, :]
    cos, sin = jnp.cos(ang), jnp.sin(ang)
    if time_major:
        cos, sin = jnp.repeat(cos, bn, axis=0), jnp.repeat(sin, bn, axis=0)
    elif l < ptile:
        cos, sin = jnp.tile(cos, (ptile // l, 1)), jnp.tile(sin, (ptile // l, 1))
    q, k, v = _row_grid_call(
        _ret_proj_kernel, [x], [g0.reshape(1, -1), wq.astype(BF16), wk.astype(BF16), wv.astype(BF16)],
        [(RET_HEADS * RET_DK, BF16), (RET_HEADS * RET_DK, BF16), (RET_HEADS * RET_DV, BF16)],
        rows, ptile, name="ret_proj", tab_ins=[cos, sin], vmem_mb=FFN_VMEM_MB)
    chunk, pad = _seq_chunking(l, RET_CHUNK)
    lp = l + pad
    q3, k3, v3 = (_to_seq(a, bn, l, time_major, pad) for a in (q, k, v))
    nc = lp // chunk
    grp = _seq_group(bn, chunk, RET_HEADS * RET_DK * RET_DV * 4)
    o3, s_new = pl.pallas_call(
        functools.partial(_ret_chunk_kernel, chunk=chunk, pad=pad, group=grp),
        grid=(bn // grp, nc),
        in_specs=[
            pl.BlockSpec((grp, chunk, RET_HEADS * RET_DK), lambda b, i: (b, i, 0)),
            pl.BlockSpec((grp, chunk, RET_HEADS * RET_DK), lambda b, i: (b, i, 0)),
            pl.BlockSpec((grp, chunk, RET_HEADS * RET_DV), lambda b, i: (b, i, 0)),
            pl.BlockSpec((grp, RET_HEADS, RET_DK, RET_DV), lambda b, i: (b, 0, 0, 0)),
        ],
        out_specs=[
            pl.BlockSpec((grp, chunk, RET_HEADS * RET_DV), lambda b, i: (b, i, 0)),
            pl.BlockSpec((grp, RET_HEADS, RET_DK, RET_DV), lambda b, i: (b, 0, 0, 0)),
        ],
        out_shape=[jax.ShapeDtypeStruct((bn, lp, RET_HEADS * RET_DV), F32),
                   jax.ShapeDtypeStruct((bn, RET_HEADS, RET_DK, RET_DV), F32)],
        compiler_params=_cparams("arbitrary", "arbitrary"),
        name="ret_chunk",
    )(q3, k3, v3, s0)
    o = _from_seq(o3, bn, l, time_major, pad)
    (y,) = _row_grid_call(
        _ret_out_kernel, [x, o], [g0.reshape(1, -1), g1.reshape(1, -1), wg.astype(BF16), wo.astype(BF16)],
        [(D_MODEL, F32)], rows, tile, scratch=[pltpu.VMEM((tile, RET_HEADS * RET_DV), BF16)], name="ret_out")
    return y, s_new


def _gla_proj_kernel(x_ref, g_ref, wq_ref, wk_ref, wv_ref, wa1_ref, wa2_ref, ba_ref,
                     q_ref, k_ref, v_ref, la_ref):
    h = _rms(x_ref[...], g_ref[...]).astype(BF16)
    q_ref[...] = (_dot(h, wq_ref[...]) * (GLA_DK ** -0.5)).astype(BF16)
    k_ref[...] = _dot(h, wk_ref[...]).astype(BF16)
    v_ref[...] = _dot(h, wv_ref[...]).astype(BF16)
    low = _dot(h, wa1_ref[...]).astype(BF16)
    logit = _dot(low, wa2_ref[...]) + ba_ref[...]
    la_ref[...] = _log_sigmoid(logit) / GLA_GATE_NORM


def _gla_chunk_kernel(q_ref, k_ref, v_ref, la_ref, s0_ref, o_ref, sout_ref, st_sc, *, chunk, group):
    i = pl.program_id(1)

    @pl.when(i == 0)
    def _():
        for g in range(group):
            for hd in range(GLA_HEADS):
                st_sc[g, hd] = s0_ref[g, hd]

    ri = lax.broadcasted_iota(jnp.int32, (chunk, chunk), 0)
    ci = lax.broadcasted_iota(jnp.int32, (chunk, chunk), 1)
    causal = ri >= ci
    tri = causal.astype(BF16)
    mid = chunk // 2
    nk = GLA_HEADS * GLA_DK
    units = [(g, hd) for g in range(group) for hd in range(GLA_HEADS)]
    bcs = []
    for g in range(group):
        la = la_ref[g]
        p1 = la.astype(BF16)
        r1 = la - p1.astype(F32)
        p2 = r1.astype(BF16)
        p3 = (r1 - p2.astype(F32)).astype(BF16)
        acc = _dot(tri, jnp.concatenate([p1, p2, p3], axis=1))
        bcs.append(acc[:, :nk] + acc[:, nk:2 * nk] + acc[:, 2 * nk:])
    prep = []
    for g, hd in units:
        ksl = slice(GLA_DK * hd, GLA_DK * (hd + 1))
        bc = bcs[g][:, ksl]
        ref = bc[mid:mid + 1, :]
        blast = bc[chunk - 1:chunk, :]
        qh = q_ref[g, :, ksl].astype(F32)
        kh = k_ref[g, :, ksl].astype(F32)
        qt = (qh * jnp.exp(bc - ref)).astype(BF16)
        kt = (kh * jnp.exp(ref - bc)).astype(BF16)
        qg = (qh * jnp.exp(bc)).astype(BF16)
        kg = (kh * jnp.exp(blast - bc)).astype(BF16)
        row_decay = jnp.broadcast_to(jnp.exp(blast), (SUBLANES, GLA_DK)).T[:, :1]
        prep.append((qt, kt, qg, kg, row_decay))
    scores = [jnp.where(causal, _dot_nt(qt, kt), 0.0).astype(BF16) for qt, kt, _, _, _ in prep]
    for (g, hd), sc, (_, _, qg, kg, row_decay) in zip(units, scores, prep):
        vsl = slice(GLA_DV * hd, GLA_DV * (hd + 1))
        vh = v_ref[g, :, vsl]
        st = st_sc[g, hd]
        o_ref[g, :, vsl] = _dot(sc, vh) + _dot(qg, st.astype(BF16))
        st_sc[g, hd] = st * row_decay + _dot_tn(kg, vh)

    @pl.when(i == pl.num_programs(1) - 1)
    def _():
        for g in range(group):
            for hd in range(GLA_HEADS):
                sout_ref[g, hd] = st_sc[g, hd]


def _gla_out_kernel(x_ref, o_ref, g0_ref, g1_ref, ng_ref, wg_ref, wo_ref, y_ref, on_sc):
    x = x_ref[...]
    h = _rms(x, g0_ref[...]).astype(BF16)
    gate = _silu(_dot(h, wg_ref[...]))
    for hd in range(GLA_HEADS):
        sl = slice(GLA_DV * hd, GLA_DV * (hd + 1))
        oh = o_ref[:, sl]
        on = oh * lax.rsqrt(jnp.mean(oh * oh, axis=-1, keepdims=True) + NORM_EPS) * ng_ref[...]
        on_sc[:, sl] = (on * gate[:, sl]).astype(BF16)
    y_ref[...] = x + _rms(_dot(on_sc[...], wo_ref[...]), g1_ref[...])


def _gla_layer(x, bn, l, time_major, s0, g0, g1, wq, wk, wv, wg, wa1, wa2, ba, norm_g, wo):
    rows = x.shape[0]
    tile = min(PROJ_ROW_TILE, rows)
    lanes = 128
    wa1p = jnp.pad(wa1, ((0, 0), (0, lanes - GLA_GATE_RANK))).astype(BF16)
    wa2p = jnp.pad(wa2, ((0, lanes - GLA_GATE_RANK), (0, 0))).astype(BF16)
    q, k, v, la = _row_grid_call(
        _gla_proj_kernel, [x],
        [g0.reshape(1, -1), wq.astype(BF16), wk.astype(BF16), wv.astype(BF16), wa1p, wa2p, ba.reshape(1, -1)],
        [(GLA_HEADS * GLA_DK, BF16), (GLA_HEADS * GLA_DK, BF16), (GLA_HEADS * GLA_DV, BF16),
         (GLA_HEADS * GLA_DK, F32)],
        rows, tile, name="gla_proj", vmem_mb=FFN_VMEM_MB)
    chunk, pad = _seq_chunking(l, GLA_CHUNK)
    lp = l + pad
    q3, k3, v3, la3 = (_to_seq(a, bn, l, time_major, pad) for a in (q, k, v, la))
    nc = lp // chunk
    grp = GLA_GROUP if bn % GLA_GROUP == 0 else 1
    o3, s_new = pl.pallas_call(
        functools.partial(_gla_chunk_kernel, chunk=chunk, group=grp),
        grid=(bn // grp, nc),
        in_specs=[
            pl.BlockSpec((grp, chunk, GLA_HEADS * GLA_DK), lambda b, i: (b, i, 0)),
            pl.BlockSpec((grp, chunk, GLA_HEADS * GLA_DK), lambda b, i: (b, i, 0)),
            pl.BlockSpec((grp, chunk, GLA_HEADS * GLA_DV), lambda b, i: (b, i, 0)),
            pl.BlockSpec((grp, chunk, GLA_HEADS * GLA_DK), lambda b, i: (b, i, 0)),
            pl.BlockSpec((grp, GLA_HEADS, GLA_DK, GLA_DV), lambda b, i: (b, 0, 0, 0)),
        ],
        out_specs=[
            pl.BlockSpec((grp, chunk, GLA_HEADS * GLA_DV), lambda b, i: (b, i, 0)),
            pl.BlockSpec((grp, GLA_HEADS, GLA_DK, GLA_DV), lambda b, i: (b, 0, 0, 0)),
        ],
        out_shape=[jax.ShapeDtypeStruct((bn, lp, GLA_HEADS * GLA_DV), F32),
                   jax.ShapeDtypeStruct((bn, GLA_HEADS, GLA_DK, GLA_DV), F32)],
        scratch_shapes=[pltpu.VMEM((grp, GLA_HEADS, GLA_DK, GLA_DV), F32)],
        compiler_params=_cparams("arbitrary", "arbitrary"),
        name="gla_chunk",
    )(q3, k3, v3, la3, s0)
    o = _from_seq(o3, bn, l, time_major, pad)
    (y,) = _row_grid_call(
        _gla_out_kernel, [x, o],
        [g0.reshape(1, -1), g1.reshape(1, -1), norm_g.reshape(1, -1), wg.astype(BF16), wo.astype(BF16)],
        [(D_MODEL, F32)], rows, tile, scratch=[pltpu.VMEM((tile, GLA_HEADS * GLA_DV), BF16)], name="gla_out",
        vmem_mb=FFN_VMEM_MB)
    return y, s_new


def _transpose_rows(x, a, b):
    return x.reshape(a, b, x.shape[1]).transpose(1, 0, 2).reshape(a * b, x.shape[1])


def _trunk(x, pos0, s5_re, s5_im, ret_s, gla_s, conv_buf, p, batch_major_mixers):
    bn, l, _ = x.shape
    depth = p['norm_g'].shape[0]
    assert not batch_major_mixers or bn == SUBLANES
    rows = x.reshape(bn * l, D_MODEL)
    time_major = False
    if not batch_major_mixers:
        rows, time_major = _transpose_rows(rows, bn, l), True
    new_re, new_im, new_ret, new_gla, new_conv = [], [], [], [], []
    for layer in range(depth):
        g = p['norm_g'][layer]
        kind = layer % N_MIXERS
        j = layer // N_MIXERS
        if kind == 0:
            rows, hr, hi = _s5_layer(rows, bn, s5_re[j], s5_im[j], g[0], g[1],
                                     p['s5_lambda_re'][j], p['s5_lambda_im'][j], p['s5_b_re'][j], p['s5_b_im'][j],
                                     p['s5_c_re'][j], p['s5_c_im'][j], p['s5_log_dt'][j], p['s5_d'][j],
                                     p['s5_w_glu'], j, x_bm=not time_major)
            time_major = True
            new_re.append(hr)
            new_im.append(hi)
        elif kind == 1:
            rows, s = _ret_layer(rows, bn, l, time_major, pos0, ret_s[j], g[0], g[1], p['ret_wq'][j],
                                 p['ret_wk'][j], p['ret_wv'][j], p['ret_wg'][j], p['ret_wo'][j])
            new_ret.append(s)
        else:
            rows, s = _gla_layer(rows, bn, l, time_major, gla_s[j], g[0], g[1], p['gla_wq'][j], p['gla_wk'][j],
                                 p['gla_wv'][j], p['gla_wg'][j], p['gla_wa1'][j], p['gla_wa2'][j], p['gla_ba'][j],
                                 p['gla_norm_g'][j], p['gla_wo'][j])
            new_gla.append(s)
        next_is_s5 = layer + 1 < depth and (layer + 1) % N_MIXERS == 0
        out_bm = batch_major_mixers and not next_is_s5
        cin = conv_buf[layer].transpose(1, 0, 2).reshape((CONV_W - 1) * bn, D_FF)
        rows, cout = _ffn_layer(rows, bn, cin, g[2], g[3], p['ffn_w_gate'], p['ffn_w_up'],
                                p['ffn_conv_w'][layer], p['ffn_conv_b'][layer], p['ffn_w_down'], layer,
                                x_bm=not time_major, out_bm=out_bm)
        time_major = not out_bm
        new_conv.append(cout.reshape(CONV_W - 1, bn, D_FF).transpose(1, 0, 2))
    if time_major:
        rows = _transpose_rows(rows, l, bn)
    y = rows.reshape(bn, l, D_MODEL)
    return (y, jnp.stack(new_re), jnp.stack(new_im), jnp.stack(new_ret), jnp.stack(new_gla),
            jnp.stack(new_conv))


def kernel(x_prompt, x_sample, state_s5_re, state_s5_im, state_ret, state_gla, cache_ffn_conv,
           norm_g, s5_lambda_re, s5_lambda_im, s5_b_re, s5_b_im, s5_c_re, s5_c_im, s5_log_dt, s5_d, s5_w_glu,
           ret_wq, ret_wk, ret_wv, ret_wg, ret_wo,
           gla_wq, gla_wk, gla_wv, gla_wg, gla_wa1, gla_wa2, gla_ba, gla_norm_g, gla_wo,
           ffn_w_gate, ffn_w_up, ffn_conv_w, ffn_conv_b, ffn_w_down):
    p = dict(norm_g=norm_g, s5_lambda_re=s5_lambda_re, s5_lambda_im=s5_lambda_im, s5_b_re=s5_b_re,
             s5_b_im=s5_b_im, s5_c_re=s5_c_re, s5_c_im=s5_c_im, s5_log_dt=s5_log_dt, s5_d=s5_d,
             s5_w_glu=s5_w_glu.astype(BF16), ret_wq=ret_wq, ret_wk=ret_wk, ret_wv=ret_wv, ret_wg=ret_wg, ret_wo=ret_wo,
             gla_wq=gla_wq, gla_wk=gla_wk, gla_wv=gla_wv, gla_wg=gla_wg, gla_wa1=gla_wa1, gla_wa2=gla_wa2,
             gla_ba=gla_ba, gla_norm_g=gla_norm_g, gla_wo=gla_wo, ffn_w_gate=ffn_w_gate.astype(BF16),
             ffn_w_up=ffn_w_up.astype(BF16), ffn_conv_w=ffn_conv_w, ffn_conv_b=ffn_conv_b,
             ffn_w_down=ffn_w_down.astype(BF16))
    bp = x_prompt.shape[0]
    z_s5 = jnp.zeros((state_s5_re.shape[0], bp) + state_s5_re.shape[2:], F32)
    z_ret = jnp.zeros((state_ret.shape[0], bp) + state_ret.shape[2:], F32)
    z_gla = jnp.zeros((state_gla.shape[0], bp) + state_gla.shape[2:], F32)
    z_conv = jnp.zeros((cache_ffn_conv.shape[0], bp) + cache_ffn_conv.shape[2:], x_prompt.dtype)
    outs_p = _trunk(x_prompt, 0, z_s5, z_s5, z_ret, z_gla, z_conv, p, batch_major_mixers=True)
    outs_s = _trunk(x_sample, PAST_LEN, state_s5_re, state_s5_im, state_ret, state_gla, cache_ffn_conv, p,
                    batch_major_mixers=False)
    return (outs_p[0], outs_s[0]) + tuple(outs_p[1:]) + tuple(outs_s[1:])
```

```python
import functools
import math

import jax
import jax.numpy as jnp
from jax import lax
from jax.experimental import pallas as pl
from jax.experimental.pallas import tpu as pltpu

F32 = jnp.float32
BF16 = jnp.bfloat16

D_MODEL = 1024
PAST_LEN = 16384
N_MIXERS = 3
S5_GROUP = 16
S5_GROUPS = D_MODEL // S5_GROUP
S5_STATE = 64
S5_SUPER = 4
S5_SG_CH = D_MODEL // S5_SUPER
S5_SG_ST = (S5_GROUPS // S5_SUPER) * S5_STATE
RET_HEADS = 4
RET_DK = D_MODEL // RET_HEADS
RET_DV = 2 * D_MODEL // RET_HEADS
RET_CHUNK = 256
ROPE_BASE = 10000.0
GLA_HEADS = 4
GLA_DK = D_MODEL // 2 // GLA_HEADS
GLA_DV = D_MODEL // GLA_HEADS
GLA_GATE_RANK = 16
GLA_GATE_NORM = 16.0
GLA_CHUNK = 64
GLA_GROUP = 8
D_FF = ((8 * D_MODEL // 3 + 255) // 256) * 256
FFN_SUB = 256
FFN_ROW_TILE = 1024
FFN_VMEM_MB = 58
S5_BU_BUFFERS = 4
LONG_CHUNK_GROUP = 2
SEQ_GROUP_BYTES = 8 * 2 ** 20
CONV_W = 3
NORM_EPS = 1e-6
SMALL_SEQ_PAD = 16
ROW_TILE = 512
SUBLANES = 8
PROJ_ROW_TILE = 1024
VMEM_MB = 52


def _cparams(*sem, vmem_mb=None):
    return pltpu.CompilerParams(dimension_semantics=sem, vmem_limit_bytes=(vmem_mb or VMEM_MB) * 2 ** 20)


def _const_spec(shape):
    zeros = (0,) * len(shape)
    return pl.BlockSpec(shape, lambda *_: zeros, pipeline_mode=pl.Buffered(1))


def _stacked_spec(shape, index):
    where = (index,) + (0,) * len(shape)
    return pl.BlockSpec((None,) + tuple(shape), lambda *_: where, pipeline_mode=pl.Buffered(1))


def _rms(x, g):
    return x * lax.rsqrt(jnp.mean(x * x, axis=-1, keepdims=True) + NORM_EPS) * g


def _sigmoid(x):
    return 1.0 / (1.0 + jnp.exp(-x))


def _silu(x):
    return x * _sigmoid(x)


def _gelu_tanh(x):
    return 0.5 * x * (1.0 + jnp.tanh(math.sqrt(2.0 / math.pi) * (x + 0.044715 * (x * x * x))))


def _log_sigmoid(x):
    return jnp.minimum(x, 0.0) - jnp.log(1.0 + jnp.exp(-jnp.abs(x)))


def _dot(a, b):
    return jnp.dot(a, b, preferred_element_type=F32)


def _dot_nt(a, b):
    return lax.dot_general(a, b, (((1,), (1,)), ((), ())), preferred_element_type=F32)


def _dot_tn(a, b):
    return lax.dot_general(a, b, (((0,), (0,)), ((), ())), preferred_element_type=F32)


def _s5_kernel(x_ref, h0re_ref, h0im_ref, g0_ref, g1_ref, are_ref, aim_ref, bre_ref, bim_ref, cc_ref, dsk_ref,
               wglu_ref, o_ref, hre_ref, him_ref, bu_sc, y_sc, bbd_ref, cbd_ref, *, bn, tt, x_bm):
    @pl.when(pl.program_id(0) == 0)
    def _():
        hre_ref[...] = h0re_ref[...]
        him_ref[...] = h0im_ref[...]
        ngl = S5_GROUPS // S5_SUPER

        def lane_tiler(k, n):
            r = lax.broadcasted_iota(jnp.int32, (k, n), 0)
            c = lax.broadcasted_iota(jnp.int32, (k, n), 1)
            return ((c & (k - 1)) == r).astype(BF16)

        tile_n = lane_tiler(S5_STATE, S5_SG_ST)
        tile_p = lane_tiler(S5_GROUP, S5_SG_CH)
        sh_p = S5_GROUP.bit_length() - 1
        sh_n = S5_STATE.bit_length() - 1
        rb = lax.broadcasted_iota(jnp.int32, (S5_SG_CH, S5_SG_ST), 0) >> sh_p
        cb = lax.broadcasted_iota(jnp.int32, (S5_SG_CH, S5_SG_ST), 1) >> sh_n
        rc = (lax.broadcasted_iota(jnp.int32, (2 * S5_SG_ST, S5_SG_CH), 0) >> sh_n) & (ngl - 1)
        cc = lax.broadcasted_iota(jnp.int32, (2 * S5_SG_ST, S5_SG_CH), 1) >> sh_p
        for sg in range(S5_SUPER):
            for part, ref in ((0, bre_ref), (1, bim_ref)):
                blk = _dot(ref[sg].astype(BF16), tile_n)
                bbd_ref[sg, :, S5_SG_ST * part:S5_SG_ST * (part + 1)] = jnp.where(rb == cb, blk, 0.0).astype(BF16)
            blk = _dot(cc_ref[sg].astype(BF16), tile_p)
            cbd_ref[sg] = jnp.where(rc == cc, blk, 0.0).astype(BF16)

    x = x_ref[...]
    if x_bm:
        x = jnp.swapaxes(x, 0, 1).reshape(tt * bn, D_MODEL)
    u = _rms(x, g0_ref[...])
    ub = u.astype(BF16)
    half = S5_SG_ST // 2
    for sg in range(S5_SUPER):
        buf = bu_sc.at[sg % S5_BU_BUFFERS]
        buf[...] = _dot(ub[:, S5_SG_CH * sg:S5_SG_CH * (sg + 1)], bbd_ref[sg])
        for hf in range(2):
            c_re = half * hf
            c_im = S5_SG_ST + half * hf
            a_lo = S5_SG_ST * sg + half * hf
            ar = jnp.broadcast_to(are_ref[:, a_lo:a_lo + half], (8, half))
            ai = jnp.broadcast_to(aim_ref[:, a_lo:a_lo + half], (8, half))
            for bt in range(bn // 8):
                r8 = 8 * bt
                hr = hre_ref[r8:r8 + 8, a_lo:a_lo + half]
                hi = him_ref[r8:r8 + 8, a_lo:a_lo + half]
                for t in range(tt):
                    r0 = t * bn + r8
                    bur = buf[r0:r0 + 8, c_re:c_re + half]
                    bui = buf[r0:r0 + 8, c_im:c_im + half]
                    hr, hi = ar * hr - ai * hi + bur, ar * hi + ai * hr + bui
                    buf[r0:r0 + 8, c_re:c_re + half] = hr
                    buf[r0:r0 + 8, c_im:c_im + half] = hi
                hre_ref[r8:r8 + 8, a_lo:a_lo + half] = hr
                him_ref[r8:r8 + 8, a_lo:a_lo + half] = hi
        y_sc[:, S5_SG_CH * sg:S5_SG_CH * (sg + 1)] = _dot(buf[...].astype(BF16), cbd_ref[sg])
    y = y_sc[...] + dsk_ref[...] * u
    z = _dot(_gelu_tanh(y).astype(BF16), wglu_ref[...])
    mix = z[:, :D_MODEL] * _sigmoid(z[:, D_MODEL:])
    o_ref[...] = x + _rms(mix, g1_ref[...])


def _s5_layer(x, bn, h0_re, h0_im, g0, g1, lam_re, lam_im, b_re, b_im, c_re, c_im, log_dt, d_skip, w_glu,
              j, x_bm=False):
    rows = x.shape[0]
    tt = min(rows // bn, max(1, ROW_TILE // bn))
    tile = tt * bn
    o_spec = pl.BlockSpec((tile, D_MODEL), lambda i: (i, 0))
    x_spec = o_spec
    if x_bm:
        assert bn == SUBLANES
        x = x.reshape(bn, rows // bn, D_MODEL)
        x_spec = pl.BlockSpec((bn, tt, D_MODEL), lambda i: (0, i, 0))
    lam = lax.complex(lam_re, lam_im)
    dt = jnp.exp(log_dt)[:, None]
    a_bar = jnp.exp(lam * dt)
    b_bar = ((a_bar - 1.0) / lam)[..., None] * lax.complex(b_re, b_im)
    bt = b_bar.transpose(0, 2, 1).reshape(S5_SUPER, S5_SG_CH, S5_STATE)
    bt_re, bt_im = jnp.real(bt), jnp.imag(bt)
    ct_re = c_re.transpose(0, 2, 1).reshape(S5_SUPER, S5_SG_ST, S5_GROUP)
    ct_im = c_im.transpose(0, 2, 1).reshape(S5_SUPER, S5_SG_ST, S5_GROUP)
    ct = jnp.concatenate([ct_re, -ct_im], axis=1)
    a_re = jnp.real(a_bar).reshape(1, S5_GROUPS * S5_STATE)
    a_im = jnp.imag(a_bar).reshape(1, S5_GROUPS * S5_STATE)
    nst = S5_GROUPS * S5_STATE

    out, new_re, new_im = pl.pallas_call(
        functools.partial(_s5_kernel, bn=bn, tt=tt, x_bm=x_bm),
        grid=(rows // tile,),
        in_specs=[
            x_spec,
            _const_spec((bn, nst)), _const_spec((bn, nst)),
            _const_spec((1, D_MODEL)), _const_spec((1, D_MODEL)),
            _const_spec((1, nst)), _const_spec((1, nst)),
            _const_spec((S5_SUPER, S5_SG_CH, S5_STATE)), _const_spec((S5_SUPER, S5_SG_CH, S5_STATE)),
            _const_spec((S5_SUPER, 2 * S5_SG_ST, S5_GROUP)),
            _const_spec((1, D_MODEL)),
            _stacked_spec((D_MODEL, 2 * D_MODEL), j),
        ],
        out_specs=[o_spec, _const_spec((bn, nst)), _const_spec((bn, nst))],
        out_shape=[jax.ShapeDtypeStruct((rows, D_MODEL), F32), jax.ShapeDtypeStruct((bn, nst), F32),
                   jax.ShapeDtypeStruct((bn, nst), F32)],
        scratch_shapes=[pltpu.VMEM((S5_BU_BUFFERS, tile, 2 * S5_SG_ST), F32), pltpu.VMEM((tile, D_MODEL), F32),
                        pltpu.VMEM((S5_SUPER, S5_SG_CH, 2 * S5_SG_ST), BF16),
                        pltpu.VMEM((S5_SUPER, 2 * S5_SG_ST, S5_SG_CH), BF16)],
        compiler_params=_cparams("arbitrary"),
        name="s5_layer",
    )(x, h0_re.reshape(bn, nst), h0_im.reshape(bn, nst), g0.reshape(1, -1), g1.reshape(1, -1), a_re, a_im,
      bt_re, bt_im, ct, d_skip.reshape(1, -1), w_glu)
    return out, new_re.reshape(bn, S5_GROUPS, S5_STATE), new_im.reshape(bn, S5_GROUPS, S5_STATE)


def _ffn_kernel(x_ref, cin_ref, g2_ref, g3_ref, wg_ref, wu_ref, wd_ref, cw_ref, cb_ref,
                o_ref, cout_ref, hdn_sc, gext_sc, carry_sc, *, bn, tile, x_bm, out_bm):
    i = pl.program_id(0)
    halo = 2 * bn
    base = gext_sc.shape[1] - tile

    @pl.when(i == 0)
    def _():
        carry_sc[...] = cin_ref[...]

    x = x_ref[...]
    if x_bm:
        x = jnp.swapaxes(x, 0, 1).reshape(tile, D_MODEL)
    h = _rms(x, g2_ref[...]).astype(BF16)
    for j in range(D_FF // FFN_SUB):
        sl = slice(FFN_SUB * j, FFN_SUB * (j + 1))
        buf = gext_sc.at[j % 2]
        gpre = _dot(h, wg_ref[:, sl])
        buf[base - halo:base, :] = carry_sc[:, sl]
        buf[base:base + tile, :] = gpre
        carry_sc[:, sl] = gpre[tile - halo:tile, :]
        gconv = cb_ref[:, sl] + cw_ref[0:1, sl] * buf[base - halo:base - halo + tile, :]
        gconv = gconv + cw_ref[1:2, sl] * buf[base - bn:base - bn + tile, :]
        gconv = gconv + cw_ref[2:3, sl] * gpre
        hdn_sc[:, sl] = (_silu(gconv) * _dot(h, wu_ref[:, sl])).astype(BF16)
    y = x + _rms(_dot(hdn_sc[...], wd_ref[...]), g3_ref[...])
    if out_bm:
        y = jnp.swapaxes(y.reshape(tile // bn, bn, D_MODEL), 0, 1)
    o_ref[...] = y

    @pl.when(i == pl.num_programs(0) - 1)
    def _():
        cout_ref[...] = carry_sc[...]


def _ffn_layer(x, bn, cin, g2, g3, w_gate, w_up, conv_w, conv_b, w_down, layer, x_bm=False, out_bm=False):
    rows = x.shape[0]
    tile = min(FFN_ROW_TILE, rows)
    tt = tile // bn
    halo = 2 * bn
    assert not (x_bm or out_bm) or bn == SUBLANES
    tm_spec = pl.BlockSpec((tile, D_MODEL), lambda i: (i, 0))
    bm_spec = pl.BlockSpec((bn, tt, D_MODEL), lambda i: (0, i, 0))
    if x_bm:
        x = x.reshape(bn, rows // bn, D_MODEL)
    out_shape = (bn, rows // bn, D_MODEL) if out_bm else (rows, D_MODEL)
    out, cout = pl.pallas_call(
        functools.partial(_ffn_kernel, bn=bn, tile=tile, x_bm=x_bm, out_bm=out_bm),
        grid=(rows // tile,),
        in_specs=[
            bm_spec if x_bm else tm_spec,
            _const_spec((halo, D_FF)),
            _const_spec((1, D_MODEL)), _const_spec((1, D_MODEL)),
            _stacked_spec((D_MODEL, D_FF), layer), _stacked_spec((D_MODEL, D_FF), layer),
            _stacked_spec((D_FF, D_MODEL), layer),
            _const_spec((CONV_W, D_FF)), _const_spec((1, D_FF)),
        ],
        out_specs=[bm_spec if out_bm else tm_spec, _const_spec((halo, D_FF))],
        out_shape=[jax.ShapeDtypeStruct(out_shape, F32), jax.ShapeDtypeStruct((halo, D_FF), F32)],
        scratch_shapes=[pltpu.VMEM((tile, D_FF), BF16),
                        pltpu.VMEM((2, halo + tile, FFN_SUB), F32),
                        pltpu.VMEM((halo, D_FF), F32)],
        compiler_params=_cparams("arbitrary", vmem_mb=FFN_VMEM_MB),
        name="conv_ffn",
    )(x, cin, g2.reshape(1, -1), g3.reshape(1, -1), w_gate, w_up, w_down, conv_w, conv_b.reshape(1, -1))
    return out.reshape(rows, D_MODEL), cout


def _ret_proj_kernel(x_ref, cos_ref, sin_ref, g_ref, wq_ref, wk_ref, wv_ref, q_ref, k_ref, v_ref):
    h = _rms(x_ref[...], g_ref[...]).astype(BF16)
    cos = cos_ref[...]
    sin = sin_ref[...]
    hw = RET_DK // 2
    for w_ref, dst, scale in ((wq_ref, q_ref, 1.0), (wk_ref, k_ref, RET_DK ** -0.5)):
        p = _dot(h, w_ref[...])
        for hd in range(RET_HEADS):
            lo = RET_DK * hd
            t1 = p[:, lo:lo + hw]
            t2 = p[:, lo + hw:lo + 2 * hw]
            dst[:, lo:lo + hw] = ((t1 * cos - t2 * sin) * scale).astype(BF16)
            dst[:, lo + hw:lo + 2 * hw] = ((t2 * cos + t1 * sin) * scale).astype(BF16)
    v_ref[...] = _dot(h, wv_ref[...]).astype(BF16)


def _ret_chunk_kernel(q_ref, k_ref, v_ref, s0_ref, o_ref, sout_ref, *, chunk, pad, group):
    @pl.when(pl.program_id(1) == 0)
    def _():
        sout_ref[...] = s0_ref[...]

    ri = lax.broadcasted_iota(jnp.int32, (chunk, chunk), 0)
    ci = lax.broadcasted_iota(jnp.int32, (chunk, chunk), 1)
    rel = (ri - ci).astype(F32)
    causal = ri >= ci
    idx = lax.broadcasted_iota(jnp.int32, (chunk, 1), 0).astype(F32)
    lgs = [math.log1p(-(2.0 ** (-5.0 - hd))) for hd in range(RET_HEADS)]
    units = [(g, hd) for g in range(group) for hd in range(RET_HEADS)]
    qsl = [slice(RET_DK * hd, RET_DK * (hd + 1)) for hd in range(RET_HEADS)]
    vsl = [slice(RET_DV * hd, RET_DV * (hd + 1)) for hd in range(RET_HEADS)]
    decay = [jnp.where(causal, jnp.exp(jnp.maximum(rel, 0.0) * lg), 0.0) for lg in lgs]
    scores = [(_dot_nt(q_ref[g, :, qsl[hd]], k_ref[g, :, qsl[hd]]) * decay[hd]).astype(BF16) for g, hd in units]
    for (g, hd), sc in zip(units, scores):
        q_decay = jnp.exp((idx + (1.0 - pad)) * lgs[hd])
        inter = _dot(q_ref[g, :, qsl[hd]], sout_ref[g, hd].astype(BF16)) * q_decay
        o_ref[g, :, vsl[hd]] = _dot(sc, v_ref[g, :, vsl[hd]]) + inter
    for g, hd in units:
        k_decay = jnp.exp((chunk - 1.0 - idx) * lgs[hd])
        kd = (k_ref[g, :, qsl[hd]].astype(F32) * k_decay).astype(BF16)
        sout_ref[g, hd] = sout_ref[g, hd] * math.exp((chunk - pad) * lgs[hd]) + _dot_tn(kd, v_ref[g, :, vsl[hd]])


def _ret_out_kernel(x_ref, o_ref, g0_ref, g1_ref, wg_ref, wo_ref, y_ref, on_sc):
    x = x_ref[...]
    h = _rms(x, g0_ref[...]).astype(BF16)
    gate = _silu(_dot(h, wg_ref[...]))
    for hd in range(RET_HEADS):
        sl = slice(RET_DV * hd, RET_DV * (hd + 1))
        oh = o_ref[:, sl]
        mu = jnp.mean(oh, axis=-1, keepdims=True)
        ctr = oh - mu
        var = jnp.mean(ctr * ctr, axis=-1, keepdims=True)
        on_sc[:, sl] = (ctr * lax.rsqrt(var + NORM_EPS) * gate[:, sl]).astype(BF16)
    y_ref[...] = x + _rms(_dot(on_sc[...], wo_ref[...]), g1_ref[...])


def _row_grid_call(kernel, row_ins, const_ins, outs, rows, tile, scratch=(), name=None, tab_ins=(),
                   vmem_mb=None):
    nt = rows // tile
    in_specs = [pl.BlockSpec((tile, a.shape[1]), lambda i: (i, 0)) for a in row_ins]
    for a in tab_ins:
        ntab = a.shape[0] // tile
        in_specs.append(pl.BlockSpec((tile, a.shape[1]), lambda i, ntab=ntab: (i % ntab, 0)))
    in_specs += [_const_spec(a.shape) for a in const_ins]
    return pl.pallas_call(
        kernel,
        grid=(nt,),
        in_specs=in_specs,
        out_specs=[pl.BlockSpec((tile, n), lambda i: (i, 0)) for n, _ in outs],
        out_shape=[jax.ShapeDtypeStruct((rows, n), dt) for n, dt in outs],
        scratch_shapes=list(scratch),
        compiler_params=_cparams("arbitrary", vmem_mb=vmem_mb),
        name=name,
    )(*row_ins, *tab_ins, *const_ins)


def _to_seq(a, bn, l, time_major, pad):
    n = a.shape[1]
    a = a.reshape(l, bn, n).transpose(1, 0, 2) if time_major else a.reshape(bn, l, n)
    if pad:
        a = jnp.pad(a, ((0, 0), (pad, 0), (0, 0)))
    return a


def _from_seq(a, bn, l, time_major, pad):
    n = a.shape[2]
    a = a[:, pad:, :]
    a = a.transpose(1, 0, 2) if time_major else a
    return a.reshape(bn * l, n)


def _seq_group(bn, chunk, state_bytes):
    if chunk > SMALL_SEQ_PAD:
        return min(bn, LONG_CHUNK_GROUP)
    grp = max(1, min(bn, SEQ_GROUP_BYTES // state_bytes))
    while bn % grp:
        grp -= 1
    return grp


def _seq_chunking(l, chunk):
    if l % chunk == 0:
        return chunk, 0
    assert l <= SMALL_SEQ_PAD
    return SMALL_SEQ_PAD, SMALL_SEQ_PAD - l


def _ret_layer(x, bn, l, time_major, pos0, s0, g0, g1, wq, wk, wv, wg, wo):
    rows = x.shape[0]
    tile = min(ROW_TILE, rows)
    ptile = min(PROJ_ROW_TILE, rows)
    half = RET_DK // 2
    freq = 1.0 / (ROPE_BASE ** jnp.linspace(0.0, 1.0, half, dtype=F32))
    pos = jnp.arange(pos0, pos0 + l, dtype=jnp.int32).astype(F32)
    ang = pos[:, None] * freq[None, :]
    cos, sin = jnp.cos(ang), jnp.sin(ang)
    if time_major:
        cos, sin = jnp.repeat(cos, bn, axis=0), jnp.repeat(sin, bn, axis=0)
    elif l < ptile:
        cos, sin = jnp.tile(cos, (ptile // l, 1)), jnp.tile(sin, (ptile // l, 1))
    q, k, v = _row_grid_call(
        _ret_proj_kernel, [x], [g0.reshape(1, -1), wq.astype(BF16), wk.astype(BF16), wv.astype(BF16)],
        [(RET_HEADS * RET_DK, BF16), (RET_HEADS * RET_DK, BF16), (RET_HEADS * RET_DV, BF16)],
        rows, ptile, name="ret_proj", tab_ins=[cos, sin], vmem_mb=FFN_VMEM_MB)
    chunk, pad = _seq_chunking(l, RET_CHUNK)
    lp = l + pad
    q3, k3, v3 = (_to_seq(a, bn, l, time_major, pad) for a in (q, k, v))
    nc = lp // chunk
    grp = _seq_group(bn, chunk, RET_HEADS * RET_DK * RET_DV * 4)
    o3, s_new = pl.pallas_call(
        functools.partial(_ret_chunk_kernel, chunk=chunk, pad=pad, group=grp),
        grid=(bn // grp, nc),
        in_specs=[
            pl.BlockSpec((grp, chunk, RET_HEADS * RET_DK), lambda b, i: (b, i, 0)),
            pl.BlockSpec((grp, chunk, RET_HEADS * RET_DK), lambda b, i: (b, i, 0)),
            pl.BlockSpec((grp, chunk, RET_HEADS * RET_DV), lambda b, i: (b, i, 0)),
            pl.BlockSpec((grp, RET_HEADS, RET_DK, RET_DV), lambda b, i: (b, 0, 0, 0)),
        ],
        out_specs=[
            pl.BlockSpec((grp, chunk, RET_HEADS * RET_DV), lambda b, i: (b, i, 0)),
            pl.BlockSpec((grp, RET_HEADS, RET_DK, RET_DV), lambda b, i: (b, 0, 0, 0)),
        ],
        out_shape=[jax.ShapeDtypeStruct((bn, lp, RET_HEADS * RET_DV), F32),
                   jax.ShapeDtypeStruct((bn, RET_HEADS, RET_DK, RET_DV), F32)],
        compiler_params=_cparams("arbitrary", "arbitrary"),
        name="ret_chunk",
    )(q3, k3, v3, s0)
    o = _from_seq(o3, bn, l, time_major, pad)
    (y,) = _row_grid_call(
        _ret_out_kernel, [x, o], [g0.reshape(1, -1), g1.reshape(1, -1), wg.astype(BF16), wo.astype(BF16)],
        [(D_MODEL, F32)], rows, tile, scratch=[pltpu.VMEM((tile, RET_HEADS * RET_DV), BF16)], name="ret_out")
    return y, s_new


def _gla_proj_kernel(x_ref, g_ref, wq_ref, wk_ref, wv_ref, wa1_ref, wa2_ref, ba_ref,
                     q_ref, k_ref, v_ref, la_ref):
    h = _rms(x_ref[...], g_ref[...]).astype(BF16)
    q_ref[...] = (_dot(h, wq_ref[...]) * (GLA_DK ** -0.5)).astype(BF16)
    k_ref[...] = _dot(h, wk_ref[...]).astype(BF16)
    v_ref[...] = _dot(h, wv_ref[...]).astype(BF16)
    low = _dot(h, wa1_ref[...]).astype(BF16)
    logit = _dot(low, wa2_ref[...]) + ba_ref[...]
    la_ref[...] = _log_sigmoid(logit) / GLA_GATE_NORM


def _gla_chunk_kernel(q_ref, k_ref, v_ref, la_ref, s0_ref, o_ref, sout_ref, st_sc, *, chunk, group):
    i = pl.program_id(1)

    @pl.when(i == 0)
    def _():
        for g in range(group):
            for hd in range(GLA_HEADS):
                st_sc[g, hd] = s0_ref[g, hd]

    ri = lax.broadcasted_iota(jnp.int32, (chunk, chunk), 0)
    ci = lax.broadcasted_iota(jnp.int32, (chunk, chunk), 1)
    causal = ri >= ci
    tri = causal.astype(BF16)
    mid = chunk // 2
    nk = GLA_HEADS * GLA_DK
    units = [(g, hd) for g in range(group) for hd in range(GLA_HEADS)]
    bcs = []
    for g in range(group):
        la = la_ref[g]
        p1 = la.astype(BF16)
        r1 = la - p1.astype(F32)
        p2 = r1.astype(BF16)
        p3 = (r1 - p2.astype(F32)).astype(BF16)
        acc = _dot(tri, jnp.concatenate([p1, p2, p3], axis=1))
        bcs.append(acc[:, :nk] + acc[:, nk:2 * nk] + acc[:, 2 * nk:])
    prep = []
    for g, hd in units:
        ksl = slice(GLA_DK * hd, GLA_DK * (hd + 1))
        bc = bcs[g][:, ksl]
        ref = bc[mid:mid + 1, :]
        blast = bc[chunk - 1:chunk, :]
        qh = q_ref[g, :, ksl].astype(F32)
        kh = k_ref[g, :, ksl].astype(F32)
        qt = (qh * jnp.exp(bc - ref)).astype(BF16)
        kt = (kh * jnp.exp(ref - bc)).astype(BF16)
        qg = (qh * jnp.exp(bc)).astype(BF16)
        kg = (kh * jnp.exp(blast - bc)).astype(BF16)
        row_decay = jnp.broadcast_to(jnp.exp(blast), (SUBLANES, GLA_DK)).T[:, :1]
        prep.append((qt, kt, qg, kg, row_decay))
    scores = [jnp.where(causal, _dot_nt(qt, kt), 0.0).astype(BF16) for qt, kt, _, _, _ in prep]
    for (g, hd), sc, (_, _, qg, kg, row_decay) in zip(units, scores, prep):
        vsl = slice(GLA_DV * hd, GLA_DV * (hd + 1))
        vh = v_ref[g, :, vsl]
        st = st_sc[g, hd]
        o_ref[g, :, vsl] = _dot(sc, vh) + _dot(qg, st.astype(BF16))
        st_sc[g, hd] = st * row_decay + _dot_tn(kg, vh)

    @pl.when(i == pl.num_programs(1) - 1)
    def _():
        for g in range(group):
            for hd in range(GLA_HEADS):
                sout_ref[g, hd] = st_sc[g, hd]


def _gla_out_kernel(x_ref, o_ref, g0_ref, g1_ref, ng_ref, wg_ref, wo_ref, y_ref, on_sc):
    x = x_ref[...]
    h = _rms(x, g0_ref[...]).astype(BF16)
    gate = _silu(_dot(h, wg_ref[...]))
    for hd in range(GLA_HEADS):
        sl = slice(GLA_DV * hd, GLA_DV * (hd + 1))
        oh = o_ref[:, sl]
        on = oh * lax.rsqrt(jnp.mean(oh * oh, axis=-1, keepdims=True) + NORM_EPS) * ng_ref[...]
        on_sc[:, sl] = (on * gate[:, sl]).astype(BF16)
    y_ref[...] = x + _rms(_dot(on_sc[...], wo_ref[...]), g1_ref[...])


def _gla_layer(x, bn, l, time_major, s0, g0, g1, wq, wk, wv, wg, wa1, wa2, ba, norm_g, wo):
    rows = x.shape[0]
    tile = min(PROJ_ROW_TILE, rows)
    lanes = 128
    wa1p = jnp.pad(wa1, ((0, 0), (0, lanes - GLA_GATE_RANK))).astype(BF16)
    wa2p = jnp.pad(wa2, ((0, lanes - GLA_GATE_RANK), (0, 0))).astype(BF16)
    q, k, v, la = _row_grid_call(
        _gla_proj_kernel, [x],
        [g0.reshape(1, -1), wq.astype(BF16), wk.astype(BF16), wv.astype(BF16), wa1p, wa2p, ba.reshape(1, -1)],
        [(GLA_HEADS * GLA_DK, BF16), (GLA_HEADS * GLA_DK, BF16), (GLA_HEADS * GLA_DV, BF16),
         (GLA_HEADS * GLA_DK, F32)],
        rows, tile, name="gla_proj", vmem_mb=FFN_VMEM_MB)
    chunk, pad = _seq_chunking(l, GLA_CHUNK)
    lp = l + pad
    q3, k3, v3, la3 = (_to_seq(a, bn, l, time_major, pad) for a in (q, k, v, la))
    nc = lp // chunk
    grp = GLA_GROUP if bn % GLA_GROUP == 0 else 1
    o3, s_new = pl.pallas_call(
        functools.partial(_gla_chunk_kernel, chunk=chunk, group=grp),
        grid=(bn // grp, nc),
        in_specs=[
            pl.BlockSpec((grp, chunk, GLA_HEADS * GLA_DK), lambda b, i: (b, i, 0)),
            pl.BlockSpec((grp, chunk, GLA_HEADS * GLA_DK), lambda b, i: (b, i, 0)),
            pl.BlockSpec((grp, chunk, GLA_HEADS * GLA_DV), lambda b, i: (b, i, 0)),
            pl.BlockSpec((grp, chunk, GLA_HEADS * GLA_DK), lambda b, i: (b, i, 0)),
            pl.BlockSpec((grp, GLA_HEADS, GLA_DK, GLA_DV), lambda b, i: (b, 0, 0, 0)),
        ],
        out_specs=[
            pl.BlockSpec((grp, chunk, GLA_HEADS * GLA_DV), lambda b, i: (b, i, 0)),
            pl.BlockSpec((grp, GLA_HEADS, GLA_DK, GLA_DV), lambda b, i: (b, 0, 0, 0)),
        ],
        out_shape=[jax.ShapeDtypeStruct((bn, lp, GLA_HEADS * GLA_DV), F32),
                   jax.ShapeDtypeStruct((bn, GLA_HEADS, GLA_DK, GLA_DV), F32)],
        scratch_shapes=[pltpu.VMEM((grp, GLA_HEADS, GLA_DK, GLA_DV), F32)],
        compiler_params=_cparams("arbitrary", "arbitrary"),
        name="gla_chunk",
    )(q3, k3, v3, la3, s0)
    o = _from_seq(o3, bn, l, time_major, pad)
    (y,) = _row_grid_call(
        _gla_out_kernel, [x, o],
        [g0.reshape(1, -1), g1.reshape(1, -1), norm_g.reshape(1, -1), wg.astype(BF16), wo.astype(BF16)],
        [(D_MODEL, F32)], rows, tile, scratch=[pltpu.VMEM((tile, GLA_HEADS * GLA_DV), BF16)], name="gla_out",
        vmem_mb=FFN_VMEM_MB)
    return y, s_new


def _transpose_rows(x, a, b):
    return x.reshape(a, b, x.shape[1]).transpose(1, 0, 2).reshape(a * b, x.shape[1])


def _trunk(x, pos0, s5_re, s5_im, ret_s, gla_s, conv_buf, p, batch_major_mixers):
    bn, l, _ = x.shape
    depth = p['norm_g'].shape[0]
    assert not batch_major_mixers or bn == SUBLANES
    rows = x.reshape(bn * l, D_MODEL)
    time_major = False
    if not batch_major_mixers:
        rows, time_major = _transpose_rows(rows, bn, l), True
    new_re, new_im, new_ret, new_gla, new_conv = [], [], [], [], []
    for layer in range(depth):
        g = p['norm_g'][layer]
        kind = layer % N_MIXERS
        j = layer // N_MIXERS
        if kind == 0:
            rows, hr, hi = _s5_layer(rows, bn, s5_re[j], s5_im[j], g[0], g[1],
                                     p['s5_lambda_re'][j], p['s5_lambda_im'][j], p['s5_b_re'][j], p['s5_b_im'][j],
                                     p['s5_c_re'][j], p['s5_c_im'][j], p['s5_log_dt'][j], p['s5_d'][j],
                                     p['s5_w_glu'], j, x_bm=not time_major)
            time_major = True
            new_re.append(hr)
            new_im.append(hi)
        elif kind == 1:
            rows, s = _ret_layer(rows, bn, l, time_major, pos0, ret_s[j], g[0], g[1], p['ret_wq'][j],
                                 p['ret_wk'][j], p['ret_wv'][j], p['ret_wg'][j], p['ret_wo'][j])
            new_ret.append(s)
        else:
            rows, s = _gla_layer(rows, bn, l, time_major, gla_s[j], g[0], g[1], p['gla_wq'][j], p['gla_wk'][j],
                                 p['gla_wv'][j], p['gla_wg'][j], p['gla_wa1'][j], p['gla_wa2'][j], p['gla_ba'][j],
                                 p['gla_norm_g'][j], p['gla_wo'][j])
            new_gla.append(s)
        next_is_s5 = layer + 1 < depth and (layer + 1) % N_MIXERS == 0
        out_bm = batch_major_mixers and not next_is_s5
        cin = conv_buf[layer].transpose(1, 0, 2).reshape((CONV_W - 1) * bn, D_FF)
        rows, cout = _ffn_layer(rows, bn, cin, g[2], g[3], p['ffn_w_gate'], p['ffn_w_up'],
                                p['ffn_conv_w'][layer], p['ffn_conv_b'][layer], p['ffn_w_down'], layer,
                                x_bm=not time_major, out_bm=out_bm)
        time_major = not out_bm
        new_conv.append(cout.reshape(CONV_W - 1, bn, D_FF).transpose(1, 0, 2))
    if time_major:
        rows = _transpose_rows(rows, l, bn)
    y = rows.reshape(bn, l, D_MODEL)
    return (y, jnp.stack(new_re), jnp.stack(new_im), jnp.stack(new_ret), jnp.stack(new_gla),
            jnp.stack(new_conv))


def kernel(x_prompt, x_sample, state_s5_re, state_s5_im, state_ret, state_gla, cache_ffn_conv,
           norm_g, s5_lambda_re, s5_lambda_im, s5_b_re, s5_b_im, s5_c_re, s5_c_im, s5_log_dt, s5_d, s5_w_glu,
           ret_wq, ret_wk, ret_wv, ret_wg, ret_wo,
           gla_wq, gla_wk, gla_wv, gla_wg, gla_wa1, gla_wa2, gla_ba, gla_norm_g, gla_wo,
           ffn_w_gate, ffn_w_up, ffn_conv_w, ffn_conv_b, ffn_w_down):
    p = dict(norm_g=norm_g, s5_lambda_re=s5_lambda_re, s5_lambda_im=s5_lambda_im, s5_b_re=s5_b_re,
             s5_b_im=s5_b_im, s5_c_re=s5_c_re, s5_c_im=s5_c_im, s5_log_dt=s5_log_dt, s5_d=s5_d,
             s5_w_glu=s5_w_glu.astype(BF16), ret_wq=ret_wq, ret_wk=ret_wk, ret_wv=ret_wv, ret_wg=ret_wg, ret_wo=ret_wo,
             gla_wq=gla_wq, gla_wk=gla_wk, gla_wv=gla_wv, gla_wg=gla_wg, gla_wa1=gla_wa1, gla_wa2=gla_wa2,
             gla_ba=gla_ba, gla_norm_g=gla_norm_g, gla_wo=gla_wo, ffn_w_gate=ffn_w_gate.astype(BF16),
             ffn_w_up=ffn_w_up.astype(BF16), ffn_conv_w=ffn_conv_w, ffn_conv_b=ffn_conv_b,
             ffn_w_down=ffn_w_down.astype(BF16))
    bp = x_prompt.shape[0]
    z_s5 = jnp.zeros((state_s5_re.shape[0], bp) + state_s5_re.shape[2:], F32)
    z_ret = jnp.zeros((state_ret.shape[0], bp) + state_ret.shape[2:], F32)
    z_gla = jnp.zeros((state_gla.shape[0], bp) + state_gla.shape[2:], F32)
    z_conv = jnp.zeros((cache_ffn_conv.shape[0], bp) + cache_ffn_conv.shape[2:], x_prompt.dtype)
    outs_p = _trunk(x_prompt, 0, z_s5, z_s5, z_ret, z_gla, z_conv, p, batch_major_mixers=True)
    outs_s = _trunk(x_sample, PAST_LEN, state_s5_re, state_s5_im, state_ret, state_gla, cache_ffn_conv, p,
                    batch_major_mixers=False)
    return (outs_p[0], outs_s[0]) + tuple(outs_p[1:]) + tuple(outs_s[1:])
```

```python
import functools
import math

import jax
import jax.numpy as jnp
from jax import lax
from jax.experimental import pallas as pl
from jax.experimental.pallas import tpu as pltpu

F32 = jnp.float32
BF16 = jnp.bfloat16

D_MODEL = 1024
PAST_LEN = 16384
N_MIXERS = 3
S5_GROUP = 16
S5_GROUPS = D_MODEL // S5_GROUP
S5_STATE = 64
S5_SUPER = 4
S5_SG_CH = D_MODEL // S5_SUPER
S5_SG_ST = (S5_GROUPS // S5_SUPER) * S5_STATE
RET_HEADS = 4
RET_DK = D_MODEL // RET_HEADS
RET_DV = 2 * D_MODEL // RET_HEADS
RET_CHUNK = 256
ROPE_BASE = 10000.0
GLA_HEADS = 4
GLA_DK = D_MODEL // 2 // GLA_HEADS
GLA_DV = D_MODEL // GLA_HEADS
GLA_GATE_RANK = 16
GLA_GATE_NORM = 16.0
GLA_CHUNK = 64
GLA_GROUP = 8
D_FF = ((8 * D_MODEL // 3 + 255) // 256) * 256
FFN_SUB = 256
FFN_ROW_TILE = 1024
FFN_VMEM_MB = 58
S5_BU_BUFFERS = 4
LONG_CHUNK_GROUP = 2
SEQ_GROUP_BYTES = 8 * 2 ** 20
CONV_W = 3
NORM_EPS = 1e-6
ROW_TILE = 512
SUBLANES = 8
PROJ_ROW_TILE = 1024
VMEM_MB = 52


def _cparams(*sem, vmem_mb=None):
    return pltpu.CompilerParams(dimension_semantics=sem, vmem_limit_bytes=(vmem_mb or VMEM_MB) * 2 ** 20)


def _const_spec(shape):
    zeros = (0,) * len(shape)
    return pl.BlockSpec(shape, lambda *_: zeros, pipeline_mode=pl.Buffered(1))


def _stacked_spec(shape, index):
    where = (index,) + (0,) * len(shape)
    return pl.BlockSpec((None,) + tuple(shape), lambda *_: where, pipeline_mode=pl.Buffered(1))


def _rms(x, g):
    return x * lax.rsqrt(jnp.mean(x * x, axis=-1, keepdims=True) + NORM_EPS) * g


def _sigmoid(x):
    return 1.0 / (1.0 + jnp.exp(-x))


def _silu(x):
    return x * _sigmoid(x)


def _gelu_tanh(x):
    return 0.5 * x * (1.0 + jnp.tanh(math.sqrt(2.0 / math.pi) * (x + 0.044715 * (x * x * x))))


def _log_sigmoid(x):
    return jnp.minimum(x, 0.0) - jnp.log(1.0 + jnp.exp(-jnp.abs(x)))


def _dot(a, b):
    return jnp.dot(a, b, preferred_element_type=F32)


def _dot_nt(a, b):
    return lax.dot_general(a, b, (((1,), (1,)), ((), ())), preferred_element_type=F32)


def _dot_tn(a, b):
    return lax.dot_general(a, b, (((0,), (0,)), ((), ())), preferred_element_type=F32)


def _s5_kernel(x_ref, h0re_ref, h0im_ref, g0_ref, g1_ref, are_ref, aim_ref, bre_ref, bim_ref, cc_ref, dsk_ref,
               wglu_ref, o_ref, hre_ref, him_ref, bu_sc, y_sc, bbd_ref, cbd_ref, *, bn, tt, x_bm):
    @pl.when(pl.program_id(0) == 0)
    def _():
        hre_ref[...] = h0re_ref[...]
        him_ref[...] = h0im_ref[...]
        ngl = S5_GROUPS // S5_SUPER

        def lane_tiler(k, n):
            r = lax.broadcasted_iota(jnp.int32, (k, n), 0)
            c = lax.broadcasted_iota(jnp.int32, (k, n), 1)
            return ((c & (k - 1)) == r).astype(BF16)

        tile_n = lane_tiler(S5_STATE, S5_SG_ST)
        tile_p = lane_tiler(S5_GROUP, S5_SG_CH)
        sh_p = S5_GROUP.bit_length() - 1
        sh_n = S5_STATE.bit_length() - 1
        rb = lax.broadcasted_iota(jnp.int32, (S5_SG_CH, S5_SG_ST), 0) >> sh_p
        cb = lax.broadcasted_iota(jnp.int32, (S5_SG_CH, S5_SG_ST), 1) >> sh_n
        rc = (lax.broadcasted_iota(jnp.int32, (2 * S5_SG_ST, S5_SG_CH), 0) >> sh_n) & (ngl - 1)
        cc = lax.broadcasted_iota(jnp.int32, (2 * S5_SG_ST, S5_SG_CH), 1) >> sh_p
        for sg in range(S5_SUPER):
            for part, ref in ((0, bre_ref), (1, bim_ref)):
                blk = _dot(ref[sg].astype(BF16), tile_n)
                bbd_ref[sg, :, S5_SG_ST * part:S5_SG_ST * (part + 1)] = jnp.where(rb == cb, blk, 0.0).astype(BF16)
            blk = _dot(cc_ref[sg].astype(BF16), tile_p)
            cbd_ref[sg] = jnp.where(rc == cc, blk, 0.0).astype(BF16)

    x = x_ref[...]
    if x_bm:
        x = jnp.swapaxes(x, 0, 1).reshape(tt * bn, D_MODEL)
    u = _rms(x, g0_ref[...])
    ub = u.astype(BF16)
    half = S5_SG_ST // 2
    for sg in range(S5_SUPER):
        buf = bu_sc.at[sg % S5_BU_BUFFERS]
        buf[...] = _dot(ub[:, S5_SG_CH * sg:S5_SG_CH * (sg + 1)], bbd_ref[sg])
        for hf in range(2):
            c_re = half * hf
            c_im = S5_SG_ST + half * hf
            a_lo = S5_SG_ST * sg + half * hf
            ar = jnp.broadcast_to(are_ref[:, a_lo:a_lo + half], (8, half))
            ai = jnp.broadcast_to(aim_ref[:, a_lo:a_lo + half], (8, half))
            for bt in range(bn // 8):
                r8 = 8 * bt
                hr = hre_ref[r8:r8 + 8, a_lo:a_lo + half]
                hi = him_ref[r8:r8 + 8, a_lo:a_lo + half]
                for t in range(tt):
                    r0 = t * bn + r8
                    bur = buf[r0:r0 + 8, c_re:c_re + half]
                    bui = buf[r0:r0 + 8, c_im:c_im + half]
                    hr, hi = ar * hr - ai * hi + bur, ar * hi + ai * hr + bui
                    buf[r0:r0 + 8, c_re:c_re + half] = hr
                    buf[r0:r0 + 8, c_im:c_im + half] = hi
                hre_ref[r8:r8 + 8, a_lo:a_lo + half] = hr
                him_ref[r8:r8 + 8, a_lo:a_lo + half] = hi
        y_sc[:, S5_SG_CH * sg:S5_SG_CH * (sg + 1)] = _dot(buf[...].astype(BF16), cbd_ref[sg])
    y = y_sc[...] + dsk_ref[...] * u
    z = _dot(_gelu_tanh(y).astype(BF16), wglu_ref[...])
    mix = z[:, :D_MODEL] * _sigmoid(z[:, D_MODEL:])
    o_ref[...] = x + _rms(mix, g1_ref[...])


def _s5_layer(x, bn, h0_re, h0_im, g0, g1, lam_re, lam_im, b_re, b_im, c_re, c_im, log_dt, d_skip, w_glu,
              j, x_bm=False):
    rows = x.shape[0]
    tt = min(rows // bn, max(1, ROW_TILE // bn))
    tile = tt * bn
    o_spec = pl.BlockSpec((tile, D_MODEL), lambda i: (i, 0))
    x_spec = o_spec
    if x_bm:
        assert bn == SUBLANES
        x = x.reshape(bn, rows // bn, D_MODEL)
        x_spec = pl.BlockSpec((bn, tt, D_MODEL), lambda i: (0, i, 0))
    lam = lax.complex(lam_re, lam_im)
    dt = jnp.exp(log_dt)[:, None]
    a_bar = jnp.exp(lam * dt)
    b_bar = ((a_bar - 1.0) / lam)[..., None] * lax.complex(b_re, b_im)
    bt = b_bar.transpose(0, 2, 1).reshape(S5_SUPER, S5_SG_CH, S5_STATE)
    bt_re, bt_im = jnp.real(bt), jnp.imag(bt)
    ct_re = c_re.transpose(0, 2, 1).reshape(S5_SUPER, S5_SG_ST, S5_GROUP)
    ct_im = c_im.transpose(0, 2, 1).reshape(S5_SUPER, S5_SG_ST, S5_GROUP)
    ct = jnp.concatenate([ct_re, -ct_im], axis=1)
    a_re = jnp.real(a_bar).reshape(1, S5_GROUPS * S5_STATE)
    a_im = jnp.imag(a_bar).reshape(1, S5_GROUPS * S5_STATE)
    nst = S5_GROUPS * S5_STATE

    out, new_re, new_im = pl.pallas_call(
        functools.partial(_s5_kernel, bn=bn, tt=tt, x_bm=x_bm),
        grid=(rows // tile,),
        in_specs=[
            x_spec,
            _const_spec((bn, nst)), _const_spec((bn, nst)),
            _const_spec((1, D_MODEL)), _const_spec((1, D_MODEL)),
            _const_spec((1, nst)), _const_spec((1, nst)),
            _const_spec((S5_SUPER, S5_SG_CH, S5_STATE)), _const_spec((S5_SUPER, S5_SG_CH, S5_STATE)),
            _const_spec((S5_SUPER, 2 * S5_SG_ST, S5_GROUP)),
            _const_spec((1, D_MODEL)),
            _stacked_spec((D_MODEL, 2 * D_MODEL), j),
        ],
        out_specs=[o_spec, _const_spec((bn, nst)), _const_spec((bn, nst))],
        out_shape=[jax.ShapeDtypeStruct((rows, D_MODEL), F32), jax.ShapeDtypeStruct((bn, nst), F32),
                   jax.ShapeDtypeStruct((bn, nst), F32)],
        scratch_shapes=[pltpu.VMEM((S5_BU_BUFFERS, tile, 2 * S5_SG_ST), F32), pltpu.VMEM((tile, D_MODEL), F32),
                        pltpu.VMEM((S5_SUPER, S5_SG_CH, 2 * S5_SG_ST), BF16),
                        pltpu.VMEM((S5_SUPER, 2 * S5_SG_ST, S5_SG_CH), BF16)],
        compiler_params=_cparams("arbitrary"),
        name="s5_layer",
    )(x, h0_re.reshape(bn, nst), h0_im.reshape(bn, nst), g0.reshape(1, -1), g1.reshape(1, -1), a_re, a_im,
      bt_re, bt_im, ct, d_skip.reshape(1, -1), w_glu)
    return out, new_re.reshape(bn, S5_GROUPS, S5_STATE), new_im.reshape(bn, S5_GROUPS, S5_STATE)


def _ffn_kernel(x_ref, cin_ref, g2_ref, g3_ref, wg_ref, wu_ref, wd_ref, cw_ref, cb_ref,
                o_ref, cout_ref, hdn_sc, gext_sc, carry_sc, *, bn, tile, x_bm, out_bm):
    i = pl.program_id(0)
    halo = 2 * bn
    base = gext_sc.shape[1] - tile

    @pl.when(i == 0)
    def _():
        carry_sc[...] = cin_ref[...]

    x = x_ref[...]
    if x_bm:
        x = jnp.swapaxes(x, 0, 1).reshape(tile, D_MODEL)
    h = _rms(x, g2_ref[...]).astype(BF16)
    for j in range(D_FF // FFN_SUB):
        sl = slice(FFN_SUB * j, FFN_SUB * (j + 1))
        buf = gext_sc.at[j % 2]
        gpre = _dot(h, wg_ref[:, sl])
        buf[base - halo:base, :] = carry_sc[:, sl]
        buf[base:base + tile, :] = gpre
        carry_sc[:, sl] = gpre[tile - halo:tile, :]
        gconv = cb_ref[:, sl] + cw_ref[0:1, sl] * buf[base - halo:base - halo + tile, :]
        gconv = gconv + cw_ref[1:2, sl] * buf[base - bn:base - bn + tile, :]
        gconv = gconv + cw_ref[2:3, sl] * gpre
        hdn_sc[:, sl] = (_silu(gconv) * _dot(h, wu_ref[:, sl])).astype(BF16)
    y = x + _rms(_dot(hdn_sc[...], wd_ref[...]), g3_ref[...])
    if out_bm:
        y = jnp.swapaxes(y.reshape(tile // bn, bn, D_MODEL), 0, 1)
    o_ref[...] = y

    @pl.when(i == pl.num_programs(0) - 1)
    def _():
        cout_ref[...] = carry_sc[...]


def _ffn_layer(x, bn, cin, g2, g3, w_gate, w_up, conv_w, conv_b, w_down, layer, x_bm=False, out_bm=False):
    rows = x.shape[0]
    tile = min(FFN_ROW_TILE, rows)
    tt = tile // bn
    halo = 2 * bn
    assert not (x_bm or out_bm) or bn == SUBLANES
    tm_spec = pl.BlockSpec((tile, D_MODEL), lambda i: (i, 0))
    bm_spec = pl.BlockSpec((bn, tt, D_MODEL), lambda i: (0, i, 0))
    if x_bm:
        x = x.reshape(bn, rows // bn, D_MODEL)
    out_shape = (bn, rows // bn, D_MODEL) if out_bm else (rows, D_MODEL)
    out, cout = pl.pallas_call(
        functools.partial(_ffn_kernel, bn=bn, tile=tile, x_bm=x_bm, out_bm=out_bm),
        grid=(rows // tile,),
        in_specs=[
            bm_spec if x_bm else tm_spec,
            _const_spec((halo, D_FF)),
            _const_spec((1, D_MODEL)), _const_spec((1, D_MODEL)),
            _stacked_spec((D_MODEL, D_FF), layer), _stacked_spec((D_MODEL, D_FF), layer),
            _stacked_spec((D_FF, D_MODEL), layer),
            _const_spec((CONV_W, D_FF)), _const_spec((1, D_FF)),
        ],
        out_specs=[bm_spec if out_bm else tm_spec, _const_spec((halo, D_FF))],
        out_shape=[jax.ShapeDtypeStruct(out_shape, F32), jax.ShapeDtypeStruct((halo, D_FF), F32)],
        scratch_shapes=[pltpu.VMEM((tile, D_FF), BF16),
                        pltpu.VMEM((2, halo + tile, FFN_SUB), F32),
                        pltpu.VMEM((halo, D_FF), F32)],
        compiler_params=_cparams("arbitrary", vmem_mb=FFN_VMEM_MB),
        name="conv_ffn",
    )(x, cin, g2.reshape(1, -1), g3.reshape(1, -1), w_gate, w_up, w_down, conv_w, conv_b.reshape(1, -1))
    return out.reshape(rows, D_MODEL), cout


def _ret_proj_kernel(x_ref, cos_ref, sin_ref, g_ref, wq_ref, wk_ref, wv_ref, q_ref, k_ref, v_ref):
    h = _rms(x_ref[...], g_ref[...]).astype(BF16)
    cos = cos_ref[...]
    sin = sin_ref[...]
    hw = RET_DK // 2
    for w_ref, dst, scale in ((wq_ref, q_ref, 1.0), (wk_ref, k_ref, RET_DK ** -0.5)):
        p = _dot(h, w_ref[...])
        for hd in range(RET_HEADS):
            lo = RET_DK * hd
            t1 = p[:, lo:lo + hw]
            t2 = p[:, lo + hw:lo + 2 * hw]
            dst[:, lo:lo + hw] = ((t1 * cos - t2 * sin) * scale).astype(BF16)
            dst[:, lo + hw:lo + 2 * hw] = ((t2 * cos + t1 * sin) * scale).astype(BF16)
    v_ref[...] = _dot(h, wv_ref[...]).astype(BF16)


def _ret_chunk_kernel(q_ref, k_ref, v_ref, s0_ref, o_ref, sout_ref, *, chunk, group):
    @pl.when(pl.program_id(1) == 0)
    def _():
        sout_ref[...] = s0_ref[...]

    ri = lax.broadcasted_iota(jnp.int32, (chunk, chunk), 0)
    ci = lax.broadcasted_iota(jnp.int32, (chunk, chunk), 1)
    rel = (ri - ci).astype(F32)
    causal = ri >= ci
    idx = lax.broadcasted_iota(jnp.int32, (chunk, 1), 0).astype(F32)
    lgs = [math.log1p(-(2.0 ** (-5.0 - hd))) for hd in range(RET_HEADS)]
    units = [(g, hd) for g in range(group) for hd in range(RET_HEADS)]
    qsl = [slice(RET_DK * hd, RET_DK * (hd + 1)) for hd in range(RET_HEADS)]
    vsl = [slice(RET_DV * hd, RET_DV * (hd + 1)) for hd in range(RET_HEADS)]
    decay = [jnp.where(causal, jnp.exp(jnp.maximum(rel, 0.0) * lg), 0.0) for lg in lgs]
    scores = [(_dot_nt(q_ref[g, :, qsl[hd]], k_ref[g, :, qsl[hd]]) * decay[hd]).astype(BF16) for g, hd in units]
    for (g, hd), sc in zip(units, scores):
        q_decay = jnp.exp((idx + 1.0) * lgs[hd])
        inter = _dot(q_ref[g, :, qsl[hd]], sout_ref[g, hd].astype(BF16)) * q_decay
        o_ref[g, :, vsl[hd]] = _dot(sc, v_ref[g, :, vsl[hd]]) + inter
    for g, hd in units:
        k_decay = jnp.exp((chunk - 1.0 - idx) * lgs[hd])
        kd = (k_ref[g, :, qsl[hd]].astype(F32) * k_decay).astype(BF16)
        sout_ref[g, hd] = sout_ref[g, hd] * math.exp(chunk * lgs[hd]) + _dot_tn(kd, v_ref[g, :, vsl[hd]])


def _ret_out_kernel(x_ref, o_ref, g0_ref, g1_ref, wg_ref, wo_ref, y_ref, on_sc):
    x = x_ref[...]
    h = _rms(x, g0_ref[...]).astype(BF16)
    gate = _silu(_dot(h, wg_ref[...]))
    for hd in range(RET_HEADS):
        sl = slice(RET_DV * hd, RET_DV * (hd + 1))
        oh = o_ref[:, sl]
        mu = jnp.mean(oh, axis=-1, keepdims=True)
        ctr = oh - mu
        var = jnp.mean(ctr * ctr, axis=-1, keepdims=True)
        on_sc[:, sl] = (ctr * lax.rsqrt(var + NORM_EPS) * gate[:, sl]).astype(BF16)
    y_ref[...] = x + _rms(_dot(on_sc[...], wo_ref[...]), g1_ref[...])


def _row_grid_call(kernel, row_ins, const_ins, outs, rows, tile, scratch=(), name=None, tab_ins=(),
                   vmem_mb=None):
    nt = rows // tile
    in_specs = [pl.BlockSpec((tile, a.shape[1]), lambda i: (i, 0)) for a in row_ins]
    for a in tab_ins:
        ntab = a.shape[0] // tile
        in_specs.append(pl.BlockSpec((tile, a.shape[1]), lambda i, ntab=ntab: (i % ntab, 0)))
    in_specs += [_const_spec(a.shape) for a in const_ins]
    return pl.pallas_call(
        kernel,
        grid=(nt,),
        in_specs=in_specs,
        out_specs=[pl.BlockSpec((tile, n), lambda i: (i, 0)) for n, _ in outs],
        out_shape=[jax.ShapeDtypeStruct((rows, n), dt) for n, dt in outs],
        scratch_shapes=list(scratch),
        compiler_params=_cparams("arbitrary", vmem_mb=vmem_mb),
        name=name,
    )(*row_ins, *tab_ins, *const_ins)


def _to_seq(a, bn, l, time_major):
    n = a.shape[1]
    return a.reshape(l, bn, n).transpose(1, 0, 2) if time_major else a.reshape(bn, l, n)


def _from_seq(a, bn, l, time_major):
    a = a.transpose(1, 0, 2) if time_major else a
    return a.reshape(bn * l, a.shape[2])


def _seq_group(bn, chunk, full_chunk, state_bytes):
    if chunk == full_chunk:
        return min(bn, LONG_CHUNK_GROUP)
    grp = max(1, min(bn, SEQ_GROUP_BYTES // state_bytes))
    while bn % grp:
        grp -= 1
    return grp


def _seq_chunking(l, chunk):
    if l % chunk == 0:
        return chunk
    assert l < chunk
    return l


def _ret_layer(x, bn, l, time_major, pos0, s0, g0, g1, wq, wk, wv, wg, wo):
    rows = x.shape[0]
    tile = min(ROW_TILE, rows)
    ptile = min(PROJ_ROW_TILE, rows)
    half = RET_DK // 2
    freq = 1.0 / (ROPE_BASE ** jnp.linspace(0.0, 1.0, half, dtype=F32))
    pos = jnp.arange(pos0, pos0 + l, dtype=jnp.int32).astype(F32)
    ang = pos[:, None] * freq[None, :]
    cos, sin = jnp.cos(ang), jnp.sin(ang)
    if time_major:
        cos, sin = jnp.repeat(cos, bn, axis=0), jnp.repeat(sin, bn, axis=0)
    elif l < ptile:
        cos, sin = jnp.tile(cos, (ptile // l, 1)), jnp.tile(sin, (ptile // l, 1))
    q, k, v = _row_grid_call(
        _ret_proj_kernel, [x], [g0.reshape(1, -1), wq.astype(BF16), wk.astype(BF16), wv.astype(BF16)],
        [(RET_HEADS * RET_DK, BF16), (RET_HEADS * RET_DK, BF16), (RET_HEADS * RET_DV, BF16)],
        rows, ptile, name="ret_proj", tab_ins=[cos, sin], vmem_mb=FFN_VMEM_MB)
    chunk = _seq_chunking(l, RET_CHUNK)
    q3, k3, v3 = (_to_seq(a, bn, l, time_major) for a in (q, k, v))
    nc = l // chunk
    grp = _seq_group(bn, chunk, RET_CHUNK, RET_HEADS * RET_DK * RET_DV * 4)
    o3, s_new = pl.pallas_call(
        functools.partial(_ret_chunk_kernel, chunk=chunk, group=grp),
        grid=(bn // grp, nc),
        in_specs=[
            pl.BlockSpec((grp, chunk, RET_HEADS * RET_DK), lambda b, i: (b, i, 0)),
            pl.BlockSpec((grp, chunk, RET_HEADS * RET_DK), lambda b, i: (b, i, 0)),
            pl.BlockSpec((grp, chunk, RET_HEADS * RET_DV), lambda b, i: (b, i, 0)),
            pl.BlockSpec((grp, RET_HEADS, RET_DK, RET_DV), lambda b, i: (b, 0, 0, 0)),
        ],
        out_specs=[
            pl.BlockSpec((grp, chunk, RET_HEADS * RET_DV), lambda b, i: (b, i, 0)),
            pl.BlockSpec((grp, RET_HEADS, RET_DK, RET_DV), lambda b, i: (b, 0, 0, 0)),
        ],
        out_shape=[jax.ShapeDtypeStruct((bn, l, RET_HEADS * RET_DV), F32),
                   jax.ShapeDtypeStruct((bn, RET_HEADS, RET_DK, RET_DV), F32)],
        compiler_params=_cparams("arbitrary", "arbitrary"),
        name="ret_chunk",
    )(q3, k3, v3, s0)
    o = _from_seq(o3, bn, l, time_major)
    (y,) = _row_grid_call(
        _ret_out_kernel, [x, o], [g0.reshape(1, -1), g1.reshape(1, -1), wg.astype(BF16), wo.astype(BF16)],
        [(D_MODEL, F32)], rows, tile, scratch=[pltpu.VMEM((tile, RET_HEADS * RET_DV), BF16)], name="ret_out")
    return y, s_new


def _gla_proj_kernel(x_ref, g_ref, wq_ref, wk_ref, wv_ref, wa1_ref, wa2_ref, ba_ref,
                     q_ref, k_ref, v_ref, la_ref):
    h = _rms(x_ref[...], g_ref[...]).astype(BF16)
    q_ref[...] = (_dot(h, wq_ref[...]) * (GLA_DK ** -0.5)).astype(BF16)
    k_ref[...] = _dot(h, wk_ref[...]).astype(BF16)
    v_ref[...] = _dot(h, wv_ref[...]).astype(BF16)
    low = _dot(h, wa1_ref[...]).astype(BF16)
    logit = _dot(low, wa2_ref[...]) + ba_ref[...]
    la_ref[...] = _log_sigmoid(logit) / GLA_GATE_NORM


def _gla_chunk_kernel(q_ref, k_ref, v_ref, la_ref, s0_ref, o_ref, sout_ref, st_sc, *, chunk, group):
    i = pl.program_id(1)

    @pl.when(i == 0)
    def _():
        for g in range(group):
            for hd in range(GLA_HEADS):
                st_sc[g, hd] = s0_ref[g, hd]

    ri = lax.broadcasted_iota(jnp.int32, (chunk, chunk), 0)
    ci = lax.broadcasted_iota(jnp.int32, (chunk, chunk), 1)
    causal = ri >= ci
    tri = causal.astype(BF16)
    mid = chunk // 2
    nk = GLA_HEADS * GLA_DK
    units = [(g, hd) for g in range(group) for hd in range(GLA_HEADS)]
    bcs = []
    for g in range(group):
        la = la_ref[g]
        p1 = la.astype(BF16)
        r1 = la - p1.astype(F32)
        p2 = r1.astype(BF16)
        p3 = (r1 - p2.astype(F32)).astype(BF16)
        acc = _dot(tri, jnp.concatenate([p1, p2, p3], axis=1))
        bcs.append(acc[:, :nk] + acc[:, nk:2 * nk] + acc[:, 2 * nk:])
    prep = []
    for g, hd in units:
        ksl = slice(GLA_DK * hd, GLA_DK * (hd + 1))
        bc = bcs[g][:, ksl]
        ref = bc[mid:mid + 1, :]
        blast = bc[chunk - 1:chunk, :]
        qh = q_ref[g, :, ksl].astype(F32)
        kh = k_ref[g, :, ksl].astype(F32)
        qt = (qh * jnp.exp(bc - ref)).astype(BF16)
        kt = (kh * jnp.exp(ref - bc)).astype(BF16)
        qg = (qh * jnp.exp(bc)).astype(BF16)
        kg = (kh * jnp.exp(blast - bc)).astype(BF16)
        row_decay = jnp.broadcast_to(jnp.exp(blast), (SUBLANES, GLA_DK)).T[:, :1]
        prep.append((qt, kt, qg, kg, row_decay))
    scores = [jnp.where(causal, _dot_nt(qt, kt), 0.0).astype(BF16) for qt, kt, _, _, _ in prep]
    for (g, hd), sc, (_, _, qg, kg, row_decay) in zip(units, scores, prep):
        vsl = slice(GLA_DV * hd, GLA_DV * (hd + 1))
        vh = v_ref[g, :, vsl]
        st = st_sc[g, hd]
        o_ref[g, :, vsl] = _dot(sc, vh) + _dot(qg, st.astype(BF16))
        st_sc[g, hd] = st * row_decay + _dot_tn(kg, vh)

    @pl.when(i == pl.num_programs(1) - 1)
    def _():
        for g in range(group):
            for hd in range(GLA_HEADS):
                sout_ref[g, hd] = st_sc[g, hd]


def _gla_out_kernel(x_ref, o_ref, g0_ref, g1_ref, ng_ref, wg_ref, wo_ref, y_ref, on_sc):
    x = x_ref[...]
    h = _rms(x, g0_ref[...]).astype(BF16)
    gate = _silu(_dot(h, wg_ref[...]))
    for hd in range(GLA_HEADS):
        sl = slice(GLA_DV * hd, GLA_DV * (hd + 1))
        oh = o_ref[:, sl]
        on = oh * lax.rsqrt(jnp.mean(oh * oh, axis=-1, keepdims=True) + NORM_EPS) * ng_ref[...]
        on_sc[:, sl] = (on * gate[:, sl]).astype(BF16)
    y_ref[...] = x + _rms(_dot(on_sc[...], wo_ref[...]), g1_ref[...])


def _gla_layer(x, bn, l, time_major, s0, g0, g1, wq, wk, wv, wg, wa1, wa2, ba, norm_g, wo):
    rows = x.shape[0]
    tile = min(PROJ_ROW_TILE, rows)
    lanes = 128
    wa1p = jnp.pad(wa1, ((0, 0), (0, lanes - GLA_GATE_RANK))).astype(BF16)
    wa2p = jnp.pad(wa2, ((0, lanes - GLA_GATE_RANK), (0, 0))).astype(BF16)
    q, k, v, la = _row_grid_call(
        _gla_proj_kernel, [x],
        [g0.reshape(1, -1), wq.astype(BF16), wk.astype(BF16), wv.astype(BF16), wa1p, wa2p, ba.reshape(1, -1)],
        [(GLA_HEADS * GLA_DK, BF16), (GLA_HEADS * GLA_DK, BF16), (GLA_HEADS * GLA_DV, BF16),
         (GLA_HEADS * GLA_DK, F32)],
        rows, tile, name="gla_proj", vmem_mb=FFN_VMEM_MB)
    chunk = _seq_chunking(l, GLA_CHUNK)
    q3, k3, v3, la3 = (_to_seq(a, bn, l, time_major) for a in (q, k, v, la))
    nc = l // chunk
    grp = GLA_GROUP if bn % GLA_GROUP == 0 else 1
    o3, s_new = pl.pallas_call(
        functools.partial(_gla_chunk_kernel, chunk=chunk, group=grp),
        grid=(bn // grp, nc),
        in_specs=[
            pl.BlockSpec((grp, chunk, GLA_HEADS * GLA_DK), lambda b, i: (b, i, 0)),
            pl.BlockSpec((grp, chunk, GLA_HEADS * GLA_DK), lambda b, i: (b, i, 0)),
            pl.BlockSpec((grp, chunk, GLA_HEADS * GLA_DV), lambda b, i: (b, i, 0)),
            pl.BlockSpec((grp, chunk, GLA_HEADS * GLA_DK), lambda b, i: (b, i, 0)),
            pl.BlockSpec((grp, GLA_HEADS, GLA_DK, GLA_DV), lambda b, i: (b, 0, 0, 0)),
        ],
        out_specs=[
            pl.BlockSpec((grp, chunk, GLA_HEADS * GLA_DV), lambda b, i: (b, i, 0)),
            pl.BlockSpec((grp, GLA_HEADS, GLA_DK, GLA_DV), lambda b, i: (b, 0, 0, 0)),
        ],
        out_shape=[jax.ShapeDtypeStruct((bn, l, GLA_HEADS * GLA_DV), F32),
                   jax.ShapeDtypeStruct((bn, GLA_HEADS, GLA_DK, GLA_DV), F32)],
        scratch_shapes=[pltpu.VMEM((grp, GLA_HEADS, GLA_DK, GLA_DV), F32)],
        compiler_params=_cparams("arbitrary", "arbitrary"),
        name="gla_chunk",
    )(q3, k3, v3, la3, s0)
    o = _from_seq(o3, bn, l, time_major)
    (y,) = _row_grid_call(
        _gla_out_kernel, [x, o],
        [g0.reshape(1, -1), g1.reshape(1, -1), norm_g.reshape(1, -1), wg.astype(BF16), wo.astype(BF16)],
        [(D_MODEL, F32)], rows, tile, scratch=[pltpu.VMEM((tile, GLA_HEADS * GLA_DV), BF16)], name="gla_out",
        vmem_mb=FFN_VMEM_MB)
    return y, s_new


def _transpose_rows(x, a, b):
    return x.reshape(a, b, x.shape[1]).transpose(1, 0, 2).reshape(a * b, x.shape[1])


def _trunk(x, pos0, s5_re, s5_im, ret_s, gla_s, conv_buf, p, batch_major_mixers):
    bn, l, _ = x.shape
    depth = p['norm_g'].shape[0]
    assert not batch_major_mixers or bn == SUBLANES
    rows = x.reshape(bn * l, D_MODEL)
    time_major = False
    if not batch_major_mixers:
        rows, time_major = _transpose_rows(rows, bn, l), True
    new_re, new_im, new_ret, new_gla, new_conv = [], [], [], [], []
    for layer in range(depth):
        g = p['norm_g'][layer]
        kind = layer % N_MIXERS
        j = layer // N_MIXERS
        if kind == 0:
            rows, hr, hi = _s5_layer(rows, bn, s5_re[j], s5_im[j], g[0], g[1],
                                     p['s5_lambda_re'][j], p['s5_lambda_im'][j], p['s5_b_re'][j], p['s5_b_im'][j],
                                     p['s5_c_re'][j], p['s5_c_im'][j], p['s5_log_dt'][j], p['s5_d'][j],
                                     p['s5_w_glu'], j, x_bm=not time_major)
            time_major = True
            new_re.append(hr)
            new_im.append(hi)
        elif kind == 1:
            rows, s = _ret_layer(rows, bn, l, time_major, pos0, ret_s[j], g[0], g[1], p['ret_wq'][j],
                                 p['ret_wk'][j], p['ret_wv'][j], p['ret_wg'][j], p['ret_wo'][j])
            new_ret.append(s)
        else:
            rows, s = _gla_layer(rows, bn, l, time_major, gla_s[j], g[0], g[1], p['gla_wq'][j], p['gla_wk'][j],
                                 p['gla_wv'][j], p['gla_wg'][j], p['gla_wa1'][j], p['gla_wa2'][j], p['gla_ba'][j],
                                 p['gla_norm_g'][j], p['gla_wo'][j])
            new_gla.append(s)
        next_is_s5 = layer + 1 < depth and (layer + 1) % N_MIXERS == 0
        out_bm = batch_major_mixers and not next_is_s5
        cin = conv_buf[layer].transpose(1, 0, 2).reshape((CONV_W - 1) * bn, D_FF)
        rows, cout = _ffn_layer(rows, bn, cin, g[2], g[3], p['ffn_w_gate'], p['ffn_w_up'],
                                p['ffn_conv_w'][layer], p['ffn_conv_b'][layer], p['ffn_w_down'], layer,
                                x_bm=not time_major, out_bm=out_bm)
        time_major = not out_bm
        new_conv.append(cout.reshape(CONV_W - 1, bn, D_FF).transpose(1, 0, 2))
    if time_major:
        rows = _transpose_rows(rows, l, bn)
    y = rows.reshape(bn, l, D_MODEL)
    return (y, jnp.stack(new_re), jnp.stack(new_im), jnp.stack(new_ret), jnp.stack(new_gla),
            jnp.stack(new_conv))


def kernel(x_prompt, x_sample, state_s5_re, state_s5_im, state_ret, state_gla, cache_ffn_conv,
           norm_g, s5_lambda_re, s5_lambda_im, s5_b_re, s5_b_im, s5_c_re, s5_c_im, s5_log_dt, s5_d, s5_w_glu,
           ret_wq, ret_wk, ret_wv, ret_wg, ret_wo,
           gla_wq, gla_wk, gla_wv, gla_wg, gla_wa1, gla_wa2, gla_ba, gla_norm_g, gla_wo,
           ffn_w_gate, ffn_w_up, ffn_conv_w, ffn_conv_b, ffn_w_down):
    p = dict(norm_g=norm_g, s5_lambda_re=s5_lambda_re, s5_lambda_im=s5_lambda_im, s5_b_re=s5_b_re,
             s5_b_im=s5_b_im, s5_c_re=s5_c_re, s5_c_im=s5_c_im, s5_log_dt=s5_log_dt, s5_d=s5_d,
             s5_w_glu=s5_w_glu.astype(BF16), ret_wq=ret_wq, ret_wk=ret_wk, ret_wv=ret_wv, ret_wg=ret_wg, ret_wo=ret_wo,
             gla_wq=gla_wq, gla_wk=gla_wk, gla_wv=gla_wv, gla_wg=gla_wg, gla_wa1=gla_wa1, gla_wa2=gla_wa2,
             gla_ba=gla_ba, gla_norm_g=gla_norm_g, gla_wo=gla_wo, ffn_w_gate=ffn_w_gate.astype(BF16),
             ffn_w_up=ffn_w_up.astype(BF16), ffn_conv_w=ffn_conv_w, ffn_conv_b=ffn_conv_b,
             ffn_w_down=ffn_w_down.astype(BF16))
    bp = x_prompt.shape[0]
    z_s5 = jnp.zeros((state_s5_re.shape[0], bp) + state_s5_re.shape[2:], F32)
    z_ret = jnp.zeros((state_ret.shape[0], bp) + state_ret.shape[2:], F32)
    z_gla = jnp.zeros((state_gla.shape[0], bp) + state_gla.shape[2:], F32)
    z_conv = jnp.zeros((cache_ffn_conv.shape[0], bp) + cache_ffn_conv.shape[2:], x_prompt.dtype)
    outs_p = _trunk(x_prompt, 0, z_s5, z_s5, z_ret, z_gla, z_conv, p, batch_major_mixers=True)
    outs_s = _trunk(x_sample, PAST_LEN, state_s5_re, state_s5_im, state_ret, state_gla, cache_ffn_conv, p,
                    batch_major_mixers=False)
    return (outs_p[0], outs_s[0]) + tuple(outs_p[1:]) + tuple(outs_s[1:])
```

```python
import functools
import math

import jax
import jax.numpy as jnp
from jax import lax
from jax.experimental import pallas as pl
from jax.experimental.pallas import tpu as pltpu

F32 = jnp.float32
BF16 = jnp.bfloat16

D_MODEL = 1024
PAST_LEN = 16384
N_MIXERS = 3
S5_GROUP = 16
S5_GROUPS = D_MODEL // S5_GROUP
S5_STATE = 64
S5_SUPER = 4
S5_SG_CH = D_MODEL // S5_SUPER
S5_SG_ST = (S5_GROUPS // S5_SUPER) * S5_STATE
RET_HEADS = 4
RET_DK = D_MODEL // RET_HEADS
RET_DV = 2 * D_MODEL // RET_HEADS
RET_CHUNK = 256
ROPE_BASE = 10000.0
GLA_HEADS = 4
GLA_DK = D_MODEL // 2 // GLA_HEADS
GLA_DV = D_MODEL // GLA_HEADS
GLA_GATE_RANK = 16
GLA_GATE_NORM = 16.0
GLA_CHUNK = 64
GLA_GROUP = 8
D_FF = ((8 * D_MODEL // 3 + 255) // 256) * 256
FFN_SUB = 256
FFN_ROW_TILE = 1024
S5_BU_BUFFERS = 4
LONG_CHUNK_GROUP = 2
SEQ_GROUP_BYTES = 8 * 2 ** 20
CONV_W = 3
NORM_EPS = 1e-6
ROW_TILE = 512
PROJ_ROW_TILE = 1024
SUBLANES = 8
V7X_VMEM_MB = 64
VMEM_MB = V7X_VMEM_MB - 12
LARGE_VMEM_MB = V7X_VMEM_MB - 6


def _cparams(*sem, vmem_mb=None):
    return pltpu.CompilerParams(dimension_semantics=sem, vmem_limit_bytes=(vmem_mb or VMEM_MB) * 2 ** 20)


def _const_spec(shape):
    zeros = (0,) * len(shape)
    return pl.BlockSpec(shape, lambda *_: zeros, pipeline_mode=pl.Buffered(1))


def _stacked_spec(shape, index):
    where = (index,) + (0,) * len(shape)
    return pl.BlockSpec((None,) + tuple(shape), lambda *_: where, pipeline_mode=pl.Buffered(1))


def _rms(x, g):
    return x * lax.rsqrt(jnp.mean(x * x, axis=-1, keepdims=True) + NORM_EPS) * g


def _sigmoid(x):
    return 1.0 / (1.0 + jnp.exp(-x))


def _silu(x):
    return x * _sigmoid(x)


def _gelu_tanh(x):
    return 0.5 * x * (1.0 + jnp.tanh(math.sqrt(2.0 / math.pi) * (x + 0.044715 * (x * x * x))))


def _log_sigmoid(x):
    return jnp.minimum(x, 0.0) - jnp.log(1.0 + jnp.exp(-jnp.abs(x)))


def _dot(a, b):
    return jnp.dot(a, b, preferred_element_type=F32)


def _dot_nt(a, b):
    return lax.dot_general(a, b, (((1,), (1,)), ((), ())), preferred_element_type=F32)


def _dot_tn(a, b):
    return lax.dot_general(a, b, (((0,), (0,)), ((), ())), preferred_element_type=F32)


def _s5_kernel(x_ref, h0re_ref, h0im_ref, g0_ref, g1_ref, are_ref, aim_ref, bre_ref, bim_ref, cc_ref, dsk_ref,
               wglu_ref, o_ref, hre_ref, him_ref, bu_sc, y_sc, bbd_ref, cbd_ref, *, bn, tt, x_bm):
    @pl.when(pl.program_id(0) == 0)
    def _():
        hre_ref[...] = h0re_ref[...]
        him_ref[...] = h0im_ref[...]
        ngl = S5_GROUPS // S5_SUPER

        def lane_tiler(k, n):
            r = lax.broadcasted_iota(jnp.int32, (k, n), 0)
            c = lax.broadcasted_iota(jnp.int32, (k, n), 1)
            return ((c & (k - 1)) == r).astype(BF16)

        tile_n = lane_tiler(S5_STATE, S5_SG_ST)
        tile_p = lane_tiler(S5_GROUP, S5_SG_CH)
        sh_p = S5_GROUP.bit_length() - 1
        sh_n = S5_STATE.bit_length() - 1
        rb = lax.broadcasted_iota(jnp.int32, (S5_SG_CH, S5_SG_ST), 0) >> sh_p
        cb = lax.broadcasted_iota(jnp.int32, (S5_SG_CH, S5_SG_ST), 1) >> sh_n
        rc = (lax.broadcasted_iota(jnp.int32, (2 * S5_SG_ST, S5_SG_CH), 0) >> sh_n) & (ngl - 1)
        cc = lax.broadcasted_iota(jnp.int32, (2 * S5_SG_ST, S5_SG_CH), 1) >> sh_p
        for sg in range(S5_SUPER):
            for part, ref in ((0, bre_ref), (1, bim_ref)):
                blk = _dot(ref[sg].astype(BF16), tile_n)
                bbd_ref[sg, :, S5_SG_ST * part:S5_SG_ST * (part + 1)] = jnp.where(rb == cb, blk, 0.0).astype(BF16)
            blk = _dot(cc_ref[sg].astype(BF16), tile_p)
            cbd_ref[sg] = jnp.where(rc == cc, blk, 0.0).astype(BF16)

    x = x_ref[...]
    if x_bm:
        x = jnp.swapaxes(x, 0, 1).reshape(tt * bn, D_MODEL)
    u = _rms(x, g0_ref[...])
    ub = u.astype(BF16)
    half = S5_SG_ST // 2
    for sg in range(S5_SUPER):
        buf = bu_sc.at[sg % S5_BU_BUFFERS]
        buf[...] = _dot(ub[:, S5_SG_CH * sg:S5_SG_CH * (sg + 1)], bbd_ref[sg])
        for hf in range(2):
            c_re = half * hf
            c_im = S5_SG_ST + half * hf
            a_lo = S5_SG_ST * sg + half * hf
            ar = jnp.broadcast_to(are_ref[:, a_lo:a_lo + half], (8, half))
            ai = jnp.broadcast_to(aim_ref[:, a_lo:a_lo + half], (8, half))
            for bt in range(bn // 8):
                r8 = 8 * bt
                hr = hre_ref[r8:r8 + 8, a_lo:a_lo + half]
                hi = him_ref[r8:r8 + 8, a_lo:a_lo + half]
                for t in range(tt):
                    r0 = t * bn + r8
                    bur = buf[r0:r0 + 8, c_re:c_re + half]
                    bui = buf[r0:r0 + 8, c_im:c_im + half]
                    hr, hi = ar * hr - ai * hi + bur, ar * hi + ai * hr + bui
                    buf[r0:r0 + 8, c_re:c_re + half] = hr
                    buf[r0:r0 + 8, c_im:c_im + half] = hi
                hre_ref[r8:r8 + 8, a_lo:a_lo + half] = hr
                him_ref[r8:r8 + 8, a_lo:a_lo + half] = hi
        y_sc[:, S5_SG_CH * sg:S5_SG_CH * (sg + 1)] = _dot(buf[...].astype(BF16), cbd_ref[sg])
    y = y_sc[...] + dsk_ref[...] * u
    z = _dot(_gelu_tanh(y).astype(BF16), wglu_ref[...])
    mix = z[:, :D_MODEL] * _sigmoid(z[:, D_MODEL:])
    o_ref[...] = x + _rms(mix, g1_ref[...])


def _s5_layer(x, bn, h0_re, h0_im, g0, g1, lam_re, lam_im, b_re, b_im, c_re, c_im, log_dt, d_skip, w_glu,
              j, x_bm=False):
    rows = x.shape[0]
    tt = min(rows // bn, max(1, ROW_TILE // bn))
    tile = tt * bn
    o_spec = pl.BlockSpec((tile, D_MODEL), lambda i: (i, 0))
    x_spec = o_spec
    if x_bm:
        assert bn == SUBLANES
        x = x.reshape(bn, rows // bn, D_MODEL)
        x_spec = pl.BlockSpec((bn, tt, D_MODEL), lambda i: (0, i, 0))
    lam = lax.complex(lam_re, lam_im)
    dt = jnp.exp(log_dt)[:, None]
    a_bar = jnp.exp(lam * dt)
    b_bar = ((a_bar - 1.0) / lam)[..., None] * lax.complex(b_re, b_im)
    bt = b_bar.transpose(0, 2, 1).reshape(S5_SUPER, S5_SG_CH, S5_STATE)
    bt_re, bt_im = jnp.real(bt), jnp.imag(bt)
    ct_re = c_re.transpose(0, 2, 1).reshape(S5_SUPER, S5_SG_ST, S5_GROUP)
    ct_im = c_im.transpose(0, 2, 1).reshape(S5_SUPER, S5_SG_ST, S5_GROUP)
    ct = jnp.concatenate([ct_re, -ct_im], axis=1)
    a_re = jnp.real(a_bar).reshape(1, S5_GROUPS * S5_STATE)
    a_im = jnp.imag(a_bar).reshape(1, S5_GROUPS * S5_STATE)
    nst = S5_GROUPS * S5_STATE

    out, new_re, new_im = pl.pallas_call(
        functools.partial(_s5_kernel, bn=bn, tt=tt, x_bm=x_bm),
        grid=(rows // tile,),
        in_specs=[
            x_spec,
            _const_spec((bn, nst)), _const_spec((bn, nst)),
            _const_spec((1, D_MODEL)), _const_spec((1, D_MODEL)),
            _const_spec((1, nst)), _const_spec((1, nst)),
            _const_spec((S5_SUPER, S5_SG_CH, S5_STATE)), _const_spec((S5_SUPER, S5_SG_CH, S5_STATE)),
            _const_spec((S5_SUPER, 2 * S5_SG_ST, S5_GROUP)),
            _const_spec((1, D_MODEL)),
            _stacked_spec((D_MODEL, 2 * D_MODEL), j),
        ],
        out_specs=[o_spec, _const_spec((bn, nst)), _const_spec((bn, nst))],
        out_shape=[jax.ShapeDtypeStruct((rows, D_MODEL), F32), jax.ShapeDtypeStruct((bn, nst), F32),
                   jax.ShapeDtypeStruct((bn, nst), F32)],
        scratch_shapes=[pltpu.VMEM((S5_BU_BUFFERS, tile, 2 * S5_SG_ST), F32), pltpu.VMEM((tile, D_MODEL), F32),
                        pltpu.VMEM((S5_SUPER, S5_SG_CH, 2 * S5_SG_ST), BF16),
                        pltpu.VMEM((S5_SUPER, 2 * S5_SG_ST, S5_SG_CH), BF16)],
        compiler_params=_cparams("arbitrary"),
        name="s5_layer",
    )(x, h0_re.reshape(bn, nst), h0_im.reshape(bn, nst), g0.reshape(1, -1), g1.reshape(1, -1), a_re, a_im,
      bt_re, bt_im, ct, d_skip.reshape(1, -1), w_glu)
    return out, new_re.reshape(bn, S5_GROUPS, S5_STATE), new_im.reshape(bn, S5_GROUPS, S5_STATE)


def _ffn_kernel(x_ref, cin_ref, g2_ref, g3_ref, wg_ref, wu_ref, wd_ref, cw_ref, cb_ref,
                o_ref, cout_ref, hdn_sc, gext_sc, carry_sc, *, bn, tile, x_bm, out_bm):
    i = pl.program_id(0)
    halo = 2 * bn
    base = gext_sc.shape[1] - tile

    @pl.when(i == 0)
    def _():
        carry_sc[...] = cin_ref[...]

    x = x_ref[...]
    if x_bm:
        x = jnp.swapaxes(x, 0, 1).reshape(tile, D_MODEL)
    h = _rms(x, g2_ref[...]).astype(BF16)
    for j in range(D_FF // FFN_SUB):
        sl = slice(FFN_SUB * j, FFN_SUB * (j + 1))
        buf = gext_sc.at[j % 2]
        gpre = _dot(h, wg_ref[:, sl])
        buf[base - halo:base, :] = carry_sc[:, sl]
        buf[base:base + tile, :] = gpre
        carry_sc[:, sl] = gpre[tile - halo:tile, :]
        gconv = cb_ref[:, sl] + cw_ref[0:1, sl] * buf[base - halo:base - halo + tile, :]
        gconv = gconv + cw_ref[1:2, sl] * buf[base - bn:base - bn + tile, :]
        gconv = gconv + cw_ref[2:3, sl] * gpre
        hdn_sc[:, sl] = (_silu(gconv) * _dot(h, wu_ref[:, sl])).astype(BF16)
    y = x + _rms(_dot(hdn_sc[...], wd_ref[...]), g3_ref[...])
    if out_bm:
        y = jnp.swapaxes(y.reshape(tile // bn, bn, D_MODEL), 0, 1)
    o_ref[...] = y

    @pl.when(i == pl.num_programs(0) - 1)
    def _():
        cout_ref[...] = carry_sc[...]


def _ffn_layer(x, bn, cin, g2, g3, w_gate, w_up, conv_w, conv_b, w_down, layer, x_bm=False, out_bm=False):
    rows = x.shape[0]
    tile = min(FFN_ROW_TILE, rows)
    tt = tile // bn
    halo = 2 * bn
    assert not (x_bm or out_bm) or bn == SUBLANES
    tm_spec = pl.BlockSpec((tile, D_MODEL), lambda i: (i, 0))
    bm_spec = pl.BlockSpec((bn, tt, D_MODEL), lambda i: (0, i, 0))
    if x_bm:
        x = x.reshape(bn, rows // bn, D_MODEL)
    out_shape = (bn, rows // bn, D_MODEL) if out_bm else (rows, D_MODEL)
    out, cout = pl.pallas_call(
        functools.partial(_ffn_kernel, bn=bn, tile=tile, x_bm=x_bm, out_bm=out_bm),
        grid=(rows // tile,),
        in_specs=[
            bm_spec if x_bm else tm_spec,
            _const_spec((halo, D_FF)),
            _const_spec((1, D_MODEL)), _const_spec((1, D_MODEL)),
            _stacked_spec((D_MODEL, D_FF), layer), _stacked_spec((D_MODEL, D_FF), layer),
            _stacked_spec((D_FF, D_MODEL), layer),
            _const_spec((CONV_W, D_FF)), _const_spec((1, D_FF)),
        ],
        out_specs=[bm_spec if out_bm else tm_spec, _const_spec((halo, D_FF))],
        out_shape=[jax.ShapeDtypeStruct(out_shape, F32), jax.ShapeDtypeStruct((halo, D_FF), F32)],
        scratch_shapes=[pltpu.VMEM((tile, D_FF), BF16),
                        pltpu.VMEM((2, halo + tile, FFN_SUB), F32),
                        pltpu.VMEM((halo, D_FF), F32)],
        compiler_params=_cparams("arbitrary", vmem_mb=LARGE_VMEM_MB),
        name="conv_ffn",
    )(x, cin, g2.reshape(1, -1), g3.reshape(1, -1), w_gate, w_up, w_down, conv_w, conv_b.reshape(1, -1))
    return out.reshape(rows, D_MODEL), cout


def _ret_proj_kernel(x_ref, cos_ref, sin_ref, g_ref, wq_ref, wk_ref, wv_ref, q_ref, k_ref, v_ref):
    h = _rms(x_ref[...], g_ref[...]).astype(BF16)
    cos = cos_ref[...]
    sin = sin_ref[...]
    hw = RET_DK // 2
    for w_ref, dst, scale in ((wq_ref, q_ref, 1.0), (wk_ref, k_ref, RET_DK ** -0.5)):
        p = _dot(h, w_ref[...])
        for hd in range(RET_HEADS):
            lo = RET_DK * hd
            t1 = p[:, lo:lo + hw]
            t2 = p[:, lo + hw:lo + 2 * hw]
            dst[:, lo:lo + hw] = ((t1 * cos - t2 * sin) * scale).astype(BF16)
            dst[:, lo + hw:lo + 2 * hw] = ((t2 * cos + t1 * sin) * scale).astype(BF16)
    v_ref[...] = _dot(h, wv_ref[...]).astype(BF16)


def _ret_chunk_kernel(q_ref, k_ref, v_ref, s0_ref, o_ref, sout_ref, *, chunk, group):
    @pl.when(pl.program_id(1) == 0)
    def _():
        sout_ref[...] = s0_ref[...]

    ri = lax.broadcasted_iota(jnp.int32, (chunk, chunk), 0)
    ci = lax.broadcasted_iota(jnp.int32, (chunk, chunk), 1)
    rel = (ri - ci).astype(F32)
    causal = ri >= ci
    idx = lax.broadcasted_iota(jnp.int32, (chunk, 1), 0).astype(F32)
    lgs = [math.log1p(-(2.0 ** (-5.0 - hd))) for hd in range(RET_HEADS)]
    units = [(g, hd) for g in range(group) for hd in range(RET_HEADS)]
    qsl = [slice(RET_DK * hd, RET_DK * (hd + 1)) for hd in range(RET_HEADS)]
    vsl = [slice(RET_DV * hd, RET_DV * (hd + 1)) for hd in range(RET_HEADS)]
    decay = [jnp.where(causal, jnp.exp(jnp.maximum(rel, 0.0) * lg), 0.0) for lg in lgs]
    scores = [(_dot_nt(q_ref[g, :, qsl[hd]], k_ref[g, :, qsl[hd]]) * decay[hd]).astype(BF16) for g, hd in units]
    for (g, hd), sc in zip(units, scores):
        q_decay = jnp.exp((idx + 1.0) * lgs[hd])
        inter = _dot(q_ref[g, :, qsl[hd]], sout_ref[g, hd].astype(BF16)) * q_decay
        o_ref[g, :, vsl[hd]] = _dot(sc, v_ref[g, :, vsl[hd]]) + inter
    for g, hd in units:
        k_decay = jnp.exp((chunk - 1.0 - idx) * lgs[hd])
        kd = (k_ref[g, :, qsl[hd]].astype(F32) * k_decay).astype(BF16)
        sout_ref[g, hd] = sout_ref[g, hd] * math.exp(chunk * lgs[hd]) + _dot_tn(kd, v_ref[g, :, vsl[hd]])


def _ret_out_kernel(x_ref, o_ref, g0_ref, g1_ref, wg_ref, wo_ref, y_ref, on_sc):
    x = x_ref[...]
    h = _rms(x, g0_ref[...]).astype(BF16)
    gate = _silu(_dot(h, wg_ref[...]))
    for hd in range(RET_HEADS):
        sl = slice(RET_DV * hd, RET_DV * (hd + 1))
        oh = o_ref[:, sl]
        mu = jnp.mean(oh, axis=-1, keepdims=True)
        ctr = oh - mu
        var = jnp.mean(ctr * ctr, axis=-1, keepdims=True)
        on_sc[:, sl] = (ctr * lax.rsqrt(var + NORM_EPS) * gate[:, sl]).astype(BF16)
    y_ref[...] = x + _rms(_dot(on_sc[...], wo_ref[...]), g1_ref[...])


def _row_grid_call(kernel, row_ins, const_ins, outs, rows, tile, scratch=(), name=None, tab_ins=(),
                   vmem_mb=None):
    nt = rows // tile
    in_specs = [pl.BlockSpec((tile, a.shape[1]), lambda i: (i, 0)) for a in row_ins]
    for a in tab_ins:
        ntab = a.shape[0] // tile
        in_specs.append(pl.BlockSpec((tile, a.shape[1]), lambda i, ntab=ntab: (i % ntab, 0)))
    in_specs += [_const_spec(a.shape) for a in const_ins]
    return pl.pallas_call(
        kernel,
        grid=(nt,),
        in_specs=in_specs,
        out_specs=[pl.BlockSpec((tile, n), lambda i: (i, 0)) for n, _ in outs],
        out_shape=[jax.ShapeDtypeStruct((rows, n), dt) for n, dt in outs],
        scratch_shapes=list(scratch),
        compiler_params=_cparams("arbitrary", vmem_mb=vmem_mb),
        name=name,
    )(*row_ins, *tab_ins, *const_ins)


def _to_seq(a, bn, l, time_major):
    n = a.shape[1]
    return a.reshape(l, bn, n).transpose(1, 0, 2) if time_major else a.reshape(bn, l, n)


def _from_seq(a, bn, l, time_major):
    a = a.transpose(1, 0, 2) if time_major else a
    return a.reshape(bn * l, a.shape[2])


def _seq_group(bn, chunk, full_chunk, state_bytes):
    if chunk == full_chunk:
        return min(bn, LONG_CHUNK_GROUP)
    grp = max(1, min(bn, SEQ_GROUP_BYTES // state_bytes))
    while bn % grp:
        grp -= 1
    return grp


def _seq_chunking(l, chunk):
    if l % chunk == 0:
        return chunk
    assert l < chunk
    return l


def _ret_layer(x, bn, l, time_major, pos0, s0, g0, g1, wq, wk, wv, wg, wo):
    rows = x.shape[0]
    tile = min(ROW_TILE, rows)
    ptile = min(PROJ_ROW_TILE, rows)
    half = RET_DK // 2
    freq = 1.0 / (ROPE_BASE ** jnp.linspace(0.0, 1.0, half, dtype=F32))
    pos = jnp.arange(pos0, pos0 + l, dtype=jnp.int32).astype(F32)
    ang = pos[:, None] * freq[None, :]
    cos, sin = jnp.cos(ang), jnp.sin(ang)
    if time_major:
        cos, sin = jnp.repeat(cos, bn, axis=0), jnp.repeat(sin, bn, axis=0)
    elif l < ptile:
        cos, sin = jnp.tile(cos, (ptile // l, 1)), jnp.tile(sin, (ptile // l, 1))
    q, k, v = _row_grid_call(
        _ret_proj_kernel, [x], [g0.reshape(1, -1), wq.astype(BF16), wk.astype(BF16), wv.astype(BF16)],
        [(RET_HEADS * RET_DK, BF16), (RET_HEADS * RET_DK, BF16), (RET_HEADS * RET_DV, BF16)],
        rows, ptile, name="ret_proj", tab_ins=[cos, sin], vmem_mb=LARGE_VMEM_MB)
    chunk = _seq_chunking(l, RET_CHUNK)
    q3, k3, v3 = (_to_seq(a, bn, l, time_major) for a in (q, k, v))
    nc = l // chunk
    grp = _seq_group(bn, chunk, RET_CHUNK, RET_HEADS * RET_DK * RET_DV * 4)
    o3, s_new = pl.pallas_call(
        functools.partial(_ret_chunk_kernel, chunk=chunk, group=grp),
        grid=(bn // grp, nc),
        in_specs=[
            pl.BlockSpec((grp, chunk, RET_HEADS * RET_DK), lambda b, i: (b, i, 0)),
            pl.BlockSpec((grp, chunk, RET_HEADS * RET_DK), lambda b, i: (b, i, 0)),
            pl.BlockSpec((grp, chunk, RET_HEADS * RET_DV), lambda b, i: (b, i, 0)),
            pl.BlockSpec((grp, RET_HEADS, RET_DK, RET_DV), lambda b, i: (b, 0, 0, 0)),
        ],
        out_specs=[
            pl.BlockSpec((grp, chunk, RET_HEADS * RET_DV), lambda b, i: (b, i, 0)),
            pl.BlockSpec((grp, RET_HEADS, RET_DK, RET_DV), lambda b, i: (b, 0, 0, 0)),
        ],
        out_shape=[jax.ShapeDtypeStruct((bn, l, RET_HEADS * RET_DV), F32),
                   jax.ShapeDtypeStruct((bn, RET_HEADS, RET_DK, RET_DV), F32)],
        compiler_params=_cparams("arbitrary", "arbitrary"),
        name="ret_chunk",
    )(q3, k3, v3, s0)
    o = _from_seq(o3, bn, l, time_major)
    (y,) = _row_grid_call(
        _ret_out_kernel, [x, o], [g0.reshape(1, -1), g1.reshape(1, -1), wg.astype(BF16), wo.astype(BF16)],
        [(D_MODEL, F32)], rows, tile, scratch=[pltpu.VMEM((tile, RET_HEADS * RET_DV), BF16)], name="ret_out")
    return y, s_new


def _gla_proj_kernel(x_ref, g_ref, wq_ref, wk_ref, wv_ref, wa1_ref, wa2_ref, ba_ref,
                     q_ref, k_ref, v_ref, la_ref):
    h = _rms(x_ref[...], g_ref[...]).astype(BF16)
    q_ref[...] = (_dot(h, wq_ref[...]) * (GLA_DK ** -0.5)).astype(BF16)
    k_ref[...] = _dot(h, wk_ref[...]).astype(BF16)
    v_ref[...] = _dot(h, wv_ref[...]).astype(BF16)
    low = _dot(h, wa1_ref[...]).astype(BF16)
    logit = _dot(low, wa2_ref[...]) + ba_ref[...]
    la_ref[...] = _log_sigmoid(logit) / GLA_GATE_NORM


def _gla_chunk_kernel(q_ref, k_ref, v_ref, la_ref, s0_ref, o_ref, sout_ref, st_sc, *, chunk, group):
    i = pl.program_id(1)

    @pl.when(i == 0)
    def _():
        for g in range(group):
            for hd in range(GLA_HEADS):
                st_sc[g, hd] = s0_ref[g, hd]

    ri = lax.broadcasted_iota(jnp.int32, (chunk, chunk), 0)
    ci = lax.broadcasted_iota(jnp.int32, (chunk, chunk), 1)
    causal = ri >= ci
    tri = causal.astype(BF16)
    nk = GLA_HEADS * GLA_DK
    units = [(g, hd) for g in range(group) for hd in range(GLA_HEADS)]
    bcs = []
    for g in range(group):
        la = la_ref[g]
        p1 = la.astype(BF16)
        r1 = la - p1.astype(F32)
        p2 = r1.astype(BF16)
        p3 = (r1 - p2.astype(F32)).astype(BF16)
        acc = _dot(tri, jnp.concatenate([p1, p2, p3], axis=1))
        bcs.append(acc[:, :nk] + acc[:, nk:2 * nk] + acc[:, 2 * nk:])
    prep = []
    for g, hd in units:
        ksl = slice(GLA_DK * hd, GLA_DK * (hd + 1))
        bc = bcs[g][:, ksl]
        blast = bc[chunk - 1:chunk, :]
        ref = 0.5 * (bc[0:1, :] + blast)
        qh = q_ref[g, :, ksl].astype(F32)
        kh = k_ref[g, :, ksl].astype(F32)
        qt = (qh * jnp.exp(bc - ref)).astype(BF16)
        kt = (kh * jnp.exp(ref - bc)).astype(BF16)
        qg = (qh * jnp.exp(bc)).astype(BF16)
        kg = (kh * jnp.exp(blast - bc)).astype(BF16)
        row_decay = jnp.broadcast_to(jnp.exp(blast), (SUBLANES, GLA_DK)).T[:, :1]
        prep.append((qt, kt, qg, kg, row_decay))
    scores = [jnp.where(causal, _dot_nt(qt, kt), 0.0).astype(BF16) for qt, kt, _, _, _ in prep]
    for (g, hd), sc, (_, _, qg, kg, row_decay) in zip(units, scores, prep):
        vsl = slice(GLA_DV * hd, GLA_DV * (hd + 1))
        vh = v_ref[g, :, vsl]
        st = st_sc[g, hd]
        o_ref[g, :, vsl] = _dot(sc, vh) + _dot(qg, st.astype(BF16))
        st_sc[g, hd] = st * row_decay + _dot_tn(kg, vh)

    @pl.when(i == pl.num_programs(1) - 1)
    def _():
        for g in range(group):
            for hd in range(GLA_HEADS):
                sout_ref[g, hd] = st_sc[g, hd]


def _gla_out_kernel(x_ref, o_ref, g0_ref, g1_ref, ng_ref, wg_ref, wo_ref, y_ref, on_sc):
    x = x_ref[...]
    h = _rms(x, g0_ref[...]).astype(BF16)
    gate = _silu(_dot(h, wg_ref[...]))
    for hd in range(GLA_HEADS):
        sl = slice(GLA_DV * hd, GLA_DV * (hd + 1))
        oh = o_ref[:, sl]
        on = oh * lax.rsqrt(jnp.mean(oh * oh, axis=-1, keepdims=True) + NORM_EPS) * ng_ref[...]
        on_sc[:, sl] = (on * gate[:, sl]).astype(BF16)
    y_ref[...] = x + _rms(_dot(on_sc[...], wo_ref[...]), g1_ref[...])


def _gla_layer(x, bn, l, time_major, s0, g0, g1, wq, wk, wv, wg, wa1, wa2, ba, norm_g, wo):
    rows = x.shape[0]
    tile = min(PROJ_ROW_TILE, rows)
    lanes = 128
    wa1p = jnp.pad(wa1, ((0, 0), (0, lanes - GLA_GATE_RANK))).astype(BF16)
    wa2p = jnp.pad(wa2, ((0, lanes - GLA_GATE_RANK), (0, 0))).astype(BF16)
    q, k, v, la = _row_grid_call(
        _gla_proj_kernel, [x],
        [g0.reshape(1, -1), wq.astype(BF16), wk.astype(BF16), wv.astype(BF16), wa1p, wa2p, ba.reshape(1, -1)],
        [(GLA_HEADS * GLA_DK, BF16), (GLA_HEADS * GLA_DK, BF16), (GLA_HEADS * GLA_DV, BF16),
         (GLA_HEADS * GLA_DK, F32)],
        rows, tile, name="gla_proj", vmem_mb=LARGE_VMEM_MB)
    chunk = _seq_chunking(l, GLA_CHUNK)
    q3, k3, v3, la3 = (_to_seq(a, bn, l, time_major) for a in (q, k, v, la))
    nc = l // chunk
    grp = GLA_GROUP if bn % GLA_GROUP == 0 else 1
    o3, s_new = pl.pallas_call(
        functools.partial(_gla_chunk_kernel, chunk=chunk, group=grp),
        grid=(bn // grp, nc),
        in_specs=[
            pl.BlockSpec((grp, chunk, GLA_HEADS * GLA_DK), lambda b, i: (b, i, 0)),
            pl.BlockSpec((grp, chunk, GLA_HEADS * GLA_DK), lambda b, i: (b, i, 0)),
            pl.BlockSpec((grp, chunk, GLA_HEADS * GLA_DV), lambda b, i: (b, i, 0)),
            pl.BlockSpec((grp, chunk, GLA_HEADS * GLA_DK), lambda b, i: (b, i, 0)),
            pl.BlockSpec((grp, GLA_HEADS, GLA_DK, GLA_DV), lambda b, i: (b, 0, 0, 0)),
        ],
        out_specs=[
            pl.BlockSpec((grp, chunk, GLA_HEADS * GLA_DV), lambda b, i: (b, i, 0)),
            pl.BlockSpec((grp, GLA_HEADS, GLA_DK, GLA_DV), lambda b, i: (b, 0, 0, 0)),
        ],
        out_shape=[jax.ShapeDtypeStruct((bn, l, GLA_HEADS * GLA_DV), F32),
                   jax.ShapeDtypeStruct((bn, GLA_HEADS, GLA_DK, GLA_DV), F32)],
        scratch_shapes=[pltpu.VMEM((grp, GLA_HEADS, GLA_DK, GLA_DV), F32)],
        compiler_params=_cparams("arbitrary", "arbitrary"),
        name="gla_chunk",
    )(q3, k3, v3, la3, s0)
    o = _from_seq(o3, bn, l, time_major)
    (y,) = _row_grid_call(
        _gla_out_kernel, [x, o],
        [g0.reshape(1, -1), g1.reshape(1, -1), norm_g.reshape(1, -1), wg.astype(BF16), wo.astype(BF16)],
        [(D_MODEL, F32)], rows, tile, scratch=[pltpu.VMEM((tile, GLA_HEADS * GLA_DV), BF16)], name="gla_out",
        vmem_mb=LARGE_VMEM_MB)
    return y, s_new


def _transpose_rows(x, a, b):
    return x.reshape(a, b, x.shape[1]).transpose(1, 0, 2).reshape(a * b, x.shape[1])


def _trunk(x, pos0, s5_re, s5_im, ret_s, gla_s, conv_buf, p, batch_major_mixers):
    bn, l, _ = x.shape
    depth = p['norm_g'].shape[0]
    assert not batch_major_mixers or bn == SUBLANES
    rows = x.reshape(bn * l, D_MODEL)
    time_major = False
    if not batch_major_mixers:
        rows, time_major = _transpose_rows(rows, bn, l), True
    new_re, new_im, new_ret, new_gla, new_conv = [], [], [], [], []
    for layer in range(depth):
        g = p['norm_g'][layer]
        kind = layer % N_MIXERS
        j = layer // N_MIXERS
        if kind == 0:
            rows, hr, hi = _s5_layer(rows, bn, s5_re[j], s5_im[j], g[0], g[1],
                                     p['s5_lambda_re'][j], p['s5_lambda_im'][j], p['s5_b_re'][j], p['s5_b_im'][j],
                                     p['s5_c_re'][j], p['s5_c_im'][j], p['s5_log_dt'][j], p['s5_d'][j],
                                     p['s5_w_glu'], j, x_bm=not time_major)
            time_major = True
            new_re.append(hr)
            new_im.append(hi)
        elif kind == 1:
            rows, s = _ret_layer(rows, bn, l, time_major, pos0, ret_s[j], g[0], g[1], p['ret_wq'][j],
                                 p['ret_wk'][j], p['ret_wv'][j], p['ret_wg'][j], p['ret_wo'][j])
            new_ret.append(s)
        else:
            rows, s = _gla_layer(rows, bn, l, time_major, gla_s[j], g[0], g[1], p['gla_wq'][j], p['gla_wk'][j],
                                 p['gla_wv'][j], p['gla_wg'][j], p['gla_wa1'][j], p['gla_wa2'][j], p['gla_ba'][j],
                                 p['gla_norm_g'][j], p['gla_wo'][j])
            new_gla.append(s)
        next_is_s5 = layer + 1 < depth and (layer + 1) % N_MIXERS == 0
        out_bm = batch_major_mixers and not next_is_s5
        cin = conv_buf[layer].transpose(1, 0, 2).reshape((CONV_W - 1) * bn, D_FF)
        rows, cout = _ffn_layer(rows, bn, cin, g[2], g[3], p['ffn_w_gate'], p['ffn_w_up'],
                                p['ffn_conv_w'][layer], p['ffn_conv_b'][layer], p['ffn_w_down'], layer,
                                x_bm=not time_major, out_bm=out_bm)
        time_major = not out_bm
        new_conv.append(cout.reshape(CONV_W - 1, bn, D_FF).transpose(1, 0, 2))
    if time_major:
        rows = _transpose_rows(rows, l, bn)
    y = rows.reshape(bn, l, D_MODEL)
    return (y, jnp.stack(new_re), jnp.stack(new_im), jnp.stack(new_ret), jnp.stack(new_gla),
            jnp.stack(new_conv))


def kernel(x_prompt, x_sample, state_s5_re, state_s5_im, state_ret, state_gla, cache_ffn_conv,
           norm_g, s5_lambda_re, s5_lambda_im, s5_b_re, s5_b_im, s5_c_re, s5_c_im, s5_log_dt, s5_d, s5_w_glu,
           ret_wq, ret_wk, ret_wv, ret_wg, ret_wo,
           gla_wq, gla_wk, gla_wv, gla_wg, gla_wa1, gla_wa2, gla_ba, gla_norm_g, gla_wo,
           ffn_w_gate, ffn_w_up, ffn_conv_w, ffn_conv_b, ffn_w_down):
    p = dict(norm_g=norm_g, s5_lambda_re=s5_lambda_re, s5_lambda_im=s5_lambda_im, s5_b_re=s5_b_re,
             s5_b_im=s5_b_im, s5_c_re=s5_c_re, s5_c_im=s5_c_im, s5_log_dt=s5_log_dt, s5_d=s5_d,
             s5_w_glu=s5_w_glu.astype(BF16), ret_wq=ret_wq, ret_wk=ret_wk, ret_wv=ret_wv, ret_wg=ret_wg, ret_wo=ret_wo,
             gla_wq=gla_wq, gla_wk=gla_wk, gla_wv=gla_wv, gla_wg=gla_wg, gla_wa1=gla_wa1, gla_wa2=gla_wa2,
             gla_ba=gla_ba, gla_norm_g=gla_norm_g, gla_wo=gla_wo, ffn_w_gate=ffn_w_gate.astype(BF16),
             ffn_w_up=ffn_w_up.astype(BF16), ffn_conv_w=ffn_conv_w, ffn_conv_b=ffn_conv_b,
             ffn_w_down=ffn_w_down.astype(BF16))
    bp = x_prompt.shape[0]
    z_s5 = jnp.zeros((state_s5_re.shape[0], bp) + state_s5_re.shape[2:], F32)
    z_ret = jnp.zeros((state_ret.shape[0], bp) + state_ret.shape[2:], F32)
    z_gla = jnp.zeros((state_gla.shape[0], bp) + state_gla.shape[2:], F32)
    z_conv = jnp.zeros((cache_ffn_conv.shape[0], bp) + cache_ffn_conv.shape[2:], x_prompt.dtype)
    outs_p = _trunk(x_prompt, 0, z_s5, z_s5, z_ret, z_gla, z_conv, p, batch_major_mixers=True)
    outs_s = _trunk(x_sample, PAST_LEN, state_s5_re, state_s5_im, state_ret, state_gla, cache_ffn_conv, p,
                    batch_major_mixers=False)
    return (outs_p[0], outs_s[0]) + tuple(outs_p[1:]) + tuple(outs_s[1:])
```

```python
import functools
import math

import jax
import jax.numpy as jnp
from jax import lax
from jax.experimental import pallas as pl
from jax.experimental.pallas import tpu as pltpu

F32 = jnp.float32
BF16 = jnp.bfloat16

D_MODEL = 1024
PAST_LEN = 16384
N_MIXERS = 3
FFN_WEIGHTS = ('ffn_w_gate', 'ffn_w_up', 'ffn_w_down')
MIXER_WEIGHTS = ((), ('ret_wq', 'ret_wk', 'ret_wv', 'ret_wg', 'ret_wo'),
                 ('gla_wq', 'gla_wk', 'gla_wv', 'gla_wg', 'gla_wo'))
S5_GROUP = 16
S5_GROUPS = D_MODEL // S5_GROUP
S5_STATE = 64
S5_SUPER = 4
S5_SG_CH = D_MODEL // S5_SUPER
S5_SG_ST = (S5_GROUPS // S5_SUPER) * S5_STATE
RET_HEADS = 4
RET_DK = D_MODEL // RET_HEADS
RET_DV = 2 * D_MODEL // RET_HEADS
RET_CHUNK = 256
ROPE_BASE = 10000.0
GLA_HEADS = 4
GLA_DK = D_MODEL // 2 // GLA_HEADS
GLA_DV = D_MODEL // GLA_HEADS
GLA_GATE_RANK = 16
GLA_GATE_NORM = 16.0
GLA_CHUNK = 64
GLA_GROUP = 8
D_FF = ((8 * D_MODEL // 3 + 255) // 256) * 256
FFN_SUB = 256
FFN_ROW_TILE = 1024
S5_BU_BUFFERS = 4
LONG_CHUNK_GROUP = 2
SEQ_GROUP_BYTES = 8 * 2 ** 20
CONV_W = 3
NORM_EPS = 1e-6
ROW_TILE = 512
PROJ_ROW_TILE = 1024
SUBLANES = 8
V7X_VMEM_MB = 64
VMEM_MB = V7X_VMEM_MB - 12
LARGE_VMEM_MB = V7X_VMEM_MB - 6


def _cparams(*sem, vmem_mb=None):
    return pltpu.CompilerParams(dimension_semantics=sem, vmem_limit_bytes=(vmem_mb or VMEM_MB) * 2 ** 20)


def _const_spec(shape):
    zeros = (0,) * len(shape)
    return pl.BlockSpec(shape, lambda *_: zeros, pipeline_mode=pl.Buffered(1))


def _stacked_spec(shape, index):
    where = (index,) + (0,) * len(shape)
    return pl.BlockSpec((None,) + tuple(shape), lambda *_: where, pipeline_mode=pl.Buffered(1))


def _rms(x, g):
    return x * lax.rsqrt(jnp.mean(x * x, axis=-1, keepdims=True) + NORM_EPS) * g


def _sigmoid(x):
    return 1.0 / (1.0 + jnp.exp(-x))


def _silu(x):
    return x * _sigmoid(x)


def _gelu_tanh(x):
    return 0.5 * x * (1.0 + jnp.tanh(math.sqrt(2.0 / math.pi) * (x + 0.044715 * (x * x * x))))


def _log_sigmoid(x):
    return jnp.minimum(x, 0.0) - jnp.log(1.0 + jnp.exp(-jnp.abs(x)))


def _dot(a, b):
    return jnp.dot(a, b, preferred_element_type=F32)


def _dot_nt(a, b):
    return lax.dot_general(a, b, (((1,), (1,)), ((), ())), preferred_element_type=F32)


def _dot_tn(a, b):
    return lax.dot_general(a, b, (((0,), (0,)), ((), ())), preferred_element_type=F32)


def _s5_kernel(x_ref, h0re_ref, h0im_ref, g0_ref, g1_ref, are_ref, aim_ref, bre_ref, bim_ref, cc_ref, dsk_ref,
               wglu_ref, o_ref, hre_ref, him_ref, bu_sc, y_sc, bbd_ref, cbd_ref, *, bn, tt, x_bm):
    @pl.when(pl.program_id(0) == 0)
    def _():
        hre_ref[...] = h0re_ref[...]
        him_ref[...] = h0im_ref[...]
        ngl = S5_GROUPS // S5_SUPER

        def lane_tiler(k, n):
            r = lax.broadcasted_iota(jnp.int32, (k, n), 0)
            c = lax.broadcasted_iota(jnp.int32, (k, n), 1)
            return ((c & (k - 1)) == r).astype(BF16)

        tile_n = lane_tiler(S5_STATE, S5_SG_ST)
        tile_p = lane_tiler(S5_GROUP, S5_SG_CH)
        sh_p = S5_GROUP.bit_length() - 1
        sh_n = S5_STATE.bit_length() - 1
        rb = lax.broadcasted_iota(jnp.int32, (S5_SG_CH, S5_SG_ST), 0) >> sh_p
        cb = lax.broadcasted_iota(jnp.int32, (S5_SG_CH, S5_SG_ST), 1) >> sh_n
        rc = (lax.broadcasted_iota(jnp.int32, (2 * S5_SG_ST, S5_SG_CH), 0) >> sh_n) & (ngl - 1)
        cc = lax.broadcasted_iota(jnp.int32, (2 * S5_SG_ST, S5_SG_CH), 1) >> sh_p
        for sg in range(S5_SUPER):
            for part, ref in ((0, bre_ref), (1, bim_ref)):
                blk = _dot(ref[sg].astype(BF16), tile_n)
                bbd_ref[sg, :, S5_SG_ST * part:S5_SG_ST * (part + 1)] = jnp.where(rb == cb, blk, 0.0).astype(BF16)
            blk = _dot(cc_ref[sg].astype(BF16), tile_p)
            cbd_ref[sg] = jnp.where(rc == cc, blk, 0.0).astype(BF16)

    x = x_ref[...]
    if x_bm:
        x = jnp.swapaxes(x, 0, 1).reshape(tt * bn, D_MODEL)
    u = _rms(x, g0_ref[...])
    ub = u.astype(BF16)
    half = S5_SG_ST // 2
    for sg in range(S5_SUPER):
        buf = bu_sc.at[sg % S5_BU_BUFFERS]
        buf[...] = _dot(ub[:, S5_SG_CH * sg:S5_SG_CH * (sg + 1)], bbd_ref[sg])
        for hf in range(2):
            c_re = half * hf
            c_im = S5_SG_ST + half * hf
            a_lo = S5_SG_ST * sg + half * hf
            ar = jnp.broadcast_to(are_ref[:, a_lo:a_lo + half], (8, half))
            ai = jnp.broadcast_to(aim_ref[:, a_lo:a_lo + half], (8, half))
            for bt in range(bn // 8):
                r8 = 8 * bt
                hr = hre_ref[r8:r8 + 8, a_lo:a_lo + half]
                hi = him_ref[r8:r8 + 8, a_lo:a_lo + half]
                for t in range(tt):
                    r0 = t * bn + r8
                    bur = buf[r0:r0 + 8, c_re:c_re + half]
                    bui = buf[r0:r0 + 8, c_im:c_im + half]
                    hr, hi = ar * hr - ai * hi + bur, ar * hi + ai * hr + bui
                    buf[r0:r0 + 8, c_re:c_re + half] = hr
                    buf[r0:r0 + 8, c_im:c_im + half] = hi
                hre_ref[r8:r8 + 8, a_lo:a_lo + half] = hr
                him_ref[r8:r8 + 8, a_lo:a_lo + half] = hi
        y_sc[:, S5_SG_CH * sg:S5_SG_CH * (sg + 1)] = _dot(buf[...].astype(BF16), cbd_ref[sg])
    y = y_sc[...] + dsk_ref[...] * u
    z = _dot(_gelu_tanh(y).astype(BF16), wglu_ref[...])
    mix = z[:, :D_MODEL] * _sigmoid(z[:, D_MODEL:])
    o_ref[...] = x + _rms(mix, g1_ref[...])


def _s5_layer(x, bn, h0_re, h0_im, g0, g1, lam_re, lam_im, b_re, b_im, c_re, c_im, log_dt, d_skip, w_glu,
              j, x_bm=False):
    rows = x.shape[0]
    tt = min(rows // bn, max(1, ROW_TILE // bn))
    tile = tt * bn
    o_spec = pl.BlockSpec((tile, D_MODEL), lambda i: (i, 0))
    x_spec = o_spec
    if x_bm:
        assert bn == SUBLANES
        x = x.reshape(bn, rows // bn, D_MODEL)
        x_spec = pl.BlockSpec((bn, tt, D_MODEL), lambda i: (0, i, 0))
    lam = lax.complex(lam_re, lam_im)
    dt = jnp.exp(log_dt)[:, None]
    a_bar = jnp.exp(lam * dt)
    b_bar = ((a_bar - 1.0) / lam)[..., None] * lax.complex(b_re, b_im)
    bt = b_bar.transpose(0, 2, 1).reshape(S5_SUPER, S5_SG_CH, S5_STATE)
    bt_re, bt_im = jnp.real(bt), jnp.imag(bt)
    ct_re = c_re.transpose(0, 2, 1).reshape(S5_SUPER, S5_SG_ST, S5_GROUP)
    ct_im = c_im.transpose(0, 2, 1).reshape(S5_SUPER, S5_SG_ST, S5_GROUP)
    ct = jnp.concatenate([ct_re, -ct_im], axis=1)
    a_re = jnp.real(a_bar).reshape(1, S5_GROUPS * S5_STATE)
    a_im = jnp.imag(a_bar).reshape(1, S5_GROUPS * S5_STATE)
    nst = S5_GROUPS * S5_STATE

    out, new_re, new_im = pl.pallas_call(
        functools.partial(_s5_kernel, bn=bn, tt=tt, x_bm=x_bm),
        grid=(rows // tile,),
        in_specs=[
            x_spec,
            _const_spec((bn, nst)), _const_spec((bn, nst)),
            _const_spec((1, D_MODEL)), _const_spec((1, D_MODEL)),
            _const_spec((1, nst)), _const_spec((1, nst)),
            _const_spec((S5_SUPER, S5_SG_CH, S5_STATE)), _const_spec((S5_SUPER, S5_SG_CH, S5_STATE)),
            _const_spec((S5_SUPER, 2 * S5_SG_ST, S5_GROUP)),
            _const_spec((1, D_MODEL)),
            _stacked_spec((D_MODEL, 2 * D_MODEL), j),
        ],
        out_specs=[o_spec, _const_spec((bn, nst)), _const_spec((bn, nst))],
        out_shape=[jax.ShapeDtypeStruct((rows, D_MODEL), F32), jax.ShapeDtypeStruct((bn, nst), F32),
                   jax.ShapeDtypeStruct((bn, nst), F32)],
        scratch_shapes=[pltpu.VMEM((S5_BU_BUFFERS, tile, 2 * S5_SG_ST), F32), pltpu.VMEM((tile, D_MODEL), F32),
                        pltpu.VMEM((S5_SUPER, S5_SG_CH, 2 * S5_SG_ST), BF16),
                        pltpu.VMEM((S5_SUPER, 2 * S5_SG_ST, S5_SG_CH), BF16)],
        compiler_params=_cparams("arbitrary"),
        name="s5_layer",
    )(x, h0_re.reshape(bn, nst), h0_im.reshape(bn, nst), g0.reshape(1, -1), g1.reshape(1, -1), a_re, a_im,
      bt_re, bt_im, ct, d_skip.reshape(1, -1), w_glu)
    return out, new_re.reshape(bn, S5_GROUPS, S5_STATE), new_im.reshape(bn, S5_GROUPS, S5_STATE)


def _ffn_kernel(x_ref, cin_ref, g2_ref, g3_ref, wg_ref, wu_ref, wd_ref, cw_ref, cb_ref, *rest,
                bn, tile, x_bm, out_bm, cast_next):
    next_f32, rest = rest[:cast_next], rest[cast_next:]
    o_ref, cout_ref, rest = rest[0], rest[1], rest[2:]
    next_bf16, (hdn_sc, gext_sc, carry_sc) = rest[:cast_next], rest[cast_next:]
    i = pl.program_id(0)
    halo = 2 * bn
    base = gext_sc.shape[1] - tile

    @pl.when(i == 0)
    def _():
        carry_sc[...] = cin_ref[...]

    x = x_ref[...]
    if x_bm:
        x = jnp.swapaxes(x, 0, 1).reshape(tile, D_MODEL)
    h = _rms(x, g2_ref[...]).astype(BF16)
    for j in range(D_FF // FFN_SUB):
        sl = slice(FFN_SUB * j, FFN_SUB * (j + 1))
        buf = gext_sc.at[j % 2]
        gpre = _dot(h, wg_ref[:, sl])
        buf[base - halo:base, :] = carry_sc[:, sl]
        buf[base:base + tile, :] = gpre
        carry_sc[:, sl] = gpre[tile - halo:tile, :]
        gconv = cb_ref[:, sl] + cw_ref[0:1, sl] * buf[base - halo:base - halo + tile, :]
        gconv = gconv + cw_ref[1:2, sl] * buf[base - bn:base - bn + tile, :]
        gconv = gconv + cw_ref[2:3, sl] * gpre
        hdn_sc[:, sl] = (_silu(gconv) * _dot(h, wu_ref[:, sl])).astype(BF16)
        if j == 0:
            for src, dst in zip(next_f32, next_bf16):
                dst[...] = src[...].astype(BF16)
    y = x + _rms(_dot(hdn_sc[...], wd_ref[...]), g3_ref[...])
    if out_bm:
        y = jnp.swapaxes(y.reshape(tile // bn, bn, D_MODEL), 0, 1)
    o_ref[...] = y

    @pl.when(i == pl.num_programs(0) - 1)
    def _():
        cout_ref[...] = carry_sc[...]


def _ffn_layer(x, bn, cin, g2, g3, w_gate, w_up, conv_w, conv_b, w_down, x_bm=False, out_bm=False,
               next_weights=()):
    rows = x.shape[0]
    tile = min(FFN_ROW_TILE, rows)
    tt = tile // bn
    halo = 2 * bn
    steps = rows // tile
    cast_ins, cast_in_specs, cast_out_specs, cast_out_shapes = [], [], [], []
    assert w_gate.dtype == w_up.dtype == w_down.dtype == BF16
    for w, idx in next_weights:
        k, n = w.shape[1:]
        kb = k // steps
        assert kb * steps == k and kb % (2 * SUBLANES) == 0
        cast_ins.append(w)
        cast_in_specs.append(pl.BlockSpec((None, kb, n), lambda i, idx=idx: (idx, i, 0)))
        cast_out_specs.append(pl.BlockSpec((kb, n), lambda i: (i, 0)))
        cast_out_shapes.append(jax.ShapeDtypeStruct((k, n), BF16))
    assert not (x_bm or out_bm) or bn == SUBLANES
    tm_spec = pl.BlockSpec((tile, D_MODEL), lambda i: (i, 0))
    bm_spec = pl.BlockSpec((bn, tt, D_MODEL), lambda i: (0, i, 0))
    if x_bm:
        x = x.reshape(bn, rows // bn, D_MODEL)
    out_shape = (bn, rows // bn, D_MODEL) if out_bm else (rows, D_MODEL)
    out, cout, *cast = pl.pallas_call(
        functools.partial(_ffn_kernel, bn=bn, tile=tile, x_bm=x_bm, out_bm=out_bm, cast_next=len(cast_ins)),
        grid=(steps,),
        in_specs=[
            bm_spec if x_bm else tm_spec,
            _const_spec((halo, D_FF)),
            _const_spec((1, D_MODEL)), _const_spec((1, D_MODEL)),
            _const_spec((D_MODEL, D_FF)), _const_spec((D_MODEL, D_FF)), _const_spec((D_FF, D_MODEL)),
            _const_spec((CONV_W, D_FF)), _const_spec((1, D_FF)),
        ] + cast_in_specs,
        out_specs=[bm_spec if out_bm else tm_spec, _const_spec((halo, D_FF))] + cast_out_specs,
        out_shape=[jax.ShapeDtypeStruct(out_shape, F32), jax.ShapeDtypeStruct((halo, D_FF), F32)] + cast_out_shapes,
        scratch_shapes=[pltpu.VMEM((tile, D_FF), BF16),
                        pltpu.VMEM((2, halo + tile, FFN_SUB), F32),
                        pltpu.VMEM((halo, D_FF), F32)],
        compiler_params=_cparams("arbitrary", vmem_mb=LARGE_VMEM_MB),
        name="conv_ffn",
    )(x, cin, g2.reshape(1, -1), g3.reshape(1, -1), w_gate, w_up, w_down, conv_w, conv_b.reshape(1, -1), *cast_ins)
    return out.reshape(rows, D_MODEL), cout, tuple(cast)


def _ret_proj_kernel(x_ref, cos_ref, sin_ref, g_ref, wq_ref, wk_ref, wv_ref, q_ref, k_ref, v_ref):
    h = _rms(x_ref[...], g_ref[...]).astype(BF16)
    cos = cos_ref[...]
    sin = sin_ref[...]
    hw = RET_DK // 2
    for w_ref, dst, scale in ((wq_ref, q_ref, 1.0), (wk_ref, k_ref, RET_DK ** -0.5)):
        p = _dot(h, w_ref[...])
        for hd in range(RET_HEADS):
            lo = RET_DK * hd
            t1 = p[:, lo:lo + hw]
            t2 = p[:, lo + hw:lo + 2 * hw]
            dst[:, lo:lo + hw] = ((t1 * cos - t2 * sin) * scale).astype(BF16)
            dst[:, lo + hw:lo + 2 * hw] = ((t2 * cos + t1 * sin) * scale).astype(BF16)
    v_ref[...] = _dot(h, wv_ref[...]).astype(BF16)


def _ret_chunk_kernel(q_ref, k_ref, v_ref, s0_ref, o_ref, sout_ref, *, chunk, group):
    @pl.when(pl.program_id(1) == 0)
    def _():
        sout_ref[...] = s0_ref[...]

    ri = lax.broadcasted_iota(jnp.int32, (chunk, chunk), 0)
    ci = lax.broadcasted_iota(jnp.int32, (chunk, chunk), 1)
    rel = (ri - ci).astype(F32)
    causal = ri >= ci
    idx = lax.broadcasted_iota(jnp.int32, (chunk, 1), 0).astype(F32)
    lgs = [math.log1p(-(2.0 ** (-5.0 - hd))) for hd in range(RET_HEADS)]
    units = [(g, hd) for g in range(group) for hd in range(RET_HEADS)]
    qsl = [slice(RET_DK * hd, RET_DK * (hd + 1)) for hd in range(RET_HEADS)]
    vsl = [slice(RET_DV * hd, RET_DV * (hd + 1)) for hd in range(RET_HEADS)]
    decay = [jnp.where(causal, jnp.exp(jnp.maximum(rel, 0.0) * lg), 0.0) for lg in lgs]
    scores = [(_dot_nt(q_ref[g, :, qsl[hd]], k_ref[g, :, qsl[hd]]) * decay[hd]).astype(BF16) for g, hd in units]
    for (g, hd), sc in zip(units, scores):
        q_decay = jnp.exp((idx + 1.0) * lgs[hd])
        inter = _dot(q_ref[g, :, qsl[hd]], sout_ref[g, hd].astype(BF16)) * q_decay
        o_ref[g, :, vsl[hd]] = _dot(sc, v_ref[g, :, vsl[hd]]) + inter
    for g, hd in units:
        k_decay = jnp.exp((chunk - 1.0 - idx) * lgs[hd])
        kd = (k_ref[g, :, qsl[hd]].astype(F32) * k_decay).astype(BF16)
        sout_ref[g, hd] = sout_ref[g, hd] * math.exp(chunk * lgs[hd]) + _dot_tn(kd, v_ref[g, :, vsl[hd]])


def _ret_out_kernel(x_ref, o_ref, g0_ref, g1_ref, wg_ref, wo_ref, y_ref, on_sc):
    x = x_ref[...]
    h = _rms(x, g0_ref[...]).astype(BF16)
    gate = _silu(_dot(h, wg_ref[...]))
    for hd in range(RET_HEADS):
        sl = slice(RET_DV * hd, RET_DV * (hd + 1))
        oh = o_ref[:, sl]
        mu = jnp.mean(oh, axis=-1, keepdims=True)
        ctr = oh - mu
        var = jnp.mean(ctr * ctr, axis=-1, keepdims=True)
        on_sc[:, sl] = (ctr * lax.rsqrt(var + NORM_EPS) * gate[:, sl]).astype(BF16)
    y_ref[...] = x + _rms(_dot(on_sc[...], wo_ref[...]), g1_ref[...])


def _row_grid_call(kernel, row_ins, const_ins, outs, rows, tile, scratch=(), name=None, tab_ins=(),
                   vmem_mb=None):
    nt = rows // tile
    in_specs = [pl.BlockSpec((tile, a.shape[1]), lambda i: (i, 0)) for a in row_ins]
    for a in tab_ins:
        ntab = a.shape[0] // tile
        in_specs.append(pl.BlockSpec((tile, a.shape[1]), lambda i, ntab=ntab: (i % ntab, 0)))
    in_specs += [_const_spec(a.shape) for a in const_ins]
    return pl.pallas_call(
        kernel,
        grid=(nt,),
        in_specs=in_specs,
        out_specs=[pl.BlockSpec((tile, n), lambda i: (i, 0)) for n, _ in outs],
        out_shape=[jax.ShapeDtypeStruct((rows, n), dt) for n, dt in outs],
        scratch_shapes=list(scratch),
        compiler_params=_cparams("arbitrary", vmem_mb=vmem_mb),
        name=name,
    )(*row_ins, *tab_ins, *const_ins)


def _to_seq(a, bn, l, time_major):
    n = a.shape[1]
    return a.reshape(l, bn, n).transpose(1, 0, 2) if time_major else a.reshape(bn, l, n)


def _from_seq(a, bn, l, time_major):
    a = a.transpose(1, 0, 2) if time_major else a
    return a.reshape(bn * l, a.shape[2])


def _seq_group(bn, chunk, full_chunk, state_bytes):
    if chunk == full_chunk:
        return min(bn, LONG_CHUNK_GROUP)
    grp = max(1, min(bn, SEQ_GROUP_BYTES // state_bytes))
    while bn % grp:
        grp -= 1
    return grp


def _seq_chunking(l, chunk):
    if l % chunk == 0:
        return chunk
    assert l < chunk
    return l


def _ret_layer(x, bn, l, time_major, pos0, s0, g0, g1, wq, wk, wv, wg, wo):
    rows = x.shape[0]
    tile = min(ROW_TILE, rows)
    ptile = min(PROJ_ROW_TILE, rows)
    half = RET_DK // 2
    freq = 1.0 / (ROPE_BASE ** jnp.linspace(0.0, 1.0, half, dtype=F32))
    pos = jnp.arange(pos0, pos0 + l, dtype=jnp.int32).astype(F32)
    ang = pos[:, None] * freq[None, :]
    cos, sin = jnp.cos(ang), jnp.sin(ang)
    if time_major:
        cos, sin = jnp.repeat(cos, bn, axis=0), jnp.repeat(sin, bn, axis=0)
    elif l < ptile:
        cos, sin = jnp.tile(cos, (ptile // l, 1)), jnp.tile(sin, (ptile // l, 1))
    q, k, v = _row_grid_call(
        _ret_proj_kernel, [x], [g0.reshape(1, -1), wq.astype(BF16), wk.astype(BF16), wv.astype(BF16)],
        [(RET_HEADS * RET_DK, BF16), (RET_HEADS * RET_DK, BF16), (RET_HEADS * RET_DV, BF16)],
        rows, ptile, name="ret_proj", tab_ins=[cos, sin], vmem_mb=LARGE_VMEM_MB)
    chunk = _seq_chunking(l, RET_CHUNK)
    q3, k3, v3 = (_to_seq(a, bn, l, time_major) for a in (q, k, v))
    nc = l // chunk
    grp = _seq_group(bn, chunk, RET_CHUNK, RET_HEADS * RET_DK * RET_DV * 4)
    o3, s_new = pl.pallas_call(
        functools.partial(_ret_chunk_kernel, chunk=chunk, group=grp),
        grid=(bn // grp, nc),
        in_specs=[
            pl.BlockSpec((grp, chunk, RET_HEADS * RET_DK), lambda b, i: (b, i, 0)),
            pl.BlockSpec((grp, chunk, RET_HEADS * RET_DK), lambda b, i: (b, i, 0)),
            pl.BlockSpec((grp, chunk, RET_HEADS * RET_DV), lambda b, i: (b, i, 0)),
            pl.BlockSpec((grp, RET_HEADS, RET_DK, RET_DV), lambda b, i: (b, 0, 0, 0)),
        ],
        out_specs=[
            pl.BlockSpec((grp, chunk, RET_HEADS * RET_DV), lambda b, i: (b, i, 0)),
            pl.BlockSpec((grp, RET_HEADS, RET_DK, RET_DV), lambda b, i: (b, 0, 0, 0)),
        ],
        out_shape=[jax.ShapeDtypeStruct((bn, l, RET_HEADS * RET_DV), F32),
                   jax.ShapeDtypeStruct((bn, RET_HEADS, RET_DK, RET_DV), F32)],
        compiler_params=_cparams("arbitrary", "arbitrary"),
        name="ret_chunk",
    )(q3, k3, v3, s0)
    o = _from_seq(o3, bn, l, time_major)
    (y,) = _row_grid_call(
        _ret_out_kernel, [x, o], [g0.reshape(1, -1), g1.reshape(1, -1), wg.astype(BF16), wo.astype(BF16)],
        [(D_MODEL, F32)], rows, tile, scratch=[pltpu.VMEM((tile, RET_HEADS * RET_DV), BF16)], name="ret_out")
    return y, s_new


def _gla_proj_kernel(x_ref, g_ref, wq_ref, wk_ref, wv_ref, wa1_ref, wa2_ref, ba_ref,
                     q_ref, k_ref, v_ref, la_ref):
    h = _rms(x_ref[...], g_ref[...]).astype(BF16)
    q_ref[...] = (_dot(h, wq_ref[...]) * (GLA_DK ** -0.5)).astype(BF16)
    k_ref[...] = _dot(h, wk_ref[...]).astype(BF16)
    v_ref[...] = _dot(h, wv_ref[...]).astype(BF16)
    low = _dot(h, wa1_ref[...]).astype(BF16)
    logit = _dot(low, wa2_ref[...]) + ba_ref[...]
    la_ref[...] = _log_sigmoid(logit) / GLA_GATE_NORM


def _gla_chunk_kernel(q_ref, k_ref, v_ref, la_ref, s0_ref, o_ref, sout_ref, st_sc, *, chunk, group):
    i = pl.program_id(1)

    @pl.when(i == 0)
    def _():
        for g in range(group):
            for hd in range(GLA_HEADS):
                st_sc[g, hd] = s0_ref[g, hd]

    ri = lax.broadcasted_iota(jnp.int32, (chunk, chunk), 0)
    ci = lax.broadcasted_iota(jnp.int32, (chunk, chunk), 1)
    causal = ri >= ci
    tri = causal.astype(BF16)
    nk = GLA_HEADS * GLA_DK
    units = [(g, hd) for g in range(group) for hd in range(GLA_HEADS)]
    bcs = []
    for g in range(group):
        la = la_ref[g]
        p1 = la.astype(BF16)
        r1 = la - p1.astype(F32)
        p2 = r1.astype(BF16)
        p3 = (r1 - p2.astype(F32)).astype(BF16)
        acc = _dot(tri, jnp.concatenate([p1, p2, p3], axis=1))
        bcs.append(acc[:, :nk] + acc[:, nk:2 * nk] + acc[:, 2 * nk:])
    prep = []
    for g, hd in units:
        ksl = slice(GLA_DK * hd, GLA_DK * (hd + 1))
        bc = bcs[g][:, ksl]
        blast = bc[chunk - 1:chunk, :]
        ref = 0.5 * (bc[0:1, :] + blast)
        qh = q_ref[g, :, ksl].astype(F32)
        kh = k_ref[g, :, ksl].astype(F32)
        qt = (qh * jnp.exp(bc - ref)).astype(BF16)
        kt = (kh * jnp.exp(ref - bc)).astype(BF16)
        qg = (qh * jnp.exp(bc)).astype(BF16)
        kg = (kh * jnp.exp(blast - bc)).astype(BF16)
        row_decay = jnp.broadcast_to(jnp.exp(blast), (SUBLANES, GLA_DK)).T[:, :1]
        prep.append((qt, kt, qg, kg, row_decay))
    scores = [jnp.where(causal, _dot_nt(qt, kt), 0.0).astype(BF16) for qt, kt, _, _, _ in prep]
    for (g, hd), sc, (_, _, qg, kg, row_decay) in zip(units, scores, prep):
        vsl = slice(GLA_DV * hd, GLA_DV * (hd + 1))
        vh = v_ref[g, :, vsl]
        st = st_sc[g, hd]
        o_ref[g, :, vsl] = _dot(sc, vh) + _dot(qg, st.astype(BF16))
        st_sc[g, hd] = st * row_decay + _dot_tn(kg, vh)

    @pl.when(i == pl.num_programs(1) - 1)
    def _():
        for g in range(group):
            for hd in range(GLA_HEADS):
                sout_ref[g, hd] = st_sc[g, hd]


def _gla_out_kernel(x_ref, o_ref, g0_ref, g1_ref, ng_ref, wg_ref, wo_ref, y_ref, on_sc):
    x = x_ref[...]
    h = _rms(x, g0_ref[...]).astype(BF16)
    gate = _silu(_dot(h, wg_ref[...]))
    for hd in range(GLA_HEADS):
        sl = slice(GLA_DV * hd, GLA_DV * (hd + 1))
        oh = o_ref[:, sl]
        on = oh * lax.rsqrt(jnp.mean(oh * oh, axis=-1, keepdims=True) + NORM_EPS) * ng_ref[...]
        on_sc[:, sl] = (on * gate[:, sl]).astype(BF16)
    y_ref[...] = x + _rms(_dot(on_sc[...], wo_ref[...]), g1_ref[...])


def _gla_layer(x, bn, l, time_major, s0, g0, g1, wq, wk, wv, wg, wa1, wa2, ba, norm_g, wo):
    rows = x.shape[0]
    tile = min(PROJ_ROW_TILE, rows)
    lanes = 128
    wa1p = jnp.pad(wa1, ((0, 0), (0, lanes - GLA_GATE_RANK))).astype(BF16)
    wa2p = jnp.pad(wa2, ((0, lanes - GLA_GATE_RANK), (0, 0))).astype(BF16)
    q, k, v, la = _row_grid_call(
        _gla_proj_kernel, [x],
        [g0.reshape(1, -1), wq.astype(BF16), wk.astype(BF16), wv.astype(BF16), wa1p, wa2p, ba.reshape(1, -1)],
        [(GLA_HEADS * GLA_DK, BF16), (GLA_HEADS * GLA_DK, BF16), (GLA_HEADS * GLA_DV, BF16),
         (GLA_HEADS * GLA_DK, F32)],
        rows, tile, name="gla_proj", vmem_mb=LARGE_VMEM_MB)
    chunk = _seq_chunking(l, GLA_CHUNK)
    q3, k3, v3, la3 = (_to_seq(a, bn, l, time_major) for a in (q, k, v, la))
    nc = l // chunk
    grp = GLA_GROUP if bn % GLA_GROUP == 0 else 1
    o3, s_new = pl.pallas_call(
        functools.partial(_gla_chunk_kernel, chunk=chunk, group=grp),
        grid=(bn // grp, nc),
        in_specs=[
            pl.BlockSpec((grp, chunk, GLA_HEADS * GLA_DK), lambda b, i: (b, i, 0)),
            pl.BlockSpec((grp, chunk, GLA_HEADS * GLA_DK), lambda b, i: (b, i, 0)),
            pl.BlockSpec((grp, chunk, GLA_HEADS * GLA_DV), lambda b, i: (b, i, 0)),
            pl.BlockSpec((grp, chunk, GLA_HEADS * GLA_DK), lambda b, i: (b, i, 0)),
            pl.BlockSpec((grp, GLA_HEADS, GLA_DK, GLA_DV), lambda b, i: (b, 0, 0, 0)),
        ],
        out_specs=[
            pl.BlockSpec((grp, chunk, GLA_HEADS * GLA_DV), lambda b, i: (b, i, 0)),
            pl.BlockSpec((grp, GLA_HEADS, GLA_DK, GLA_DV), lambda b, i: (b, 0, 0, 0)),
        ],
        out_shape=[jax.ShapeDtypeStruct((bn, l, GLA_HEADS * GLA_DV), F32),
                   jax.ShapeDtypeStruct((bn, GLA_HEADS, GLA_DK, GLA_DV), F32)],
        scratch_shapes=[pltpu.VMEM((grp, GLA_HEADS, GLA_DK, GLA_DV), F32)],
        compiler_params=_cparams("arbitrary", "arbitrary"),
        name="gla_chunk",
    )(q3, k3, v3, la3, s0)
    o = _from_seq(o3, bn, l, time_major)
    (y,) = _row_grid_call(
        _gla_out_kernel, [x, o],
        [g0.reshape(1, -1), g1.reshape(1, -1), norm_g.reshape(1, -1), wg.astype(BF16), wo.astype(BF16)],
        [(D_MODEL, F32)], rows, tile, scratch=[pltpu.VMEM((tile, GLA_HEADS * GLA_DV), BF16)], name="gla_out",
        vmem_mb=LARGE_VMEM_MB)
    return y, s_new


def _transpose_rows(x, a, b):
    return x.reshape(a, b, x.shape[1]).transpose(1, 0, 2).reshape(a * b, x.shape[1])


def _trunk(x, pos0, s5_re, s5_im, ret_s, gla_s, conv_buf, p, batch_major_mixers, bf16_w):
    bf16_w = dict(bf16_w)

    def w(name, idx):
        return bf16_w.get((name, idx), p[name][idx])
    bn, l, _ = x.shape
    depth = p['norm_g'].shape[0]
    assert not batch_major_mixers or bn == SUBLANES
    rows = x.reshape(bn * l, D_MODEL)
    time_major = False
    if not batch_major_mixers:
        rows, time_major = _transpose_rows(rows, bn, l), True
    new_re, new_im, new_ret, new_gla, new_conv = [], [], [], [], []
    for layer in range(depth):
        g = p['norm_g'][layer]
        kind = layer % N_MIXERS
        j = layer // N_MIXERS
        if kind == 0:
            rows, hr, hi = _s5_layer(rows, bn, s5_re[j], s5_im[j], g[0], g[1],
                                     p['s5_lambda_re'][j], p['s5_lambda_im'][j], p['s5_b_re'][j], p['s5_b_im'][j],
                                     p['s5_c_re'][j], p['s5_c_im'][j], p['s5_log_dt'][j], p['s5_d'][j],
                                     p['s5_w_glu'], j, x_bm=not time_major)
            time_major = True
            new_re.append(hr)
            new_im.append(hi)
        elif kind == 1:
            rows, s = _ret_layer(rows, bn, l, time_major, pos0, ret_s[j], g[0], g[1], w('ret_wq', j),
                                 w('ret_wk', j), w('ret_wv', j), w('ret_wg', j), w('ret_wo', j))
            new_ret.append(s)
        else:
            rows, s = _gla_layer(rows, bn, l, time_major, gla_s[j], g[0], g[1], w('gla_wq', j), w('gla_wk', j),
                                 w('gla_wv', j), w('gla_wg', j), p['gla_wa1'][j], p['gla_wa2'][j], p['gla_ba'][j],
                                 p['gla_norm_g'][j], w('gla_wo', j))
            new_gla.append(s)
        next_is_s5 = layer + 1 < depth and (layer + 1) % N_MIXERS == 0
        out_bm = batch_major_mixers and not next_is_s5
        cin = conv_buf[layer].transpose(1, 0, 2).reshape((CONV_W - 1) * bn, D_FF)
        todo = []
        if layer + 1 < depth:
            mixer = MIXER_WEIGHTS[(layer + 1) % N_MIXERS]
            todo = [(n, layer + 1) for n in FFN_WEIGHTS] + [(n, (layer + 1) // N_MIXERS) for n in mixer]
            todo = [key for key in todo if key not in bf16_w]
        rows, cout, cast = _ffn_layer(rows, bn, cin, g[2], g[3], w('ffn_w_gate', layer), w('ffn_w_up', layer),
                                      p['ffn_conv_w'][layer], p['ffn_conv_b'][layer], w('ffn_w_down', layer),
                                      x_bm=not time_major, out_bm=out_bm,
                                      next_weights=[(p[n], idx) for n, idx in todo])
        bf16_w.update(zip(todo, cast))
        time_major = not out_bm
        new_conv.append(cout.reshape(CONV_W - 1, bn, D_FF).transpose(1, 0, 2))
    if time_major:
        rows = _transpose_rows(rows, l, bn)
    y = rows.reshape(bn, l, D_MODEL)
    return (y, jnp.stack(new_re), jnp.stack(new_im), jnp.stack(new_ret), jnp.stack(new_gla),
            jnp.stack(new_conv)), bf16_w


def kernel(x_prompt, x_sample, state_s5_re, state_s5_im, state_ret, state_gla, cache_ffn_conv,
           norm_g, s5_lambda_re, s5_lambda_im, s5_b_re, s5_b_im, s5_c_re, s5_c_im, s5_log_dt, s5_d, s5_w_glu,
           ret_wq, ret_wk, ret_wv, ret_wg, ret_wo,
           gla_wq, gla_wk, gla_wv, gla_wg, gla_wa1, gla_wa2, gla_ba, gla_norm_g, gla_wo,
           ffn_w_gate, ffn_w_up, ffn_conv_w, ffn_conv_b, ffn_w_down):
    p = dict(norm_g=norm_g, s5_lambda_re=s5_lambda_re, s5_lambda_im=s5_lambda_im, s5_b_re=s5_b_re,
             s5_b_im=s5_b_im, s5_c_re=s5_c_re, s5_c_im=s5_c_im, s5_log_dt=s5_log_dt, s5_d=s5_d,
             s5_w_glu=s5_w_glu.astype(BF16), ret_wq=ret_wq, ret_wk=ret_wk, ret_wv=ret_wv, ret_wg=ret_wg, ret_wo=ret_wo,
             gla_wq=gla_wq, gla_wk=gla_wk, gla_wv=gla_wv, gla_wg=gla_wg, gla_wa1=gla_wa1, gla_wa2=gla_wa2,
             gla_ba=gla_ba, gla_norm_g=gla_norm_g, gla_wo=gla_wo, ffn_w_gate=ffn_w_gate, ffn_w_up=ffn_w_up,
             ffn_conv_w=ffn_conv_w, ffn_conv_b=ffn_conv_b, ffn_w_down=ffn_w_down)
    bp = x_prompt.shape[0]
    z_s5 = jnp.zeros((state_s5_re.shape[0], bp) + state_s5_re.shape[2:], F32)
    z_ret = jnp.zeros((state_ret.shape[0], bp) + state_ret.shape[2:], F32)
    z_gla = jnp.zeros((state_gla.shape[0], bp) + state_gla.shape[2:], F32)
    z_conv = jnp.zeros((cache_ffn_conv.shape[0], bp) + cache_ffn_conv.shape[2:], x_prompt.dtype)
    first = {(n, 0): p[n][0].astype(BF16) for n in FFN_WEIGHTS}
    outs_p, bf16_w = _trunk(x_prompt, 0, z_s5, z_s5, z_ret, z_gla, z_conv, p, True, first)
    outs_s, _ = _trunk(x_sample, PAST_LEN, state_s5_re, state_s5_im, state_ret, state_gla, cache_ffn_conv, p,
                       False, bf16_w)
    return (outs_p[0], outs_s[0]) + tuple(outs_p[1:]) + tuple(outs_s[1:])
```

```python
import functools
import math

import jax
import jax.numpy as jnp
from jax import lax
from jax.experimental import pallas as pl
from jax.experimental.pallas import tpu as pltpu

F32 = jnp.float32
BF16 = jnp.bfloat16

D_MODEL = 1024
PAST_LEN = 16384
N_MIXERS = 3
FFN_WEIGHTS = ('ffn_w_gate', 'ffn_w_up', 'ffn_w_down')
MIXER_WEIGHTS = ((), ('ret_wq', 'ret_wk', 'ret_wv', 'ret_wg', 'ret_wo'),
                 ('gla_wq', 'gla_wk', 'gla_wv', 'gla_wg', 'gla_wo'))
S5_GROUP = 16
S5_GROUPS = D_MODEL // S5_GROUP
S5_STATE = 64
S5_SUPER = 4
S5_SG_CH = D_MODEL // S5_SUPER
S5_SG_ST = (S5_GROUPS // S5_SUPER) * S5_STATE
RET_HEADS = 4
RET_DK = D_MODEL // RET_HEADS
RET_DV = 2 * D_MODEL // RET_HEADS
RET_CHUNK = 256
ROPE_BASE = 10000.0
GLA_HEADS = 4
GLA_DK = D_MODEL // 2 // GLA_HEADS
GLA_DV = D_MODEL // GLA_HEADS
GLA_GATE_RANK = 16
GLA_GATE_NORM = 16.0
GLA_CHUNK = 64
GLA_GROUP = 8
D_FF = ((8 * D_MODEL // 3 + 255) // 256) * 256
FFN_SUB = 256
FFN_ROW_TILE = 1024
S5_BU_BUFFERS = 4
N_S5_INPUTS = 12
HOSTED_SEQS = 2
LONG_CHUNK_GROUP = 2
SEQ_GROUP_BYTES = 8 * 2 ** 20
CONV_W = 3
NORM_EPS = 1e-6
ROW_TILE = 512
PROJ_ROW_TILE = 1024
SUBLANES = 8
V7X_VMEM_MB = 64
VMEM_MB = V7X_VMEM_MB - 12
LARGE_VMEM_MB = V7X_VMEM_MB - 6


def _cparams(*sem, vmem_mb=None):
    return pltpu.CompilerParams(dimension_semantics=sem, vmem_limit_bytes=(vmem_mb or VMEM_MB) * 2 ** 20)


def _const_spec(shape):
    zeros = (0,) * len(shape)
    return pl.BlockSpec(shape, lambda *_: zeros, pipeline_mode=pl.Buffered(1))


def _stacked_spec(shape, index):
    where = (index,) + (0,) * len(shape)
    return pl.BlockSpec((None,) + tuple(shape), lambda *_: where, pipeline_mode=pl.Buffered(1))


def _rms(x, g):
    return x * lax.rsqrt(jnp.mean(x * x, axis=-1, keepdims=True) + NORM_EPS) * g


def _sigmoid(x):
    return 1.0 / (1.0 + jnp.exp(-x))


def _silu(x):
    return x * _sigmoid(x)


def _gelu_tanh(x):
    return 0.5 * x * (1.0 + jnp.tanh(math.sqrt(2.0 / math.pi) * (x + 0.044715 * (x * x * x))))


def _log_sigmoid(x):
    return jnp.minimum(x, 0.0) - jnp.log(1.0 + jnp.exp(-jnp.abs(x)))


def _dot(a, b):
    return jnp.dot(a, b, preferred_element_type=F32)


def _dot_nt(a, b):
    return lax.dot_general(a, b, (((1,), (1,)), ((), ())), preferred_element_type=F32)


def _dot_tn(a, b):
    return lax.dot_general(a, b, (((0,), (0,)), ((), ())), preferred_element_type=F32)


def _s5_kernel(x_ref, h0re_ref, h0im_ref, g0_ref, g1_ref, are_ref, aim_ref, bre_ref, bim_ref, cc_ref, dsk_ref,
               wglu_ref, *rest, bn, tt, x_bm, hosted, n_host_in):
    host_in, rest = rest[:n_host_in], rest[n_host_in:]
    o_ref, hre_ref, him_ref, rest = rest[0], rest[1], rest[2], rest[3:]
    host_out, (bu_sc, y_sc, bbd_ref, cbd_ref) = rest[:2 if hosted else 0], rest[2 if hosted else 0:]
    @pl.when(pl.program_id(0) == 0)
    def _():
        hre_ref[...] = h0re_ref[...]
        him_ref[...] = h0im_ref[...]
        ngl = S5_GROUPS // S5_SUPER

        def lane_tiler(k, n):
            r = lax.broadcasted_iota(jnp.int32, (k, n), 0)
            c = lax.broadcasted_iota(jnp.int32, (k, n), 1)
            return ((c & (k - 1)) == r).astype(BF16)

        tile_n = lane_tiler(S5_STATE, S5_SG_ST)
        tile_p = lane_tiler(S5_GROUP, S5_SG_CH)
        sh_p = S5_GROUP.bit_length() - 1
        sh_n = S5_STATE.bit_length() - 1
        rb = lax.broadcasted_iota(jnp.int32, (S5_SG_CH, S5_SG_ST), 0) >> sh_p
        cb = lax.broadcasted_iota(jnp.int32, (S5_SG_CH, S5_SG_ST), 1) >> sh_n
        rc = (lax.broadcasted_iota(jnp.int32, (2 * S5_SG_ST, S5_SG_CH), 0) >> sh_n) & (ngl - 1)
        cc = lax.broadcasted_iota(jnp.int32, (2 * S5_SG_ST, S5_SG_CH), 1) >> sh_p
        for sg in range(S5_SUPER):
            for part, ref in ((0, bre_ref), (1, bim_ref)):
                blk = _dot(ref[sg].astype(BF16), tile_n)
                bbd_ref[sg, :, S5_SG_ST * part:S5_SG_ST * (part + 1)] = jnp.where(rb == cb, blk, 0.0).astype(BF16)
            blk = _dot(cc_ref[sg].astype(BF16), tile_p)
            cbd_ref[sg] = jnp.where(rc == cc, blk, 0.0).astype(BF16)

    x = x_ref[...]
    if x_bm:
        x = jnp.swapaxes(x, 0, 1).reshape(tt * bn, D_MODEL)
    u = _rms(x, g0_ref[...])
    ub = u.astype(BF16)
    half = S5_SG_ST // 2
    host_stages = iter(())
    if hosted:
        hq, hk, hv, hs = host_in[:4]
        host_stages = _ret_unit_stages(hq, hk, hv, hs, host_out[0], host_out[1], hq.shape[1], hosted)
    for sg in range(S5_SUPER):
        buf = bu_sc.at[sg % S5_BU_BUFFERS]
        buf[...] = _dot(ub[:, S5_SG_CH * sg:S5_SG_CH * (sg + 1)], bbd_ref[sg])
        for hf in range(2):
            c_re = half * hf
            c_im = S5_SG_ST + half * hf
            a_lo = S5_SG_ST * sg + half * hf
            ar = jnp.broadcast_to(are_ref[:, a_lo:a_lo + half], (8, half))
            ai = jnp.broadcast_to(aim_ref[:, a_lo:a_lo + half], (8, half))
            for bt in range(bn // 8):
                r8 = 8 * bt
                hr = hre_ref[r8:r8 + 8, a_lo:a_lo + half]
                hi = him_ref[r8:r8 + 8, a_lo:a_lo + half]
                for t in range(tt):
                    r0 = t * bn + r8
                    bur = buf[r0:r0 + 8, c_re:c_re + half]
                    bui = buf[r0:r0 + 8, c_im:c_im + half]
                    hr, hi = ar * hr - ai * hi + bur, ar * hi + ai * hr + bui
                    buf[r0:r0 + 8, c_re:c_re + half] = hr
                    buf[r0:r0 + 8, c_im:c_im + half] = hi
                hre_ref[r8:r8 + 8, a_lo:a_lo + half] = hr
                him_ref[r8:r8 + 8, a_lo:a_lo + half] = hi
        y_sc[:, S5_SG_CH * sg:S5_SG_CH * (sg + 1)] = _dot(buf[...].astype(BF16), cbd_ref[sg])
        next(host_stages, None)
    y = y_sc[...] + dsk_ref[...] * u
    z = _dot(_gelu_tanh(y).astype(BF16), wglu_ref[...])
    mix = z[:, :D_MODEL] * _sigmoid(z[:, D_MODEL:])
    o_ref[...] = x + _rms(mix, g1_ref[...])


def _s5_layer(x, bn, h0_re, h0_im, g0, g1, lam_re, lam_im, b_re, b_im, c_re, c_im, log_dt, d_skip, w_glu,
              j, x_bm=False, host=None):
    rows = x.shape[0]
    tt = min(rows // bn, max(1, ROW_TILE // bn))
    tile = tt * bn
    steps = rows // tile
    host_ins, host_in_specs, host_out_specs, host_out_shapes = [], [], [], []
    aliases = {}
    if host is not None:
        hq, hk, hv, hs, first, acc = host
        assert first % HOSTED_SEQS == 0
        blk0 = first // HOSTED_SEQS
        host_ins = [hq, hk, hv, hs]
        for a in host_ins:
            host_in_specs.append(pl.BlockSpec((HOSTED_SEQS,) + a.shape[1:],
                                              lambda i, nd=a.ndim: (blk0 + i,) + (0,) * (nd - 1)))
        host_out_specs = [pl.BlockSpec((HOSTED_SEQS,) + hv.shape[1:], lambda i: (i, 0, 0)),
                          pl.BlockSpec((HOSTED_SEQS,) + hs.shape[1:], lambda i: (blk0 + i, 0, 0, 0))]
        host_out_shapes = [jax.ShapeDtypeStruct((steps * HOSTED_SEQS,) + hv.shape[1:], F32),
                           jax.ShapeDtypeStruct(hs.shape, F32)]
        if acc is not None:
            host_ins.append(acc)
            host_in_specs.append(pl.BlockSpec(memory_space=pl.ANY))
            aliases = {N_S5_INPUTS + 4: 4}
    o_spec = pl.BlockSpec((tile, D_MODEL), lambda i: (i, 0))
    x_spec = o_spec
    if x_bm:
        assert bn == SUBLANES
        x = x.reshape(bn, rows // bn, D_MODEL)
        x_spec = pl.BlockSpec((bn, tt, D_MODEL), lambda i: (0, i, 0))
    lam = lax.complex(lam_re, lam_im)
    dt = jnp.exp(log_dt)[:, None]
    a_bar = jnp.exp(lam * dt)
    b_bar = ((a_bar - 1.0) / lam)[..., None] * lax.complex(b_re, b_im)
    bt = b_bar.transpose(0, 2, 1).reshape(S5_SUPER, S5_SG_CH, S5_STATE)
    bt_re, bt_im = jnp.real(bt), jnp.imag(bt)
    ct_re = c_re.transpose(0, 2, 1).reshape(S5_SUPER, S5_SG_ST, S5_GROUP)
    ct_im = c_im.transpose(0, 2, 1).reshape(S5_SUPER, S5_SG_ST, S5_GROUP)
    ct = jnp.concatenate([ct_re, -ct_im], axis=1)
    a_re = jnp.real(a_bar).reshape(1, S5_GROUPS * S5_STATE)
    a_im = jnp.imag(a_bar).reshape(1, S5_GROUPS * S5_STATE)
    nst = S5_GROUPS * S5_STATE

    out, new_re, new_im, *hosted = pl.pallas_call(
        functools.partial(_s5_kernel, bn=bn, tt=tt, x_bm=x_bm, hosted=HOSTED_SEQS if host_ins else 0,
                          n_host_in=len(host_ins)),
        grid=(steps,),
        in_specs=[
            x_spec,
            _const_spec((bn, nst)), _const_spec((bn, nst)),
            _const_spec((1, D_MODEL)), _const_spec((1, D_MODEL)),
            _const_spec((1, nst)), _const_spec((1, nst)),
            _const_spec((S5_SUPER, S5_SG_CH, S5_STATE)), _const_spec((S5_SUPER, S5_SG_CH, S5_STATE)),
            _const_spec((S5_SUPER, 2 * S5_SG_ST, S5_GROUP)),
            _const_spec((1, D_MODEL)),
            _stacked_spec((D_MODEL, 2 * D_MODEL), j),
        ] + host_in_specs,
        out_specs=[o_spec, _const_spec((bn, nst)), _const_spec((bn, nst))] + host_out_specs,
        out_shape=[jax.ShapeDtypeStruct((rows, D_MODEL), F32), jax.ShapeDtypeStruct((bn, nst), F32),
                   jax.ShapeDtypeStruct((bn, nst), F32)] + host_out_shapes,
        scratch_shapes=[pltpu.VMEM((S5_BU_BUFFERS, tile, 2 * S5_SG_ST), F32), pltpu.VMEM((tile, D_MODEL), F32),
                        pltpu.VMEM((S5_SUPER, S5_SG_CH, 2 * S5_SG_ST), BF16),
                        pltpu.VMEM((S5_SUPER, 2 * S5_SG_ST, S5_SG_CH), BF16)],
        input_output_aliases=aliases,
        compiler_params=_cparams("arbitrary", vmem_mb=LARGE_VMEM_MB if host_ins else None),
        name="s5_layer",
    )(x, h0_re.reshape(bn, nst), h0_im.reshape(bn, nst), g0.reshape(1, -1), g1.reshape(1, -1), a_re, a_im,
      bt_re, bt_im, ct, d_skip.reshape(1, -1), w_glu, *host_ins)
    return out, new_re.reshape(bn, S5_GROUPS, S5_STATE), new_im.reshape(bn, S5_GROUPS, S5_STATE), tuple(hosted)


def _ffn_kernel(x_ref, cin_ref, g2_ref, g3_ref, wg_ref, wu_ref, wd_ref, cw_ref, cb_ref, *rest,
                bn, tile, x_bm, out_bm, cast_next):
    next_f32, rest = rest[:cast_next], rest[cast_next:]
    o_ref, cout_ref, rest = rest[0], rest[1], rest[2:]
    next_bf16, (hdn_sc, gext_sc, carry_sc) = rest[:cast_next], rest[cast_next:]
    i = pl.program_id(0)
    halo = 2 * bn
    base = gext_sc.shape[1] - tile

    @pl.when(i == 0)
    def _():
        carry_sc[...] = cin_ref[...]

    x = x_ref[...]
    if x_bm:
        x = jnp.swapaxes(x, 0, 1).reshape(tile, D_MODEL)
    h = _rms(x, g2_ref[...]).astype(BF16)
    for j in range(D_FF // FFN_SUB):
        sl = slice(FFN_SUB * j, FFN_SUB * (j + 1))
        buf = gext_sc.at[j % 2]
        gpre = _dot(h, wg_ref[:, sl])
        buf[base - halo:base, :] = carry_sc[:, sl]
        buf[base:base + tile, :] = gpre
        carry_sc[:, sl] = gpre[tile - halo:tile, :]
        gconv = cb_ref[:, sl] + cw_ref[0:1, sl] * buf[base - halo:base - halo + tile, :]
        gconv = gconv + cw_ref[1:2, sl] * buf[base - bn:base - bn + tile, :]
        gconv = gconv + cw_ref[2:3, sl] * gpre
        hdn_sc[:, sl] = (_silu(gconv) * _dot(h, wu_ref[:, sl])).astype(BF16)
        if j == 0:
            for src, dst in zip(next_f32, next_bf16):
                dst[...] = src[...].astype(BF16)
    y = x + _rms(_dot(hdn_sc[...], wd_ref[...]), g3_ref[...])
    if out_bm:
        y = jnp.swapaxes(y.reshape(tile // bn, bn, D_MODEL), 0, 1)
    o_ref[...] = y

    @pl.when(i == pl.num_programs(0) - 1)
    def _():
        cout_ref[...] = carry_sc[...]


def _ffn_layer(x, bn, cin, g2, g3, w_gate, w_up, conv_w, conv_b, w_down, x_bm=False, out_bm=False,
               next_weights=()):
    rows = x.shape[0]
    tile = min(FFN_ROW_TILE, rows)
    tt = tile // bn
    halo = 2 * bn
    steps = rows // tile
    cast_ins, cast_in_specs, cast_out_specs, cast_out_shapes = [], [], [], []
    assert w_gate.dtype == w_up.dtype == w_down.dtype == BF16
    for w, idx in next_weights:
        k, n = w.shape[1:]
        kb = k // steps
        assert kb * steps == k and kb % (2 * SUBLANES) == 0
        cast_ins.append(w)
        cast_in_specs.append(pl.BlockSpec((None, kb, n), lambda i, idx=idx: (idx, i, 0)))
        cast_out_specs.append(pl.BlockSpec((kb, n), lambda i: (i, 0)))
        cast_out_shapes.append(jax.ShapeDtypeStruct((k, n), BF16))
    assert not (x_bm or out_bm) or bn == SUBLANES
    tm_spec = pl.BlockSpec((tile, D_MODEL), lambda i: (i, 0))
    bm_spec = pl.BlockSpec((bn, tt, D_MODEL), lambda i: (0, i, 0))
    if x_bm:
        x = x.reshape(bn, rows // bn, D_MODEL)
    out_shape = (bn, rows // bn, D_MODEL) if out_bm else (rows, D_MODEL)
    out, cout, *cast = pl.pallas_call(
        functools.partial(_ffn_kernel, bn=bn, tile=tile, x_bm=x_bm, out_bm=out_bm, cast_next=len(cast_ins)),
        grid=(steps,),
        in_specs=[
            bm_spec if x_bm else tm_spec,
            _const_spec((halo, D_FF)),
            _const_spec((1, D_MODEL)), _const_spec((1, D_MODEL)),
            _const_spec((D_MODEL, D_FF)), _const_spec((D_MODEL, D_FF)), _const_spec((D_FF, D_MODEL)),
            _const_spec((CONV_W, D_FF)), _const_spec((1, D_FF)),
        ] + cast_in_specs,
        out_specs=[bm_spec if out_bm else tm_spec, _const_spec((halo, D_FF))] + cast_out_specs,
        out_shape=[jax.ShapeDtypeStruct(out_shape, F32), jax.ShapeDtypeStruct((halo, D_FF), F32)] + cast_out_shapes,
        scratch_shapes=[pltpu.VMEM((tile, D_FF), BF16),
                        pltpu.VMEM((2, halo + tile, FFN_SUB), F32),
                        pltpu.VMEM((halo, D_FF), F32)],
        compiler_params=_cparams("arbitrary", vmem_mb=LARGE_VMEM_MB),
        name="conv_ffn",
    )(x, cin, g2.reshape(1, -1), g3.reshape(1, -1), w_gate, w_up, w_down, conv_w, conv_b.reshape(1, -1), *cast_ins)
    return out.reshape(rows, D_MODEL), cout, tuple(cast)


def _ret_proj_kernel(x_ref, cos_ref, sin_ref, g_ref, wq_ref, wk_ref, wv_ref, q_ref, k_ref, v_ref):
    h = _rms(x_ref[...], g_ref[...]).astype(BF16)
    cos = cos_ref[...]
    sin = sin_ref[...]
    hw = RET_DK // 2
    for w_ref, dst, scale in ((wq_ref, q_ref, 1.0), (wk_ref, k_ref, RET_DK ** -0.5)):
        p = _dot(h, w_ref[...])
        for hd in range(RET_HEADS):
            lo = RET_DK * hd
            t1 = p[:, lo:lo + hw]
            t2 = p[:, lo + hw:lo + 2 * hw]
            dst[:, lo:lo + hw] = ((t1 * cos - t2 * sin) * scale).astype(BF16)
            dst[:, lo + hw:lo + 2 * hw] = ((t2 * cos + t1 * sin) * scale).astype(BF16)
    v_ref[...] = _dot(h, wv_ref[...]).astype(BF16)


def _ret_unit_stages(q_ref, k_ref, v_ref, sin_ref, o_ref, sout_ref, chunk, group):
    ri = lax.broadcasted_iota(jnp.int32, (chunk, chunk), 0)
    ci = lax.broadcasted_iota(jnp.int32, (chunk, chunk), 1)
    rel = (ri - ci).astype(F32)
    causal = ri >= ci
    idx = lax.broadcasted_iota(jnp.int32, (chunk, 1), 0).astype(F32)
    lgs = [math.log1p(-(2.0 ** (-5.0 - hd))) for hd in range(RET_HEADS)]
    units = [(g, hd) for g in range(group) for hd in range(RET_HEADS)]
    qsl = [slice(RET_DK * hd, RET_DK * (hd + 1)) for hd in range(RET_HEADS)]
    vsl = [slice(RET_DV * hd, RET_DV * (hd + 1)) for hd in range(RET_HEADS)]
    decay = [jnp.where(causal, jnp.exp(jnp.maximum(rel, 0.0) * lg), 0.0) for lg in lgs]
    scores = [(_dot_nt(q_ref[g, :, qsl[hd]], k_ref[g, :, qsl[hd]]) * decay[hd]).astype(BF16) for g, hd in units]
    yield
    for (g, hd), sc in zip(units, scores):
        q_decay = jnp.exp((idx + 1.0) * lgs[hd])
        inter = _dot(q_ref[g, :, qsl[hd]], sin_ref[g, hd].astype(BF16)) * q_decay
        o_ref[g, :, vsl[hd]] = _dot(sc, v_ref[g, :, vsl[hd]]) + inter
    yield
    for g, hd in units:
        k_decay = jnp.exp((chunk - 1.0 - idx) * lgs[hd])
        kd = (k_ref[g, :, qsl[hd]].astype(F32) * k_decay).astype(BF16)
        sout_ref[g, hd] = sin_ref[g, hd] * math.exp(chunk * lgs[hd]) + _dot_tn(kd, v_ref[g, :, vsl[hd]])


def _ret_chunk_kernel(q_ref, k_ref, v_ref, s0_ref, o_ref, sout_ref, *, chunk, group):
    @pl.when(pl.program_id(1) == 0)
    def _():
        sout_ref[...] = s0_ref[...]

    for _ in _ret_unit_stages(q_ref, k_ref, v_ref, sout_ref, o_ref, sout_ref, chunk, group):
        pass


def _ret_out_kernel(x_ref, o_ref, g0_ref, g1_ref, wg_ref, wo_ref, y_ref, on_sc):
    x = x_ref[...]
    h = _rms(x, g0_ref[...]).astype(BF16)
    gate = _silu(_dot(h, wg_ref[...]))
    for hd in range(RET_HEADS):
        sl = slice(RET_DV * hd, RET_DV * (hd + 1))
        oh = o_ref[:, sl]
        mu = jnp.mean(oh, axis=-1, keepdims=True)
        ctr = oh - mu
        var = jnp.mean(ctr * ctr, axis=-1, keepdims=True)
        on_sc[:, sl] = (ctr * lax.rsqrt(var + NORM_EPS) * gate[:, sl]).astype(BF16)
    y_ref[...] = x + _rms(_dot(on_sc[...], wo_ref[...]), g1_ref[...])


def _row_grid_call(kernel, row_ins, const_ins, outs, rows, tile, scratch=(), name=None, tab_ins=(),
                   vmem_mb=None):
    nt = rows // tile
    in_specs = [pl.BlockSpec((tile, a.shape[1]), lambda i: (i, 0)) for a in row_ins]
    for a in tab_ins:
        ntab = a.shape[0] // tile
        in_specs.append(pl.BlockSpec((tile, a.shape[1]), lambda i, ntab=ntab: (i % ntab, 0)))
    in_specs += [_const_spec(a.shape) for a in const_ins]
    return pl.pallas_call(
        kernel,
        grid=(nt,),
        in_specs=in_specs,
        out_specs=[pl.BlockSpec((tile, n), lambda i: (i, 0)) for n, _ in outs],
        out_shape=[jax.ShapeDtypeStruct((rows, n), dt) for n, dt in outs],
        scratch_shapes=list(scratch),
        compiler_params=_cparams("arbitrary", vmem_mb=vmem_mb),
        name=name,
    )(*row_ins, *tab_ins, *const_ins)


def _to_seq(a, bn, l, time_major):
    n = a.shape[1]
    return a.reshape(l, bn, n).transpose(1, 0, 2) if time_major else a.reshape(bn, l, n)


def _from_seq(a, bn, l, time_major):
    a = a.transpose(1, 0, 2) if time_major else a
    return a.reshape(bn * l, a.shape[2])


def _seq_group(bn, chunk, full_chunk, state_bytes):
    if chunk == full_chunk:
        return min(bn, LONG_CHUNK_GROUP)
    grp = max(1, min(bn, SEQ_GROUP_BYTES // state_bytes))
    while bn % grp:
        grp -= 1
    return grp


def _seq_chunking(l, chunk):
    if l % chunk == 0:
        return chunk
    assert l < chunk
    return l


def _ret_proj_stage(x, bn, l, time_major, pos0, g0, wq, wk, wv):
    rows = x.shape[0]
    ptile = min(PROJ_ROW_TILE, rows)
    half = RET_DK // 2
    freq = 1.0 / (ROPE_BASE ** jnp.linspace(0.0, 1.0, half, dtype=F32))
    pos = jnp.arange(pos0, pos0 + l, dtype=jnp.int32).astype(F32)
    ang = pos[:, None] * freq[None, :]
    cos, sin = jnp.cos(ang), jnp.sin(ang)
    if time_major:
        cos, sin = jnp.repeat(cos, bn, axis=0), jnp.repeat(sin, bn, axis=0)
    elif l < ptile:
        cos, sin = jnp.tile(cos, (ptile // l, 1)), jnp.tile(sin, (ptile // l, 1))
    q, k, v = _row_grid_call(
        _ret_proj_kernel, [x], [g0.reshape(1, -1), wq.astype(BF16), wk.astype(BF16), wv.astype(BF16)],
        [(RET_HEADS * RET_DK, BF16), (RET_HEADS * RET_DK, BF16), (RET_HEADS * RET_DV, BF16)],
        rows, ptile, name="ret_proj", tab_ins=[cos, sin], vmem_mb=LARGE_VMEM_MB)
    return tuple(_to_seq(a, bn, l, time_major) for a in (q, k, v))


def _ret_chunk_stage(q3, k3, v3, s0):
    bn, l, _ = q3.shape
    chunk = _seq_chunking(l, RET_CHUNK)
    grp = _seq_group(bn, chunk, RET_CHUNK, RET_HEADS * RET_DK * RET_DV * 4)
    return pl.pallas_call(
        functools.partial(_ret_chunk_kernel, chunk=chunk, group=grp),
        grid=(bn // grp, l // chunk),
        in_specs=[
            pl.BlockSpec((grp, chunk, RET_HEADS * RET_DK), lambda b, i: (b, i, 0)),
            pl.BlockSpec((grp, chunk, RET_HEADS * RET_DK), lambda b, i: (b, i, 0)),
            pl.BlockSpec((grp, chunk, RET_HEADS * RET_DV), lambda b, i: (b, i, 0)),
            pl.BlockSpec((grp, RET_HEADS, RET_DK, RET_DV), lambda b, i: (b, 0, 0, 0)),
        ],
        out_specs=[
            pl.BlockSpec((grp, chunk, RET_HEADS * RET_DV), lambda b, i: (b, i, 0)),
            pl.BlockSpec((grp, RET_HEADS, RET_DK, RET_DV), lambda b, i: (b, 0, 0, 0)),
        ],
        out_shape=[jax.ShapeDtypeStruct((bn, l, RET_HEADS * RET_DV), F32),
                   jax.ShapeDtypeStruct((bn, RET_HEADS, RET_DK, RET_DV), F32)],
        compiler_params=_cparams("arbitrary", "arbitrary"),
        name="ret_chunk",
    )(q3, k3, v3, s0)


def _ret_out_stage(x, o3, time_major, g0, g1, wg, wo):
    rows = x.shape[0]
    tile = min(ROW_TILE, rows)
    bn, l, _ = o3.shape
    o = _from_seq(o3, bn, l, time_major)
    (y,) = _row_grid_call(
        _ret_out_kernel, [x, o], [g0.reshape(1, -1), g1.reshape(1, -1), wg.astype(BF16), wo.astype(BF16)],
        [(D_MODEL, F32)], rows, tile, scratch=[pltpu.VMEM((tile, RET_HEADS * RET_DV), BF16)], name="ret_out")
    return y


def _gla_proj_kernel(x_ref, g_ref, wq_ref, wk_ref, wv_ref, wa1_ref, wa2_ref, ba_ref,
                     q_ref, k_ref, v_ref, la_ref):
    h = _rms(x_ref[...], g_ref[...]).astype(BF16)
    q_ref[...] = (_dot(h, wq_ref[...]) * (GLA_DK ** -0.5)).astype(BF16)
    k_ref[...] = _dot(h, wk_ref[...]).astype(BF16)
    v_ref[...] = _dot(h, wv_ref[...]).astype(BF16)
    low = _dot(h, wa1_ref[...]).astype(BF16)
    logit = _dot(low, wa2_ref[...]) + ba_ref[...]
    la_ref[...] = _log_sigmoid(logit) / GLA_GATE_NORM


def _gla_chunk_kernel(q_ref, k_ref, v_ref, la_ref, s0_ref, o_ref, sout_ref, st_sc, *, chunk, group):
    i = pl.program_id(1)

    @pl.when(i == 0)
    def _():
        for g in range(group):
            for hd in range(GLA_HEADS):
                st_sc[g, hd] = s0_ref[g, hd]

    ri = lax.broadcasted_iota(jnp.int32, (chunk, chunk), 0)
    ci = lax.broadcasted_iota(jnp.int32, (chunk, chunk), 1)
    causal = ri >= ci
    tri = causal.astype(BF16)
    nk = GLA_HEADS * GLA_DK
    units = [(g, hd) for g in range(group) for hd in range(GLA_HEADS)]
    bcs = []
    for g in range(group):
        la = la_ref[g]
        p1 = la.astype(BF16)
        r1 = la - p1.astype(F32)
        p2 = r1.astype(BF16)
        p3 = (r1 - p2.astype(F32)).astype(BF16)
        acc = _dot(tri, jnp.concatenate([p1, p2, p3], axis=1))
        bcs.append(acc[:, :nk] + acc[:, nk:2 * nk] + acc[:, 2 * nk:])
    prep = []
    for g, hd in units:
        ksl = slice(GLA_DK * hd, GLA_DK * (hd + 1))
        bc = bcs[g][:, ksl]
        blast = bc[chunk - 1:chunk, :]
        ref = 0.5 * (bc[0:1, :] + blast)
        qh = q_ref[g, :, ksl].astype(F32)
        kh = k_ref[g, :, ksl].astype(F32)
        qt = (qh * jnp.exp(bc - ref)).astype(BF16)
        kt = (kh * jnp.exp(ref - bc)).astype(BF16)
        qg = (qh * jnp.exp(bc)).astype(BF16)
        kg = (kh * jnp.exp(blast - bc)).astype(BF16)
        row_decay = jnp.broadcast_to(jnp.exp(blast), (SUBLANES, GLA_DK)).T[:, :1]
        prep.append((qt, kt, qg, kg, row_decay))
    scores = [jnp.where(causal, _dot_nt(qt, kt), 0.0).astype(BF16) for qt, kt, _, _, _ in prep]
    for (g, hd), sc, (_, _, qg, kg, row_decay) in zip(units, scores, prep):
        vsl = slice(GLA_DV * hd, GLA_DV * (hd + 1))
        vh = v_ref[g, :, vsl]
        st = st_sc[g, hd]
        o_ref[g, :, vsl] = _dot(sc, vh) + _dot(qg, st.astype(BF16))
        st_sc[g, hd] = st * row_decay + _dot_tn(kg, vh)

    @pl.when(i == pl.num_programs(1) - 1)
    def _():
        for g in range(group):
            for hd in range(GLA_HEADS):
                sout_ref[g, hd] = st_sc[g, hd]


def _gla_out_kernel(x_ref, o_ref, g0_ref, g1_ref, ng_ref, wg_ref, wo_ref, y_ref, on_sc):
    x = x_ref[...]
    h = _rms(x, g0_ref[...]).astype(BF16)
    gate = _silu(_dot(h, wg_ref[...]))
    for hd in range(GLA_HEADS):
        sl = slice(GLA_DV * hd, GLA_DV * (hd + 1))
        oh = o_ref[:, sl]
        on = oh * lax.rsqrt(jnp.mean(oh * oh, axis=-1, keepdims=True) + NORM_EPS) * ng_ref[...]
        on_sc[:, sl] = (on * gate[:, sl]).astype(BF16)
    y_ref[...] = x + _rms(_dot(on_sc[...], wo_ref[...]), g1_ref[...])


def _gla_layer(x, bn, l, time_major, s0, g0, g1, wq, wk, wv, wg, wa1, wa2, ba, norm_g, wo):
    rows = x.shape[0]
    tile = min(PROJ_ROW_TILE, rows)
    lanes = 128
    wa1p = jnp.pad(wa1, ((0, 0), (0, lanes - GLA_GATE_RANK))).astype(BF16)
    wa2p = jnp.pad(wa2, ((0, lanes - GLA_GATE_RANK), (0, 0))).astype(BF16)
    q, k, v, la = _row_grid_call(
        _gla_proj_kernel, [x],
        [g0.reshape(1, -1), wq.astype(BF16), wk.astype(BF16), wv.astype(BF16), wa1p, wa2p, ba.reshape(1, -1)],
        [(GLA_HEADS * GLA_DK, BF16), (GLA_HEADS * GLA_DK, BF16), (GLA_HEADS * GLA_DV, BF16),
         (GLA_HEADS * GLA_DK, F32)],
        rows, tile, name="gla_proj", vmem_mb=LARGE_VMEM_MB)
    chunk = _seq_chunking(l, GLA_CHUNK)
    q3, k3, v3, la3 = (_to_seq(a, bn, l, time_major) for a in (q, k, v, la))
    nc = l // chunk
    grp = GLA_GROUP if bn % GLA_GROUP == 0 else 1
    o3, s_new = pl.pallas_call(
        functools.partial(_gla_chunk_kernel, chunk=chunk, group=grp),
        grid=(bn // grp, nc),
        in_specs=[
            pl.BlockSpec((grp, chunk, GLA_HEADS * GLA_DK), lambda b, i: (b, i, 0)),
            pl.BlockSpec((grp, chunk, GLA_HEADS * GLA_DK), lambda b, i: (b, i, 0)),
            pl.BlockSpec((grp, chunk, GLA_HEADS * GLA_DV), lambda b, i: (b, i, 0)),
            pl.BlockSpec((grp, chunk, GLA_HEADS * GLA_DK), lambda b, i: (b, i, 0)),
            pl.BlockSpec((grp, GLA_HEADS, GLA_DK, GLA_DV), lambda b, i: (b, 0, 0, 0)),
        ],
        out_specs=[
            pl.BlockSpec((grp, chunk, GLA_HEADS * GLA_DV), lambda b, i: (b, i, 0)),
            pl.BlockSpec((grp, GLA_HEADS, GLA_DK, GLA_DV), lambda b, i: (b, 0, 0, 0)),
        ],
        out_shape=[jax.ShapeDtypeStruct((bn, l, GLA_HEADS * GLA_DV), F32),
                   jax.ShapeDtypeStruct((bn, GLA_HEADS, GLA_DK, GLA_DV), F32)],
        scratch_shapes=[pltpu.VMEM((grp, GLA_HEADS, GLA_DK, GLA_DV), F32)],
        compiler_params=_cparams("arbitrary", "arbitrary"),
        name="gla_chunk",
    )(q3, k3, v3, la3, s0)
    o = _from_seq(o3, bn, l, time_major)
    (y,) = _row_grid_call(
        _gla_out_kernel, [x, o],
        [g0.reshape(1, -1), g1.reshape(1, -1), norm_g.reshape(1, -1), wg.astype(BF16), wo.astype(BF16)],
        [(D_MODEL, F32)], rows, tile, scratch=[pltpu.VMEM((tile, GLA_HEADS * GLA_DV), BF16)], name="gla_out",
        vmem_mb=LARGE_VMEM_MB)
    return y, s_new


def _transpose_rows(x, a, b):
    return x.reshape(a, b, x.shape[1]).transpose(1, 0, 2).reshape(a * b, x.shape[1])


def _trunk(x, pos0, s5_re, s5_im, ret_s, gla_s, conv_buf, p, batch_major_mixers, bf16_w, cast_weights,
           defer_ret=False, host=None):
    bf16_w = dict(bf16_w)

    def w(name, idx):
        return bf16_w.get((name, idx), p[name][idx])
    bn, l, _ = x.shape
    depth = p['norm_g'].shape[0]
    assert not batch_major_mixers or bn == SUBLANES
    rows = x.reshape(bn * l, D_MODEL)
    time_major = False
    if not batch_major_mixers:
        rows, time_major = _transpose_rows(rows, bn, l), True
    new_re, new_im, new_ret, new_gla, new_conv = [], [], [], [], []
    hosted_o, hosted_s, hosted_n = [], None, 0
    for layer in range(depth):
        g = p['norm_g'][layer]
        kind = layer % N_MIXERS
        j = layer // N_MIXERS
        if kind == 0:
            todo_host = None
            if host is not None and hosted_n < host[0].shape[0]:
                todo_host = tuple(host) + (hosted_n, hosted_s)
            rows, hr, hi, hres = _s5_layer(rows, bn, s5_re[j], s5_im[j], g[0], g[1],
                                           p['s5_lambda_re'][j], p['s5_lambda_im'][j], p['s5_b_re'][j],
                                           p['s5_b_im'][j], p['s5_c_re'][j], p['s5_c_im'][j], p['s5_log_dt'][j],
                                           p['s5_d'][j], p['s5_w_glu'], j, x_bm=not time_major, host=todo_host)
            if hres:
                hosted_o.append(hres[0])
                hosted_s = hres[1]
                hosted_n += hres[0].shape[0]
            time_major = True
            new_re.append(hr)
            new_im.append(hi)
        elif kind == 1:
            q3, k3, v3 = _ret_proj_stage(rows, bn, l, time_major, pos0, g[0], w('ret_wq', j), w('ret_wk', j),
                                         w('ret_wv', j))
            if defer_ret:
                o3, s, more = yield q3, k3, v3, ret_s[j]
                bf16_w.update(more)
            else:
                o3, s = _ret_chunk_stage(q3, k3, v3, ret_s[j])
            rows = _ret_out_stage(rows, o3, time_major, g[0], g[1], w('ret_wg', j), w('ret_wo', j))
            new_ret.append(s)
        else:
            rows, s = _gla_layer(rows, bn, l, time_major, gla_s[j], g[0], g[1], w('gla_wq', j), w('gla_wk', j),
                                 w('gla_wv', j), w('gla_wg', j), p['gla_wa1'][j], p['gla_wa2'][j], p['gla_ba'][j],
                                 p['gla_norm_g'][j], w('gla_wo', j))
            new_gla.append(s)
        next_is_s5 = layer + 1 < depth and (layer + 1) % N_MIXERS == 0
        out_bm = batch_major_mixers and not next_is_s5
        cin = conv_buf[layer].transpose(1, 0, 2).reshape((CONV_W - 1) * bn, D_FF)
        todo = []
        if cast_weights and layer + 1 < depth:
            mixer = MIXER_WEIGHTS[(layer + 1) % N_MIXERS]
            todo = [(n, layer + 1) for n in FFN_WEIGHTS] + [(n, (layer + 1) // N_MIXERS) for n in mixer]
            todo = [key for key in todo if key not in bf16_w]
        rows, cout, cast = _ffn_layer(rows, bn, cin, g[2], g[3], w('ffn_w_gate', layer), w('ffn_w_up', layer),
                                      p['ffn_conv_w'][layer], p['ffn_conv_b'][layer], w('ffn_w_down', layer),
                                      x_bm=not time_major, out_bm=out_bm,
                                      next_weights=[(p[n], idx) for n, idx in todo])
        bf16_w.update(zip(todo, cast))
        time_major = not out_bm
        new_conv.append(cout.reshape(CONV_W - 1, bn, D_FF).transpose(1, 0, 2))
    if time_major:
        rows = _transpose_rows(rows, l, bn)
    y = rows.reshape(bn, l, D_MODEL)
    hosted = None
    if host is not None:
        assert hosted_n == host[0].shape[0]
        hosted = (jnp.concatenate(hosted_o), hosted_s)
    outs = (y, jnp.stack(new_re), jnp.stack(new_im), jnp.stack(new_ret), jnp.stack(new_gla),
            jnp.stack(new_conv))
    return outs, bf16_w, hosted


def _finish(gen, reply=None):
    try:
        gen.send(reply)
    except StopIteration as done:
        return done.value
    raise AssertionError("unexpected request")


def kernel(x_prompt, x_sample, state_s5_re, state_s5_im, state_ret, state_gla, cache_ffn_conv,
           norm_g, s5_lambda_re, s5_lambda_im, s5_b_re, s5_b_im, s5_c_re, s5_c_im, s5_log_dt, s5_d, s5_w_glu,
           ret_wq, ret_wk, ret_wv, ret_wg, ret_wo,
           gla_wq, gla_wk, gla_wv, gla_wg, gla_wa1, gla_wa2, gla_ba, gla_norm_g, gla_wo,
           ffn_w_gate, ffn_w_up, ffn_conv_w, ffn_conv_b, ffn_w_down):
    p = dict(norm_g=norm_g, s5_lambda_re=s5_lambda_re, s5_lambda_im=s5_lambda_im, s5_b_re=s5_b_re,
             s5_b_im=s5_b_im, s5_c_re=s5_c_re, s5_c_im=s5_c_im, s5_log_dt=s5_log_dt, s5_d=s5_d,
             s5_w_glu=s5_w_glu.astype(BF16), ret_wq=ret_wq, ret_wk=ret_wk, ret_wv=ret_wv, ret_wg=ret_wg, ret_wo=ret_wo,
             gla_wq=gla_wq, gla_wk=gla_wk, gla_wv=gla_wv, gla_wg=gla_wg, gla_wa1=gla_wa1, gla_wa2=gla_wa2,
             gla_ba=gla_ba, gla_norm_g=gla_norm_g, gla_wo=gla_wo, ffn_w_gate=ffn_w_gate, ffn_w_up=ffn_w_up,
             ffn_conv_w=ffn_conv_w, ffn_conv_b=ffn_conv_b, ffn_w_down=ffn_w_down)
    bp = x_prompt.shape[0]
    z_s5 = jnp.zeros((state_s5_re.shape[0], bp) + state_s5_re.shape[2:], F32)
    z_ret = jnp.zeros((state_ret.shape[0], bp) + state_ret.shape[2:], F32)
    z_gla = jnp.zeros((state_gla.shape[0], bp) + state_gla.shape[2:], F32)
    z_conv = jnp.zeros((cache_ffn_conv.shape[0], bp) + cache_ffn_conv.shape[2:], x_prompt.dtype)
    first = {(n, 0): p[n][0].astype(BF16) for n in FFN_WEIGHTS}
    first.update({(n, j): p[n][j].astype(BF16) for n in ('ret_wq', 'ret_wk', 'ret_wv') for j in range(ret_wq.shape[0])})
    sample = _trunk(x_sample, PAST_LEN, state_s5_re, state_s5_im, state_ret, state_gla, cache_ffn_conv, p,
                    False, first, False, defer_ret=True)
    request = next(sample)
    outs_p, bf16_w, hosted = _finish(_trunk(x_prompt, 0, z_s5, z_s5, z_ret, z_gla, z_conv, p, True, first, True,
                                            host=request))
    outs_s, _, _ = _finish(sample, hosted + (bf16_w,))
    return (outs_p[0], outs_s[0]) + tuple(outs_p[1:]) + tuple(outs_s[1:])
```

```python
import functools
import math

import jax
import jax.numpy as jnp
from jax import lax
from jax.experimental import pallas as pl
from jax.experimental.pallas import tpu as pltpu

F32 = jnp.float32
BF16 = jnp.bfloat16

D_MODEL = 1024
PAST_LEN = 16384
N_MIXERS = 3
FFN_WEIGHTS = ('ffn_w_gate', 'ffn_w_up', 'ffn_w_down')
MIXER_WEIGHTS = ((), ('ret_wq', 'ret_wk', 'ret_wv', 'ret_wg', 'ret_wo'),
                 ('gla_wq', 'gla_wk', 'gla_wv', 'gla_wg', 'gla_wo'))
S5_GROUP = 16
S5_GROUPS = D_MODEL // S5_GROUP
S5_STATE = 64
S5_SUPER = 4
S5_SG_CH = D_MODEL // S5_SUPER
S5_SG_ST = (S5_GROUPS // S5_SUPER) * S5_STATE
RET_HEADS = 4
RET_DK = D_MODEL // RET_HEADS
RET_DV = 2 * D_MODEL // RET_HEADS
RET_CHUNK = 256
ROPE_BASE = 10000.0
GLA_HEADS = 4
GLA_DK = D_MODEL // 2 // GLA_HEADS
GLA_DV = D_MODEL // GLA_HEADS
GLA_GATE_RANK = 16
GLA_GATE_NORM = 16.0
GLA_CHUNK = 64
GLA_GROUP = 8
D_FF = ((8 * D_MODEL // 3 + 255) // 256) * 256
FFN_SUB = 256
FFN_ROW_TILE = 1024
S5_BU_BUFFERS = 4
N_S5_INPUTS = 12
HOSTED_SEQS = 2
LONG_CHUNK_GROUP = 2
SEQ_GROUP_BYTES = 8 * 2 ** 20
CONV_W = 3
NORM_EPS = 1e-6
ROW_TILE = 512
PROJ_ROW_TILE = 1024
SUBLANES = 8
V7X_VMEM_MB = 64
VMEM_MB = V7X_VMEM_MB - 12
LARGE_VMEM_MB = V7X_VMEM_MB - 6


def _cparams(*sem, vmem_mb=None):
    return pltpu.CompilerParams(dimension_semantics=sem, vmem_limit_bytes=(vmem_mb or VMEM_MB) * 2 ** 20)


def _const_spec(shape):
    zeros = (0,) * len(shape)
    return pl.BlockSpec(shape, lambda *_: zeros, pipeline_mode=pl.Buffered(1))


def _stacked_spec(shape, index):
    where = (index,) + (0,) * len(shape)
    return pl.BlockSpec((None,) + tuple(shape), lambda *_: where, pipeline_mode=pl.Buffered(1))


def _rms(x, g):
    return x * lax.rsqrt(jnp.mean(x * x, axis=-1, keepdims=True) + NORM_EPS) * g


def _sigmoid(x):
    return 1.0 / (1.0 + jnp.exp(-x))


def _silu(x):
    return x * _sigmoid(x)


def _gelu_tanh(x):
    return 0.5 * x * (1.0 + jnp.tanh(math.sqrt(2.0 / math.pi) * (x + 0.044715 * (x * x * x))))


def _log_sigmoid(x):
    return jnp.minimum(x, 0.0) - jnp.log(1.0 + jnp.exp(-jnp.abs(x)))


def _dot(a, b):
    return jnp.dot(a, b, preferred_element_type=F32)


def _dot_nt(a, b):
    return lax.dot_general(a, b, (((1,), (1,)), ((), ())), preferred_element_type=F32)


def _dot_tn(a, b):
    return lax.dot_general(a, b, (((0,), (0,)), ((), ())), preferred_element_type=F32)


def _s5_kernel(x_ref, h0re_ref, h0im_ref, g0_ref, g1_ref, are_ref, aim_ref, bre_ref, bim_ref, cc_ref, dsk_ref,
               wglu_ref, *rest, bn, tt, x_bm, hosted, n_host_in):
    host_in, rest = rest[:n_host_in], rest[n_host_in:]
    o_ref, hre_ref, him_ref, rest = rest[0], rest[1], rest[2], rest[3:]
    host_out, (bu_sc, y_sc, bbd_ref, cbd_ref) = rest[:2 if hosted else 0], rest[2 if hosted else 0:]
    @pl.when(pl.program_id(0) == 0)
    def _():
        hre_ref[...] = h0re_ref[...]
        him_ref[...] = h0im_ref[...]
        ngl = S5_GROUPS // S5_SUPER

        def lane_tiler(k, n):
            r = lax.broadcasted_iota(jnp.int32, (k, n), 0)
            c = lax.broadcasted_iota(jnp.int32, (k, n), 1)
            return ((c & (k - 1)) == r).astype(BF16)

        tile_n = lane_tiler(S5_STATE, S5_SG_ST)
        tile_p = lane_tiler(S5_GROUP, S5_SG_CH)
        sh_p = S5_GROUP.bit_length() - 1
        sh_n = S5_STATE.bit_length() - 1
        rb = lax.broadcasted_iota(jnp.int32, (S5_SG_CH, S5_SG_ST), 0) >> sh_p
        cb = lax.broadcasted_iota(jnp.int32, (S5_SG_CH, S5_SG_ST), 1) >> sh_n
        rc = (lax.broadcasted_iota(jnp.int32, (2 * S5_SG_ST, S5_SG_CH), 0) >> sh_n) & (ngl - 1)
        cc = lax.broadcasted_iota(jnp.int32, (2 * S5_SG_ST, S5_SG_CH), 1) >> sh_p
        for sg in range(S5_SUPER):
            for part, ref in ((0, bre_ref), (1, bim_ref)):
                blk = _dot(ref[sg].astype(BF16), tile_n)
                bbd_ref[sg, :, S5_SG_ST * part:S5_SG_ST * (part + 1)] = jnp.where(rb == cb, blk, 0.0).astype(BF16)
            blk = _dot(cc_ref[sg].astype(BF16), tile_p)
            cbd_ref[sg] = jnp.where(rc == cc, blk, 0.0).astype(BF16)

    x = x_ref[...]
    if x_bm:
        x = jnp.swapaxes(x, 0, 1).reshape(tt * bn, D_MODEL)
    u = _rms(x, g0_ref[...])
    ub = u.astype(BF16)
    half = S5_SG_ST // 2
    host_stages = iter(())
    if hosted:
        hq, hk, hv, hs = host_in[:4]
        host_stages = _ret_unit_stages(hq, hk, hv, hs, host_out[0], host_out[1], hq.shape[1], hosted)
    for sg in range(S5_SUPER):
        buf = bu_sc.at[sg % S5_BU_BUFFERS]
        buf[...] = _dot(ub[:, S5_SG_CH * sg:S5_SG_CH * (sg + 1)], bbd_ref[sg])
        for hf in range(2):
            c_re = half * hf
            c_im = S5_SG_ST + half * hf
            a_lo = S5_SG_ST * sg + half * hf
            ar = jnp.broadcast_to(are_ref[:, a_lo:a_lo + half], (8, half))
            ai = jnp.broadcast_to(aim_ref[:, a_lo:a_lo + half], (8, half))
            for bt in range(bn // 8):
                r8 = 8 * bt
                hr = hre_ref[r8:r8 + 8, a_lo:a_lo + half]
                hi = him_ref[r8:r8 + 8, a_lo:a_lo + half]
                for t in range(tt):
                    r0 = t * bn + r8
                    bur = buf[r0:r0 + 8, c_re:c_re + half]
                    bui = buf[r0:r0 + 8, c_im:c_im + half]
                    hr, hi = ar * hr - ai * hi + bur, ar * hi + ai * hr + bui
                    buf[r0:r0 + 8, c_re:c_re + half] = hr
                    buf[r0:r0 + 8, c_im:c_im + half] = hi
                hre_ref[r8:r8 + 8, a_lo:a_lo + half] = hr
                him_ref[r8:r8 + 8, a_lo:a_lo + half] = hi
        y_sc[:, S5_SG_CH * sg:S5_SG_CH * (sg + 1)] = _dot(buf[...].astype(BF16), cbd_ref[sg])
        next(host_stages, None)
    y = y_sc[...] + dsk_ref[...] * u
    z = _dot(_gelu_tanh(y).astype(BF16), wglu_ref[...])
    mix = z[:, :D_MODEL] * _sigmoid(z[:, D_MODEL:])
    o_ref[...] = x + _rms(mix, g1_ref[...])


def _s5_layer(x, bn, h0_re, h0_im, g0, g1, lam_re, lam_im, b_re, b_im, c_re, c_im, log_dt, d_skip, w_glu,
              j, x_bm=False, host=None):
    rows = x.shape[0]
    tt = min(rows // bn, max(1, ROW_TILE // bn))
    tile = tt * bn
    steps = rows // tile
    host_ins, host_in_specs, host_out_specs, host_out_shapes = [], [], [], []
    aliases = {}
    if host is not None:
        hq, hk, hv, hs, first, acc = host
        assert first % HOSTED_SEQS == 0
        blk0 = first // HOSTED_SEQS
        host_ins = [hq, hk, hv, hs]
        for a in host_ins:
            host_in_specs.append(pl.BlockSpec((HOSTED_SEQS,) + a.shape[1:],
                                              lambda i, nd=a.ndim: (blk0 + i,) + (0,) * (nd - 1)))
        host_out_specs = [pl.BlockSpec((HOSTED_SEQS,) + hv.shape[1:], lambda i: (i, 0, 0)),
                          pl.BlockSpec((HOSTED_SEQS,) + hs.shape[1:], lambda i: (blk0 + i, 0, 0, 0))]
        host_out_shapes = [jax.ShapeDtypeStruct((steps * HOSTED_SEQS,) + hv.shape[1:], F32),
                           jax.ShapeDtypeStruct(hs.shape, F32)]
        if acc is not None:
            host_ins.append(acc)
            host_in_specs.append(pl.BlockSpec(memory_space=pl.ANY))
            aliases = {N_S5_INPUTS + 4: 4}
    o_spec = pl.BlockSpec((tile, D_MODEL), lambda i: (i, 0))
    x_spec = o_spec
    if x_bm:
        assert bn == SUBLANES
        x = x.reshape(bn, rows // bn, D_MODEL)
        x_spec = pl.BlockSpec((bn, tt, D_MODEL), lambda i: (0, i, 0))
    lam = lax.complex(lam_re, lam_im)
    dt = jnp.exp(log_dt)[:, None]
    a_bar = jnp.exp(lam * dt)
    b_bar = ((a_bar - 1.0) / lam)[..., None] * lax.complex(b_re, b_im)
    bt = b_bar.transpose(0, 2, 1).reshape(S5_SUPER, S5_SG_CH, S5_STATE)
    bt_re, bt_im = jnp.real(bt), jnp.imag(bt)
    ct_re = c_re.transpose(0, 2, 1).reshape(S5_SUPER, S5_SG_ST, S5_GROUP)
    ct_im = c_im.transpose(0, 2, 1).reshape(S5_SUPER, S5_SG_ST, S5_GROUP)
    ct = jnp.concatenate([ct_re, -ct_im], axis=1)
    a_re = jnp.real(a_bar).reshape(1, S5_GROUPS * S5_STATE)
    a_im = jnp.imag(a_bar).reshape(1, S5_GROUPS * S5_STATE)
    nst = S5_GROUPS * S5_STATE

    out, new_re, new_im, *hosted = pl.pallas_call(
        functools.partial(_s5_kernel, bn=bn, tt=tt, x_bm=x_bm, hosted=HOSTED_SEQS if host_ins else 0,
                          n_host_in=len(host_ins)),
        grid=(steps,),
        in_specs=[
            x_spec,
            _const_spec((bn, nst)), _const_spec((bn, nst)),
            _const_spec((1, D_MODEL)), _const_spec((1, D_MODEL)),
            _const_spec((1, nst)), _const_spec((1, nst)),
            _const_spec((S5_SUPER, S5_SG_CH, S5_STATE)), _const_spec((S5_SUPER, S5_SG_CH, S5_STATE)),
            _const_spec((S5_SUPER, 2 * S5_SG_ST, S5_GROUP)),
            _const_spec((1, D_MODEL)),
            _stacked_spec((D_MODEL, 2 * D_MODEL), j),
        ] + host_in_specs,
        out_specs=[o_spec, _const_spec((bn, nst)), _const_spec((bn, nst))] + host_out_specs,
        out_shape=[jax.ShapeDtypeStruct((rows, D_MODEL), F32), jax.ShapeDtypeStruct((bn, nst), F32),
                   jax.ShapeDtypeStruct((bn, nst), F32)] + host_out_shapes,
        scratch_shapes=[pltpu.VMEM((S5_BU_BUFFERS, tile, 2 * S5_SG_ST), F32), pltpu.VMEM((tile, D_MODEL), F32),
                        pltpu.VMEM((S5_SUPER, S5_SG_CH, 2 * S5_SG_ST), BF16),
                        pltpu.VMEM((S5_SUPER, 2 * S5_SG_ST, S5_SG_CH), BF16)],
        input_output_aliases=aliases,
        compiler_params=_cparams("arbitrary", vmem_mb=LARGE_VMEM_MB if host_ins else None),
        name="s5_layer",
    )(x, h0_re.reshape(bn, nst), h0_im.reshape(bn, nst), g0.reshape(1, -1), g1.reshape(1, -1), a_re, a_im,
      bt_re, bt_im, ct, d_skip.reshape(1, -1), w_glu, *host_ins)
    return out, new_re.reshape(bn, S5_GROUPS, S5_STATE), new_im.reshape(bn, S5_GROUPS, S5_STATE), tuple(hosted)


def _ffn_kernel(x_ref, cin_ref, g2_ref, g3_ref, wg_ref, wu_ref, wd_ref, cw_ref, cb_ref, *rest,
                bn, tile, x_bm, out_bm, cast_next):
    next_f32, rest = rest[:cast_next], rest[cast_next:]
    o_ref, cout_ref, rest = rest[0], rest[1], rest[2:]
    next_bf16, (hdn_sc, gext_sc, carry_sc) = rest[:cast_next], rest[cast_next:]
    i = pl.program_id(0)
    halo = 2 * bn
    base = gext_sc.shape[1] - tile

    @pl.when(i == 0)
    def _():
        carry_sc[...] = cin_ref[...]

    x = x_ref[...]
    if x_bm:
        x = jnp.swapaxes(x, 0, 1).reshape(tile, D_MODEL)
    h = _rms(x, g2_ref[...]).astype(BF16)
    for j in range(D_FF // FFN_SUB):
        sl = slice(FFN_SUB * j, FFN_SUB * (j + 1))
        buf = gext_sc.at[j % 2]
        gpre = _dot(h, wg_ref[:, sl])
        buf[base - halo:base, :] = carry_sc[:, sl]
        buf[base:base + tile, :] = gpre
        carry_sc[:, sl] = gpre[tile - halo:tile, :]
        gconv = cb_ref[:, sl] + cw_ref[0:1, sl] * buf[base - halo:base - halo + tile, :]
        gconv = gconv + cw_ref[1:2, sl] * buf[base - bn:base - bn + tile, :]
        gconv = gconv + cw_ref[2:3, sl] * gpre
        hdn_sc[:, sl] = (_silu(gconv) * _dot(h, wu_ref[:, sl])).astype(BF16)
        if j == 0:
            for src, dst in zip(next_f32, next_bf16):
                dst[...] = src[...].astype(BF16)
    y = x + _rms(_dot(hdn_sc[...], wd_ref[...]), g3_ref[...])
    if out_bm:
        y = jnp.swapaxes(y.reshape(tile // bn, bn, D_MODEL), 0, 1)
    o_ref[...] = y

    @pl.when(i == pl.num_programs(0) - 1)
    def _():
        cout_ref[...] = carry_sc[...]


def _ffn_layer(x, bn, cin, g2, g3, w_gate, w_up, conv_w, conv_b, w_down, x_bm=False, out_bm=False,
               next_weights=()):
    rows = x.shape[0]
    tile = min(FFN_ROW_TILE, rows)
    tt = tile // bn
    halo = 2 * bn
    steps = rows // tile
    cast_ins, cast_in_specs, cast_out_specs, cast_out_shapes = [], [], [], []
    assert w_gate.dtype == w_up.dtype == w_down.dtype == BF16
    for w, idx in next_weights:
        k, n = w.shape[1:]
        kb = k // steps
        assert kb * steps == k and kb % (2 * SUBLANES) == 0
        cast_ins.append(w)
        cast_in_specs.append(pl.BlockSpec((None, kb, n), lambda i, idx=idx: (idx, i, 0)))
        cast_out_specs.append(pl.BlockSpec((kb, n), lambda i: (i, 0)))
        cast_out_shapes.append(jax.ShapeDtypeStruct((k, n), BF16))
    assert not (x_bm or out_bm) or bn == SUBLANES
    tm_spec = pl.BlockSpec((tile, D_MODEL), lambda i: (i, 0))
    bm_spec = pl.BlockSpec((bn, tt, D_MODEL), lambda i: (0, i, 0))
    if x_bm:
        x = x.reshape(bn, rows // bn, D_MODEL)
    out_shape = (bn, rows // bn, D_MODEL) if out_bm else (rows, D_MODEL)
    out, cout, *cast = pl.pallas_call(
        functools.partial(_ffn_kernel, bn=bn, tile=tile, x_bm=x_bm, out_bm=out_bm, cast_next=len(cast_ins)),
        grid=(steps,),
        in_specs=[
            bm_spec if x_bm else tm_spec,
            _const_spec((halo, D_FF)),
            _const_spec((1, D_MODEL)), _const_spec((1, D_MODEL)),
            _const_spec((D_MODEL, D_FF)), _const_spec((D_MODEL, D_FF)), _const_spec((D_FF, D_MODEL)),
            _const_spec((CONV_W, D_FF)), _const_spec((1, D_FF)),
        ] + cast_in_specs,
        out_specs=[bm_spec if out_bm else tm_spec, _const_spec((halo, D_FF))] + cast_out_specs,
        out_shape=[jax.ShapeDtypeStruct(out_shape, F32), jax.ShapeDtypeStruct((halo, D_FF), F32)] + cast_out_shapes,
        scratch_shapes=[pltpu.VMEM((tile, D_FF), BF16),
                        pltpu.VMEM((2, halo + tile, FFN_SUB), F32),
                        pltpu.VMEM((halo, D_FF), F32)],
        compiler_params=_cparams("arbitrary", vmem_mb=LARGE_VMEM_MB),
        name="conv_ffn",
    )(x, cin, g2.reshape(1, -1), g3.reshape(1, -1), w_gate, w_up, w_down, conv_w, conv_b.reshape(1, -1), *cast_ins)
    return out.reshape(rows, D_MODEL), cout, tuple(cast)


def _ret_proj_kernel(x_ref, cos_ref, sin_ref, g_ref, wq_ref, wk_ref, wv_ref, q_ref, k_ref, v_ref):
    h = _rms(x_ref[...], g_ref[...]).astype(BF16)
    cos = cos_ref[...]
    sin = sin_ref[...]
    hw = RET_DK // 2
    for w_ref, dst, scale in ((wq_ref, q_ref, 1.0), (wk_ref, k_ref, RET_DK ** -0.5)):
        p = _dot(h, w_ref[...])
        for hd in range(RET_HEADS):
            lo = RET_DK * hd
            t1 = p[:, lo:lo + hw]
            t2 = p[:, lo + hw:lo + 2 * hw]
            dst[:, lo:lo + hw] = ((t1 * cos - t2 * sin) * scale).astype(BF16)
            dst[:, lo + hw:lo + 2 * hw] = ((t2 * cos + t1 * sin) * scale).astype(BF16)
    v_ref[...] = _dot(h, wv_ref[...]).astype(BF16)


def _ret_unit_stages(q_ref, k_ref, v_ref, sin_ref, o_ref, sout_ref, chunk, group):
    ri = lax.broadcasted_iota(jnp.int32, (chunk, chunk), 0)
    ci = lax.broadcasted_iota(jnp.int32, (chunk, chunk), 1)
    rel = (ri - ci).astype(F32)
    causal = ri >= ci
    idx = lax.broadcasted_iota(jnp.int32, (chunk, 1), 0).astype(F32)
    lgs = [math.log1p(-(2.0 ** (-5.0 - hd))) for hd in range(RET_HEADS)]
    units = [(g, hd) for g in range(group) for hd in range(RET_HEADS)]
    qsl = [slice(RET_DK * hd, RET_DK * (hd + 1)) for hd in range(RET_HEADS)]
    vsl = [slice(RET_DV * hd, RET_DV * (hd + 1)) for hd in range(RET_HEADS)]
    decay = [jnp.where(causal, jnp.exp(jnp.maximum(rel, 0.0) * lg), 0.0) for lg in lgs]
    scores = [(_dot_nt(q_ref[g, :, qsl[hd]], k_ref[g, :, qsl[hd]]) * decay[hd]).astype(BF16) for g, hd in units]
    yield
    for (g, hd), sc in zip(units, scores):
        q_decay = jnp.exp((idx + 1.0) * lgs[hd])
        inter = _dot(q_ref[g, :, qsl[hd]], sin_ref[g, hd].astype(BF16)) * q_decay
        o_ref[g, :, vsl[hd]] = _dot(sc, v_ref[g, :, vsl[hd]]) + inter
    yield
    for g, hd in units:
        k_decay = jnp.exp((chunk - 1.0 - idx) * lgs[hd])
        kd = (k_ref[g, :, qsl[hd]].astype(F32) * k_decay).astype(BF16)
        sout_ref[g, hd] = sin_ref[g, hd] * math.exp(chunk * lgs[hd]) + _dot_tn(kd, v_ref[g, :, vsl[hd]])


def _ret_chunk_kernel(q_ref, k_ref, v_ref, s0_ref, o_ref, sout_ref, *, chunk, group):
    @pl.when(pl.program_id(1) == 0)
    def _():
        sout_ref[...] = s0_ref[...]

    for _ in _ret_unit_stages(q_ref, k_ref, v_ref, sout_ref, o_ref, sout_ref, chunk, group):
        pass


def _ret_head_norm(oh, gate):
    mu = jnp.mean(oh, axis=-1, keepdims=True)
    ctr = oh - mu
    var = jnp.mean(ctr * ctr, axis=-1, keepdims=True)
    return (ctr * lax.rsqrt(var + NORM_EPS) * gate).astype(BF16)


def _ret_out_kernel(x_ref, o_ref, g0_ref, g1_ref, wg_ref, wo_ref, y_ref, on_sc):
    x = x_ref[...]
    h = _rms(x, g0_ref[...]).astype(BF16)
    gate = _silu(_dot(h, wg_ref[...]))
    for hd in range(RET_HEADS):
        sl = slice(RET_DV * hd, RET_DV * (hd + 1))
        on_sc[:, sl] = _ret_head_norm(o_ref[:, sl], gate[:, sl])
    y_ref[...] = x + _rms(_dot(on_sc[...], wo_ref[...]), g1_ref[...])


def _ret_chunk_out_kernel(q_ref, k_ref, v_ref, x_ref, s0_ref, g0_ref, g1_ref, wg_ref, wo_ref, y_ref, sout_ref,
                          o_sc, on_sc, *, chunk, group):
    @pl.when(pl.program_id(1) == 0)
    def _():
        sout_ref[...] = s0_ref[...]

    stages = _ret_unit_stages(q_ref, k_ref, v_ref, sout_ref, o_sc, sout_ref, chunk, group)
    n = group * chunk
    x = x_ref[...].reshape(n, D_MODEL)
    h = _rms(x, g0_ref[...]).astype(BF16)
    next(stages)
    next(stages)
    for hd in range(RET_HEADS):
        sl = slice(RET_DV * hd, RET_DV * (hd + 1))
        gate = _silu(_dot(h, wg_ref[:, sl]))
        on_sc[:, sl] = _ret_head_norm(o_sc[:, :, sl].reshape(n, RET_DV), gate)
    proj = _dot(on_sc[...], wo_ref[...])
    next(stages, None)
    y = x + _rms(proj, g1_ref[...])
    y_ref[...] = y.reshape(group, chunk, D_MODEL)


def _row_grid_call(kernel, row_ins, const_ins, outs, rows, tile, scratch=(), name=None, tab_ins=(),
                   vmem_mb=None):
    nt = rows // tile
    in_specs = [pl.BlockSpec((tile, a.shape[1]), lambda i: (i, 0)) for a in row_ins]
    for a in tab_ins:
        ntab = a.shape[0] // tile
        in_specs.append(pl.BlockSpec((tile, a.shape[1]), lambda i, ntab=ntab: (i % ntab, 0)))
    in_specs += [_const_spec(a.shape) for a in const_ins]
    return pl.pallas_call(
        kernel,
        grid=(nt,),
        in_specs=in_specs,
        out_specs=[pl.BlockSpec((tile, n), lambda i: (i, 0)) for n, _ in outs],
        out_shape=[jax.ShapeDtypeStruct((rows, n), dt) for n, dt in outs],
        scratch_shapes=list(scratch),
        compiler_params=_cparams("arbitrary", vmem_mb=vmem_mb),
        name=name,
    )(*row_ins, *tab_ins, *const_ins)


def _to_seq(a, bn, l, time_major):
    n = a.shape[1]
    return a.reshape(l, bn, n).transpose(1, 0, 2) if time_major else a.reshape(bn, l, n)


def _from_seq(a, bn, l, time_major):
    a = a.transpose(1, 0, 2) if time_major else a
    return a.reshape(bn * l, a.shape[2])


def _seq_group(bn, chunk, full_chunk, state_bytes):
    if chunk == full_chunk:
        return min(bn, LONG_CHUNK_GROUP)
    grp = max(1, min(bn, SEQ_GROUP_BYTES // state_bytes))
    while bn % grp:
        grp -= 1
    return grp


def _seq_chunking(l, chunk):
    if l % chunk == 0:
        return chunk
    assert l < chunk
    return l


def _ret_proj_stage(x, bn, l, time_major, pos0, g0, wq, wk, wv):
    rows = x.shape[0]
    ptile = min(PROJ_ROW_TILE, rows)
    half = RET_DK // 2
    freq = 1.0 / (ROPE_BASE ** jnp.linspace(0.0, 1.0, half, dtype=F32))
    pos = jnp.arange(pos0, pos0 + l, dtype=jnp.int32).astype(F32)
    ang = pos[:, None] * freq[None, :]
    cos, sin = jnp.cos(ang), jnp.sin(ang)
    if time_major:
        cos, sin = jnp.repeat(cos, bn, axis=0), jnp.repeat(sin, bn, axis=0)
    elif l < ptile:
        cos, sin = jnp.tile(cos, (ptile // l, 1)), jnp.tile(sin, (ptile // l, 1))
    q, k, v = _row_grid_call(
        _ret_proj_kernel, [x], [g0.reshape(1, -1), wq.astype(BF16), wk.astype(BF16), wv.astype(BF16)],
        [(RET_HEADS * RET_DK, BF16), (RET_HEADS * RET_DK, BF16), (RET_HEADS * RET_DV, BF16)],
        rows, ptile, name="ret_proj", tab_ins=[cos, sin], vmem_mb=LARGE_VMEM_MB)
    return tuple(_to_seq(a, bn, l, time_major) for a in (q, k, v))


def _ret_chunk_stage(q3, k3, v3, s0):
    bn, l, _ = q3.shape
    chunk = _seq_chunking(l, RET_CHUNK)
    grp = _seq_group(bn, chunk, RET_CHUNK, RET_HEADS * RET_DK * RET_DV * 4)
    return pl.pallas_call(
        functools.partial(_ret_chunk_kernel, chunk=chunk, group=grp),
        grid=(bn // grp, l // chunk),
        in_specs=[
            pl.BlockSpec((grp, chunk, RET_HEADS * RET_DK), lambda b, i: (b, i, 0)),
            pl.BlockSpec((grp, chunk, RET_HEADS * RET_DK), lambda b, i: (b, i, 0)),
            pl.BlockSpec((grp, chunk, RET_HEADS * RET_DV), lambda b, i: (b, i, 0)),
            pl.BlockSpec((grp, RET_HEADS, RET_DK, RET_DV), lambda b, i: (b, 0, 0, 0)),
        ],
        out_specs=[
            pl.BlockSpec((grp, chunk, RET_HEADS * RET_DV), lambda b, i: (b, i, 0)),
            pl.BlockSpec((grp, RET_HEADS, RET_DK, RET_DV), lambda b, i: (b, 0, 0, 0)),
        ],
        out_shape=[jax.ShapeDtypeStruct((bn, l, RET_HEADS * RET_DV), F32),
                   jax.ShapeDtypeStruct((bn, RET_HEADS, RET_DK, RET_DV), F32)],
        compiler_params=_cparams("arbitrary", "arbitrary"),
        name="ret_chunk",
    )(q3, k3, v3, s0)


def _ret_chunk_out_stage(x, q3, k3, v3, s0, g0, g1, wg, wo):
    bn, l, _ = q3.shape
    grp = _seq_group(bn, RET_CHUNK, RET_CHUNK, 0)
    seq_spec = lambda n: pl.BlockSpec((grp, RET_CHUNK, n), lambda b, i: (b, i, 0))
    state_spec = pl.BlockSpec((grp, RET_HEADS, RET_DK, RET_DV), lambda b, i: (b, 0, 0, 0))
    y, s_new = pl.pallas_call(
        functools.partial(_ret_chunk_out_kernel, chunk=RET_CHUNK, group=grp),
        grid=(bn // grp, l // RET_CHUNK),
        in_specs=[seq_spec(RET_HEADS * RET_DK), seq_spec(RET_HEADS * RET_DK), seq_spec(RET_HEADS * RET_DV),
                  seq_spec(D_MODEL), state_spec, _const_spec((1, D_MODEL)), _const_spec((1, D_MODEL)),
                  _const_spec((D_MODEL, RET_HEADS * RET_DV)), _const_spec((RET_HEADS * RET_DV, D_MODEL))],
        out_specs=[seq_spec(D_MODEL), state_spec],
        out_shape=[jax.ShapeDtypeStruct((bn, l, D_MODEL), F32),
                   jax.ShapeDtypeStruct((bn, RET_HEADS, RET_DK, RET_DV), F32)],
        scratch_shapes=[pltpu.VMEM((grp, RET_CHUNK, RET_HEADS * RET_DV), F32),
                        pltpu.VMEM((grp * RET_CHUNK, RET_HEADS * RET_DV), BF16)],
        compiler_params=_cparams("arbitrary", "arbitrary", vmem_mb=LARGE_VMEM_MB),
        name="ret_chunk_out",
    )(q3, k3, v3, x.reshape(bn, l, D_MODEL), s0, g0.reshape(1, -1), g1.reshape(1, -1), wg.astype(BF16),
      wo.astype(BF16))
    return y.reshape(bn * l, D_MODEL), s_new


def _ret_out_stage(x, o3, time_major, g0, g1, wg, wo):
    rows = x.shape[0]
    tile = min(ROW_TILE, rows)
    bn, l, _ = o3.shape
    o = _from_seq(o3, bn, l, time_major)
    (y,) = _row_grid_call(
        _ret_out_kernel, [x, o], [g0.reshape(1, -1), g1.reshape(1, -1), wg.astype(BF16), wo.astype(BF16)],
        [(D_MODEL, F32)], rows, tile, scratch=[pltpu.VMEM((tile, RET_HEADS * RET_DV), BF16)], name="ret_out")
    return y


def _gla_proj_kernel(x_ref, g_ref, wq_ref, wk_ref, wv_ref, wa1_ref, wa2_ref, ba_ref,
                     q_ref, k_ref, v_ref, la_ref):
    h = _rms(x_ref[...], g_ref[...]).astype(BF16)
    q_ref[...] = (_dot(h, wq_ref[...]) * (GLA_DK ** -0.5)).astype(BF16)
    k_ref[...] = _dot(h, wk_ref[...]).astype(BF16)
    v_ref[...] = _dot(h, wv_ref[...]).astype(BF16)
    low = _dot(h, wa1_ref[...]).astype(BF16)
    logit = _dot(low, wa2_ref[...]) + ba_ref[...]
    la_ref[...] = _log_sigmoid(logit) / GLA_GATE_NORM


def _gla_unit_stages(q_ref, k_ref, v_ref, la_ref, o_ref, st_sc, chunk, group):
    ri = lax.broadcasted_iota(jnp.int32, (chunk, chunk), 0)
    ci = lax.broadcasted_iota(jnp.int32, (chunk, chunk), 1)
    causal = ri >= ci
    tri = causal.astype(BF16)
    nk = GLA_HEADS * GLA_DK
    units = [(g, hd) for g in range(group) for hd in range(GLA_HEADS)]
    bcs = []
    for g in range(group):
        la = la_ref[g]
        p1 = la.astype(BF16)
        r1 = la - p1.astype(F32)
        p2 = r1.astype(BF16)
        p3 = (r1 - p2.astype(F32)).astype(BF16)
        acc = _dot(tri, jnp.concatenate([p1, p2, p3], axis=1))
        bcs.append(acc[:, :nk] + acc[:, nk:2 * nk] + acc[:, 2 * nk:])
    prep = []
    for g, hd in units:
        ksl = slice(GLA_DK * hd, GLA_DK * (hd + 1))
        bc = bcs[g][:, ksl]
        blast = bc[chunk - 1:chunk, :]
        ref = 0.5 * (bc[0:1, :] + blast)
        qh = q_ref[g, :, ksl].astype(F32)
        kh = k_ref[g, :, ksl].astype(F32)
        qt = (qh * jnp.exp(bc - ref)).astype(BF16)
        kt = (kh * jnp.exp(ref - bc)).astype(BF16)
        qg = (qh * jnp.exp(bc)).astype(BF16)
        kg = (kh * jnp.exp(blast - bc)).astype(BF16)
        row_decay = jnp.broadcast_to(jnp.exp(blast), (SUBLANES, GLA_DK)).T[:, :1]
        prep.append((qt, kt, qg, kg, row_decay))
    yield
    scores = [jnp.where(causal, _dot_nt(qt, kt), 0.0).astype(BF16) for qt, kt, _, _, _ in prep]
    for (g, hd), sc, (_, _, qg, kg, row_decay) in zip(units, scores, prep):
        vsl = slice(GLA_DV * hd, GLA_DV * (hd + 1))
        vh = v_ref[g, :, vsl]
        st = st_sc[g, hd]
        o_ref[g, :, vsl] = _dot(sc, vh) + _dot(qg, st.astype(BF16))
        st_sc[g, hd] = st * row_decay + _dot_tn(kg, vh)


def _gla_copy_states(src, dst, group):
    for g in range(group):
        for hd in range(GLA_HEADS):
            dst[g, hd] = src[g, hd]


def _gla_chunk_kernel(q_ref, k_ref, v_ref, la_ref, s0_ref, o_ref, sout_ref, st_sc, *, chunk, group):
    @pl.when(pl.program_id(1) == 0)
    def _():
        _gla_copy_states(s0_ref, st_sc, group)

    for _ in _gla_unit_stages(q_ref, k_ref, v_ref, la_ref, o_ref, st_sc, chunk, group):
        pass

    @pl.when(pl.program_id(1) == pl.num_programs(1) - 1)
    def _():
        _gla_copy_states(st_sc, sout_ref, group)


def _gla_head_norm(oh, norm_g, gate):
    on = oh * lax.rsqrt(jnp.mean(oh * oh, axis=-1, keepdims=True) + NORM_EPS) * norm_g
    return (on * gate).astype(BF16)


def _gla_chunk_out_kernel(q_ref, k_ref, v_ref, la_ref, x_ref, s0_ref, g0_ref, g1_ref, ng_ref, wg_ref, wo_ref,
                          y_ref, sout_ref, st_sc, o_sc, gate_sc, on_sc, *, chunk, group):
    @pl.when(pl.program_id(1) == 0)
    def _():
        _gla_copy_states(s0_ref, st_sc, group)

    n = group * chunk
    x = x_ref[...].reshape(n, D_MODEL)
    h = _rms(x, g0_ref[...]).astype(BF16)
    stages = _gla_unit_stages(q_ref, k_ref, v_ref, la_ref, o_sc, st_sc, chunk, group)
    next(stages)
    gate_sc[...] = _silu(_dot(h, wg_ref[...]))
    next(stages, None)
    for hd in range(GLA_HEADS):
        sl = slice(GLA_DV * hd, GLA_DV * (hd + 1))
        on_sc[:, sl] = _gla_head_norm(o_sc[:, :, sl].reshape(n, GLA_DV), ng_ref[...], gate_sc[:, sl])
    y = x + _rms(_dot(on_sc[...], wo_ref[...]), g1_ref[...])
    y_ref[...] = y.reshape(group, chunk, D_MODEL)

    @pl.when(pl.program_id(1) == pl.num_programs(1) - 1)
    def _():
        _gla_copy_states(st_sc, sout_ref, group)


def _gla_out_kernel(x_ref, o_ref, g0_ref, g1_ref, ng_ref, wg_ref, wo_ref, y_ref, on_sc):
    x = x_ref[...]
    h = _rms(x, g0_ref[...]).astype(BF16)
    gate = _silu(_dot(h, wg_ref[...]))
    for hd in range(GLA_HEADS):
        sl = slice(GLA_DV * hd, GLA_DV * (hd + 1))
        on_sc[:, sl] = _gla_head_norm(o_ref[:, sl], ng_ref[...], gate[:, sl])
    y_ref[...] = x + _rms(_dot(on_sc[...], wo_ref[...]), g1_ref[...])


def _gla_layer(x, bn, l, time_major, s0, g0, g1, wq, wk, wv, wg, wa1, wa2, ba, norm_g, wo):
    rows = x.shape[0]
    tile = min(PROJ_ROW_TILE, rows)
    lanes = 128
    wa1p = jnp.pad(wa1, ((0, 0), (0, lanes - GLA_GATE_RANK))).astype(BF16)
    wa2p = jnp.pad(wa2, ((0, lanes - GLA_GATE_RANK), (0, 0))).astype(BF16)
    q, k, v, la = _row_grid_call(
        _gla_proj_kernel, [x],
        [g0.reshape(1, -1), wq.astype(BF16), wk.astype(BF16), wv.astype(BF16), wa1p, wa2p, ba.reshape(1, -1)],
        [(GLA_HEADS * GLA_DK, BF16), (GLA_HEADS * GLA_DK, BF16), (GLA_HEADS * GLA_DV, BF16),
         (GLA_HEADS * GLA_DK, F32)],
        rows, tile, name="gla_proj", vmem_mb=LARGE_VMEM_MB)
    chunk = _seq_chunking(l, GLA_CHUNK)
    q3, k3, v3, la3 = (_to_seq(a, bn, l, time_major) for a in (q, k, v, la))
    nc = l // chunk
    grp = GLA_GROUP if bn % GLA_GROUP == 0 else 1
    out_consts = [g0.reshape(1, -1), g1.reshape(1, -1), norm_g.reshape(1, -1), wg.astype(BF16), wo.astype(BF16)]
    if not time_major and chunk == GLA_CHUNK:
        seq_spec = lambda n: pl.BlockSpec((grp, chunk, n), lambda b, i: (b, i, 0))
        state_spec = pl.BlockSpec((grp, GLA_HEADS, GLA_DK, GLA_DV), lambda b, i: (b, 0, 0, 0))
        y, s_new = pl.pallas_call(
            functools.partial(_gla_chunk_out_kernel, chunk=chunk, group=grp),
            grid=(bn // grp, nc),
            in_specs=[seq_spec(GLA_HEADS * GLA_DK), seq_spec(GLA_HEADS * GLA_DK), seq_spec(GLA_HEADS * GLA_DV),
                      seq_spec(GLA_HEADS * GLA_DK), seq_spec(D_MODEL), state_spec]
            + [_const_spec(a.shape) for a in out_consts],
            out_specs=[seq_spec(D_MODEL), state_spec],
            out_shape=[jax.ShapeDtypeStruct((bn, l, D_MODEL), F32),
                       jax.ShapeDtypeStruct((bn, GLA_HEADS, GLA_DK, GLA_DV), F32)],
            scratch_shapes=[pltpu.VMEM((grp, GLA_HEADS, GLA_DK, GLA_DV), F32),
                            pltpu.VMEM((grp, chunk, GLA_HEADS * GLA_DV), F32),
                            pltpu.VMEM((grp * chunk, GLA_HEADS * GLA_DV), F32),
                            pltpu.VMEM((grp * chunk, GLA_HEADS * GLA_DV), BF16)],
            compiler_params=_cparams("arbitrary", "arbitrary"),
            name="gla_chunk_out",
        )(q3, k3, v3, la3, x.reshape(bn, l, D_MODEL), s0, *out_consts)
        return y.reshape(rows, D_MODEL), s_new
    o3, s_new = pl.pallas_call(
        functools.partial(_gla_chunk_kernel, chunk=chunk, group=grp),
        grid=(bn // grp, nc),
        in_specs=[
            pl.BlockSpec((grp, chunk, GLA_HEADS * GLA_DK), lambda b, i: (b, i, 0)),
            pl.BlockSpec((grp, chunk, GLA_HEADS * GLA_DK), lambda b, i: (b, i, 0)),
            pl.BlockSpec((grp, chunk, GLA_HEADS * GLA_DV), lambda b, i: (b, i, 0)),
            pl.BlockSpec((grp, chunk, GLA_HEADS * GLA_DK), lambda b, i: (b, i, 0)),
            pl.BlockSpec((grp, GLA_HEADS, GLA_DK, GLA_DV), lambda b, i: (b, 0, 0, 0)),
        ],
        out_specs=[
            pl.BlockSpec((grp, chunk, GLA_HEADS * GLA_DV), lambda b, i: (b, i, 0)),
            pl.BlockSpec((grp, GLA_HEADS, GLA_DK, GLA_DV), lambda b, i: (b, 0, 0, 0)),
        ],
        out_shape=[jax.ShapeDtypeStruct((bn, l, GLA_HEADS * GLA_DV), F32),
                   jax.ShapeDtypeStruct((bn, GLA_HEADS, GLA_DK, GLA_DV), F32)],
        scratch_shapes=[pltpu.VMEM((grp, GLA_HEADS, GLA_DK, GLA_DV), F32)],
        compiler_params=_cparams("arbitrary", "arbitrary"),
        name="gla_chunk",
    )(q3, k3, v3, la3, s0)
    o = _from_seq(o3, bn, l, time_major)
    (y,) = _row_grid_call(
        _gla_out_kernel, [x, o], out_consts,
        [(D_MODEL, F32)], rows, tile, scratch=[pltpu.VMEM((tile, GLA_HEADS * GLA_DV), BF16)], name="gla_out",
        vmem_mb=LARGE_VMEM_MB)
    return y, s_new


def _transpose_rows(x, a, b):
    return x.reshape(a, b, x.shape[1]).transpose(1, 0, 2).reshape(a * b, x.shape[1])


def _trunk(x, pos0, s5_re, s5_im, ret_s, gla_s, conv_buf, p, batch_major_mixers, bf16_w, cast_weights,
           defer_ret=False, host=None):
    bf16_w = dict(bf16_w)

    def w(name, idx):
        return bf16_w.get((name, idx), p[name][idx])
    bn, l, _ = x.shape
    depth = p['norm_g'].shape[0]
    assert not batch_major_mixers or bn == SUBLANES
    rows = x.reshape(bn * l, D_MODEL)
    time_major = False
    if not batch_major_mixers:
        rows, time_major = _transpose_rows(rows, bn, l), True
    new_re, new_im, new_ret, new_gla, new_conv = [], [], [], [], []
    hosted_o, hosted_s, hosted_n = [], None, 0
    for layer in range(depth):
        g = p['norm_g'][layer]
        kind = layer % N_MIXERS
        j = layer // N_MIXERS
        if kind == 0:
            todo_host = None
            if host is not None and hosted_n < host[0].shape[0]:
                todo_host = tuple(host) + (hosted_n, hosted_s)
            rows, hr, hi, hres = _s5_layer(rows, bn, s5_re[j], s5_im[j], g[0], g[1],
                                           p['s5_lambda_re'][j], p['s5_lambda_im'][j], p['s5_b_re'][j],
                                           p['s5_b_im'][j], p['s5_c_re'][j], p['s5_c_im'][j], p['s5_log_dt'][j],
                                           p['s5_d'][j], p['s5_w_glu'], j, x_bm=not time_major, host=todo_host)
            if hres:
                hosted_o.append(hres[0])
                hosted_s = hres[1]
                hosted_n += hres[0].shape[0]
            time_major = True
            new_re.append(hr)
            new_im.append(hi)
        elif kind == 1:
            q3, k3, v3 = _ret_proj_stage(rows, bn, l, time_major, pos0, g[0], w('ret_wq', j), w('ret_wk', j),
                                         w('ret_wv', j))
            if not defer_ret and not time_major and l % RET_CHUNK == 0:
                rows, s = _ret_chunk_out_stage(rows, q3, k3, v3, ret_s[j], g[0], g[1], w('ret_wg', j),
                                               w('ret_wo', j))
            else:
                if defer_ret:
                    o3, s, more = yield q3, k3, v3, ret_s[j]
                    bf16_w.update(more)
                else:
                    o3, s = _ret_chunk_stage(q3, k3, v3, ret_s[j])
                rows = _ret_out_stage(rows, o3, time_major, g[0], g[1], w('ret_wg', j), w('ret_wo', j))
            new_ret.append(s)
        else:
            rows, s = _gla_layer(rows, bn, l, time_major, gla_s[j], g[0], g[1], w('gla_wq', j), w('gla_wk', j),
                                 w('gla_wv', j), w('gla_wg', j), p['gla_wa1'][j], p['gla_wa2'][j], p['gla_ba'][j],
                                 p['gla_norm_g'][j], w('gla_wo', j))
            new_gla.append(s)
        next_is_s5 = layer + 1 < depth and (layer + 1) % N_MIXERS == 0
        out_bm = batch_major_mixers and not next_is_s5
        cin = conv_buf[layer].transpose(1, 0, 2).reshape((CONV_W - 1) * bn, D_FF)
        todo = []
        if cast_weights and layer + 1 < depth:
            mixer = MIXER_WEIGHTS[(layer + 1) % N_MIXERS]
            todo = [(n, layer + 1) for n in FFN_WEIGHTS] + [(n, (layer + 1) // N_MIXERS) for n in mixer]
            todo = [key for key in todo if key not in bf16_w]
        rows, cout, cast = _ffn_layer(rows, bn, cin, g[2], g[3], w('ffn_w_gate', layer), w('ffn_w_up', layer),
                                      p['ffn_conv_w'][layer], p['ffn_conv_b'][layer], w('ffn_w_down', layer),
                                      x_bm=not time_major, out_bm=out_bm,
                                      next_weights=[(p[n], idx) for n, idx in todo])
        bf16_w.update(zip(todo, cast))
        time_major = not out_bm
        new_conv.append(cout.reshape(CONV_W - 1, bn, D_FF).transpose(1, 0, 2))
    if time_major:
        rows = _transpose_rows(rows, l, bn)
    y = rows.reshape(bn, l, D_MODEL)
    hosted = None
    if host is not None:
        assert hosted_n == host[0].shape[0]
        hosted = (jnp.concatenate(hosted_o), hosted_s)
    outs = (y, jnp.stack(new_re), jnp.stack(new_im), jnp.stack(new_ret), jnp.stack(new_gla),
            jnp.stack(new_conv))
    return outs, bf16_w, hosted


def _finish(gen, reply=None):
    try:
        gen.send(reply)
    except StopIteration as done:
        return done.value
    raise AssertionError("unexpected request")


def kernel(x_prompt, x_sample, state_s5_re, state_s5_im, state_ret, state_gla, cache_ffn_conv,
           norm_g, s5_lambda_re, s5_lambda_im, s5_b_re, s5_b_im, s5_c_re, s5_c_im, s5_log_dt, s5_d, s5_w_glu,
           ret_wq, ret_wk, ret_wv, ret_wg, ret_wo,
           gla_wq, gla_wk, gla_wv, gla_wg, gla_wa1, gla_wa2, gla_ba, gla_norm_g, gla_wo,
           ffn_w_gate, ffn_w_up, ffn_conv_w, ffn_conv_b, ffn_w_down):
    p = dict(norm_g=norm_g, s5_lambda_re=s5_lambda_re, s5_lambda_im=s5_lambda_im, s5_b_re=s5_b_re,
             s5_b_im=s5_b_im, s5_c_re=s5_c_re, s5_c_im=s5_c_im, s5_log_dt=s5_log_dt, s5_d=s5_d,
             s5_w_glu=s5_w_glu.astype(BF16), ret_wq=ret_wq, ret_wk=ret_wk, ret_wv=ret_wv, ret_wg=ret_wg, ret_wo=ret_wo,
             gla_wq=gla_wq, gla_wk=gla_wk, gla_wv=gla_wv, gla_wg=gla_wg, gla_wa1=gla_wa1, gla_wa2=gla_wa2,
             gla_ba=gla_ba, gla_norm_g=gla_norm_g, gla_wo=gla_wo, ffn_w_gate=ffn_w_gate, ffn_w_up=ffn_w_up,
             ffn_conv_w=ffn_conv_w, ffn_conv_b=ffn_conv_b, ffn_w_down=ffn_w_down)
    bp = x_prompt.shape[0]
    z_s5 = jnp.zeros((state_s5_re.shape[0], bp) + state_s5_re.shape[2:], F32)
    z_ret = jnp.zeros((state_ret.shape[0], bp) + state_ret.shape[2:], F32)
    z_gla = jnp.zeros((state_gla.shape[0], bp) + state_gla.shape[2:], F32)
    z_conv = jnp.zeros((cache_ffn_conv.shape[0], bp) + cache_ffn_conv.shape[2:], x_prompt.dtype)
    first = {(n, 0): p[n][0].astype(BF16) for n in FFN_WEIGHTS}
    first.update({(n, j): p[n][j].astype(BF16) for n in ('ret_wq', 'ret_wk', 'ret_wv') for j in range(ret_wq.shape[0])})
    sample = _trunk(x_sample, PAST_LEN, state_s5_re, state_s5_im, state_ret, state_gla, cache_ffn_conv, p,
                    False, first, False, defer_ret=True)
    request = next(sample)
    outs_p, bf16_w, hosted = _finish(_trunk(x_prompt, 0, z_s5, z_s5, z_ret, z_gla, z_conv, p, True, first, True,
                                            host=request))
    outs_s, _, _ = _finish(sample, hosted + (bf16_w,))
    return (outs_p[0], outs_s[0]) + tuple(outs_p[1:]) + tuple(outs_s[1:])
```

```python
import functools
import math

import jax
import jax.numpy as jnp
from jax import lax
from jax.experimental import pallas as pl
from jax.experimental.pallas import tpu as pltpu

F32 = jnp.float32
BF16 = jnp.bfloat16

D_MODEL = 1024
PAST_LEN = 16384
N_MIXERS = 3
FFN_WEIGHTS = ('ffn_w_gate', 'ffn_w_up', 'ffn_w_down')
MIXER_WEIGHTS = ((), ('ret_wq', 'ret_wk', 'ret_wv', 'ret_wg', 'ret_wo'),
                 ('gla_wq', 'gla_wk', 'gla_wv', 'gla_wg', 'gla_wo'))
S5_GROUP = 16
S5_GROUPS = D_MODEL // S5_GROUP
S5_STATE = 64
S5_SUPER = 4
S5_SG_CH = D_MODEL // S5_SUPER
S5_SG_ST = (S5_GROUPS // S5_SUPER) * S5_STATE
RET_HEADS = 4
RET_DK = D_MODEL // RET_HEADS
RET_DV = 2 * D_MODEL // RET_HEADS
RET_CHUNK = 256
ROPE_BASE = 10000.0
GLA_HEADS = 4
GLA_DK = D_MODEL // 2 // GLA_HEADS
GLA_DV = D_MODEL // GLA_HEADS
GLA_GATE_RANK = 16
GLA_GATE_NORM = 16.0
GLA_CHUNK = 64
GLA_GROUP = 8
D_FF = ((8 * D_MODEL // 3 + 255) // 256) * 256
FFN_SUB = 256
FFN_ROW_TILE = 1024
S5_BU_BUFFERS = 4
N_S5_INPUTS = 12
HOSTED_SEQS = 2
LONG_CHUNK_GROUP = 2
SEQ_GROUP_BYTES = 8 * 2 ** 20
CONV_W = 3
NORM_EPS = 1e-6
ROW_TILE = 512
PROJ_ROW_TILE = 1024
SUBLANES = 8
V7X_VMEM_MB = 64
VMEM_MB = V7X_VMEM_MB - 12
LARGE_VMEM_MB = V7X_VMEM_MB - 6


def _cparams(*sem, vmem_mb=None):
    return pltpu.CompilerParams(dimension_semantics=sem, vmem_limit_bytes=(vmem_mb or VMEM_MB) * 2 ** 20)


def _const_spec(shape):
    zeros = (0,) * len(shape)
    return pl.BlockSpec(shape, lambda *_: zeros, pipeline_mode=pl.Buffered(1))


def _stacked_spec(shape, index):
    where = (index,) + (0,) * len(shape)
    return pl.BlockSpec((None,) + tuple(shape), lambda *_: where, pipeline_mode=pl.Buffered(1))


def _rms(x, g):
    return x * lax.rsqrt(jnp.mean(x * x, axis=-1, keepdims=True) + NORM_EPS) * g


def _sigmoid(x):
    return 1.0 / (1.0 + jnp.exp(-x))


def _silu(x):
    return x * _sigmoid(x)


def _gelu_tanh(x):
    return 0.5 * x * (1.0 + jnp.tanh(math.sqrt(2.0 / math.pi) * (x + 0.044715 * (x * x * x))))


def _log_sigmoid(x):
    return jnp.minimum(x, 0.0) - jnp.log(1.0 + jnp.exp(-jnp.abs(x)))


def _dot(a, b):
    return jnp.dot(a, b, preferred_element_type=F32)


def _dot_nt(a, b):
    return lax.dot_general(a, b, (((1,), (1,)), ((), ())), preferred_element_type=F32)


def _dot_tn(a, b):
    return lax.dot_general(a, b, (((0,), (0,)), ((), ())), preferred_element_type=F32)


def _s5_kernel(x_ref, h0re_ref, h0im_ref, g0_ref, g1_ref, are_ref, aim_ref, bre_ref, bim_ref, cc_ref, dsk_ref,
               wglu_ref, *rest, bn, tt, x_bm, hosted, n_host_in):
    host_in, rest = rest[:n_host_in], rest[n_host_in:]
    o_ref, hre_ref, him_ref, rest = rest[0], rest[1], rest[2], rest[3:]
    host_out, (bu_sc, y_sc, bbd_ref, cbd_ref) = rest[:2 if hosted else 0], rest[2 if hosted else 0:]
    @pl.when(pl.program_id(0) == 0)
    def _():
        hre_ref[...] = h0re_ref[...]
        him_ref[...] = h0im_ref[...]
        ngl = S5_GROUPS // S5_SUPER

        def lane_tiler(k, n):
            r = lax.broadcasted_iota(jnp.int32, (k, n), 0)
            c = lax.broadcasted_iota(jnp.int32, (k, n), 1)
            return ((c & (k - 1)) == r).astype(BF16)

        tile_n = lane_tiler(S5_STATE, S5_SG_ST)
        tile_p = lane_tiler(S5_GROUP, S5_SG_CH)
        sh_p = S5_GROUP.bit_length() - 1
        sh_n = S5_STATE.bit_length() - 1
        rb = lax.broadcasted_iota(jnp.int32, (S5_SG_CH, S5_SG_ST), 0) >> sh_p
        cb = lax.broadcasted_iota(jnp.int32, (S5_SG_CH, S5_SG_ST), 1) >> sh_n
        rc = (lax.broadcasted_iota(jnp.int32, (2 * S5_SG_ST, S5_SG_CH), 0) >> sh_n) & (ngl - 1)
        cc = lax.broadcasted_iota(jnp.int32, (2 * S5_SG_ST, S5_SG_CH), 1) >> sh_p
        for sg in range(S5_SUPER):
            for part, ref in ((0, bre_ref), (1, bim_ref)):
                blk = _dot(ref[sg].astype(BF16), tile_n)
                bbd_ref[sg, :, S5_SG_ST * part:S5_SG_ST * (part + 1)] = jnp.where(rb == cb, blk, 0.0).astype(BF16)
            blk = _dot(cc_ref[sg].astype(BF16), tile_p)
            cbd_ref[sg] = jnp.where(rc == cc, blk, 0.0).astype(BF16)

    x = x_ref[...]
    if x_bm:
        x = jnp.swapaxes(x, 0, 1).reshape(tt * bn, D_MODEL)
    u = _rms(x, g0_ref[...])
    ub = u.astype(BF16)
    half = S5_SG_ST // 2
    host_stages = iter(())
    if hosted:
        hq, hk, hv, hs = host_in[:4]
        host_stages = _ret_unit_stages(hq, hk, hv, hs, host_out[0], host_out[1], hq.shape[1], hosted)
    for sg in range(S5_SUPER):
        buf = bu_sc.at[sg % S5_BU_BUFFERS]
        buf[...] = _dot(ub[:, S5_SG_CH * sg:S5_SG_CH * (sg + 1)], bbd_ref[sg])
        for hf in range(2):
            c_re = half * hf
            c_im = S5_SG_ST + half * hf
            a_lo = S5_SG_ST * sg + half * hf
            ar = jnp.broadcast_to(are_ref[:, a_lo:a_lo + half], (8, half))
            ai = jnp.broadcast_to(aim_ref[:, a_lo:a_lo + half], (8, half))
            for bt in range(bn // 8):
                r8 = 8 * bt
                hr = hre_ref[r8:r8 + 8, a_lo:a_lo + half]
                hi = him_ref[r8:r8 + 8, a_lo:a_lo + half]
                for t in range(tt):
                    r0 = t * bn + r8
                    bur = buf[r0:r0 + 8, c_re:c_re + half]
                    bui = buf[r0:r0 + 8, c_im:c_im + half]
                    hr, hi = ar * hr - ai * hi + bur, ar * hi + ai * hr + bui
                    buf[r0:r0 + 8, c_re:c_re + half] = hr
                    buf[r0:r0 + 8, c_im:c_im + half] = hi
                hre_ref[r8:r8 + 8, a_lo:a_lo + half] = hr
                him_ref[r8:r8 + 8, a_lo:a_lo + half] = hi
        y_sc[:, S5_SG_CH * sg:S5_SG_CH * (sg + 1)] = _dot(buf[...].astype(BF16), cbd_ref[sg])
        next(host_stages, None)
    y = y_sc[...] + dsk_ref[...] * u
    z = _dot(_gelu_tanh(y).astype(BF16), wglu_ref[...])
    mix = z[:, :D_MODEL] * _sigmoid(z[:, D_MODEL:])
    o_ref[...] = x + _rms(mix, g1_ref[...])


def _s5_layer(x, bn, h0_re, h0_im, g0, g1, lam_re, lam_im, b_re, b_im, c_re, c_im, log_dt, d_skip, w_glu,
              j, x_bm=False, host=None):
    rows = x.shape[0]
    tt = min(rows // bn, max(1, ROW_TILE // bn))
    tile = tt * bn
    steps = rows // tile
    host_ins, host_in_specs, host_out_specs, host_out_shapes = [], [], [], []
    aliases = {}
    if host is not None:
        hq, hk, hv, hs, first, acc = host
        assert first % HOSTED_SEQS == 0
        blk0 = first // HOSTED_SEQS
        host_ins = [hq, hk, hv, hs]
        for a in host_ins:
            host_in_specs.append(pl.BlockSpec((HOSTED_SEQS,) + a.shape[1:],
                                              lambda i, nd=a.ndim: (blk0 + i,) + (0,) * (nd - 1)))
        host_out_specs = [pl.BlockSpec((HOSTED_SEQS,) + hv.shape[1:], lambda i: (i, 0, 0)),
                          pl.BlockSpec((HOSTED_SEQS,) + hs.shape[1:], lambda i: (blk0 + i, 0, 0, 0))]
        host_out_shapes = [jax.ShapeDtypeStruct((steps * HOSTED_SEQS,) + hv.shape[1:], F32),
                           jax.ShapeDtypeStruct(hs.shape, F32)]
        if acc is not None:
            host_ins.append(acc)
            host_in_specs.append(pl.BlockSpec(memory_space=pl.ANY))
            aliases = {N_S5_INPUTS + 4: 4}
    o_spec = pl.BlockSpec((tile, D_MODEL), lambda i: (i, 0))
    x_spec = o_spec
    if x_bm:
        assert bn == SUBLANES
        x = x.reshape(bn, rows // bn, D_MODEL)
        x_spec = pl.BlockSpec((bn, tt, D_MODEL), lambda i: (0, i, 0))
    lam = lax.complex(lam_re, lam_im)
    dt = jnp.exp(log_dt)[:, None]
    a_bar = jnp.exp(lam * dt)
    b_bar = ((a_bar - 1.0) / lam)[..., None] * lax.complex(b_re, b_im)
    bt = b_bar.transpose(0, 2, 1).reshape(S5_SUPER, S5_SG_CH, S5_STATE)
    bt_re, bt_im = jnp.real(bt), jnp.imag(bt)
    ct_re = c_re.transpose(0, 2, 1).reshape(S5_SUPER, S5_SG_ST, S5_GROUP)
    ct_im = c_im.transpose(0, 2, 1).reshape(S5_SUPER, S5_SG_ST, S5_GROUP)
    ct = jnp.concatenate([ct_re, -ct_im], axis=1)
    a_re = jnp.real(a_bar).reshape(1, S5_GROUPS * S5_STATE)
    a_im = jnp.imag(a_bar).reshape(1, S5_GROUPS * S5_STATE)
    nst = S5_GROUPS * S5_STATE

    out, new_re, new_im, *hosted = pl.pallas_call(
        functools.partial(_s5_kernel, bn=bn, tt=tt, x_bm=x_bm, hosted=HOSTED_SEQS if host_ins else 0,
                          n_host_in=len(host_ins)),
        grid=(steps,),
        in_specs=[
            x_spec,
            _const_spec((bn, nst)), _const_spec((bn, nst)),
            _const_spec((1, D_MODEL)), _const_spec((1, D_MODEL)),
            _const_spec((1, nst)), _const_spec((1, nst)),
            _const_spec((S5_SUPER, S5_SG_CH, S5_STATE)), _const_spec((S5_SUPER, S5_SG_CH, S5_STATE)),
            _const_spec((S5_SUPER, 2 * S5_SG_ST, S5_GROUP)),
            _const_spec((1, D_MODEL)),
            _stacked_spec((D_MODEL, 2 * D_MODEL), j),
        ] + host_in_specs,
        out_specs=[o_spec, _const_spec((bn, nst)), _const_spec((bn, nst))] + host_out_specs,
        out_shape=[jax.ShapeDtypeStruct((rows, D_MODEL), F32), jax.ShapeDtypeStruct((bn, nst), F32),
                   jax.ShapeDtypeStruct((bn, nst), F32)] + host_out_shapes,
        scratch_shapes=[pltpu.VMEM((S5_BU_BUFFERS, tile, 2 * S5_SG_ST), F32), pltpu.VMEM((tile, D_MODEL), F32),
                        pltpu.VMEM((S5_SUPER, S5_SG_CH, 2 * S5_SG_ST), BF16),
                        pltpu.VMEM((S5_SUPER, 2 * S5_SG_ST, S5_SG_CH), BF16)],
        input_output_aliases=aliases,
        compiler_params=_cparams("arbitrary", vmem_mb=LARGE_VMEM_MB if host_ins else None),
        name="s5_layer",
    )(x, h0_re.reshape(bn, nst), h0_im.reshape(bn, nst), g0.reshape(1, -1), g1.reshape(1, -1), a_re, a_im,
      bt_re, bt_im, ct, d_skip.reshape(1, -1), w_glu, *host_ins)
    return out, new_re.reshape(bn, S5_GROUPS, S5_STATE), new_im.reshape(bn, S5_GROUPS, S5_STATE), tuple(hosted)


def _ffn_kernel(x_ref, cin_ref, g2_ref, g3_ref, wg_ref, wu_ref, wd_ref, cw_ref, cb_ref, *rest,
                bn, tile, x_bm, out_bm, cast_next):
    next_f32, rest = rest[:cast_next], rest[cast_next:]
    o_ref, cout_ref, rest = rest[0], rest[1], rest[2:]
    next_bf16, (hdn_sc, gext_sc, carry_sc) = rest[:cast_next], rest[cast_next:]
    i = pl.program_id(0)
    halo = 2 * bn
    base = gext_sc.shape[1] - tile

    @pl.when(i == 0)
    def _():
        carry_sc[...] = cin_ref[...]

    x = x_ref[...]
    if x_bm:
        x = jnp.swapaxes(x, 0, 1).reshape(tile, D_MODEL)
    h = _rms(x, g2_ref[...]).astype(BF16)
    for j in range(D_FF // FFN_SUB):
        sl = slice(FFN_SUB * j, FFN_SUB * (j + 1))
        buf = gext_sc.at[j % 2]
        gpre = _dot(h, wg_ref[:, sl])
        buf[base - halo:base, :] = carry_sc[:, sl]
        buf[base:base + tile, :] = gpre
        carry_sc[:, sl] = gpre[tile - halo:tile, :]
        gconv = cb_ref[:, sl] + cw_ref[0:1, sl] * buf[base - halo:base - halo + tile, :]
        gconv = gconv + cw_ref[1:2, sl] * buf[base - bn:base - bn + tile, :]
        gconv = gconv + cw_ref[2:3, sl] * gpre
        hdn_sc[:, sl] = (_silu(gconv) * _dot(h, wu_ref[:, sl])).astype(BF16)
        if j == 0:
            for src, dst in zip(next_f32, next_bf16):
                dst[...] = src[...].astype(BF16)
    y = x + _rms(_dot(hdn_sc[...], wd_ref[...]), g3_ref[...])
    if out_bm:
        y = jnp.swapaxes(y.reshape(tile // bn, bn, D_MODEL), 0, 1)
    o_ref[...] = y

    @pl.when(i == pl.num_programs(0) - 1)
    def _():
        cout_ref[...] = carry_sc[...]


def _ffn_layer(x, bn, cin, g2, g3, w_gate, w_up, conv_w, conv_b, w_down, x_bm=False, out_bm=False,
               next_weights=()):
    rows = x.shape[0]
    tile = min(FFN_ROW_TILE, rows)
    tt = tile // bn
    halo = 2 * bn
    steps = rows // tile
    cast_ins, cast_in_specs, cast_out_specs, cast_out_shapes = [], [], [], []
    assert w_gate.dtype == w_up.dtype == w_down.dtype == BF16
    for w, idx in next_weights:
        k, n = w.shape[1:]
        kb = k // steps
        assert kb * steps == k and kb % (2 * SUBLANES) == 0
        cast_ins.append(w)
        cast_in_specs.append(pl.BlockSpec((None, kb, n), lambda i, idx=idx: (idx, i, 0)))
        cast_out_specs.append(pl.BlockSpec((kb, n), lambda i: (i, 0)))
        cast_out_shapes.append(jax.ShapeDtypeStruct((k, n), BF16))
    assert not (x_bm or out_bm) or bn == SUBLANES
    tm_spec = pl.BlockSpec((tile, D_MODEL), lambda i: (i, 0))
    bm_spec = pl.BlockSpec((bn, tt, D_MODEL), lambda i: (0, i, 0))
    if x_bm:
        x = x.reshape(bn, rows // bn, D_MODEL)
    out_shape = (bn, rows // bn, D_MODEL) if out_bm else (rows, D_MODEL)
    out, cout, *cast = pl.pallas_call(
        functools.partial(_ffn_kernel, bn=bn, tile=tile, x_bm=x_bm, out_bm=out_bm, cast_next=len(cast_ins)),
        grid=(steps,),
        in_specs=[
            bm_spec if x_bm else tm_spec,
            _const_spec((halo, D_FF)),
            _const_spec((1, D_MODEL)), _const_spec((1, D_MODEL)),
            _const_spec((D_MODEL, D_FF)), _const_spec((D_MODEL, D_FF)), _const_spec((D_FF, D_MODEL)),
            _const_spec((CONV_W, D_FF)), _const_spec((1, D_FF)),
        ] + cast_in_specs,
        out_specs=[bm_spec if out_bm else tm_spec, _const_spec((halo, D_FF))] + cast_out_specs,
        out_shape=[jax.ShapeDtypeStruct(out_shape, F32), jax.ShapeDtypeStruct((halo, D_FF), F32)] + cast_out_shapes,
        scratch_shapes=[pltpu.VMEM((tile, D_FF), BF16),
                        pltpu.VMEM((2, halo + tile, FFN_SUB), F32),
                        pltpu.VMEM((halo, D_FF), F32)],
        compiler_params=_cparams("arbitrary", vmem_mb=LARGE_VMEM_MB),
        name="conv_ffn",
    )(x, cin, g2.reshape(1, -1), g3.reshape(1, -1), w_gate, w_up, w_down, conv_w, conv_b.reshape(1, -1), *cast_ins)
    return out.reshape(rows, D_MODEL), cout, tuple(cast)


def _ret_proj_kernel(x_ref, cos_ref, sin_ref, g_ref, wq_ref, wk_ref, wv_ref, q_ref, k_ref, v_ref):
    h = _rms(x_ref[...], g_ref[...]).astype(BF16)
    cos = cos_ref[...]
    sin = sin_ref[...]
    hw = RET_DK // 2
    for w_ref, dst, scale in ((wq_ref, q_ref, 1.0), (wk_ref, k_ref, RET_DK ** -0.5)):
        p = _dot(h, w_ref[...])
        for hd in range(RET_HEADS):
            lo = RET_DK * hd
            t1 = p[:, lo:lo + hw]
            t2 = p[:, lo + hw:lo + 2 * hw]
            dst[:, lo:lo + hw] = ((t1 * cos - t2 * sin) * scale).astype(BF16)
            dst[:, lo + hw:lo + 2 * hw] = ((t2 * cos + t1 * sin) * scale).astype(BF16)
    v_ref[...] = _dot(h, wv_ref[...]).astype(BF16)


def _ret_unit_stages(q_ref, k_ref, v_ref, sin_ref, o_ref, sout_ref, chunk, group):
    ri = lax.broadcasted_iota(jnp.int32, (chunk, chunk), 0)
    ci = lax.broadcasted_iota(jnp.int32, (chunk, chunk), 1)
    rel = (ri - ci).astype(F32)
    causal = ri >= ci
    idx = lax.broadcasted_iota(jnp.int32, (chunk, 1), 0).astype(F32)
    lgs = [math.log1p(-(2.0 ** (-5.0 - hd))) for hd in range(RET_HEADS)]
    units = [(g, hd) for g in range(group) for hd in range(RET_HEADS)]
    qsl = [slice(RET_DK * hd, RET_DK * (hd + 1)) for hd in range(RET_HEADS)]
    vsl = [slice(RET_DV * hd, RET_DV * (hd + 1)) for hd in range(RET_HEADS)]
    decay = [jnp.where(causal, jnp.exp(jnp.maximum(rel, 0.0) * lg), 0.0) for lg in lgs]
    scores = [(_dot_nt(q_ref[g, :, qsl[hd]], k_ref[g, :, qsl[hd]]) * decay[hd]).astype(BF16) for g, hd in units]
    yield
    for (g, hd), sc in zip(units, scores):
        q_decay = jnp.exp((idx + 1.0) * lgs[hd])
        inter = _dot(q_ref[g, :, qsl[hd]], sin_ref[g, hd].astype(BF16)) * q_decay
        o_ref[g, :, vsl[hd]] = _dot(sc, v_ref[g, :, vsl[hd]]) + inter
    yield
    for g, hd in units:
        k_decay = jnp.exp((chunk - 1.0 - idx) * lgs[hd])
        kd = (k_ref[g, :, qsl[hd]].astype(F32) * k_decay).astype(BF16)
        sout_ref[g, hd] = sin_ref[g, hd] * math.exp(chunk * lgs[hd]) + _dot_tn(kd, v_ref[g, :, vsl[hd]])


def _ret_chunk_kernel(q_ref, k_ref, v_ref, s0_ref, o_ref, sout_ref, *, chunk, group):
    @pl.when(pl.program_id(1) == 0)
    def _():
        sout_ref[...] = s0_ref[...]

    for _ in _ret_unit_stages(q_ref, k_ref, v_ref, sout_ref, o_ref, sout_ref, chunk, group):
        pass


def _ret_head_norm(oh, gate):
    mu = jnp.mean(oh, axis=-1, keepdims=True)
    ctr = oh - mu
    var = jnp.mean(ctr * ctr, axis=-1, keepdims=True)
    return (ctr * lax.rsqrt(var + NORM_EPS) * gate).astype(BF16)


def _ret_out_kernel(x_ref, o_ref, g0_ref, g1_ref, wg_ref, wo_ref, y_ref, on_sc):
    x = x_ref[...]
    h = _rms(x, g0_ref[...]).astype(BF16)
    gate = _silu(_dot(h, wg_ref[...]))
    for hd in range(RET_HEADS):
        sl = slice(RET_DV * hd, RET_DV * (hd + 1))
        on_sc[:, sl] = _ret_head_norm(o_ref[:, sl], gate[:, sl])
    y_ref[...] = x + _rms(_dot(on_sc[...], wo_ref[...]), g1_ref[...])


def _ret_mixer_kernel(x_ref, cos_ref, sin_ref, s0_ref, g0_ref, g1_ref, wq_ref, wk_ref, wv_ref, wg_ref, wo_ref,
                      y_ref, sout_ref, q_sc, k_sc, v_sc, o_sc, on_sc, *, chunk, group):
    @pl.when(pl.program_id(1) == 0)
    def _():
        sout_ref[...] = s0_ref[...]

    n = group * chunk
    x = x_ref[...].reshape(n, D_MODEL)
    h = _rms(x, g0_ref[...]).astype(BF16)
    cos = cos_ref[...]
    sin = sin_ref[...]
    hw = RET_DK // 2
    for w_ref, dst, scale in ((wq_ref, q_sc, 1.0), (wk_ref, k_sc, RET_DK ** -0.5)):
        p = _dot(h, w_ref[...]).reshape(group, chunk, RET_HEADS * RET_DK)
        for hd in range(RET_HEADS):
            lo = RET_DK * hd
            t1 = p[:, :, lo:lo + hw]
            t2 = p[:, :, lo + hw:lo + 2 * hw]
            dst[:, :, lo:lo + hw] = ((t1 * cos - t2 * sin) * scale).astype(BF16)
            dst[:, :, lo + hw:lo + 2 * hw] = ((t2 * cos + t1 * sin) * scale).astype(BF16)
    v_sc[...] = _dot(h, wv_ref[...]).astype(BF16).reshape(group, chunk, RET_HEADS * RET_DV)
    stages = _ret_unit_stages(q_sc, k_sc, v_sc, sout_ref, o_sc, sout_ref, chunk, group)
    next(stages)
    next(stages)
    for hd in range(RET_HEADS):
        sl = slice(RET_DV * hd, RET_DV * (hd + 1))
        gate = _silu(_dot(h, wg_ref[:, sl]))
        on_sc[:, sl] = _ret_head_norm(o_sc[:, :, sl].reshape(n, RET_DV), gate)
    proj = _dot(on_sc[...], wo_ref[...])
    next(stages, None)
    y = x + _rms(proj, g1_ref[...])
    y_ref[...] = y.reshape(group, chunk, D_MODEL)


def _row_grid_call(kernel, row_ins, const_ins, outs, rows, tile, scratch=(), name=None, tab_ins=(),
                   vmem_mb=None):
    nt = rows // tile
    in_specs = [pl.BlockSpec((tile, a.shape[1]), lambda i: (i, 0)) for a in row_ins]
    for a in tab_ins:
        ntab = a.shape[0] // tile
        in_specs.append(pl.BlockSpec((tile, a.shape[1]), lambda i, ntab=ntab: (i % ntab, 0)))
    in_specs += [_const_spec(a.shape) for a in const_ins]
    return pl.pallas_call(
        kernel,
        grid=(nt,),
        in_specs=in_specs,
        out_specs=[pl.BlockSpec((tile, n), lambda i: (i, 0)) for n, _ in outs],
        out_shape=[jax.ShapeDtypeStruct((rows, n), dt) for n, dt in outs],
        scratch_shapes=list(scratch),
        compiler_params=_cparams("arbitrary", vmem_mb=vmem_mb),
        name=name,
    )(*row_ins, *tab_ins, *const_ins)


def _to_seq(a, bn, l, time_major):
    n = a.shape[1]
    return a.reshape(l, bn, n).transpose(1, 0, 2) if time_major else a.reshape(bn, l, n)


def _from_seq(a, bn, l, time_major):
    a = a.transpose(1, 0, 2) if time_major else a
    return a.reshape(bn * l, a.shape[2])


def _seq_group(bn, chunk, full_chunk, state_bytes):
    if chunk == full_chunk:
        return min(bn, LONG_CHUNK_GROUP)
    grp = max(1, min(bn, SEQ_GROUP_BYTES // state_bytes))
    while bn % grp:
        grp -= 1
    return grp


def _seq_chunking(l, chunk):
    if l % chunk == 0:
        return chunk
    assert l < chunk
    return l


def _rotary_tables(pos0, l):
    half = RET_DK // 2
    freq = 1.0 / (ROPE_BASE ** jnp.linspace(0.0, 1.0, half, dtype=F32))
    pos = jnp.arange(pos0, pos0 + l, dtype=jnp.int32).astype(F32)
    ang = pos[:, None] * freq[None, :]
    return jnp.cos(ang), jnp.sin(ang)


def _ret_proj_stage(x, bn, l, time_major, pos0, g0, wq, wk, wv):
    rows = x.shape[0]
    ptile = min(PROJ_ROW_TILE, rows)
    cos, sin = _rotary_tables(pos0, l)
    if time_major:
        cos, sin = jnp.repeat(cos, bn, axis=0), jnp.repeat(sin, bn, axis=0)
    elif l < ptile:
        cos, sin = jnp.tile(cos, (ptile // l, 1)), jnp.tile(sin, (ptile // l, 1))
    q, k, v = _row_grid_call(
        _ret_proj_kernel, [x], [g0.reshape(1, -1), wq.astype(BF16), wk.astype(BF16), wv.astype(BF16)],
        [(RET_HEADS * RET_DK, BF16), (RET_HEADS * RET_DK, BF16), (RET_HEADS * RET_DV, BF16)],
        rows, ptile, name="ret_proj", tab_ins=[cos, sin], vmem_mb=LARGE_VMEM_MB)
    return tuple(_to_seq(a, bn, l, time_major) for a in (q, k, v))


def _ret_chunk_stage(q3, k3, v3, s0):
    bn, l, _ = q3.shape
    chunk = _seq_chunking(l, RET_CHUNK)
    grp = _seq_group(bn, chunk, RET_CHUNK, RET_HEADS * RET_DK * RET_DV * 4)
    return pl.pallas_call(
        functools.partial(_ret_chunk_kernel, chunk=chunk, group=grp),
        grid=(bn // grp, l // chunk),
        in_specs=[
            pl.BlockSpec((grp, chunk, RET_HEADS * RET_DK), lambda b, i: (b, i, 0)),
            pl.BlockSpec((grp, chunk, RET_HEADS * RET_DK), lambda b, i: (b, i, 0)),
            pl.BlockSpec((grp, chunk, RET_HEADS * RET_DV), lambda b, i: (b, i, 0)),
            pl.BlockSpec((grp, RET_HEADS, RET_DK, RET_DV), lambda b, i: (b, 0, 0, 0)),
        ],
        out_specs=[
            pl.BlockSpec((grp, chunk, RET_HEADS * RET_DV), lambda b, i: (b, i, 0)),
            pl.BlockSpec((grp, RET_HEADS, RET_DK, RET_DV), lambda b, i: (b, 0, 0, 0)),
        ],
        out_shape=[jax.ShapeDtypeStruct((bn, l, RET_HEADS * RET_DV), F32),
                   jax.ShapeDtypeStruct((bn, RET_HEADS, RET_DK, RET_DV), F32)],
        compiler_params=_cparams("arbitrary", "arbitrary"),
        name="ret_chunk",
    )(q3, k3, v3, s0)


def _ret_mixer_stage(x, bn, l, pos0, s0, g0, g1, wq, wk, wv, wg, wo):
    grp = _seq_group(bn, RET_CHUNK, RET_CHUNK, 0)
    cos, sin = _rotary_tables(pos0, l)
    consts = [g0.reshape(1, -1), g1.reshape(1, -1)] + [a.astype(BF16) for a in (wq, wk, wv, wg, wo)]
    seq_spec = pl.BlockSpec((grp, RET_CHUNK, D_MODEL), lambda b, i: (b, i, 0))
    tab_spec = pl.BlockSpec((RET_CHUNK, RET_DK // 2), lambda b, i: (i, 0))
    state_spec = pl.BlockSpec((grp, RET_HEADS, RET_DK, RET_DV), lambda b, i: (b, 0, 0, 0))
    y, s_new = pl.pallas_call(
        functools.partial(_ret_mixer_kernel, chunk=RET_CHUNK, group=grp),
        grid=(bn // grp, l // RET_CHUNK),
        in_specs=[seq_spec, tab_spec, tab_spec, state_spec] + [_const_spec(a.shape) for a in consts],
        out_specs=[seq_spec, state_spec],
        out_shape=[jax.ShapeDtypeStruct((bn, l, D_MODEL), F32),
                   jax.ShapeDtypeStruct((bn, RET_HEADS, RET_DK, RET_DV), F32)],
        scratch_shapes=[pltpu.VMEM((grp, RET_CHUNK, RET_HEADS * RET_DK), BF16),
                        pltpu.VMEM((grp, RET_CHUNK, RET_HEADS * RET_DK), BF16),
                        pltpu.VMEM((grp, RET_CHUNK, RET_HEADS * RET_DV), BF16),
                        pltpu.VMEM((grp, RET_CHUNK, RET_HEADS * RET_DV), F32),
                        pltpu.VMEM((grp * RET_CHUNK, RET_HEADS * RET_DV), BF16)],
        compiler_params=_cparams("arbitrary", "arbitrary", vmem_mb=LARGE_VMEM_MB),
        name="ret_mixer",
    )(x.reshape(bn, l, D_MODEL), cos, sin, s0, *consts)
    return y.reshape(bn * l, D_MODEL), s_new


def _ret_out_stage(x, o3, time_major, g0, g1, wg, wo):
    rows = x.shape[0]
    tile = min(ROW_TILE, rows)
    bn, l, _ = o3.shape
    o = _from_seq(o3, bn, l, time_major)
    (y,) = _row_grid_call(
        _ret_out_kernel, [x, o], [g0.reshape(1, -1), g1.reshape(1, -1), wg.astype(BF16), wo.astype(BF16)],
        [(D_MODEL, F32)], rows, tile, scratch=[pltpu.VMEM((tile, RET_HEADS * RET_DV), BF16)], name="ret_out")
    return y


def _gla_proj_kernel(x_ref, g_ref, wq_ref, wk_ref, wv_ref, wa1_ref, wa2_ref, ba_ref,
                     q_ref, k_ref, v_ref, la_ref):
    h = _rms(x_ref[...], g_ref[...]).astype(BF16)
    q_ref[...] = (_dot(h, wq_ref[...]) * (GLA_DK ** -0.5)).astype(BF16)
    k_ref[...] = _dot(h, wk_ref[...]).astype(BF16)
    v_ref[...] = _dot(h, wv_ref[...]).astype(BF16)
    low = _dot(h, wa1_ref[...]).astype(BF16)
    logit = _dot(low, wa2_ref[...]) + ba_ref[...]
    la_ref[...] = _log_sigmoid(logit) / GLA_GATE_NORM


def _gla_unit_stages(q_ref, k_ref, v_ref, la_ref, o_ref, st_sc, chunk, group):
    ri = lax.broadcasted_iota(jnp.int32, (chunk, chunk), 0)
    ci = lax.broadcasted_iota(jnp.int32, (chunk, chunk), 1)
    causal = ri >= ci
    tri = causal.astype(BF16)
    nk = GLA_HEADS * GLA_DK
    units = [(g, hd) for g in range(group) for hd in range(GLA_HEADS)]
    bcs = []
    for g in range(group):
        la = la_ref[g]
        p1 = la.astype(BF16)
        r1 = la - p1.astype(F32)
        p2 = r1.astype(BF16)
        p3 = (r1 - p2.astype(F32)).astype(BF16)
        acc = _dot(tri, jnp.concatenate([p1, p2, p3], axis=1))
        bcs.append(acc[:, :nk] + acc[:, nk:2 * nk] + acc[:, 2 * nk:])
    prep = []
    for g, hd in units:
        ksl = slice(GLA_DK * hd, GLA_DK * (hd + 1))
        bc = bcs[g][:, ksl]
        blast = bc[chunk - 1:chunk, :]
        ref = 0.5 * (bc[0:1, :] + blast)
        qh = q_ref[g, :, ksl].astype(F32)
        kh = k_ref[g, :, ksl].astype(F32)
        qt = (qh * jnp.exp(bc - ref)).astype(BF16)
        kt = (kh * jnp.exp(ref - bc)).astype(BF16)
        qg = (qh * jnp.exp(bc)).astype(BF16)
        kg = (kh * jnp.exp(blast - bc)).astype(BF16)
        row_decay = jnp.broadcast_to(jnp.exp(blast), (SUBLANES, GLA_DK)).T[:, :1]
        prep.append((qt, kt, qg, kg, row_decay))
    yield
    scores = [jnp.where(causal, _dot_nt(qt, kt), 0.0).astype(BF16) for qt, kt, _, _, _ in prep]
    for (g, hd), sc, (_, _, qg, kg, row_decay) in zip(units, scores, prep):
        vsl = slice(GLA_DV * hd, GLA_DV * (hd + 1))
        vh = v_ref[g, :, vsl]
        st = st_sc[g, hd]
        o_ref[g, :, vsl] = _dot(sc, vh) + _dot(qg, st.astype(BF16))
        st_sc[g, hd] = st * row_decay + _dot_tn(kg, vh)


def _gla_copy_states(src, dst, group):
    for g in range(group):
        for hd in range(GLA_HEADS):
            dst[g, hd] = src[g, hd]


def _gla_chunk_kernel(q_ref, k_ref, v_ref, la_ref, s0_ref, o_ref, sout_ref, st_sc, *, chunk, group):
    @pl.when(pl.program_id(1) == 0)
    def _():
        _gla_copy_states(s0_ref, st_sc, group)

    for _ in _gla_unit_stages(q_ref, k_ref, v_ref, la_ref, o_ref, st_sc, chunk, group):
        pass

    @pl.when(pl.program_id(1) == pl.num_programs(1) - 1)
    def _():
        _gla_copy_states(st_sc, sout_ref, group)


def _gla_head_norm(oh, norm_g, gate):
    on = oh * lax.rsqrt(jnp.mean(oh * oh, axis=-1, keepdims=True) + NORM_EPS) * norm_g
    return (on * gate).astype(BF16)


def _gla_chunk_out_kernel(q_ref, k_ref, v_ref, la_ref, x_ref, s0_ref, g0_ref, g1_ref, ng_ref, wg_ref, wo_ref,
                          y_ref, sout_ref, st_sc, o_sc, gate_sc, on_sc, *, chunk, group):
    @pl.when(pl.program_id(1) == 0)
    def _():
        _gla_copy_states(s0_ref, st_sc, group)

    n = group * chunk
    x = x_ref[...].reshape(n, D_MODEL)
    h = _rms(x, g0_ref[...]).astype(BF16)
    stages = _gla_unit_stages(q_ref, k_ref, v_ref, la_ref, o_sc, st_sc, chunk, group)
    next(stages)
    gate_sc[...] = _silu(_dot(h, wg_ref[...]))
    next(stages, None)
    for hd in range(GLA_HEADS):
        sl = slice(GLA_DV * hd, GLA_DV * (hd + 1))
        on_sc[:, sl] = _gla_head_norm(o_sc[:, :, sl].reshape(n, GLA_DV), ng_ref[...], gate_sc[:, sl])
    y = x + _rms(_dot(on_sc[...], wo_ref[...]), g1_ref[...])
    y_ref[...] = y.reshape(group, chunk, D_MODEL)

    @pl.when(pl.program_id(1) == pl.num_programs(1) - 1)
    def _():
        _gla_copy_states(st_sc, sout_ref, group)


def _gla_out_kernel(x_ref, o_ref, g0_ref, g1_ref, ng_ref, wg_ref, wo_ref, y_ref, on_sc):
    x = x_ref[...]
    h = _rms(x, g0_ref[...]).astype(BF16)
    gate = _silu(_dot(h, wg_ref[...]))
    for hd in range(GLA_HEADS):
        sl = slice(GLA_DV * hd, GLA_DV * (hd + 1))
        on_sc[:, sl] = _gla_head_norm(o_ref[:, sl], ng_ref[...], gate[:, sl])
    y_ref[...] = x + _rms(_dot(on_sc[...], wo_ref[...]), g1_ref[...])


def _gla_layer(x, bn, l, time_major, s0, g0, g1, wq, wk, wv, wg, wa1, wa2, ba, norm_g, wo):
    rows = x.shape[0]
    tile = min(PROJ_ROW_TILE, rows)
    lanes = 128
    wa1p = jnp.pad(wa1, ((0, 0), (0, lanes - GLA_GATE_RANK))).astype(BF16)
    wa2p = jnp.pad(wa2, ((0, lanes - GLA_GATE_RANK), (0, 0))).astype(BF16)
    q, k, v, la = _row_grid_call(
        _gla_proj_kernel, [x],
        [g0.reshape(1, -1), wq.astype(BF16), wk.astype(BF16), wv.astype(BF16), wa1p, wa2p, ba.reshape(1, -1)],
        [(GLA_HEADS * GLA_DK, BF16), (GLA_HEADS * GLA_DK, BF16), (GLA_HEADS * GLA_DV, BF16),
         (GLA_HEADS * GLA_DK, F32)],
        rows, tile, name="gla_proj", vmem_mb=LARGE_VMEM_MB)
    chunk = _seq_chunking(l, GLA_CHUNK)
    q3, k3, v3, la3 = (_to_seq(a, bn, l, time_major) for a in (q, k, v, la))
    nc = l // chunk
    grp = GLA_GROUP if bn % GLA_GROUP == 0 else 1
    out_consts = [g0.reshape(1, -1), g1.reshape(1, -1), norm_g.reshape(1, -1), wg.astype(BF16), wo.astype(BF16)]
    if not time_major and chunk == GLA_CHUNK:
        seq_spec = lambda n: pl.BlockSpec((grp, chunk, n), lambda b, i: (b, i, 0))
        state_spec = pl.BlockSpec((grp, GLA_HEADS, GLA_DK, GLA_DV), lambda b, i: (b, 0, 0, 0))
        y, s_new = pl.pallas_call(
            functools.partial(_gla_chunk_out_kernel, chunk=chunk, group=grp),
            grid=(bn // grp, nc),
            in_specs=[seq_spec(GLA_HEADS * GLA_DK), seq_spec(GLA_HEADS * GLA_DK), seq_spec(GLA_HEADS * GLA_DV),
                      seq_spec(GLA_HEADS * GLA_DK), seq_spec(D_MODEL), state_spec]
            + [_const_spec(a.shape) for a in out_consts],
            out_specs=[seq_spec(D_MODEL), state_spec],
            out_shape=[jax.ShapeDtypeStruct((bn, l, D_MODEL), F32),
                       jax.ShapeDtypeStruct((bn, GLA_HEADS, GLA_DK, GLA_DV), F32)],
            scratch_shapes=[pltpu.VMEM((grp, GLA_HEADS, GLA_DK, GLA_DV), F32),
                            pltpu.VMEM((grp, chunk, GLA_HEADS * GLA_DV), F32),
                            pltpu.VMEM((grp * chunk, GLA_HEADS * GLA_DV), F32),
                            pltpu.VMEM((grp * chunk, GLA_HEADS * GLA_DV), BF16)],
            compiler_params=_cparams("arbitrary", "arbitrary"),
            name="gla_chunk_out",
        )(q3, k3, v3, la3, x.reshape(bn, l, D_MODEL), s0, *out_consts)
        return y.reshape(rows, D_MODEL), s_new
    o3, s_new = pl.pallas_call(
        functools.partial(_gla_chunk_kernel, chunk=chunk, group=grp),
        grid=(bn // grp, nc),
        in_specs=[
            pl.BlockSpec((grp, chunk, GLA_HEADS * GLA_DK), lambda b, i: (b, i, 0)),
            pl.BlockSpec((grp, chunk, GLA_HEADS * GLA_DK), lambda b, i: (b, i, 0)),
            pl.BlockSpec((grp, chunk, GLA_HEADS * GLA_DV), lambda b, i: (b, i, 0)),
            pl.BlockSpec((grp, chunk, GLA_HEADS * GLA_DK), lambda b, i: (b, i, 0)),
            pl.BlockSpec((grp, GLA_HEADS, GLA_DK, GLA_DV), lambda b, i: (b, 0, 0, 0)),
        ],
        out_specs=[
            pl.BlockSpec((grp, chunk, GLA_HEADS * GLA_DV), lambda b, i: (b, i, 0)),
            pl.BlockSpec((grp, GLA_HEADS, GLA_DK, GLA_DV), lambda b, i: (b, 0, 0, 0)),
        ],
        out_shape=[jax.ShapeDtypeStruct((bn, l, GLA_HEADS * GLA_DV), F32),
                   jax.ShapeDtypeStruct((bn, GLA_HEADS, GLA_DK, GLA_DV), F32)],
        scratch_shapes=[pltpu.VMEM((grp, GLA_HEADS, GLA_DK, GLA_DV), F32)],
        compiler_params=_cparams("arbitrary", "arbitrary"),
        name="gla_chunk",
    )(q3, k3, v3, la3, s0)
    o = _from_seq(o3, bn, l, time_major)
    (y,) = _row_grid_call(
        _gla_out_kernel, [x, o], out_consts,
        [(D_MODEL, F32)], rows, tile, scratch=[pltpu.VMEM((tile, GLA_HEADS * GLA_DV), BF16)], name="gla_out",
        vmem_mb=LARGE_VMEM_MB)
    return y, s_new


def _transpose_rows(x, a, b):
    return x.reshape(a, b, x.shape[1]).transpose(1, 0, 2).reshape(a * b, x.shape[1])


def _trunk(x, pos0, s5_re, s5_im, ret_s, gla_s, conv_buf, p, batch_major_mixers, bf16_w, cast_weights,
           defer_ret=False, host=None):
    bf16_w = dict(bf16_w)

    def w(name, idx):
        return bf16_w.get((name, idx), p[name][idx])
    bn, l, _ = x.shape
    depth = p['norm_g'].shape[0]
    assert not batch_major_mixers or bn == SUBLANES
    rows = x.reshape(bn * l, D_MODEL)
    time_major = False
    if not batch_major_mixers:
        rows, time_major = _transpose_rows(rows, bn, l), True
    new_re, new_im, new_ret, new_gla, new_conv = [], [], [], [], []
    hosted_o, hosted_s, hosted_n = [], None, 0
    for layer in range(depth):
        g = p['norm_g'][layer]
        kind = layer % N_MIXERS
        j = layer // N_MIXERS
        if kind == 0:
            todo_host = None
            if host is not None and hosted_n < host[0].shape[0]:
                todo_host = tuple(host) + (hosted_n, hosted_s)
            rows, hr, hi, hres = _s5_layer(rows, bn, s5_re[j], s5_im[j], g[0], g[1],
                                           p['s5_lambda_re'][j], p['s5_lambda_im'][j], p['s5_b_re'][j],
                                           p['s5_b_im'][j], p['s5_c_re'][j], p['s5_c_im'][j], p['s5_log_dt'][j],
                                           p['s5_d'][j], p['s5_w_glu'], j, x_bm=not time_major, host=todo_host)
            if hres:
                hosted_o.append(hres[0])
                hosted_s = hres[1]
                hosted_n += hres[0].shape[0]
            time_major = True
            new_re.append(hr)
            new_im.append(hi)
        elif kind == 1:
            if not defer_ret and not time_major and l % RET_CHUNK == 0:
                rows, s = _ret_mixer_stage(rows, bn, l, pos0, ret_s[j], g[0], g[1],
                                           *(w(n, j) for n in MIXER_WEIGHTS[kind]))
            else:
                q3, k3, v3 = _ret_proj_stage(rows, bn, l, time_major, pos0, g[0], w('ret_wq', j), w('ret_wk', j),
                                             w('ret_wv', j))
                if defer_ret:
                    o3, s, more = yield q3, k3, v3, ret_s[j]
                    bf16_w.update(more)
                else:
                    o3, s = _ret_chunk_stage(q3, k3, v3, ret_s[j])
                rows = _ret_out_stage(rows, o3, time_major, g[0], g[1], w('ret_wg', j), w('ret_wo', j))
            new_ret.append(s)
        else:
            rows, s = _gla_layer(rows, bn, l, time_major, gla_s[j], g[0], g[1], w('gla_wq', j), w('gla_wk', j),
                                 w('gla_wv', j), w('gla_wg', j), p['gla_wa1'][j], p['gla_wa2'][j], p['gla_ba'][j],
                                 p['gla_norm_g'][j], w('gla_wo', j))
            new_gla.append(s)
        next_is_s5 = layer + 1 < depth and (layer + 1) % N_MIXERS == 0
        out_bm = batch_major_mixers and not next_is_s5
        cin = conv_buf[layer].transpose(1, 0, 2).reshape((CONV_W - 1) * bn, D_FF)
        todo = []
        if cast_weights and layer + 1 < depth:
            mixer = MIXER_WEIGHTS[(layer + 1) % N_MIXERS]
            todo = [(n, layer + 1) for n in FFN_WEIGHTS] + [(n, (layer + 1) // N_MIXERS) for n in mixer]
            todo = [key for key in todo if key not in bf16_w]
        rows, cout, cast = _ffn_layer(rows, bn, cin, g[2], g[3], w('ffn_w_gate', layer), w('ffn_w_up', layer),
                                      p['ffn_conv_w'][layer], p['ffn_conv_b'][layer], w('ffn_w_down', layer),
                                      x_bm=not time_major, out_bm=out_bm,
                                      next_weights=[(p[n], idx) for n, idx in todo])
        bf16_w.update(zip(todo, cast))
        time_major = not out_bm
        new_conv.append(cout.reshape(CONV_W - 1, bn, D_FF).transpose(1, 0, 2))
    if time_major:
        rows = _transpose_rows(rows, l, bn)
    y = rows.reshape(bn, l, D_MODEL)
    hosted = None
    if host is not None:
        assert hosted_n == host[0].shape[0]
        hosted = (jnp.concatenate(hosted_o), hosted_s)
    outs = (y, jnp.stack(new_re), jnp.stack(new_im), jnp.stack(new_ret), jnp.stack(new_gla),
            jnp.stack(new_conv))
    return outs, bf16_w, hosted


def _finish(gen, reply=None):
    try:
        gen.send(reply)
    except StopIteration as done:
        return done.value
    raise AssertionError("unexpected request")


def kernel(x_prompt, x_sample, state_s5_re, state_s5_im, state_ret, state_gla, cache_ffn_conv,
           norm_g, s5_lambda_re, s5_lambda_im, s5_b_re, s5_b_im, s5_c_re, s5_c_im, s5_log_dt, s5_d, s5_w_glu,
           ret_wq, ret_wk, ret_wv, ret_wg, ret_wo,
           gla_wq, gla_wk, gla_wv, gla_wg, gla_wa1, gla_wa2, gla_ba, gla_norm_g, gla_wo,
           ffn_w_gate, ffn_w_up, ffn_conv_w, ffn_conv_b, ffn_w_down):
    p = dict(norm_g=norm_g, s5_lambda_re=s5_lambda_re, s5_lambda_im=s5_lambda_im, s5_b_re=s5_b_re,
             s5_b_im=s5_b_im, s5_c_re=s5_c_re, s5_c_im=s5_c_im, s5_log_dt=s5_log_dt, s5_d=s5_d,
             s5_w_glu=s5_w_glu.astype(BF16), ret_wq=ret_wq, ret_wk=ret_wk, ret_wv=ret_wv, ret_wg=ret_wg, ret_wo=ret_wo,
             gla_wq=gla_wq, gla_wk=gla_wk, gla_wv=gla_wv, gla_wg=gla_wg, gla_wa1=gla_wa1, gla_wa2=gla_wa2,
             gla_ba=gla_ba, gla_norm_g=gla_norm_g, gla_wo=gla_wo, ffn_w_gate=ffn_w_gate, ffn_w_up=ffn_w_up,
             ffn_conv_w=ffn_conv_w, ffn_conv_b=ffn_conv_b, ffn_w_down=ffn_w_down)
    bp = x_prompt.shape[0]
    z_s5 = jnp.zeros((state_s5_re.shape[0], bp) + state_s5_re.shape[2:], F32)
    z_ret = jnp.zeros((state_ret.shape[0], bp) + state_ret.shape[2:], F32)
    z_gla = jnp.zeros((state_gla.shape[0], bp) + state_gla.shape[2:], F32)
    z_conv = jnp.zeros((cache_ffn_conv.shape[0], bp) + cache_ffn_conv.shape[2:], x_prompt.dtype)
    first = {(n, 0): p[n][0].astype(BF16) for n in FFN_WEIGHTS}
    first.update({(n, j): p[n][j].astype(BF16) for n in ('ret_wq', 'ret_wk', 'ret_wv') for j in range(ret_wq.shape[0])})
    sample = _trunk(x_sample, PAST_LEN, state_s5_re, state_s5_im, state_ret, state_gla, cache_ffn_conv, p,
                    False, first, False, defer_ret=True)
    request = next(sample)
    outs_p, bf16_w, hosted = _finish(_trunk(x_prompt, 0, z_s5, z_s5, z_ret, z_gla, z_conv, p, True, first, True,
                                            host=request))
    outs_s, _, _ = _finish(sample, hosted + (bf16_w,))
    return (outs_p[0], outs_s[0]) + tuple(outs_p[1:]) + tuple(outs_s[1:])
```

```python
import functools
import math

import jax
import jax.numpy as jnp
from jax import lax
from jax.experimental import pallas as pl
from jax.experimental.pallas import tpu as pltpu

F32 = jnp.float32
BF16 = jnp.bfloat16

D_MODEL = 1024
PAST_LEN = 16384
N_MIXERS = 3
FFN_WEIGHTS = ('ffn_w_gate', 'ffn_w_up', 'ffn_w_down')
MIXER_WEIGHTS = ((), ('ret_wq', 'ret_wk', 'ret_wv', 'ret_wg', 'ret_wo'),
                 ('gla_wq', 'gla_wk', 'gla_wv', 'gla_wg', 'gla_wo'))
S5_GROUP = 16
S5_GROUPS = D_MODEL // S5_GROUP
S5_STATE = 64
S5_SUPER = 4
S5_SG_CH = D_MODEL // S5_SUPER
S5_SG_ST = (S5_GROUPS // S5_SUPER) * S5_STATE
RET_HEADS = 4
RET_DK = D_MODEL // RET_HEADS
RET_DV = 2 * D_MODEL // RET_HEADS
RET_CHUNK = 256
ROPE_BASE = 10000.0
GLA_HEADS = 4
GLA_DK = D_MODEL // 2 // GLA_HEADS
GLA_DV = D_MODEL // GLA_HEADS
GLA_GATE_RANK = 16
GLA_GATE_NORM = 16.0
GLA_CHUNK = 64
GLA_GROUP = 8
D_FF = ((8 * D_MODEL // 3 + 255) // 256) * 256
FFN_SUB = 256
FFN_ROW_TILE = 1024
S5_BU_BUFFERS = 4
N_S5_INPUTS = 12
HOSTED_SEQS = 2
LONG_CHUNK_GROUP = 2
SEQ_GROUP_BYTES = 8 * 2 ** 20
CONV_W = 3
NORM_EPS = 1e-6
ROW_TILE = 512
PROJ_ROW_TILE = 1024
SUBLANES = 8
V7X_VMEM_MB = 64
VMEM_MB = V7X_VMEM_MB - 12
LARGE_VMEM_MB = V7X_VMEM_MB - 6


def _cparams(*sem, vmem_mb=None):
    return pltpu.CompilerParams(dimension_semantics=sem, vmem_limit_bytes=(vmem_mb or VMEM_MB) * 2 ** 20)


def _const_spec(shape):
    zeros = (0,) * len(shape)
    return pl.BlockSpec(shape, lambda *_: zeros, pipeline_mode=pl.Buffered(1))


def _stacked_spec(shape, index):
    where = (index,) + (0,) * len(shape)
    return pl.BlockSpec((None,) + tuple(shape), lambda *_: where, pipeline_mode=pl.Buffered(1))


def _rms(x, g):
    return x * lax.rsqrt(jnp.mean(x * x, axis=-1, keepdims=True) + NORM_EPS) * g


def _sigmoid(x):
    return 1.0 / (1.0 + jnp.exp(-x))


def _silu(x):
    return x * _sigmoid(x)


def _gelu_tanh(x):
    return 0.5 * x * (1.0 + jnp.tanh(math.sqrt(2.0 / math.pi) * (x + 0.044715 * (x * x * x))))


def _log_sigmoid(x):
    return jnp.minimum(x, 0.0) - jnp.log(1.0 + jnp.exp(-jnp.abs(x)))


def _dot(a, b):
    return jnp.dot(a, b, preferred_element_type=F32)


def _dot_nt(a, b):
    return lax.dot_general(a, b, (((1,), (1,)), ((), ())), preferred_element_type=F32)


def _dot_tn(a, b):
    return lax.dot_general(a, b, (((0,), (0,)), ((), ())), preferred_element_type=F32)


def _s5_kernel(x_ref, h0re_ref, h0im_ref, g0_ref, g1_ref, are_ref, aim_ref, bre_ref, bim_ref, cc_ref, dsk_ref,
               wglu_ref, *rest, bn, tt, x_bm, hosted, n_host_in):
    host_in, rest = rest[:n_host_in], rest[n_host_in:]
    o_ref, hre_ref, him_ref, rest = rest[0], rest[1], rest[2], rest[3:]
    host_out, (bu_sc, y_sc, bbd_ref, cbd_ref) = rest[:2 if hosted else 0], rest[2 if hosted else 0:]
    @pl.when(pl.program_id(0) == 0)
    def _():
        hre_ref[...] = h0re_ref[...]
        him_ref[...] = h0im_ref[...]
        ngl = S5_GROUPS // S5_SUPER

        def lane_tiler(k, n):
            r = lax.broadcasted_iota(jnp.int32, (k, n), 0)
            c = lax.broadcasted_iota(jnp.int32, (k, n), 1)
            return ((c & (k - 1)) == r).astype(BF16)

        tile_n = lane_tiler(S5_STATE, S5_SG_ST)
        tile_p = lane_tiler(S5_GROUP, S5_SG_CH)
        sh_p = S5_GROUP.bit_length() - 1
        sh_n = S5_STATE.bit_length() - 1
        rb = lax.broadcasted_iota(jnp.int32, (S5_SG_CH, S5_SG_ST), 0) >> sh_p
        cb = lax.broadcasted_iota(jnp.int32, (S5_SG_CH, S5_SG_ST), 1) >> sh_n
        rc = (lax.broadcasted_iota(jnp.int32, (2 * S5_SG_ST, S5_SG_CH), 0) >> sh_n) & (ngl - 1)
        cc = lax.broadcasted_iota(jnp.int32, (2 * S5_SG_ST, S5_SG_CH), 1) >> sh_p
        for sg in range(S5_SUPER):
            for part, ref in ((0, bre_ref), (1, bim_ref)):
                blk = _dot(ref[sg].astype(BF16), tile_n)
                bbd_ref[sg, :, S5_SG_ST * part:S5_SG_ST * (part + 1)] = jnp.where(rb == cb, blk, 0.0).astype(BF16)
            blk = _dot(cc_ref[sg].astype(BF16), tile_p)
            cbd_ref[sg] = jnp.where(rc == cc, blk, 0.0).astype(BF16)

    x = x_ref[...]
    if x_bm:
        x = jnp.swapaxes(x, 0, 1).reshape(tt * bn, D_MODEL)
    u = _rms(x, g0_ref[...])
    ub = u.astype(BF16)
    half = S5_SG_ST // 2
    host_stages = iter(())
    if hosted:
        hq, hk, hv, hs = host_in[:4]
        host_stages = _ret_unit_stages(hq, hk, hv, hs, host_out[0], host_out[1], hq.shape[1], hosted)
    for sg in range(S5_SUPER):
        buf = bu_sc.at[sg % S5_BU_BUFFERS]
        buf[...] = _dot(ub[:, S5_SG_CH * sg:S5_SG_CH * (sg + 1)], bbd_ref[sg])
        for hf in range(2):
            c_re = half * hf
            c_im = S5_SG_ST + half * hf
            a_lo = S5_SG_ST * sg + half * hf
            ar = jnp.broadcast_to(are_ref[:, a_lo:a_lo + half], (8, half))
            ai = jnp.broadcast_to(aim_ref[:, a_lo:a_lo + half], (8, half))
            for bt in range(bn // 8):
                r8 = 8 * bt
                hr = hre_ref[r8:r8 + 8, a_lo:a_lo + half]
                hi = him_ref[r8:r8 + 8, a_lo:a_lo + half]
                for t in range(tt):
                    r0 = t * bn + r8
                    bur = buf[r0:r0 + 8, c_re:c_re + half]
                    bui = buf[r0:r0 + 8, c_im:c_im + half]
                    hr, hi = ar * hr - ai * hi + bur, ar * hi + ai * hr + bui
                    buf[r0:r0 + 8, c_re:c_re + half] = hr
                    buf[r0:r0 + 8, c_im:c_im + half] = hi
                hre_ref[r8:r8 + 8, a_lo:a_lo + half] = hr
                him_ref[r8:r8 + 8, a_lo:a_lo + half] = hi
        y_sc[:, S5_SG_CH * sg:S5_SG_CH * (sg + 1)] = _dot(buf[...].astype(BF16), cbd_ref[sg])
        next(host_stages, None)
    y = y_sc[...] + dsk_ref[...] * u
    z = _dot(_gelu_tanh(y).astype(BF16), wglu_ref[...])
    mix = z[:, :D_MODEL] * _sigmoid(z[:, D_MODEL:])
    o_ref[...] = x + _rms(mix, g1_ref[...])


def _s5_layer(x, bn, h0_re, h0_im, g0, g1, lam_re, lam_im, b_re, b_im, c_re, c_im, log_dt, d_skip, w_glu,
              j, x_bm=False, host=None):
    rows = x.shape[0]
    tt = min(rows // bn, max(1, ROW_TILE // bn))
    tile = tt * bn
    steps = rows // tile
    host_ins, host_in_specs, host_out_specs, host_out_shapes = [], [], [], []
    aliases = {}
    if host is not None:
        hq, hk, hv, hs, first, acc = host
        assert first % HOSTED_SEQS == 0
        blk0 = first // HOSTED_SEQS
        host_ins = [hq, hk, hv, hs]
        for a in host_ins:
            host_in_specs.append(pl.BlockSpec((HOSTED_SEQS,) + a.shape[1:],
                                              lambda i, nd=a.ndim: (blk0 + i,) + (0,) * (nd - 1)))
        host_out_specs = [pl.BlockSpec((HOSTED_SEQS,) + hv.shape[1:], lambda i: (i, 0, 0)),
                          pl.BlockSpec((HOSTED_SEQS,) + hs.shape[1:], lambda i: (blk0 + i, 0, 0, 0))]
        host_out_shapes = [jax.ShapeDtypeStruct((steps * HOSTED_SEQS,) + hv.shape[1:], F32),
                           jax.ShapeDtypeStruct(hs.shape, F32)]
        if acc is not None:
            host_ins.append(acc)
            host_in_specs.append(pl.BlockSpec(memory_space=pl.ANY))
            aliases = {N_S5_INPUTS + 4: 4}
    o_spec = pl.BlockSpec((tile, D_MODEL), lambda i: (i, 0))
    x_spec = o_spec
    if x_bm:
        assert bn == SUBLANES
        x = x.reshape(bn, rows // bn, D_MODEL)
        x_spec = pl.BlockSpec((bn, tt, D_MODEL), lambda i: (0, i, 0))
    lam = lax.complex(lam_re, lam_im)
    dt = jnp.exp(log_dt)[:, None]
    a_bar = jnp.exp(lam * dt)
    b_bar = ((a_bar - 1.0) / lam)[..., None] * lax.complex(b_re, b_im)
    bt = b_bar.transpose(0, 2, 1).reshape(S5_SUPER, S5_SG_CH, S5_STATE)
    bt_re, bt_im = jnp.real(bt), jnp.imag(bt)
    ct_re = c_re.transpose(0, 2, 1).reshape(S5_SUPER, S5_SG_ST, S5_GROUP)
    ct_im = c_im.transpose(0, 2, 1).reshape(S5_SUPER, S5_SG_ST, S5_GROUP)
    ct = jnp.concatenate([ct_re, -ct_im], axis=1)
    a_re = jnp.real(a_bar).reshape(1, S5_GROUPS * S5_STATE)
    a_im = jnp.imag(a_bar).reshape(1, S5_GROUPS * S5_STATE)
    nst = S5_GROUPS * S5_STATE

    out, new_re, new_im, *hosted = pl.pallas_call(
        functools.partial(_s5_kernel, bn=bn, tt=tt, x_bm=x_bm, hosted=HOSTED_SEQS if host_ins else 0,
                          n_host_in=len(host_ins)),
        grid=(steps,),
        in_specs=[
            x_spec,
            _const_spec((bn, nst)), _const_spec((bn, nst)),
            _const_spec((1, D_MODEL)), _const_spec((1, D_MODEL)),
            _const_spec((1, nst)), _const_spec((1, nst)),
            _const_spec((S5_SUPER, S5_SG_CH, S5_STATE)), _const_spec((S5_SUPER, S5_SG_CH, S5_STATE)),
            _const_spec((S5_SUPER, 2 * S5_SG_ST, S5_GROUP)),
            _const_spec((1, D_MODEL)),
            _stacked_spec((D_MODEL, 2 * D_MODEL), j),
        ] + host_in_specs,
        out_specs=[o_spec, _const_spec((bn, nst)), _const_spec((bn, nst))] + host_out_specs,
        out_shape=[jax.ShapeDtypeStruct((rows, D_MODEL), F32), jax.ShapeDtypeStruct((bn, nst), F32),
                   jax.ShapeDtypeStruct((bn, nst), F32)] + host_out_shapes,
        scratch_shapes=[pltpu.VMEM((S5_BU_BUFFERS, tile, 2 * S5_SG_ST), F32), pltpu.VMEM((tile, D_MODEL), F32),
                        pltpu.VMEM((S5_SUPER, S5_SG_CH, 2 * S5_SG_ST), BF16),
                        pltpu.VMEM((S5_SUPER, 2 * S5_SG_ST, S5_SG_CH), BF16)],
        input_output_aliases=aliases,
        compiler_params=_cparams("arbitrary", vmem_mb=LARGE_VMEM_MB if host_ins else None),
        name="s5_layer",
    )(x, h0_re.reshape(bn, nst), h0_im.reshape(bn, nst), g0.reshape(1, -1), g1.reshape(1, -1), a_re, a_im,
      bt_re, bt_im, ct, d_skip.reshape(1, -1), w_glu, *host_ins)
    return out, new_re.reshape(bn, S5_GROUPS, S5_STATE), new_im.reshape(bn, S5_GROUPS, S5_STATE), tuple(hosted)


def _ffn_kernel(x_ref, cin_ref, g2_ref, g3_ref, wg_ref, wu_ref, wd_ref, cw_ref, cb_ref, *rest,
                bn, tile, x_bm, out_bm, cast_next):
    next_f32, rest = rest[:cast_next], rest[cast_next:]
    o_ref, cout_ref, rest = rest[0], rest[1], rest[2:]
    next_bf16, (hdn_sc, gext_sc, carry_sc) = rest[:cast_next], rest[cast_next:]
    i = pl.program_id(0)
    halo = 2 * bn
    base = gext_sc.shape[1] - tile

    @pl.when(i == 0)
    def _():
        carry_sc[...] = cin_ref[...]

    x = x_ref[...]
    if x_bm:
        x = jnp.swapaxes(x, 0, 1).reshape(tile, D_MODEL)
    h = _rms(x, g2_ref[...]).astype(BF16)
    for j in range(D_FF // FFN_SUB):
        sl = slice(FFN_SUB * j, FFN_SUB * (j + 1))
        buf = gext_sc.at[j % 2]
        gpre = _dot(h, wg_ref[:, sl])
        buf[base - halo:base, :] = carry_sc[:, sl]
        buf[base:base + tile, :] = gpre
        carry_sc[:, sl] = gpre[tile - halo:tile, :]
        gconv = cb_ref[:, sl] + cw_ref[0:1, sl] * buf[base - halo:base - halo + tile, :]
        gconv = gconv + cw_ref[1:2, sl] * buf[base - bn:base - bn + tile, :]
        gconv = gconv + cw_ref[2:3, sl] * gpre
        hdn_sc[:, sl] = (_silu(gconv) * _dot(h, wu_ref[:, sl])).astype(BF16)
        if j == 0:
            for src, dst in zip(next_f32, next_bf16):
                dst[...] = src[...].astype(BF16)
    y = x + _rms(_dot(hdn_sc[...], wd_ref[...]), g3_ref[...])
    if out_bm:
        y = jnp.swapaxes(y.reshape(tile // bn, bn, D_MODEL), 0, 1)
    o_ref[...] = y

    @pl.when(i == pl.num_programs(0) - 1)
    def _():
        cout_ref[...] = carry_sc[...]


def _ffn_layer(x, bn, cin, g2, g3, w_gate, w_up, conv_w, conv_b, w_down, x_bm=False, out_bm=False,
               next_weights=()):
    rows = x.shape[0]
    tile = min(FFN_ROW_TILE, rows)
    tt = tile // bn
    halo = 2 * bn
    steps = rows // tile
    cast_ins, cast_in_specs, cast_out_specs, cast_out_shapes = [], [], [], []
    assert w_gate.dtype == w_up.dtype == w_down.dtype == BF16
    for w, idx in next_weights:
        k, n = w.shape[1:]
        kb = k // steps
        assert kb * steps == k and kb % (2 * SUBLANES) == 0
        cast_ins.append(w)
        cast_in_specs.append(pl.BlockSpec((None, kb, n), lambda i, idx=idx: (idx, i, 0)))
        cast_out_specs.append(pl.BlockSpec((kb, n), lambda i: (i, 0)))
        cast_out_shapes.append(jax.ShapeDtypeStruct((k, n), BF16))
    assert not (x_bm or out_bm) or bn == SUBLANES
    tm_spec = pl.BlockSpec((tile, D_MODEL), lambda i: (i, 0))
    bm_spec = pl.BlockSpec((bn, tt, D_MODEL), lambda i: (0, i, 0))
    if x_bm:
        x = x.reshape(bn, rows // bn, D_MODEL)
    out_shape = (bn, rows // bn, D_MODEL) if out_bm else (rows, D_MODEL)
    out, cout, *cast = pl.pallas_call(
        functools.partial(_ffn_kernel, bn=bn, tile=tile, x_bm=x_bm, out_bm=out_bm, cast_next=len(cast_ins)),
        grid=(steps,),
        in_specs=[
            bm_spec if x_bm else tm_spec,
            _const_spec((halo, D_FF)),
            _const_spec((1, D_MODEL)), _const_spec((1, D_MODEL)),
            _const_spec((D_MODEL, D_FF)), _const_spec((D_MODEL, D_FF)), _const_spec((D_FF, D_MODEL)),
            _const_spec((CONV_W, D_FF)), _const_spec((1, D_FF)),
        ] + cast_in_specs,
        out_specs=[bm_spec if out_bm else tm_spec, _const_spec((halo, D_FF))] + cast_out_specs,
        out_shape=[jax.ShapeDtypeStruct(out_shape, F32), jax.ShapeDtypeStruct((halo, D_FF), F32)] + cast_out_shapes,
        scratch_shapes=[pltpu.VMEM((tile, D_FF), BF16),
                        pltpu.VMEM((2, halo + tile, FFN_SUB), F32),
                        pltpu.VMEM((halo, D_FF), F32)],
        compiler_params=_cparams("arbitrary", vmem_mb=LARGE_VMEM_MB),
        name="conv_ffn",
    )(x, cin, g2.reshape(1, -1), g3.reshape(1, -1), w_gate, w_up, w_down, conv_w, conv_b.reshape(1, -1), *cast_ins)
    return out.reshape(rows, D_MODEL), cout, tuple(cast)


def _ret_proj_kernel(x_ref, cos_ref, sin_ref, g_ref, wq_ref, wk_ref, wv_ref, q_ref, k_ref, v_ref):
    h = _rms(x_ref[...], g_ref[...]).astype(BF16)
    cos = cos_ref[...]
    sin = sin_ref[...]
    hw = RET_DK // 2
    for w_ref, dst, scale in ((wq_ref, q_ref, 1.0), (wk_ref, k_ref, RET_DK ** -0.5)):
        p = _dot(h, w_ref[...])
        for hd in range(RET_HEADS):
            lo = RET_DK * hd
            t1 = p[:, lo:lo + hw]
            t2 = p[:, lo + hw:lo + 2 * hw]
            dst[:, lo:lo + hw] = ((t1 * cos - t2 * sin) * scale).astype(BF16)
            dst[:, lo + hw:lo + 2 * hw] = ((t2 * cos + t1 * sin) * scale).astype(BF16)
    v_ref[...] = _dot(h, wv_ref[...]).astype(BF16)


def _ret_unit_stages(q_ref, k_ref, v_ref, sin_ref, o_ref, sout_ref, chunk, group):
    ri = lax.broadcasted_iota(jnp.int32, (chunk, chunk), 0)
    ci = lax.broadcasted_iota(jnp.int32, (chunk, chunk), 1)
    rel = (ri - ci).astype(F32)
    causal = ri >= ci
    idx = lax.broadcasted_iota(jnp.int32, (chunk, 1), 0).astype(F32)
    lgs = [math.log1p(-(2.0 ** (-5.0 - hd))) for hd in range(RET_HEADS)]
    units = [(g, hd) for g in range(group) for hd in range(RET_HEADS)]
    qsl = [slice(RET_DK * hd, RET_DK * (hd + 1)) for hd in range(RET_HEADS)]
    vsl = [slice(RET_DV * hd, RET_DV * (hd + 1)) for hd in range(RET_HEADS)]
    decay = [jnp.where(causal, jnp.exp(jnp.maximum(rel, 0.0) * lg), 0.0) for lg in lgs]
    scores = [(_dot_nt(q_ref[g, :, qsl[hd]], k_ref[g, :, qsl[hd]]) * decay[hd]).astype(BF16) for g, hd in units]
    yield
    for (g, hd), sc in zip(units, scores):
        q_decay = jnp.exp((idx + 1.0) * lgs[hd])
        inter = _dot(q_ref[g, :, qsl[hd]], sin_ref[g, hd].astype(BF16)) * q_decay
        o_ref[g, :, vsl[hd]] = _dot(sc, v_ref[g, :, vsl[hd]]) + inter
    yield
    for g, hd in units:
        k_decay = jnp.exp((chunk - 1.0 - idx) * lgs[hd])
        kd = (k_ref[g, :, qsl[hd]].astype(F32) * k_decay).astype(BF16)
        sout_ref[g, hd] = sin_ref[g, hd] * math.exp(chunk * lgs[hd]) + _dot_tn(kd, v_ref[g, :, vsl[hd]])


def _ret_chunk_kernel(q_ref, k_ref, v_ref, s0_ref, o_ref, sout_ref, *, chunk, group):
    @pl.when(pl.program_id(1) == 0)
    def _():
        sout_ref[...] = s0_ref[...]

    for _ in _ret_unit_stages(q_ref, k_ref, v_ref, sout_ref, o_ref, sout_ref, chunk, group):
        pass


def _ret_head_norm(oh, gate):
    mu = jnp.mean(oh, axis=-1, keepdims=True)
    ctr = oh - mu
    var = jnp.mean(ctr * ctr, axis=-1, keepdims=True)
    return (ctr * lax.rsqrt(var + NORM_EPS) * gate).astype(BF16)


def _ret_out_kernel(x_ref, o_ref, g0_ref, g1_ref, wg_ref, wo_ref, y_ref, on_sc):
    x = x_ref[...]
    h = _rms(x, g0_ref[...]).astype(BF16)
    gate = _silu(_dot(h, wg_ref[...]))
    for hd in range(RET_HEADS):
        sl = slice(RET_DV * hd, RET_DV * (hd + 1))
        on_sc[:, sl] = _ret_head_norm(o_ref[:, sl], gate[:, sl])
    y_ref[...] = x + _rms(_dot(on_sc[...], wo_ref[...]), g1_ref[...])


def _ret_mixer_kernel(x_ref, cos_ref, sin_ref, s0_ref, g0_ref, g1_ref, wq_ref, wk_ref, wv_ref, wg_ref, wo_ref,
                      y_ref, sout_ref, q_sc, k_sc, v_sc, o_sc, on_sc, *, chunk, group):
    @pl.when(pl.program_id(1) == 0)
    def _():
        sout_ref[...] = s0_ref[...]

    n = group * chunk
    x = x_ref[...].reshape(n, D_MODEL)
    h = _rms(x, g0_ref[...]).astype(BF16)
    cos = cos_ref[...]
    sin = sin_ref[...]
    hw = RET_DK // 2
    for w_ref, dst, scale in ((wq_ref, q_sc, 1.0), (wk_ref, k_sc, RET_DK ** -0.5)):
        p = _dot(h, w_ref[...]).reshape(group, chunk, RET_HEADS * RET_DK)
        for hd in range(RET_HEADS):
            lo = RET_DK * hd
            t1 = p[:, :, lo:lo + hw]
            t2 = p[:, :, lo + hw:lo + 2 * hw]
            dst[:, :, lo:lo + hw] = ((t1 * cos - t2 * sin) * scale).astype(BF16)
            dst[:, :, lo + hw:lo + 2 * hw] = ((t2 * cos + t1 * sin) * scale).astype(BF16)
    v_sc[...] = _dot(h, wv_ref[...]).astype(BF16).reshape(group, chunk, RET_HEADS * RET_DV)
    stages = _ret_unit_stages(q_sc, k_sc, v_sc, sout_ref, o_sc, sout_ref, chunk, group)
    next(stages)
    next(stages)
    for hd in range(RET_HEADS):
        sl = slice(RET_DV * hd, RET_DV * (hd + 1))
        gate = _silu(_dot(h, wg_ref[:, sl]))
        on_sc[:, sl] = _ret_head_norm(o_sc[:, :, sl].reshape(n, RET_DV), gate)
    proj = _dot(on_sc[...], wo_ref[...])
    next(stages, None)
    y = x + _rms(proj, g1_ref[...])
    y_ref[...] = y.reshape(group, chunk, D_MODEL)


def _row_grid_call(kernel, row_ins, const_ins, outs, rows, tile, scratch=(), name=None, tab_ins=(),
                   vmem_mb=None):
    nt = rows // tile
    in_specs = [pl.BlockSpec((tile, a.shape[1]), lambda i: (i, 0)) for a in row_ins]
    for a in tab_ins:
        ntab = a.shape[0] // tile
        in_specs.append(pl.BlockSpec((tile, a.shape[1]), lambda i, ntab=ntab: (i % ntab, 0)))
    in_specs += [_const_spec(a.shape) for a in const_ins]
    return pl.pallas_call(
        kernel,
        grid=(nt,),
        in_specs=in_specs,
        out_specs=[pl.BlockSpec((tile, n), lambda i: (i, 0)) for n, _ in outs],
        out_shape=[jax.ShapeDtypeStruct((rows, n), dt) for n, dt in outs],
        scratch_shapes=list(scratch),
        compiler_params=_cparams("arbitrary", vmem_mb=vmem_mb),
        name=name,
    )(*row_ins, *tab_ins, *const_ins)


def _to_seq(a, bn, l, time_major):
    n = a.shape[1]
    return a.reshape(l, bn, n).transpose(1, 0, 2) if time_major else a.reshape(bn, l, n)


def _from_seq(a, bn, l, time_major):
    a = a.transpose(1, 0, 2) if time_major else a
    return a.reshape(bn * l, a.shape[2])


def _seq_group(bn, chunk, full_chunk, state_bytes):
    if chunk == full_chunk:
        return min(bn, LONG_CHUNK_GROUP)
    grp = max(1, min(bn, SEQ_GROUP_BYTES // state_bytes))
    while bn % grp:
        grp -= 1
    return grp


def _seq_chunking(l, chunk):
    if l % chunk == 0:
        return chunk
    assert l < chunk
    return l


def _rotary_tables(pos0, l):
    half = RET_DK // 2
    freq = 1.0 / (ROPE_BASE ** jnp.linspace(0.0, 1.0, half, dtype=F32))
    pos = jnp.arange(pos0, pos0 + l, dtype=jnp.int32).astype(F32)
    ang = pos[:, None] * freq[None, :]
    return jnp.cos(ang), jnp.sin(ang)


def _ret_proj_stage(x, bn, l, time_major, pos0, g0, wq, wk, wv):
    rows = x.shape[0]
    ptile = min(PROJ_ROW_TILE, rows)
    cos, sin = _rotary_tables(pos0, l)
    if time_major:
        cos, sin = jnp.repeat(cos, bn, axis=0), jnp.repeat(sin, bn, axis=0)
    elif l < ptile:
        cos, sin = jnp.tile(cos, (ptile // l, 1)), jnp.tile(sin, (ptile // l, 1))
    q, k, v = _row_grid_call(
        _ret_proj_kernel, [x], [g0.reshape(1, -1), wq.astype(BF16), wk.astype(BF16), wv.astype(BF16)],
        [(RET_HEADS * RET_DK, BF16), (RET_HEADS * RET_DK, BF16), (RET_HEADS * RET_DV, BF16)],
        rows, ptile, name="ret_proj", tab_ins=[cos, sin], vmem_mb=LARGE_VMEM_MB)
    return tuple(_to_seq(a, bn, l, time_major) for a in (q, k, v))


def _ret_chunk_stage(q3, k3, v3, s0):
    bn, l, _ = q3.shape
    chunk = _seq_chunking(l, RET_CHUNK)
    grp = _seq_group(bn, chunk, RET_CHUNK, RET_HEADS * RET_DK * RET_DV * 4)
    return pl.pallas_call(
        functools.partial(_ret_chunk_kernel, chunk=chunk, group=grp),
        grid=(bn // grp, l // chunk),
        in_specs=[
            pl.BlockSpec((grp, chunk, RET_HEADS * RET_DK), lambda b, i: (b, i, 0)),
            pl.BlockSpec((grp, chunk, RET_HEADS * RET_DK), lambda b, i: (b, i, 0)),
            pl.BlockSpec((grp, chunk, RET_HEADS * RET_DV), lambda b, i: (b, i, 0)),
            pl.BlockSpec((grp, RET_HEADS, RET_DK, RET_DV), lambda b, i: (b, 0, 0, 0)),
        ],
        out_specs=[
            pl.BlockSpec((grp, chunk, RET_HEADS * RET_DV), lambda b, i: (b, i, 0)),
            pl.BlockSpec((grp, RET_HEADS, RET_DK, RET_DV), lambda b, i: (b, 0, 0, 0)),
        ],
        out_shape=[jax.ShapeDtypeStruct((bn, l, RET_HEADS * RET_DV), F32),
                   jax.ShapeDtypeStruct((bn, RET_HEADS, RET_DK, RET_DV), F32)],
        compiler_params=_cparams("arbitrary", "arbitrary"),
        name="ret_chunk",
    )(q3, k3, v3, s0)


def _ret_mixer_stage(x, bn, l, pos0, s0, g0, g1, wq, wk, wv, wg, wo):
    grp = _seq_group(bn, RET_CHUNK, RET_CHUNK, 0)
    cos, sin = _rotary_tables(pos0, l)
    consts = [g0.reshape(1, -1), g1.reshape(1, -1)] + [a.astype(BF16) for a in (wq, wk, wv, wg, wo)]
    seq_spec = pl.BlockSpec((grp, RET_CHUNK, D_MODEL), lambda b, i: (b, i, 0))
    tab_spec = pl.BlockSpec((RET_CHUNK, RET_DK // 2), lambda b, i: (i, 0))
    state_spec = pl.BlockSpec((grp, RET_HEADS, RET_DK, RET_DV), lambda b, i: (b, 0, 0, 0))
    y, s_new = pl.pallas_call(
        functools.partial(_ret_mixer_kernel, chunk=RET_CHUNK, group=grp),
        grid=(bn // grp, l // RET_CHUNK),
        in_specs=[seq_spec, tab_spec, tab_spec, state_spec] + [_const_spec(a.shape) for a in consts],
        out_specs=[seq_spec, state_spec],
        out_shape=[jax.ShapeDtypeStruct((bn, l, D_MODEL), F32),
                   jax.ShapeDtypeStruct((bn, RET_HEADS, RET_DK, RET_DV), F32)],
        scratch_shapes=[pltpu.VMEM((grp, RET_CHUNK, RET_HEADS * RET_DK), BF16),
                        pltpu.VMEM((grp, RET_CHUNK, RET_HEADS * RET_DK), BF16),
                        pltpu.VMEM((grp, RET_CHUNK, RET_HEADS * RET_DV), BF16),
                        pltpu.VMEM((grp, RET_CHUNK, RET_HEADS * RET_DV), F32),
                        pltpu.VMEM((grp * RET_CHUNK, RET_HEADS * RET_DV), BF16)],
        compiler_params=_cparams("arbitrary", "arbitrary", vmem_mb=LARGE_VMEM_MB),
        name="ret_mixer",
    )(x.reshape(bn, l, D_MODEL), cos, sin, s0, *consts)
    return y.reshape(bn * l, D_MODEL), s_new


def _ret_out_stage(x, o3, time_major, g0, g1, wg, wo):
    rows = x.shape[0]
    tile = min(ROW_TILE, rows)
    bn, l, _ = o3.shape
    o = _from_seq(o3, bn, l, time_major)
    (y,) = _row_grid_call(
        _ret_out_kernel, [x, o], [g0.reshape(1, -1), g1.reshape(1, -1), wg.astype(BF16), wo.astype(BF16)],
        [(D_MODEL, F32)], rows, tile, scratch=[pltpu.VMEM((tile, RET_HEADS * RET_DV), BF16)], name="ret_out")
    return y


def _gla_project(h, wq_ref, wk_ref, wv_ref, wa1_ref, wa2_ref, ba_ref):
    q = (_dot(h, wq_ref[...]) * (GLA_DK ** -0.5)).astype(BF16)
    k = _dot(h, wk_ref[...]).astype(BF16)
    v = _dot(h, wv_ref[...]).astype(BF16)
    low = _dot(h, wa1_ref[...]).astype(BF16)
    logit = _dot(low, wa2_ref[...]) + ba_ref[...]
    return q, k, v, _log_sigmoid(logit) / GLA_GATE_NORM


def _gla_proj_kernel(x_ref, g_ref, wq_ref, wk_ref, wv_ref, wa1_ref, wa2_ref, ba_ref,
                     q_ref, k_ref, v_ref, la_ref):
    h = _rms(x_ref[...], g_ref[...]).astype(BF16)
    q_ref[...], k_ref[...], v_ref[...], la_ref[...] = _gla_project(h, wq_ref, wk_ref, wv_ref, wa1_ref, wa2_ref,
                                                                 ba_ref)


def _gla_unit_stages(q_ref, k_ref, v_ref, la_ref, o_ref, st_sc, chunk, group):
    ri = lax.broadcasted_iota(jnp.int32, (chunk, chunk), 0)
    ci = lax.broadcasted_iota(jnp.int32, (chunk, chunk), 1)
    causal = ri >= ci
    tri = causal.astype(BF16)
    nk = GLA_HEADS * GLA_DK
    units = [(g, hd) for g in range(group) for hd in range(GLA_HEADS)]
    bcs = []
    for g in range(group):
        la = la_ref[g]
        p1 = la.astype(BF16)
        r1 = la - p1.astype(F32)
        p2 = r1.astype(BF16)
        p3 = (r1 - p2.astype(F32)).astype(BF16)
        acc = _dot(tri, jnp.concatenate([p1, p2, p3], axis=1))
        bcs.append(acc[:, :nk] + acc[:, nk:2 * nk] + acc[:, 2 * nk:])
    prep = []
    for g, hd in units:
        ksl = slice(GLA_DK * hd, GLA_DK * (hd + 1))
        bc = bcs[g][:, ksl]
        blast = bc[chunk - 1:chunk, :]
        ref = 0.5 * (bc[0:1, :] + blast)
        qh = q_ref[g, :, ksl].astype(F32)
        kh = k_ref[g, :, ksl].astype(F32)
        qt = (qh * jnp.exp(bc - ref)).astype(BF16)
        kt = (kh * jnp.exp(ref - bc)).astype(BF16)
        qg = (qh * jnp.exp(bc)).astype(BF16)
        kg = (kh * jnp.exp(blast - bc)).astype(BF16)
        row_decay = jnp.broadcast_to(jnp.exp(blast), (SUBLANES, GLA_DK)).T[:, :1]
        prep.append((qt, kt, qg, kg, row_decay))
    yield
    scores = [jnp.where(causal, _dot_nt(qt, kt), 0.0).astype(BF16) for qt, kt, _, _, _ in prep]
    for (g, hd), sc, (_, _, qg, kg, row_decay) in zip(units, scores, prep):
        vsl = slice(GLA_DV * hd, GLA_DV * (hd + 1))
        vh = v_ref[g, :, vsl]
        st = st_sc[g, hd]
        o_ref[g, :, vsl] = _dot(sc, vh) + _dot(qg, st.astype(BF16))
        st_sc[g, hd] = st * row_decay + _dot_tn(kg, vh)


def _gla_copy_states(src, dst, group):
    for g in range(group):
        for hd in range(GLA_HEADS):
            dst[g, hd] = src[g, hd]


def _gla_chunk_kernel(q_ref, k_ref, v_ref, la_ref, s0_ref, o_ref, sout_ref, st_sc, *, chunk, group):
    @pl.when(pl.program_id(1) == 0)
    def _():
        _gla_copy_states(s0_ref, st_sc, group)

    for _ in _gla_unit_stages(q_ref, k_ref, v_ref, la_ref, o_ref, st_sc, chunk, group):
        pass

    @pl.when(pl.program_id(1) == pl.num_programs(1) - 1)
    def _():
        _gla_copy_states(st_sc, sout_ref, group)


def _gla_head_norm(oh, norm_g, gate):
    on = oh * lax.rsqrt(jnp.mean(oh * oh, axis=-1, keepdims=True) + NORM_EPS) * norm_g
    return (on * gate).astype(BF16)


def _gla_mixer_kernel(x_ref, s0_ref, g0_ref, wq_ref, wk_ref, wv_ref, wa1_ref, wa2_ref, ba_ref, g1_ref, ng_ref,
                      wg_ref, wo_ref, y_ref, sout_ref, st_sc, q_sc, k_sc, v_sc, la_sc, o_sc, gate_sc, on_sc,
                      *, chunk, group):
    @pl.when(pl.program_id(1) == 0)
    def _():
        _gla_copy_states(s0_ref, st_sc, group)

    n = group * chunk
    x = x_ref[...].reshape(n, D_MODEL)
    h = _rms(x, g0_ref[...]).astype(BF16)
    for dst, val in zip((q_sc, k_sc, v_sc, la_sc), _gla_project(h, wq_ref, wk_ref, wv_ref, wa1_ref, wa2_ref, ba_ref)):
        dst[...] = val.reshape(dst.shape)
    stages = _gla_unit_stages(q_sc, k_sc, v_sc, la_sc, o_sc, st_sc, chunk, group)
    next(stages)
    gate_sc[...] = _silu(_dot(h, wg_ref[...]))
    next(stages, None)
    for hd in range(GLA_HEADS):
        sl = slice(GLA_DV * hd, GLA_DV * (hd + 1))
        on_sc[:, sl] = _gla_head_norm(o_sc[:, :, sl].reshape(n, GLA_DV), ng_ref[...], gate_sc[:, sl])
    y = x + _rms(_dot(on_sc[...], wo_ref[...]), g1_ref[...])
    y_ref[...] = y.reshape(group, chunk, D_MODEL)

    @pl.when(pl.program_id(1) == pl.num_programs(1) - 1)
    def _():
        _gla_copy_states(st_sc, sout_ref, group)


def _gla_out_kernel(x_ref, o_ref, g0_ref, g1_ref, ng_ref, wg_ref, wo_ref, y_ref, on_sc):
    x = x_ref[...]
    h = _rms(x, g0_ref[...]).astype(BF16)
    gate = _silu(_dot(h, wg_ref[...]))
    for hd in range(GLA_HEADS):
        sl = slice(GLA_DV * hd, GLA_DV * (hd + 1))
        on_sc[:, sl] = _gla_head_norm(o_ref[:, sl], ng_ref[...], gate[:, sl])
    y_ref[...] = x + _rms(_dot(on_sc[...], wo_ref[...]), g1_ref[...])


def _gla_layer(x, bn, l, time_major, s0, g0, g1, wq, wk, wv, wg, wa1, wa2, ba, norm_g, wo):
    rows = x.shape[0]
    tile = min(PROJ_ROW_TILE, rows)
    lanes = 128
    wa1p = jnp.pad(wa1, ((0, 0), (0, lanes - GLA_GATE_RANK))).astype(BF16)
    wa2p = jnp.pad(wa2, ((0, lanes - GLA_GATE_RANK), (0, 0))).astype(BF16)
    proj_consts = [g0.reshape(1, -1), wq.astype(BF16), wk.astype(BF16), wv.astype(BF16), wa1p, wa2p,
                   ba.reshape(1, -1)]
    out_consts = [g0.reshape(1, -1), g1.reshape(1, -1), norm_g.reshape(1, -1), wg.astype(BF16), wo.astype(BF16)]
    chunk = _seq_chunking(l, GLA_CHUNK)
    nc = l // chunk
    grp = GLA_GROUP if bn % GLA_GROUP == 0 else 1
    nk, nv = GLA_HEADS * GLA_DK, GLA_HEADS * GLA_DV
    if not time_major and chunk == GLA_CHUNK:
        consts = proj_consts + out_consts[1:]
        seq_spec = pl.BlockSpec((grp, chunk, D_MODEL), lambda b, i: (b, i, 0))
        state_spec = pl.BlockSpec((grp, GLA_HEADS, GLA_DK, GLA_DV), lambda b, i: (b, 0, 0, 0))
        y, s_new = pl.pallas_call(
            functools.partial(_gla_mixer_kernel, chunk=chunk, group=grp),
            grid=(bn // grp, nc),
            in_specs=[seq_spec, state_spec] + [_const_spec(a.shape) for a in consts],
            out_specs=[seq_spec, state_spec],
            out_shape=[jax.ShapeDtypeStruct((bn, l, D_MODEL), F32),
                       jax.ShapeDtypeStruct((bn, GLA_HEADS, GLA_DK, GLA_DV), F32)],
            scratch_shapes=[pltpu.VMEM((grp, GLA_HEADS, GLA_DK, GLA_DV), F32),
                            pltpu.VMEM((grp, chunk, nk), BF16), pltpu.VMEM((grp, chunk, nk), BF16),
                            pltpu.VMEM((grp, chunk, nv), BF16), pltpu.VMEM((grp, chunk, nk), F32),
                            pltpu.VMEM((grp, chunk, nv), F32),
                            pltpu.VMEM((grp * chunk, nv), F32),
                            pltpu.VMEM((grp * chunk, nv), BF16)],
            compiler_params=_cparams("arbitrary", "arbitrary"),
            name="gla_mixer",
        )(x.reshape(bn, l, D_MODEL), s0, *consts)
        return y.reshape(rows, D_MODEL), s_new
    q, k, v, la = _row_grid_call(
        _gla_proj_kernel, [x], proj_consts, [(nk, BF16), (nk, BF16), (nv, BF16), (nk, F32)],
        rows, tile, name="gla_proj", vmem_mb=LARGE_VMEM_MB)
    q3, k3, v3, la3 = (_to_seq(a, bn, l, time_major) for a in (q, k, v, la))
    o3, s_new = pl.pallas_call(
        functools.partial(_gla_chunk_kernel, chunk=chunk, group=grp),
        grid=(bn // grp, nc),
        in_specs=[
            pl.BlockSpec((grp, chunk, GLA_HEADS * GLA_DK), lambda b, i: (b, i, 0)),
            pl.BlockSpec((grp, chunk, GLA_HEADS * GLA_DK), lambda b, i: (b, i, 0)),
            pl.BlockSpec((grp, chunk, GLA_HEADS * GLA_DV), lambda b, i: (b, i, 0)),
            pl.BlockSpec((grp, chunk, GLA_HEADS * GLA_DK), lambda b, i: (b, i, 0)),
            pl.BlockSpec((grp, GLA_HEADS, GLA_DK, GLA_DV), lambda b, i: (b, 0, 0, 0)),
        ],
        out_specs=[
            pl.BlockSpec((grp, chunk, GLA_HEADS * GLA_DV), lambda b, i: (b, i, 0)),
            pl.BlockSpec((grp, GLA_HEADS, GLA_DK, GLA_DV), lambda b, i: (b, 0, 0, 0)),
        ],
        out_shape=[jax.ShapeDtypeStruct((bn, l, GLA_HEADS * GLA_DV), F32),
                   jax.ShapeDtypeStruct((bn, GLA_HEADS, GLA_DK, GLA_DV), F32)],
        scratch_shapes=[pltpu.VMEM((grp, GLA_HEADS, GLA_DK, GLA_DV), F32)],
        compiler_params=_cparams("arbitrary", "arbitrary"),
        name="gla_chunk",
    )(q3, k3, v3, la3, s0)
    o = _from_seq(o3, bn, l, time_major)
    (y,) = _row_grid_call(
        _gla_out_kernel, [x, o], out_consts,
        [(D_MODEL, F32)], rows, tile, scratch=[pltpu.VMEM((tile, GLA_HEADS * GLA_DV), BF16)], name="gla_out",
        vmem_mb=LARGE_VMEM_MB)
    return y, s_new


def _transpose_rows(x, a, b):
    return x.reshape(a, b, x.shape[1]).transpose(1, 0, 2).reshape(a * b, x.shape[1])


def _trunk(x, pos0, s5_re, s5_im, ret_s, gla_s, conv_buf, p, batch_major_mixers, bf16_w, cast_weights,
           defer_ret=False, host=None):
    bf16_w = dict(bf16_w)

    def w(name, idx):
        return bf16_w.get((name, idx), p[name][idx])
    bn, l, _ = x.shape
    depth = p['norm_g'].shape[0]
    assert not batch_major_mixers or bn == SUBLANES
    rows = x.reshape(bn * l, D_MODEL)
    time_major = False
    if not batch_major_mixers:
        rows, time_major = _transpose_rows(rows, bn, l), True
    new_re, new_im, new_ret, new_gla, new_conv = [], [], [], [], []
    hosted_o, hosted_s, hosted_n = [], None, 0
    for layer in range(depth):
        g = p['norm_g'][layer]
        kind = layer % N_MIXERS
        j = layer // N_MIXERS
        if kind == 0:
            todo_host = None
            if host is not None and hosted_n < host[0].shape[0]:
                todo_host = tuple(host) + (hosted_n, hosted_s)
            rows, hr, hi, hres = _s5_layer(rows, bn, s5_re[j], s5_im[j], g[0], g[1],
                                           p['s5_lambda_re'][j], p['s5_lambda_im'][j], p['s5_b_re'][j],
                                           p['s5_b_im'][j], p['s5_c_re'][j], p['s5_c_im'][j], p['s5_log_dt'][j],
                                           p['s5_d'][j], p['s5_w_glu'], j, x_bm=not time_major, host=todo_host)
            if hres:
                hosted_o.append(hres[0])
                hosted_s = hres[1]
                hosted_n += hres[0].shape[0]
            time_major = True
            new_re.append(hr)
            new_im.append(hi)
        elif kind == 1:
            if not defer_ret and not time_major and l % RET_CHUNK == 0:
                rows, s = _ret_mixer_stage(rows, bn, l, pos0, ret_s[j], g[0], g[1],
                                           *(w(n, j) for n in MIXER_WEIGHTS[kind]))
            else:
                q3, k3, v3 = _ret_proj_stage(rows, bn, l, time_major, pos0, g[0], w('ret_wq', j), w('ret_wk', j),
                                             w('ret_wv', j))
                if defer_ret:
                    o3, s, more = yield q3, k3, v3, ret_s[j]
                    bf16_w.update(more)
                else:
                    o3, s = _ret_chunk_stage(q3, k3, v3, ret_s[j])
                rows = _ret_out_stage(rows, o3, time_major, g[0], g[1], w('ret_wg', j), w('ret_wo', j))
            new_ret.append(s)
        else:
            rows, s = _gla_layer(rows, bn, l, time_major, gla_s[j], g[0], g[1], w('gla_wq', j), w('gla_wk', j),
                                 w('gla_wv', j), w('gla_wg', j), p['gla_wa1'][j], p['gla_wa2'][j], p['gla_ba'][j],
                                 p['gla_norm_g'][j], w('gla_wo', j))
            new_gla.append(s)
        next_is_s5 = layer + 1 < depth and (layer + 1) % N_MIXERS == 0
        out_bm = batch_major_mixers and not next_is_s5
        cin = conv_buf[layer].transpose(1, 0, 2).reshape((CONV_W - 1) * bn, D_FF)
        todo = []
        if cast_weights and layer + 1 < depth:
            mixer = MIXER_WEIGHTS[(layer + 1) % N_MIXERS]
            todo = [(n, layer + 1) for n in FFN_WEIGHTS] + [(n, (layer + 1) // N_MIXERS) for n in mixer]
            todo = [key for key in todo if key not in bf16_w]
        rows, cout, cast = _ffn_layer(rows, bn, cin, g[2], g[3], w('ffn_w_gate', layer), w('ffn_w_up', layer),
                                      p['ffn_conv_w'][layer], p['ffn_conv_b'][layer], w('ffn_w_down', layer),
                                      x_bm=not time_major, out_bm=out_bm,
                                      next_weights=[(p[n], idx) for n, idx in todo])
        bf16_w.update(zip(todo, cast))
        time_major = not out_bm
        new_conv.append(cout.reshape(CONV_W - 1, bn, D_FF).transpose(1, 0, 2))
    if time_major:
        rows = _transpose_rows(rows, l, bn)
    y = rows.reshape(bn, l, D_MODEL)
    hosted = None
    if host is not None:
        assert hosted_n == host[0].shape[0]
        hosted = (jnp.concatenate(hosted_o), hosted_s)
    outs = (y, jnp.stack(new_re), jnp.stack(new_im), jnp.stack(new_ret), jnp.stack(new_gla),
            jnp.stack(new_conv))
    return outs, bf16_w, hosted


def _finish(gen, reply=None):
    try:
        gen.send(reply)
    except StopIteration as done:
        return done.value
    raise AssertionError("unexpected request")


def kernel(x_prompt, x_sample, state_s5_re, state_s5_im, state_ret, state_gla, cache_ffn_conv,
           norm_g, s5_lambda_re, s5_lambda_im, s5_b_re, s5_b_im, s5_c_re, s5_c_im, s5_log_dt, s5_d, s5_w_glu,
           ret_wq, ret_wk, ret_wv, ret_wg, ret_wo,
           gla_wq, gla_wk, gla_wv, gla_wg, gla_wa1, gla_wa2, gla_ba, gla_norm_g, gla_wo,
           ffn_w_gate, ffn_w_up, ffn_conv_w, ffn_conv_b, ffn_w_down):
    p = dict(norm_g=norm_g, s5_lambda_re=s5_lambda_re, s5_lambda_im=s5_lambda_im, s5_b_re=s5_b_re,
             s5_b_im=s5_b_im, s5_c_re=s5_c_re, s5_c_im=s5_c_im, s5_log_dt=s5_log_dt, s5_d=s5_d,
             s5_w_glu=s5_w_glu.astype(BF16), ret_wq=ret_wq, ret_wk=ret_wk, ret_wv=ret_wv, ret_wg=ret_wg, ret_wo=ret_wo,
             gla_wq=gla_wq, gla_wk=gla_wk, gla_wv=gla_wv, gla_wg=gla_wg, gla_wa1=gla_wa1, gla_wa2=gla_wa2,
             gla_ba=gla_ba, gla_norm_g=gla_norm_g, gla_wo=gla_wo, ffn_w_gate=ffn_w_gate, ffn_w_up=ffn_w_up,
             ffn_conv_w=ffn_conv_w, ffn_conv_b=ffn_conv_b, ffn_w_down=ffn_w_down)
    bp = x_prompt.shape[0]
    z_s5 = jnp.zeros((state_s5_re.shape[0], bp) + state_s5_re.shape[2:], F32)
    z_ret = jnp.zeros((state_ret.shape[0], bp) + state_ret.shape[2:], F32)
    z_gla = jnp.zeros((state_gla.shape[0], bp) + state_gla.shape[2:], F32)
    z_conv = jnp.zeros((cache_ffn_conv.shape[0], bp) + cache_ffn_conv.shape[2:], x_prompt.dtype)
    first = {(n, 0): p[n][0].astype(BF16) for n in FFN_WEIGHTS}
    first.update({(n, j): p[n][j].astype(BF16) for n in ('ret_wq', 'ret_wk', 'ret_wv') for j in range(ret_wq.shape[0])})
    sample = _trunk(x_sample, PAST_LEN, state_s5_re, state_s5_im, state_ret, state_gla, cache_ffn_conv, p,
                    False, first, False, defer_ret=True)
    request = next(sample)
    outs_p, bf16_w, hosted = _finish(_trunk(x_prompt, 0, z_s5, z_s5, z_ret, z_gla, z_conv, p, True, first, True,
                                            host=request))
    outs_s, _, _ = _finish(sample, hosted + (bf16_w,))
    return (outs_p[0], outs_s[0]) + tuple(outs_p[1:]) + tuple(outs_s[1:])
```

```python
import functools
import math

import jax
import jax.numpy as jnp
from jax import lax
from jax.experimental import pallas as pl
from jax.experimental.pallas import tpu as pltpu

F32 = jnp.float32
BF16 = jnp.bfloat16

D_MODEL = 1024
PAST_LEN = 16384
N_MIXERS = 3
FFN_WEIGHTS = ('ffn_w_gate', 'ffn_w_up', 'ffn_w_down')
MIXER_WEIGHTS = ((), ('ret_wq', 'ret_wk', 'ret_wv', 'ret_wg', 'ret_wo'),
                 ('gla_wq', 'gla_wk', 'gla_wv', 'gla_wg', 'gla_wo'))
S5_GROUP = 16
S5_GROUPS = D_MODEL // S5_GROUP
S5_STATE = 64
S5_SUPER = 4
S5_SG_CH = D_MODEL // S5_SUPER
S5_SG_ST = (S5_GROUPS // S5_SUPER) * S5_STATE
RET_HEADS = 4
RET_DK = D_MODEL // RET_HEADS
RET_DV = 2 * D_MODEL // RET_HEADS
RET_CHUNK = 256
ROPE_BASE = 10000.0
GLA_HEADS = 4
GLA_DK = D_MODEL // 2 // GLA_HEADS
GLA_DV = D_MODEL // GLA_HEADS
GLA_GATE_RANK = 16
GLA_GATE_NORM = 16.0
GLA_CHUNK = 64
GLA_GROUP = 8
D_FF = ((8 * D_MODEL // 3 + 255) // 256) * 256
FFN_SUB = 256
FFN_ROW_TILE = 1024
S5_BU_BUFFERS = 4
N_S5_INPUTS = 12
HOSTED_SEQS = 2
LONG_CHUNK_GROUP = 2
SEQ_GROUP_BYTES = 8 * 2 ** 20
CONV_W = 3
NORM_EPS = 1e-6
ROW_TILE = 512
PROJ_ROW_TILE = 1024
SUBLANES = 8
V7X_VMEM_MB = 64
VMEM_MB = V7X_VMEM_MB - 12
LARGE_VMEM_MB = V7X_VMEM_MB - 6


def _cparams(*sem, vmem_mb=None):
    return pltpu.CompilerParams(dimension_semantics=sem, vmem_limit_bytes=(vmem_mb or VMEM_MB) * 2 ** 20)


def _const_spec(shape):
    zeros = (0,) * len(shape)
    return pl.BlockSpec(shape, lambda *_: zeros, pipeline_mode=pl.Buffered(1))


def _stacked_spec(shape, index):
    where = (index,) + (0,) * len(shape)
    return pl.BlockSpec((None,) + tuple(shape), lambda *_: where, pipeline_mode=pl.Buffered(1))


def _rms(x, g):
    return x * lax.rsqrt(jnp.mean(x * x, axis=-1, keepdims=True) + NORM_EPS) * g


def _sigmoid(x):
    return 1.0 / (1.0 + jnp.exp(-x))


def _silu(x):
    return x * _sigmoid(x)


def _gelu_tanh(x):
    return 0.5 * x * (1.0 + jnp.tanh(math.sqrt(2.0 / math.pi) * (x + 0.044715 * (x * x * x))))


def _log_sigmoid(x):
    return jnp.minimum(x, 0.0) - jnp.log(1.0 + jnp.exp(-jnp.abs(x)))


def _dot(a, b):
    return jnp.dot(a, b, preferred_element_type=F32)


def _dot_nt(a, b):
    return lax.dot_general(a, b, (((1,), (1,)), ((), ())), preferred_element_type=F32)


def _dot_tn(a, b):
    return lax.dot_general(a, b, (((0,), (0,)), ((), ())), preferred_element_type=F32)


def _s5_kernel(x_ref, h0re_ref, h0im_ref, g0_ref, g1_ref, are_ref, aim_ref, bre_ref, bim_ref, cc_ref, dsk_ref,
               wglu_ref, *rest, bn, tt, x_bm, hosted, n_host_in):
    host_in, rest = rest[:n_host_in], rest[n_host_in:]
    o_ref, hre_ref, him_ref, rest = rest[0], rest[1], rest[2], rest[3:]
    host_out, (bu_sc, y_sc, bbd_ref, cbd_ref) = rest[:2 if hosted else 0], rest[2 if hosted else 0:]
    @pl.when(pl.program_id(0) == 0)
    def _():
        hre_ref[...] = h0re_ref[...]
        him_ref[...] = h0im_ref[...]
        ngl = S5_GROUPS // S5_SUPER

        def lane_tiler(k, n):
            r = lax.broadcasted_iota(jnp.int32, (k, n), 0)
            c = lax.broadcasted_iota(jnp.int32, (k, n), 1)
            return ((c & (k - 1)) == r).astype(BF16)

        tile_n = lane_tiler(S5_STATE, S5_SG_ST)
        tile_p = lane_tiler(S5_GROUP, S5_SG_CH)
        sh_p = S5_GROUP.bit_length() - 1
        sh_n = S5_STATE.bit_length() - 1
        rb = lax.broadcasted_iota(jnp.int32, (S5_SG_CH, S5_SG_ST), 0) >> sh_p
        cb = lax.broadcasted_iota(jnp.int32, (S5_SG_CH, S5_SG_ST), 1) >> sh_n
        rc = (lax.broadcasted_iota(jnp.int32, (2 * S5_SG_ST, S5_SG_CH), 0) >> sh_n) & (ngl - 1)
        cc = lax.broadcasted_iota(jnp.int32, (2 * S5_SG_ST, S5_SG_CH), 1) >> sh_p
        for sg in range(S5_SUPER):
            for part, ref in ((0, bre_ref), (1, bim_ref)):
                blk = _dot(ref[sg].astype(BF16), tile_n)
                bbd_ref[sg, :, S5_SG_ST * part:S5_SG_ST * (part + 1)] = jnp.where(rb == cb, blk, 0.0).astype(BF16)
            blk = _dot(cc_ref[sg].astype(BF16), tile_p)
            cbd_ref[sg] = jnp.where(rc == cc, blk, 0.0).astype(BF16)

    x = x_ref[...]
    if x_bm:
        x = jnp.swapaxes(x, 0, 1).reshape(tt * bn, D_MODEL)
    u = _rms(x, g0_ref[...])
    ub = u.astype(BF16)
    half = S5_SG_ST // 2
    host_stages = iter(())
    if hosted:
        hq, hk, hv, hs = host_in[:4]
        host_stages = _ret_unit_stages(hq, hk, hv, hs, host_out[0], host_out[1], hq.shape[1], hosted)
    for sg in range(S5_SUPER):
        buf = bu_sc.at[sg % S5_BU_BUFFERS]
        buf[...] = _dot(ub[:, S5_SG_CH * sg:S5_SG_CH * (sg + 1)], bbd_ref[sg])
        for hf in range(2):
            c_re = half * hf
            c_im = S5_SG_ST + half * hf
            a_lo = S5_SG_ST * sg + half * hf
            ar = jnp.broadcast_to(are_ref[:, a_lo:a_lo + half], (8, half))
            ai = jnp.broadcast_to(aim_ref[:, a_lo:a_lo + half], (8, half))
            for bt in range(bn // 8):
                r8 = 8 * bt
                hr = hre_ref[r8:r8 + 8, a_lo:a_lo + half]
                hi = him_ref[r8:r8 + 8, a_lo:a_lo + half]
                for t in range(tt):
                    r0 = t * bn + r8
                    bur = buf[r0:r0 + 8, c_re:c_re + half]
                    bui = buf[r0:r0 + 8, c_im:c_im + half]
                    hr, hi = ar * hr - ai * hi + bur, ar * hi + ai * hr + bui
                    buf[r0:r0 + 8, c_re:c_re + half] = hr
                    buf[r0:r0 + 8, c_im:c_im + half] = hi
                hre_ref[r8:r8 + 8, a_lo:a_lo + half] = hr
                him_ref[r8:r8 + 8, a_lo:a_lo + half] = hi
        y_sc[:, S5_SG_CH * sg:S5_SG_CH * (sg + 1)] = _dot(buf[...].astype(BF16), cbd_ref[sg])
        next(host_stages, None)
    y = y_sc[...] + dsk_ref[...] * u
    z = _dot(_gelu_tanh(y).astype(BF16), wglu_ref[...])
    mix = z[:, :D_MODEL] * _sigmoid(z[:, D_MODEL:])
    o_ref[...] = x + _rms(mix, g1_ref[...])


def _s5_layer(x, bn, h0_re, h0_im, g0, g1, lam_re, lam_im, b_re, b_im, c_re, c_im, log_dt, d_skip, w_glu,
              j, x_bm=False, host=None):
    rows = x.shape[0]
    tt = min(rows // bn, max(1, ROW_TILE // bn))
    tile = tt * bn
    steps = rows // tile
    host_ins, host_in_specs, host_out_specs, host_out_shapes = [], [], [], []
    aliases = {}
    if host is not None:
        hq, hk, hv, hs, first, acc = host
        assert first % HOSTED_SEQS == 0
        blk0 = first // HOSTED_SEQS
        host_ins = [hq, hk, hv, hs]
        for a in host_ins:
            host_in_specs.append(pl.BlockSpec((HOSTED_SEQS,) + a.shape[1:],
                                              lambda i, nd=a.ndim: (blk0 + i,) + (0,) * (nd - 1)))
        host_out_specs = [pl.BlockSpec((HOSTED_SEQS,) + hv.shape[1:], lambda i: (i, 0, 0)),
                          pl.BlockSpec((HOSTED_SEQS,) + hs.shape[1:], lambda i: (blk0 + i, 0, 0, 0))]
        host_out_shapes = [jax.ShapeDtypeStruct((steps * HOSTED_SEQS,) + hv.shape[1:], F32),
                           jax.ShapeDtypeStruct(hs.shape, F32)]
        if acc is not None:
            host_ins.append(acc)
            host_in_specs.append(pl.BlockSpec(memory_space=pl.ANY))
            aliases = {N_S5_INPUTS + 4: 4}
    o_spec = pl.BlockSpec((tile, D_MODEL), lambda i: (i, 0))
    x_spec = o_spec
    if x_bm:
        assert bn == SUBLANES
        x = x.reshape(bn, rows // bn, D_MODEL)
        x_spec = pl.BlockSpec((bn, tt, D_MODEL), lambda i: (0, i, 0))
    lam = lax.complex(lam_re, lam_im)
    dt = jnp.exp(log_dt)[:, None]
    a_bar = jnp.exp(lam * dt)
    b_bar = ((a_bar - 1.0) / lam)[..., None] * lax.complex(b_re, b_im)
    bt = b_bar.transpose(0, 2, 1).reshape(S5_SUPER, S5_SG_CH, S5_STATE)
    bt_re, bt_im = jnp.real(bt), jnp.imag(bt)
    ct_re = c_re.transpose(0, 2, 1).reshape(S5_SUPER, S5_SG_ST, S5_GROUP)
    ct_im = c_im.transpose(0, 2, 1).reshape(S5_SUPER, S5_SG_ST, S5_GROUP)
    ct = jnp.concatenate([ct_re, -ct_im], axis=1)
    a_re = jnp.real(a_bar).reshape(1, S5_GROUPS * S5_STATE)
    a_im = jnp.imag(a_bar).reshape(1, S5_GROUPS * S5_STATE)
    nst = S5_GROUPS * S5_STATE

    out, new_re, new_im, *hosted = pl.pallas_call(
        functools.partial(_s5_kernel, bn=bn, tt=tt, x_bm=x_bm, hosted=HOSTED_SEQS if host_ins else 0,
                          n_host_in=len(host_ins)),
        grid=(steps,),
        in_specs=[
            x_spec,
            _const_spec((bn, nst)), _const_spec((bn, nst)),
            _const_spec((1, D_MODEL)), _const_spec((1, D_MODEL)),
            _const_spec((1, nst)), _const_spec((1, nst)),
            _const_spec((S5_SUPER, S5_SG_CH, S5_STATE)), _const_spec((S5_SUPER, S5_SG_CH, S5_STATE)),
            _const_spec((S5_SUPER, 2 * S5_SG_ST, S5_GROUP)),
            _const_spec((1, D_MODEL)),
            _stacked_spec((D_MODEL, 2 * D_MODEL), j),
        ] + host_in_specs,
        out_specs=[o_spec, _const_spec((bn, nst)), _const_spec((bn, nst))] + host_out_specs,
        out_shape=[jax.ShapeDtypeStruct((rows, D_MODEL), F32), jax.ShapeDtypeStruct((bn, nst), F32),
                   jax.ShapeDtypeStruct((bn, nst), F32)] + host_out_shapes,
        scratch_shapes=[pltpu.VMEM((S5_BU_BUFFERS, tile, 2 * S5_SG_ST), F32), pltpu.VMEM((tile, D_MODEL), F32),
                        pltpu.VMEM((S5_SUPER, S5_SG_CH, 2 * S5_SG_ST), BF16),
                        pltpu.VMEM((S5_SUPER, 2 * S5_SG_ST, S5_SG_CH), BF16)],
        input_output_aliases=aliases,
        compiler_params=_cparams("arbitrary", vmem_mb=LARGE_VMEM_MB if host_ins else None),
        name="s5_layer",
    )(x, h0_re.reshape(bn, nst), h0_im.reshape(bn, nst), g0.reshape(1, -1), g1.reshape(1, -1), a_re, a_im,
      bt_re, bt_im, ct, d_skip.reshape(1, -1), w_glu, *host_ins)
    return out, new_re.reshape(bn, S5_GROUPS, S5_STATE), new_im.reshape(bn, S5_GROUPS, S5_STATE), tuple(hosted)


def _ffn_kernel(x_ref, cin_ref, g2_ref, g3_ref, wg_ref, wu_ref, wd_ref, cw_ref, cb_ref, *rest,
                bn, tile, x_bm, out_bm, cast_next):
    next_f32, rest = rest[:cast_next], rest[cast_next:]
    o_ref, cout_ref, rest = rest[0], rest[1], rest[2:]
    next_bf16, (hdn_sc, gext_sc, carry_sc) = rest[:cast_next], rest[cast_next:]
    i = pl.program_id(0)
    halo = 2 * bn
    base = gext_sc.shape[1] - tile

    @pl.when(i == 0)
    def _():
        carry_sc[...] = cin_ref[...]

    x = x_ref[...]
    if x_bm:
        x = jnp.swapaxes(x, 0, 1).reshape(tile, D_MODEL)
    h = _rms(x, g2_ref[...]).astype(BF16)
    for j in range(D_FF // FFN_SUB):
        sl = slice(FFN_SUB * j, FFN_SUB * (j + 1))
        buf = gext_sc.at[j % 2]
        gpre = _dot(h, wg_ref[:, sl])
        buf[base - halo:base, :] = carry_sc[:, sl]
        buf[base:base + tile, :] = gpre
        carry_sc[:, sl] = gpre[tile - halo:tile, :]
        gconv = cb_ref[:, sl] + cw_ref[0:1, sl] * buf[base - halo:base - halo + tile, :]
        gconv = gconv + cw_ref[1:2, sl] * buf[base - bn:base - bn + tile, :]
        gconv = gconv + cw_ref[2:3, sl] * gpre
        hdn_sc[:, sl] = (_silu(gconv) * _dot(h, wu_ref[:, sl])).astype(BF16)
        if j == 0:
            for src, dst in zip(next_f32, next_bf16):
                dst[...] = src[...].astype(BF16)
    y = x + _rms(_dot(hdn_sc[...], wd_ref[...]), g3_ref[...])
    if out_bm:
        y = jnp.swapaxes(y.reshape(tile // bn, bn, D_MODEL), 0, 1)
    o_ref[...] = y

    @pl.when(i == pl.num_programs(0) - 1)
    def _():
        cout_ref[...] = carry_sc[...]


def _ffn_layer(x, bn, cin, g2, g3, w_gate, w_up, conv_w, conv_b, w_down, x_bm=False, out_bm=False,
               next_weights=()):
    rows = x.shape[0]
    tile = min(FFN_ROW_TILE, rows)
    tt = tile // bn
    halo = 2 * bn
    steps = rows // tile
    cast_ins, cast_in_specs, cast_out_specs, cast_out_shapes = [], [], [], []
    assert w_gate.dtype == w_up.dtype == w_down.dtype == BF16
    for w, idx in next_weights:
        k, n = w.shape[1:]
        kb = k // steps
        assert kb * steps == k and kb % (2 * SUBLANES) == 0
        cast_ins.append(w)
        cast_in_specs.append(pl.BlockSpec((None, kb, n), lambda i, idx=idx: (idx, i, 0)))
        cast_out_specs.append(pl.BlockSpec((kb, n), lambda i: (i, 0)))
        cast_out_shapes.append(jax.ShapeDtypeStruct((k, n), BF16))
    assert not (x_bm or out_bm) or bn == SUBLANES
    tm_spec = pl.BlockSpec((tile, D_MODEL), lambda i: (i, 0))
    bm_spec = pl.BlockSpec((bn, tt, D_MODEL), lambda i: (0, i, 0))
    if x_bm:
        x = x.reshape(bn, rows // bn, D_MODEL)
    out_shape = (bn, rows // bn, D_MODEL) if out_bm else (rows, D_MODEL)
    out, cout, *cast = pl.pallas_call(
        functools.partial(_ffn_kernel, bn=bn, tile=tile, x_bm=x_bm, out_bm=out_bm, cast_next=len(cast_ins)),
        grid=(steps,),
        in_specs=[
            bm_spec if x_bm else tm_spec,
            _const_spec((halo, D_FF)),
            _const_spec((1, D_MODEL)), _const_spec((1, D_MODEL)),
            _const_spec((D_MODEL, D_FF)), _const_spec((D_MODEL, D_FF)), _const_spec((D_FF, D_MODEL)),
            _const_spec((CONV_W, D_FF)), _const_spec((1, D_FF)),
        ] + cast_in_specs,
        out_specs=[bm_spec if out_bm else tm_spec, _const_spec((halo, D_FF))] + cast_out_specs,
        out_shape=[jax.ShapeDtypeStruct(out_shape, F32), jax.ShapeDtypeStruct((halo, D_FF), F32)] + cast_out_shapes,
        scratch_shapes=[pltpu.VMEM((tile, D_FF), BF16),
                        pltpu.VMEM((2, halo + tile, FFN_SUB), F32),
                        pltpu.VMEM((halo, D_FF), F32)],
        compiler_params=_cparams("arbitrary", vmem_mb=LARGE_VMEM_MB),
        name="conv_ffn",
    )(x, cin, g2.reshape(1, -1), g3.reshape(1, -1), w_gate, w_up, w_down, conv_w, conv_b.reshape(1, -1), *cast_ins)
    return out.reshape(rows, D_MODEL), cout, tuple(cast)


def _ret_proj_kernel(x_ref, cos_ref, sin_ref, g_ref, wq_ref, wk_ref, wv_ref, q_ref, k_ref, v_ref):
    h = _rms(x_ref[...], g_ref[...]).astype(BF16)
    cos = cos_ref[...]
    sin = sin_ref[...]
    hw = RET_DK // 2
    for w_ref, dst, scale in ((wq_ref, q_ref, 1.0), (wk_ref, k_ref, RET_DK ** -0.5)):
        p = _dot(h, w_ref[...])
        for hd in range(RET_HEADS):
            lo = RET_DK * hd
            t1 = p[:, lo:lo + hw]
            t2 = p[:, lo + hw:lo + 2 * hw]
            dst[:, lo:lo + hw] = ((t1 * cos - t2 * sin) * scale).astype(BF16)
            dst[:, lo + hw:lo + 2 * hw] = ((t2 * cos + t1 * sin) * scale).astype(BF16)
    v_ref[...] = _dot(h, wv_ref[...]).astype(BF16)


def _ret_unit_stages(q_ref, k_ref, v_ref, sin_ref, o_ref, sout_ref, chunk, group):
    ri = lax.broadcasted_iota(jnp.int32, (chunk, chunk), 0)
    ci = lax.broadcasted_iota(jnp.int32, (chunk, chunk), 1)
    rel = (ri - ci).astype(F32)
    causal = ri >= ci
    idx = lax.broadcasted_iota(jnp.int32, (chunk, 1), 0).astype(F32)
    lgs = [math.log1p(-(2.0 ** (-5.0 - hd))) for hd in range(RET_HEADS)]
    units = [(g, hd) for g in range(group) for hd in range(RET_HEADS)]
    qsl = [slice(RET_DK * hd, RET_DK * (hd + 1)) for hd in range(RET_HEADS)]
    vsl = [slice(RET_DV * hd, RET_DV * (hd + 1)) for hd in range(RET_HEADS)]
    decay = [jnp.where(causal, jnp.exp(jnp.maximum(rel, 0.0) * lg), 0.0) for lg in lgs]
    scores = [(_dot_nt(q_ref[g, :, qsl[hd]], k_ref[g, :, qsl[hd]]) * decay[hd]).astype(BF16) for g, hd in units]
    yield
    for (g, hd), sc in zip(units, scores):
        q_decay = jnp.exp((idx + 1.0) * lgs[hd])
        inter = _dot(q_ref[g, :, qsl[hd]], sin_ref[g, hd].astype(BF16)) * q_decay
        o_ref[g, :, vsl[hd]] = _dot(sc, v_ref[g, :, vsl[hd]]) + inter
    yield
    for g, hd in units:
        k_decay = jnp.exp((chunk - 1.0 - idx) * lgs[hd])
        kd = (k_ref[g, :, qsl[hd]].astype(F32) * k_decay).astype(BF16)
        sout_ref[g, hd] = sin_ref[g, hd] * math.exp(chunk * lgs[hd]) + _dot_tn(kd, v_ref[g, :, vsl[hd]])


def _ret_chunk_kernel(q_ref, k_ref, v_ref, s0_ref, o_ref, sout_ref, *, chunk, group):
    @pl.when(pl.program_id(1) == 0)
    def _():
        sout_ref[...] = s0_ref[...]

    for _ in _ret_unit_stages(q_ref, k_ref, v_ref, sout_ref, o_ref, sout_ref, chunk, group):
        pass


def _ret_head_norm(oh, gate):
    mu = jnp.mean(oh, axis=-1, keepdims=True)
    ctr = oh - mu
    var = jnp.mean(ctr * ctr, axis=-1, keepdims=True)
    return (ctr * lax.rsqrt(var + NORM_EPS) * gate).astype(BF16)


def _ret_out_kernel(x_ref, o_ref, g0_ref, g1_ref, wg_ref, wo_ref, y_ref, on_sc):
    x = x_ref[...]
    h = _rms(x, g0_ref[...]).astype(BF16)
    gate = _silu(_dot(h, wg_ref[...]))
    for hd in range(RET_HEADS):
        sl = slice(RET_DV * hd, RET_DV * (hd + 1))
        on_sc[:, sl] = _ret_head_norm(o_ref[:, sl], gate[:, sl])
    y_ref[...] = x + _rms(_dot(on_sc[...], wo_ref[...]), g1_ref[...])


def _ret_mixer_kernel(x_ref, cos_ref, sin_ref, *rest, chunk, group, zero_state):
    s0_ref, rest = (None, rest) if zero_state else (rest[0], rest[1:])
    g0_ref, g1_ref, wq_ref, wk_ref, wv_ref, wg_ref, wo_ref, y_ref, sout_ref, q_sc, k_sc, v_sc, o_sc, on_sc = rest

    @pl.when(pl.program_id(1) == 0)
    def _():
        sout_ref[...] = jnp.zeros(sout_ref.shape, F32) if zero_state else s0_ref[...]

    n = group * chunk
    x = x_ref[...].reshape(n, D_MODEL)
    h = _rms(x, g0_ref[...]).astype(BF16)
    cos = cos_ref[...]
    sin = sin_ref[...]
    hw = RET_DK // 2
    for w_ref, dst, scale in ((wq_ref, q_sc, 1.0), (wk_ref, k_sc, RET_DK ** -0.5)):
        p = _dot(h, w_ref[...]).reshape(group, chunk, RET_HEADS * RET_DK)
        for hd in range(RET_HEADS):
            lo = RET_DK * hd
            t1 = p[:, :, lo:lo + hw]
            t2 = p[:, :, lo + hw:lo + 2 * hw]
            dst[:, :, lo:lo + hw] = ((t1 * cos - t2 * sin) * scale).astype(BF16)
            dst[:, :, lo + hw:lo + 2 * hw] = ((t2 * cos + t1 * sin) * scale).astype(BF16)
    v_sc[...] = _dot(h, wv_ref[...]).astype(BF16).reshape(group, chunk, RET_HEADS * RET_DV)
    stages = _ret_unit_stages(q_sc, k_sc, v_sc, sout_ref, o_sc, sout_ref, chunk, group)
    next(stages)
    next(stages)
    for hd in range(RET_HEADS):
        sl = slice(RET_DV * hd, RET_DV * (hd + 1))
        gate = _silu(_dot(h, wg_ref[:, sl]))
        on_sc[:, sl] = _ret_head_norm(o_sc[:, :, sl].reshape(n, RET_DV), gate)
    proj = _dot(on_sc[...], wo_ref[...])
    next(stages, None)
    y = x + _rms(proj, g1_ref[...])
    y_ref[...] = y.reshape(group, chunk, D_MODEL)


def _row_grid_call(kernel, row_ins, const_ins, outs, rows, tile, scratch=(), name=None, tab_ins=(),
                   vmem_mb=None):
    nt = rows // tile
    in_specs = [pl.BlockSpec((tile, a.shape[1]), lambda i: (i, 0)) for a in row_ins]
    for a in tab_ins:
        ntab = a.shape[0] // tile
        in_specs.append(pl.BlockSpec((tile, a.shape[1]), lambda i, ntab=ntab: (i % ntab, 0)))
    in_specs += [_const_spec(a.shape) for a in const_ins]
    return pl.pallas_call(
        kernel,
        grid=(nt,),
        in_specs=in_specs,
        out_specs=[pl.BlockSpec((tile, n), lambda i: (i, 0)) for n, _ in outs],
        out_shape=[jax.ShapeDtypeStruct((rows, n), dt) for n, dt in outs],
        scratch_shapes=list(scratch),
        compiler_params=_cparams("arbitrary", vmem_mb=vmem_mb),
        name=name,
    )(*row_ins, *tab_ins, *const_ins)


def _to_seq(a, bn, l, time_major):
    n = a.shape[1]
    return a.reshape(l, bn, n).transpose(1, 0, 2) if time_major else a.reshape(bn, l, n)


def _from_seq(a, bn, l, time_major):
    a = a.transpose(1, 0, 2) if time_major else a
    return a.reshape(bn * l, a.shape[2])


def _seq_group(bn, chunk, full_chunk, state_bytes):
    if chunk == full_chunk:
        return min(bn, LONG_CHUNK_GROUP)
    grp = max(1, min(bn, SEQ_GROUP_BYTES // state_bytes))
    while bn % grp:
        grp -= 1
    return grp


def _seq_chunking(l, chunk):
    if l % chunk == 0:
        return chunk
    assert l < chunk
    return l


def _rotary_tables(pos0, l):
    half = RET_DK // 2
    freq = 1.0 / (ROPE_BASE ** jnp.linspace(0.0, 1.0, half, dtype=F32))
    pos = jnp.arange(pos0, pos0 + l, dtype=jnp.int32).astype(F32)
    ang = pos[:, None] * freq[None, :]
    return jnp.cos(ang), jnp.sin(ang)


def _ret_proj_stage(x, bn, l, time_major, pos0, g0, wq, wk, wv):
    rows = x.shape[0]
    ptile = min(PROJ_ROW_TILE, rows)
    cos, sin = _rotary_tables(pos0, l)
    if time_major:
        cos, sin = jnp.repeat(cos, bn, axis=0), jnp.repeat(sin, bn, axis=0)
    elif l < ptile:
        cos, sin = jnp.tile(cos, (ptile // l, 1)), jnp.tile(sin, (ptile // l, 1))
    q, k, v = _row_grid_call(
        _ret_proj_kernel, [x], [g0.reshape(1, -1), wq.astype(BF16), wk.astype(BF16), wv.astype(BF16)],
        [(RET_HEADS * RET_DK, BF16), (RET_HEADS * RET_DK, BF16), (RET_HEADS * RET_DV, BF16)],
        rows, ptile, name="ret_proj", tab_ins=[cos, sin], vmem_mb=LARGE_VMEM_MB)
    return tuple(_to_seq(a, bn, l, time_major) for a in (q, k, v))


def _ret_chunk_stage(q3, k3, v3, s0):
    bn, l, _ = q3.shape
    chunk = _seq_chunking(l, RET_CHUNK)
    grp = _seq_group(bn, chunk, RET_CHUNK, RET_HEADS * RET_DK * RET_DV * 4)
    return pl.pallas_call(
        functools.partial(_ret_chunk_kernel, chunk=chunk, group=grp),
        grid=(bn // grp, l // chunk),
        in_specs=[
            pl.BlockSpec((grp, chunk, RET_HEADS * RET_DK), lambda b, i: (b, i, 0)),
            pl.BlockSpec((grp, chunk, RET_HEADS * RET_DK), lambda b, i: (b, i, 0)),
            pl.BlockSpec((grp, chunk, RET_HEADS * RET_DV), lambda b, i: (b, i, 0)),
            pl.BlockSpec((grp, RET_HEADS, RET_DK, RET_DV), lambda b, i: (b, 0, 0, 0)),
        ],
        out_specs=[
            pl.BlockSpec((grp, chunk, RET_HEADS * RET_DV), lambda b, i: (b, i, 0)),
            pl.BlockSpec((grp, RET_HEADS, RET_DK, RET_DV), lambda b, i: (b, 0, 0, 0)),
        ],
        out_shape=[jax.ShapeDtypeStruct((bn, l, RET_HEADS * RET_DV), F32),
                   jax.ShapeDtypeStruct((bn, RET_HEADS, RET_DK, RET_DV), F32)],
        compiler_params=_cparams("arbitrary", "arbitrary"),
        name="ret_chunk",
    )(q3, k3, v3, s0)


def _ret_mixer_stage(x, bn, l, pos0, s0, g0, g1, wq, wk, wv, wg, wo):
    grp = _seq_group(bn, RET_CHUNK, RET_CHUNK, 0)
    cos, sin = _rotary_tables(pos0, l)
    consts = [g0.reshape(1, -1), g1.reshape(1, -1)] + [a.astype(BF16) for a in (wq, wk, wv, wg, wo)]
    seq_spec = pl.BlockSpec((grp, RET_CHUNK, D_MODEL), lambda b, i: (b, i, 0))
    tab_spec = pl.BlockSpec((RET_CHUNK, RET_DK // 2), lambda b, i: (i, 0))
    state_spec = pl.BlockSpec((grp, RET_HEADS, RET_DK, RET_DV), lambda b, i: (b, 0, 0, 0))
    states = [] if s0 is None else [s0]
    y, s_new = pl.pallas_call(
        functools.partial(_ret_mixer_kernel, chunk=RET_CHUNK, group=grp, zero_state=s0 is None),
        grid=(bn // grp, l // RET_CHUNK),
        in_specs=[seq_spec, tab_spec, tab_spec] + [state_spec] * len(states) + [_const_spec(a.shape) for a in consts],
        out_specs=[seq_spec, state_spec],
        out_shape=[jax.ShapeDtypeStruct((bn, l, D_MODEL), F32),
                   jax.ShapeDtypeStruct((bn, RET_HEADS, RET_DK, RET_DV), F32)],
        scratch_shapes=[pltpu.VMEM((grp, RET_CHUNK, RET_HEADS * RET_DK), BF16),
                        pltpu.VMEM((grp, RET_CHUNK, RET_HEADS * RET_DK), BF16),
                        pltpu.VMEM((grp, RET_CHUNK, RET_HEADS * RET_DV), BF16),
                        pltpu.VMEM((grp, RET_CHUNK, RET_HEADS * RET_DV), F32),
                        pltpu.VMEM((grp * RET_CHUNK, RET_HEADS * RET_DV), BF16)],
        compiler_params=_cparams("arbitrary", "arbitrary", vmem_mb=LARGE_VMEM_MB),
        name="ret_mixer",
    )(x.reshape(bn, l, D_MODEL), cos, sin, *states, *consts)
    return y.reshape(bn * l, D_MODEL), s_new


def _ret_out_stage(x, o3, time_major, g0, g1, wg, wo):
    rows = x.shape[0]
    tile = min(ROW_TILE, rows)
    bn, l, _ = o3.shape
    o = _from_seq(o3, bn, l, time_major)
    (y,) = _row_grid_call(
        _ret_out_kernel, [x, o], [g0.reshape(1, -1), g1.reshape(1, -1), wg.astype(BF16), wo.astype(BF16)],
        [(D_MODEL, F32)], rows, tile, scratch=[pltpu.VMEM((tile, RET_HEADS * RET_DV), BF16)], name="ret_out")
    return y


def _gla_project(h, wq_ref, wk_ref, wv_ref, wa1_ref, wa2_ref, ba_ref):
    q = (_dot(h, wq_ref[...]) * (GLA_DK ** -0.5)).astype(BF16)
    k = _dot(h, wk_ref[...]).astype(BF16)
    v = _dot(h, wv_ref[...]).astype(BF16)
    low = _dot(h, wa1_ref[...]).astype(BF16)
    logit = _dot(low, wa2_ref[...]) + ba_ref[...]
    return q, k, v, _log_sigmoid(logit) / GLA_GATE_NORM


def _gla_proj_kernel(x_ref, g_ref, wq_ref, wk_ref, wv_ref, wa1_ref, wa2_ref, ba_ref,
                     q_ref, k_ref, v_ref, la_ref):
    h = _rms(x_ref[...], g_ref[...]).astype(BF16)
    q_ref[...], k_ref[...], v_ref[...], la_ref[...] = _gla_project(h, wq_ref, wk_ref, wv_ref, wa1_ref, wa2_ref,
                                                                 ba_ref)


def _gla_unit_stages(q_ref, k_ref, v_ref, la_ref, o_ref, st_sc, chunk, group):
    ri = lax.broadcasted_iota(jnp.int32, (chunk, chunk), 0)
    ci = lax.broadcasted_iota(jnp.int32, (chunk, chunk), 1)
    causal = ri >= ci
    tri = causal.astype(BF16)
    nk = GLA_HEADS * GLA_DK
    units = [(g, hd) for g in range(group) for hd in range(GLA_HEADS)]
    bcs = []
    for g in range(group):
        la = la_ref[g]
        p1 = la.astype(BF16)
        r1 = la - p1.astype(F32)
        p2 = r1.astype(BF16)
        p3 = (r1 - p2.astype(F32)).astype(BF16)
        acc = _dot(tri, jnp.concatenate([p1, p2, p3], axis=1))
        bcs.append(acc[:, :nk] + acc[:, nk:2 * nk] + acc[:, 2 * nk:])
    prep = []
    for g, hd in units:
        ksl = slice(GLA_DK * hd, GLA_DK * (hd + 1))
        bc = bcs[g][:, ksl]
        blast = bc[chunk - 1:chunk, :]
        ref = 0.5 * (bc[0:1, :] + blast)
        qh = q_ref[g, :, ksl].astype(F32)
        kh = k_ref[g, :, ksl].astype(F32)
        qt = (qh * jnp.exp(bc - ref)).astype(BF16)
        kt = (kh * jnp.exp(ref - bc)).astype(BF16)
        qg = (qh * jnp.exp(bc)).astype(BF16)
        kg = (kh * jnp.exp(blast - bc)).astype(BF16)
        row_decay = jnp.broadcast_to(jnp.exp(blast), (SUBLANES, GLA_DK)).T[:, :1]
        prep.append((qt, kt, qg, kg, row_decay))
    yield
    scores = [jnp.where(causal, _dot_nt(qt, kt), 0.0).astype(BF16) for qt, kt, _, _, _ in prep]
    for (g, hd), sc, (_, _, qg, kg, row_decay) in zip(units, scores, prep):
        vsl = slice(GLA_DV * hd, GLA_DV * (hd + 1))
        vh = v_ref[g, :, vsl]
        st = st_sc[g, hd]
        o_ref[g, :, vsl] = _dot(sc, vh) + _dot(qg, st.astype(BF16))
        st_sc[g, hd] = st * row_decay + _dot_tn(kg, vh)


def _gla_copy_states(src, dst, group):
    for g in range(group):
        for hd in range(GLA_HEADS):
            dst[g, hd] = src[g, hd]


def _gla_chunk_kernel(q_ref, k_ref, v_ref, la_ref, s0_ref, o_ref, sout_ref, st_sc, *, chunk, group):
    @pl.when(pl.program_id(1) == 0)
    def _():
        _gla_copy_states(s0_ref, st_sc, group)

    for _ in _gla_unit_stages(q_ref, k_ref, v_ref, la_ref, o_ref, st_sc, chunk, group):
        pass

    @pl.when(pl.program_id(1) == pl.num_programs(1) - 1)
    def _():
        _gla_copy_states(st_sc, sout_ref, group)


def _gla_head_norm(oh, norm_g, gate):
    on = oh * lax.rsqrt(jnp.mean(oh * oh, axis=-1, keepdims=True) + NORM_EPS) * norm_g
    return (on * gate).astype(BF16)


def _gla_mixer_kernel(x_ref, *rest, chunk, group, zero_state):
    s0_ref, rest = (None, rest) if zero_state else (rest[0], rest[1:])
    (g0_ref, wq_ref, wk_ref, wv_ref, wa1_ref, wa2_ref, ba_ref, g1_ref, ng_ref, wg_ref, wo_ref, y_ref, sout_ref,
     st_sc, q_sc, k_sc, v_sc, la_sc, o_sc, gate_sc, on_sc) = rest

    @pl.when(pl.program_id(1) == 0)
    def _():
        if zero_state:
            st_sc[...] = jnp.zeros(st_sc.shape, F32)
        else:
            _gla_copy_states(s0_ref, st_sc, group)

    n = group * chunk
    x = x_ref[...].reshape(n, D_MODEL)
    h = _rms(x, g0_ref[...]).astype(BF16)
    for dst, val in zip((q_sc, k_sc, v_sc, la_sc), _gla_project(h, wq_ref, wk_ref, wv_ref, wa1_ref, wa2_ref, ba_ref)):
        dst[...] = val.reshape(dst.shape)
    stages = _gla_unit_stages(q_sc, k_sc, v_sc, la_sc, o_sc, st_sc, chunk, group)
    next(stages)
    gate_sc[...] = _silu(_dot(h, wg_ref[...]))
    next(stages, None)
    for hd in range(GLA_HEADS):
        sl = slice(GLA_DV * hd, GLA_DV * (hd + 1))
        on_sc[:, sl] = _gla_head_norm(o_sc[:, :, sl].reshape(n, GLA_DV), ng_ref[...], gate_sc[:, sl])
    y = x + _rms(_dot(on_sc[...], wo_ref[...]), g1_ref[...])
    y_ref[...] = y.reshape(group, chunk, D_MODEL)

    @pl.when(pl.program_id(1) == pl.num_programs(1) - 1)
    def _():
        _gla_copy_states(st_sc, sout_ref, group)


def _gla_out_kernel(x_ref, o_ref, g0_ref, g1_ref, ng_ref, wg_ref, wo_ref, y_ref, on_sc):
    x = x_ref[...]
    h = _rms(x, g0_ref[...]).astype(BF16)
    gate = _silu(_dot(h, wg_ref[...]))
    for hd in range(GLA_HEADS):
        sl = slice(GLA_DV * hd, GLA_DV * (hd + 1))
        on_sc[:, sl] = _gla_head_norm(o_ref[:, sl], ng_ref[...], gate[:, sl])
    y_ref[...] = x + _rms(_dot(on_sc[...], wo_ref[...]), g1_ref[...])


def _gla_layer(x, bn, l, time_major, s0, g0, g1, wq, wk, wv, wg, wa1, wa2, ba, norm_g, wo):
    rows = x.shape[0]
    tile = min(PROJ_ROW_TILE, rows)
    lanes = 128
    wa1p = jnp.pad(wa1, ((0, 0), (0, lanes - GLA_GATE_RANK))).astype(BF16)
    wa2p = jnp.pad(wa2, ((0, lanes - GLA_GATE_RANK), (0, 0))).astype(BF16)
    proj_consts = [g0.reshape(1, -1), wq.astype(BF16), wk.astype(BF16), wv.astype(BF16), wa1p, wa2p,
                   ba.reshape(1, -1)]
    out_consts = [g0.reshape(1, -1), g1.reshape(1, -1), norm_g.reshape(1, -1), wg.astype(BF16), wo.astype(BF16)]
    chunk = _seq_chunking(l, GLA_CHUNK)
    nc = l // chunk
    grp = GLA_GROUP if bn % GLA_GROUP == 0 else 1
    nk, nv = GLA_HEADS * GLA_DK, GLA_HEADS * GLA_DV
    if not time_major and chunk == GLA_CHUNK:
        consts = proj_consts + out_consts[1:]
        seq_spec = pl.BlockSpec((grp, chunk, D_MODEL), lambda b, i: (b, i, 0))
        state_spec = pl.BlockSpec((grp, GLA_HEADS, GLA_DK, GLA_DV), lambda b, i: (b, 0, 0, 0))
        states = [] if s0 is None else [s0]
        y, s_new = pl.pallas_call(
            functools.partial(_gla_mixer_kernel, chunk=chunk, group=grp, zero_state=s0 is None),
            grid=(bn // grp, nc),
            in_specs=[seq_spec] + [state_spec] * len(states) + [_const_spec(a.shape) for a in consts],
            out_specs=[seq_spec, state_spec],
            out_shape=[jax.ShapeDtypeStruct((bn, l, D_MODEL), F32),
                       jax.ShapeDtypeStruct((bn, GLA_HEADS, GLA_DK, GLA_DV), F32)],
            scratch_shapes=[pltpu.VMEM((grp, GLA_HEADS, GLA_DK, GLA_DV), F32),
                            pltpu.VMEM((grp, chunk, nk), BF16), pltpu.VMEM((grp, chunk, nk), BF16),
                            pltpu.VMEM((grp, chunk, nv), BF16), pltpu.VMEM((grp, chunk, nk), F32),
                            pltpu.VMEM((grp, chunk, nv), F32),
                            pltpu.VMEM((grp * chunk, nv), F32),
                            pltpu.VMEM((grp * chunk, nv), BF16)],
            compiler_params=_cparams("arbitrary", "arbitrary"),
            name="gla_mixer",
        )(x.reshape(bn, l, D_MODEL), *states, *consts)
        return y.reshape(rows, D_MODEL), s_new
    if s0 is None:
        s0 = jnp.zeros((bn, GLA_HEADS, GLA_DK, GLA_DV), F32)
    q, k, v, la = _row_grid_call(
        _gla_proj_kernel, [x], proj_consts, [(nk, BF16), (nk, BF16), (nv, BF16), (nk, F32)],
        rows, tile, name="gla_proj", vmem_mb=LARGE_VMEM_MB)
    q3, k3, v3, la3 = (_to_seq(a, bn, l, time_major) for a in (q, k, v, la))
    o3, s_new = pl.pallas_call(
        functools.partial(_gla_chunk_kernel, chunk=chunk, group=grp),
        grid=(bn // grp, nc),
        in_specs=[
            pl.BlockSpec((grp, chunk, GLA_HEADS * GLA_DK), lambda b, i: (b, i, 0)),
            pl.BlockSpec((grp, chunk, GLA_HEADS * GLA_DK), lambda b, i: (b, i, 0)),
            pl.BlockSpec((grp, chunk, GLA_HEADS * GLA_DV), lambda b, i: (b, i, 0)),
            pl.BlockSpec((grp, chunk, GLA_HEADS * GLA_DK), lambda b, i: (b, i, 0)),
            pl.BlockSpec((grp, GLA_HEADS, GLA_DK, GLA_DV), lambda b, i: (b, 0, 0, 0)),
        ],
        out_specs=[
            pl.BlockSpec((grp, chunk, GLA_HEADS * GLA_DV), lambda b, i: (b, i, 0)),
            pl.BlockSpec((grp, GLA_HEADS, GLA_DK, GLA_DV), lambda b, i: (b, 0, 0, 0)),
        ],
        out_shape=[jax.ShapeDtypeStruct((bn, l, GLA_HEADS * GLA_DV), F32),
                   jax.ShapeDtypeStruct((bn, GLA_HEADS, GLA_DK, GLA_DV), F32)],
        scratch_shapes=[pltpu.VMEM((grp, GLA_HEADS, GLA_DK, GLA_DV), F32)],
        compiler_params=_cparams("arbitrary", "arbitrary"),
        name="gla_chunk",
    )(q3, k3, v3, la3, s0)
    o = _from_seq(o3, bn, l, time_major)
    (y,) = _row_grid_call(
        _gla_out_kernel, [x, o], out_consts,
        [(D_MODEL, F32)], rows, tile, scratch=[pltpu.VMEM((tile, GLA_HEADS * GLA_DV), BF16)], name="gla_out",
        vmem_mb=LARGE_VMEM_MB)
    return y, s_new


def _transpose_rows(x, a, b):
    return x.reshape(a, b, x.shape[1]).transpose(1, 0, 2).reshape(a * b, x.shape[1])


def _trunk(x, pos0, s5_re, s5_im, ret_s, gla_s, conv_buf, p, batch_major_mixers, bf16_w, cast_weights,
           defer_ret=False, host=None):
    bf16_w = dict(bf16_w)

    def w(name, idx):
        return bf16_w.get((name, idx), p[name][idx])
    bn, l, _ = x.shape
    depth = p['norm_g'].shape[0]
    assert not batch_major_mixers or bn == SUBLANES
    rows = x.reshape(bn * l, D_MODEL)
    time_major = False
    if not batch_major_mixers:
        rows, time_major = _transpose_rows(rows, bn, l), True
    new_re, new_im, new_ret, new_gla, new_conv = [], [], [], [], []
    hosted_o, hosted_s, hosted_n = [], None, 0
    for layer in range(depth):
        g = p['norm_g'][layer]
        kind = layer % N_MIXERS
        j = layer // N_MIXERS
        if kind == 0:
            todo_host = None
            if host is not None and hosted_n < host[0].shape[0]:
                todo_host = tuple(host) + (hosted_n, hosted_s)
            rows, hr, hi, hres = _s5_layer(rows, bn, s5_re[j], s5_im[j], g[0], g[1],
                                           p['s5_lambda_re'][j], p['s5_lambda_im'][j], p['s5_b_re'][j],
                                           p['s5_b_im'][j], p['s5_c_re'][j], p['s5_c_im'][j], p['s5_log_dt'][j],
                                           p['s5_d'][j], p['s5_w_glu'], j, x_bm=not time_major, host=todo_host)
            if hres:
                hosted_o.append(hres[0])
                hosted_s = hres[1]
                hosted_n += hres[0].shape[0]
            time_major = True
            new_re.append(hr)
            new_im.append(hi)
        elif kind == 1:
            s0 = None if ret_s is None else ret_s[j]
            if not defer_ret and not time_major and l % RET_CHUNK == 0:
                rows, s = _ret_mixer_stage(rows, bn, l, pos0, s0, g[0], g[1],
                                           *(w(n, j) for n in MIXER_WEIGHTS[kind]))
            else:
                if s0 is None:
                    s0 = jnp.zeros((bn, RET_HEADS, RET_DK, RET_DV), F32)
                q3, k3, v3 = _ret_proj_stage(rows, bn, l, time_major, pos0, g[0], w('ret_wq', j), w('ret_wk', j),
                                             w('ret_wv', j))
                if defer_ret:
                    o3, s, more = yield q3, k3, v3, s0
                    bf16_w.update(more)
                else:
                    o3, s = _ret_chunk_stage(q3, k3, v3, s0)
                rows = _ret_out_stage(rows, o3, time_major, g[0], g[1], w('ret_wg', j), w('ret_wo', j))
            new_ret.append(s)
        else:
            rows, s = _gla_layer(rows, bn, l, time_major, None if gla_s is None else gla_s[j], g[0], g[1],
                                 w('gla_wq', j), w('gla_wk', j),
                                 w('gla_wv', j), w('gla_wg', j), p['gla_wa1'][j], p['gla_wa2'][j], p['gla_ba'][j],
                                 p['gla_norm_g'][j], w('gla_wo', j))
            new_gla.append(s)
        next_is_s5 = layer + 1 < depth and (layer + 1) % N_MIXERS == 0
        out_bm = batch_major_mixers and not next_is_s5
        cin = conv_buf[layer].transpose(1, 0, 2).reshape((CONV_W - 1) * bn, D_FF)
        todo = []
        if cast_weights and layer + 1 < depth:
            mixer = MIXER_WEIGHTS[(layer + 1) % N_MIXERS]
            todo = [(n, layer + 1) for n in FFN_WEIGHTS] + [(n, (layer + 1) // N_MIXERS) for n in mixer]
            todo = [key for key in todo if key not in bf16_w]
        rows, cout, cast = _ffn_layer(rows, bn, cin, g[2], g[3], w('ffn_w_gate', layer), w('ffn_w_up', layer),
                                      p['ffn_conv_w'][layer], p['ffn_conv_b'][layer], w('ffn_w_down', layer),
                                      x_bm=not time_major, out_bm=out_bm,
                                      next_weights=[(p[n], idx) for n, idx in todo])
        bf16_w.update(zip(todo, cast))
        time_major = not out_bm
        new_conv.append(cout.reshape(CONV_W - 1, bn, D_FF).transpose(1, 0, 2))
    if time_major:
        rows = _transpose_rows(rows, l, bn)
    y = rows.reshape(bn, l, D_MODEL)
    hosted = None
    if host is not None:
        assert hosted_n == host[0].shape[0]
        hosted = (jnp.concatenate(hosted_o), hosted_s)
    outs = (y, jnp.stack(new_re), jnp.stack(new_im), jnp.stack(new_ret), jnp.stack(new_gla),
            jnp.stack(new_conv))
    return outs, bf16_w, hosted


def _finish(gen, reply=None):
    try:
        gen.send(reply)
    except StopIteration as done:
        return done.value
    raise AssertionError("unexpected request")


def kernel(x_prompt, x_sample, state_s5_re, state_s5_im, state_ret, state_gla, cache_ffn_conv,
           norm_g, s5_lambda_re, s5_lambda_im, s5_b_re, s5_b_im, s5_c_re, s5_c_im, s5_log_dt, s5_d, s5_w_glu,
           ret_wq, ret_wk, ret_wv, ret_wg, ret_wo,
           gla_wq, gla_wk, gla_wv, gla_wg, gla_wa1, gla_wa2, gla_ba, gla_norm_g, gla_wo,
           ffn_w_gate, ffn_w_up, ffn_conv_w, ffn_conv_b, ffn_w_down):
    p = dict(norm_g=norm_g, s5_lambda_re=s5_lambda_re, s5_lambda_im=s5_lambda_im, s5_b_re=s5_b_re,
             s5_b_im=s5_b_im, s5_c_re=s5_c_re, s5_c_im=s5_c_im, s5_log_dt=s5_log_dt, s5_d=s5_d,
             s5_w_glu=s5_w_glu.astype(BF16), ret_wq=ret_wq, ret_wk=ret_wk, ret_wv=ret_wv, ret_wg=ret_wg, ret_wo=ret_wo,
             gla_wq=gla_wq, gla_wk=gla_wk, gla_wv=gla_wv, gla_wg=gla_wg, gla_wa1=gla_wa1, gla_wa2=gla_wa2,
             gla_ba=gla_ba, gla_norm_g=gla_norm_g, gla_wo=gla_wo, ffn_w_gate=ffn_w_gate, ffn_w_up=ffn_w_up,
             ffn_conv_w=ffn_conv_w, ffn_conv_b=ffn_conv_b, ffn_w_down=ffn_w_down)
    bp = x_prompt.shape[0]
    z_s5 = jnp.zeros((state_s5_re.shape[0], bp) + state_s5_re.shape[2:], F32)
    z_conv = jnp.zeros((cache_ffn_conv.shape[0], bp) + cache_ffn_conv.shape[2:], x_prompt.dtype)
    first = {(n, 0): p[n][0].astype(BF16) for n in FFN_WEIGHTS}
    first.update({(n, j): p[n][j].astype(BF16) for n in ('ret_wq', 'ret_wk', 'ret_wv') for j in range(ret_wq.shape[0])})
    sample = _trunk(x_sample, PAST_LEN, state_s5_re, state_s5_im, state_ret, state_gla, cache_ffn_conv, p,
                    False, first, False, defer_ret=True)
    request = next(sample)
    outs_p, bf16_w, hosted = _finish(_trunk(x_prompt, 0, z_s5, z_s5, None, None, z_conv, p, True, first, True,
                                            host=request))
    outs_s, _, _ = _finish(sample, hosted + (bf16_w,))
    return (outs_p[0], outs_s[0]) + tuple(outs_p[1:]) + tuple(outs_s[1:])
```

```python
import functools
import math

import jax
import jax.numpy as jnp
from jax import lax
from jax.experimental import pallas as pl
from jax.experimental.pallas import tpu as pltpu

F32 = jnp.float32
BF16 = jnp.bfloat16

D_MODEL = 1024
PAST_LEN = 16384
N_MIXERS = 3
FFN_WEIGHTS = ('ffn_w_gate', 'ffn_w_up', 'ffn_w_down')
MIXER_WEIGHTS = ((), ('ret_wq', 'ret_wk', 'ret_wv', 'ret_wg', 'ret_wo'),
                 ('gla_wq', 'gla_wk', 'gla_wv', 'gla_wg', 'gla_wo'))
S5_GROUP = 16
S5_GROUPS = D_MODEL // S5_GROUP
S5_STATE = 64
S5_SUPER = 4
S5_SG_CH = D_MODEL // S5_SUPER
S5_SG_ST = (S5_GROUPS // S5_SUPER) * S5_STATE
RET_HEADS = 4
RET_DK = D_MODEL // RET_HEADS
RET_DV = 2 * D_MODEL // RET_HEADS
RET_CHUNK = 256
ROPE_BASE = 10000.0
GLA_HEADS = 4
GLA_DK = D_MODEL // 2 // GLA_HEADS
GLA_DV = D_MODEL // GLA_HEADS
GLA_GATE_RANK = 16
GLA_GATE_NORM = 16.0
GLA_CHUNK = 64
GLA_GROUP = 8
D_FF = ((8 * D_MODEL // 3 + 255) // 256) * 256
FFN_SUB = 256
FFN_ROW_TILE = 1024
S5_BU_BUFFERS = 4
N_S5_INPUTS = 12
HOSTED_SEQS = 2
LONG_CHUNK_GROUP = 2
SEQ_GROUP_BYTES = 8 * 2 ** 20
CONV_W = 3
NORM_EPS = 1e-6
ROW_TILE = 512
PROJ_ROW_TILE = 1024
SUBLANES = 8
V7X_VMEM_MB = 64
VMEM_MB = V7X_VMEM_MB - 12
LARGE_VMEM_MB = V7X_VMEM_MB - 6


def _cparams(*sem, vmem_mb=None):
    return pltpu.CompilerParams(dimension_semantics=sem, vmem_limit_bytes=(vmem_mb or VMEM_MB) * 2 ** 20)


def _const_spec(shape):
    zeros = (0,) * len(shape)
    return pl.BlockSpec(shape, lambda *_: zeros, pipeline_mode=pl.Buffered(1))


def _stacked_spec(shape, index):
    where = (index,) + (0,) * len(shape)
    return pl.BlockSpec((None,) + tuple(shape), lambda *_: where, pipeline_mode=pl.Buffered(1))


def _rms(x, g):
    return x * lax.rsqrt(jnp.mean(x * x, axis=-1, keepdims=True) + NORM_EPS) * g


def _sigmoid(x):
    return 1.0 / (1.0 + jnp.exp(-x))


def _silu(x):
    return x * _sigmoid(x)


def _gelu_tanh(x):
    return 0.5 * x * (1.0 + jnp.tanh(math.sqrt(2.0 / math.pi) * (x + 0.044715 * (x * x * x))))


def _log_sigmoid(x):
    return jnp.minimum(x, 0.0) - jnp.log(1.0 + jnp.exp(-jnp.abs(x)))


def _dot(a, b):
    return jnp.dot(a, b, preferred_element_type=F32)


def _dot_nt(a, b):
    return lax.dot_general(a, b, (((1,), (1,)), ((), ())), preferred_element_type=F32)


def _dot_tn(a, b):
    return lax.dot_general(a, b, (((0,), (0,)), ((), ())), preferred_element_type=F32)


def _s5_kernel(x_ref, *rest, bn, tt, x_bm, hosted, n_host_in, zero_state):
    (h0re_ref, h0im_ref), rest = ((None, None), rest) if zero_state else (rest[:2], rest[2:])
    (g0_ref, g1_ref, are_ref, aim_ref, bre_ref, bim_ref, cc_ref, dsk_ref, wglu_ref), rest = rest[:9], rest[9:]
    host_in, rest = rest[:n_host_in], rest[n_host_in:]
    o_ref, hre_ref, him_ref, rest = rest[0], rest[1], rest[2], rest[3:]
    host_out, (bu_sc, y_sc, bbd_ref, cbd_ref) = rest[:2 if hosted else 0], rest[2 if hosted else 0:]
    @pl.when(pl.program_id(0) == 0)
    def _():
        hre_ref[...] = jnp.zeros(hre_ref.shape, F32) if zero_state else h0re_ref[...]
        him_ref[...] = jnp.zeros(him_ref.shape, F32) if zero_state else h0im_ref[...]
        ngl = S5_GROUPS // S5_SUPER

        def lane_tiler(k, n):
            r = lax.broadcasted_iota(jnp.int32, (k, n), 0)
            c = lax.broadcasted_iota(jnp.int32, (k, n), 1)
            return ((c & (k - 1)) == r).astype(BF16)

        tile_n = lane_tiler(S5_STATE, S5_SG_ST)
        tile_p = lane_tiler(S5_GROUP, S5_SG_CH)
        sh_p = S5_GROUP.bit_length() - 1
        sh_n = S5_STATE.bit_length() - 1
        rb = lax.broadcasted_iota(jnp.int32, (S5_SG_CH, S5_SG_ST), 0) >> sh_p
        cb = lax.broadcasted_iota(jnp.int32, (S5_SG_CH, S5_SG_ST), 1) >> sh_n
        rc = (lax.broadcasted_iota(jnp.int32, (2 * S5_SG_ST, S5_SG_CH), 0) >> sh_n) & (ngl - 1)
        cc = lax.broadcasted_iota(jnp.int32, (2 * S5_SG_ST, S5_SG_CH), 1) >> sh_p
        for sg in range(S5_SUPER):
            for part, ref in ((0, bre_ref), (1, bim_ref)):
                blk = _dot(ref[sg].astype(BF16), tile_n)
                bbd_ref[sg, :, S5_SG_ST * part:S5_SG_ST * (part + 1)] = jnp.where(rb == cb, blk, 0.0).astype(BF16)
            blk = _dot(cc_ref[sg].astype(BF16), tile_p)
            cbd_ref[sg] = jnp.where(rc == cc, blk, 0.0).astype(BF16)

    x = x_ref[...]
    if x_bm:
        x = jnp.swapaxes(x, 0, 1).reshape(tt * bn, D_MODEL)
    u = _rms(x, g0_ref[...])
    ub = u.astype(BF16)
    half = S5_SG_ST // 2
    host_stages = iter(())
    if hosted:
        hq, hk, hv, hs = host_in[:4]
        host_stages = _ret_unit_stages(hq, hk, hv, hs, host_out[0], host_out[1], hq.shape[1], hosted)
    for sg in range(S5_SUPER):
        buf = bu_sc.at[sg % S5_BU_BUFFERS]
        buf[...] = _dot(ub[:, S5_SG_CH * sg:S5_SG_CH * (sg + 1)], bbd_ref[sg])
        for hf in range(2):
            c_re = half * hf
            c_im = S5_SG_ST + half * hf
            a_lo = S5_SG_ST * sg + half * hf
            ar = jnp.broadcast_to(are_ref[:, a_lo:a_lo + half], (8, half))
            ai = jnp.broadcast_to(aim_ref[:, a_lo:a_lo + half], (8, half))
            for bt in range(bn // 8):
                r8 = 8 * bt
                hr = hre_ref[r8:r8 + 8, a_lo:a_lo + half]
                hi = him_ref[r8:r8 + 8, a_lo:a_lo + half]
                for t in range(tt):
                    r0 = t * bn + r8
                    bur = buf[r0:r0 + 8, c_re:c_re + half]
                    bui = buf[r0:r0 + 8, c_im:c_im + half]
                    hr, hi = ar * hr - ai * hi + bur, ar * hi + ai * hr + bui
                    buf[r0:r0 + 8, c_re:c_re + half] = hr
                    buf[r0:r0 + 8, c_im:c_im + half] = hi
                hre_ref[r8:r8 + 8, a_lo:a_lo + half] = hr
                him_ref[r8:r8 + 8, a_lo:a_lo + half] = hi
        y_sc[:, S5_SG_CH * sg:S5_SG_CH * (sg + 1)] = _dot(buf[...].astype(BF16), cbd_ref[sg])
        next(host_stages, None)
    y = y_sc[...] + dsk_ref[...] * u
    z = _dot(_gelu_tanh(y).astype(BF16), wglu_ref[...])
    mix = z[:, :D_MODEL] * _sigmoid(z[:, D_MODEL:])
    o_ref[...] = x + _rms(mix, g1_ref[...])


def _s5_layer(x, bn, h0_re, h0_im, g0, g1, lam_re, lam_im, b_re, b_im, c_re, c_im, log_dt, d_skip, w_glu,
              j, x_bm=False, host=None):
    rows = x.shape[0]
    tt = min(rows // bn, max(1, ROW_TILE // bn))
    tile = tt * bn
    steps = rows // tile
    host_ins, host_in_specs, host_out_specs, host_out_shapes = [], [], [], []
    aliases = {}
    if host is not None:
        hq, hk, hv, hs, first, acc = host
        assert first % HOSTED_SEQS == 0
        blk0 = first // HOSTED_SEQS
        host_ins = [hq, hk, hv, hs]
        for a in host_ins:
            host_in_specs.append(pl.BlockSpec((HOSTED_SEQS,) + a.shape[1:],
                                              lambda i, nd=a.ndim: (blk0 + i,) + (0,) * (nd - 1)))
        host_out_specs = [pl.BlockSpec((HOSTED_SEQS,) + hv.shape[1:], lambda i: (i, 0, 0)),
                          pl.BlockSpec((HOSTED_SEQS,) + hs.shape[1:], lambda i: (blk0 + i, 0, 0, 0))]
        host_out_shapes = [jax.ShapeDtypeStruct((steps * HOSTED_SEQS,) + hv.shape[1:], F32),
                           jax.ShapeDtypeStruct(hs.shape, F32)]
        if acc is not None:
            host_ins.append(acc)
            host_in_specs.append(pl.BlockSpec(memory_space=pl.ANY))
            aliases = {N_S5_INPUTS - (2 if h0_re is None else 0) + 4: 4}
    o_spec = pl.BlockSpec((tile, D_MODEL), lambda i: (i, 0))
    x_spec = o_spec
    if x_bm:
        assert bn == SUBLANES
        x = x.reshape(bn, rows // bn, D_MODEL)
        x_spec = pl.BlockSpec((bn, tt, D_MODEL), lambda i: (0, i, 0))
    lam = lax.complex(lam_re, lam_im)
    dt = jnp.exp(log_dt)[:, None]
    a_bar = jnp.exp(lam * dt)
    b_bar = ((a_bar - 1.0) / lam)[..., None] * lax.complex(b_re, b_im)
    bt = b_bar.transpose(0, 2, 1).reshape(S5_SUPER, S5_SG_CH, S5_STATE)
    bt_re, bt_im = jnp.real(bt), jnp.imag(bt)
    ct_re = c_re.transpose(0, 2, 1).reshape(S5_SUPER, S5_SG_ST, S5_GROUP)
    ct_im = c_im.transpose(0, 2, 1).reshape(S5_SUPER, S5_SG_ST, S5_GROUP)
    ct = jnp.concatenate([ct_re, -ct_im], axis=1)
    a_re = jnp.real(a_bar).reshape(1, S5_GROUPS * S5_STATE)
    a_im = jnp.imag(a_bar).reshape(1, S5_GROUPS * S5_STATE)
    nst = S5_GROUPS * S5_STATE
    states = [] if h0_re is None else [h0_re.reshape(bn, nst), h0_im.reshape(bn, nst)]

    out, new_re, new_im, *hosted = pl.pallas_call(
        functools.partial(_s5_kernel, bn=bn, tt=tt, x_bm=x_bm, hosted=HOSTED_SEQS if host_ins else 0,
                          n_host_in=len(host_ins), zero_state=not states),
        grid=(steps,),
        in_specs=[x_spec] + [_const_spec((bn, nst))] * len(states) + [
            _const_spec((1, D_MODEL)), _const_spec((1, D_MODEL)),
            _const_spec((1, nst)), _const_spec((1, nst)),
            _const_spec((S5_SUPER, S5_SG_CH, S5_STATE)), _const_spec((S5_SUPER, S5_SG_CH, S5_STATE)),
            _const_spec((S5_SUPER, 2 * S5_SG_ST, S5_GROUP)),
            _const_spec((1, D_MODEL)),
            _stacked_spec((D_MODEL, 2 * D_MODEL), j),
        ] + host_in_specs,
        out_specs=[o_spec, _const_spec((bn, nst)), _const_spec((bn, nst))] + host_out_specs,
        out_shape=[jax.ShapeDtypeStruct((rows, D_MODEL), F32), jax.ShapeDtypeStruct((bn, nst), F32),
                   jax.ShapeDtypeStruct((bn, nst), F32)] + host_out_shapes,
        scratch_shapes=[pltpu.VMEM((S5_BU_BUFFERS, tile, 2 * S5_SG_ST), F32), pltpu.VMEM((tile, D_MODEL), F32),
                        pltpu.VMEM((S5_SUPER, S5_SG_CH, 2 * S5_SG_ST), BF16),
                        pltpu.VMEM((S5_SUPER, 2 * S5_SG_ST, S5_SG_CH), BF16)],
        input_output_aliases=aliases,
        compiler_params=_cparams("arbitrary", vmem_mb=LARGE_VMEM_MB if host_ins else None),
        name="s5_layer",
    )(x, *states, g0.reshape(1, -1), g1.reshape(1, -1), a_re, a_im,
      bt_re, bt_im, ct, d_skip.reshape(1, -1), w_glu, *host_ins)
    return out, new_re.reshape(bn, S5_GROUPS, S5_STATE), new_im.reshape(bn, S5_GROUPS, S5_STATE), tuple(hosted)


def _ffn_kernel(x_ref, cin_ref, g2_ref, g3_ref, wg_ref, wu_ref, wd_ref, cw_ref, cb_ref, *rest,
                bn, tile, x_bm, out_bm, cast_next):
    next_f32, rest = rest[:cast_next], rest[cast_next:]
    o_ref, cout_ref, rest = rest[0], rest[1], rest[2:]
    next_bf16, (hdn_sc, gext_sc, carry_sc) = rest[:cast_next], rest[cast_next:]
    i = pl.program_id(0)
    halo = 2 * bn
    base = gext_sc.shape[1] - tile

    @pl.when(i == 0)
    def _():
        carry_sc[...] = cin_ref[...]

    x = x_ref[...]
    if x_bm:
        x = jnp.swapaxes(x, 0, 1).reshape(tile, D_MODEL)
    h = _rms(x, g2_ref[...]).astype(BF16)
    for j in range(D_FF // FFN_SUB):
        sl = slice(FFN_SUB * j, FFN_SUB * (j + 1))
        buf = gext_sc.at[j % 2]
        gpre = _dot(h, wg_ref[:, sl])
        buf[base - halo:base, :] = carry_sc[:, sl]
        buf[base:base + tile, :] = gpre
        carry_sc[:, sl] = gpre[tile - halo:tile, :]
        gconv = cb_ref[:, sl] + cw_ref[0:1, sl] * buf[base - halo:base - halo + tile, :]
        gconv = gconv + cw_ref[1:2, sl] * buf[base - bn:base - bn + tile, :]
        gconv = gconv + cw_ref[2:3, sl] * gpre
        hdn_sc[:, sl] = (_silu(gconv) * _dot(h, wu_ref[:, sl])).astype(BF16)
        if j == 0:
            for src, dst in zip(next_f32, next_bf16):
                dst[...] = src[...].astype(BF16)
    y = x + _rms(_dot(hdn_sc[...], wd_ref[...]), g3_ref[...])
    if out_bm:
        y = jnp.swapaxes(y.reshape(tile // bn, bn, D_MODEL), 0, 1)
    o_ref[...] = y

    @pl.when(i == pl.num_programs(0) - 1)
    def _():
        cout_ref[...] = carry_sc[...]


def _ffn_layer(x, bn, cin, g2, g3, w_gate, w_up, conv_w, conv_b, w_down, x_bm=False, out_bm=False,
               next_weights=()):
    rows = x.shape[0]
    tile = min(FFN_ROW_TILE, rows)
    tt = tile // bn
    halo = 2 * bn
    steps = rows // tile
    cast_ins, cast_in_specs, cast_out_specs, cast_out_shapes = [], [], [], []
    assert w_gate.dtype == w_up.dtype == w_down.dtype == BF16
    for w, idx in next_weights:
        k, n = w.shape[1:]
        kb = k // steps
        assert kb * steps == k and kb % (2 * SUBLANES) == 0
        cast_ins.append(w)
        cast_in_specs.append(pl.BlockSpec((None, kb, n), lambda i, idx=idx: (idx, i, 0)))
        cast_out_specs.append(pl.BlockSpec((kb, n), lambda i: (i, 0)))
        cast_out_shapes.append(jax.ShapeDtypeStruct((k, n), BF16))
    assert not (x_bm or out_bm) or bn == SUBLANES
    tm_spec = pl.BlockSpec((tile, D_MODEL), lambda i: (i, 0))
    bm_spec = pl.BlockSpec((bn, tt, D_MODEL), lambda i: (0, i, 0))
    if x_bm:
        x = x.reshape(bn, rows // bn, D_MODEL)
    out_shape = (bn, rows // bn, D_MODEL) if out_bm else (rows, D_MODEL)
    out, cout, *cast = pl.pallas_call(
        functools.partial(_ffn_kernel, bn=bn, tile=tile, x_bm=x_bm, out_bm=out_bm, cast_next=len(cast_ins)),
        grid=(steps,),
        in_specs=[
            bm_spec if x_bm else tm_spec,
            _const_spec((halo, D_FF)),
            _const_spec((1, D_MODEL)), _const_spec((1, D_MODEL)),
            _const_spec((D_MODEL, D_FF)), _const_spec((D_MODEL, D_FF)), _const_spec((D_FF, D_MODEL)),
            _const_spec((CONV_W, D_FF)), _const_spec((1, D_FF)),
        ] + cast_in_specs,
        out_specs=[bm_spec if out_bm else tm_spec, _const_spec((halo, D_FF))] + cast_out_specs,
        out_shape=[jax.ShapeDtypeStruct(out_shape, F32), jax.ShapeDtypeStruct((halo, D_FF), F32)] + cast_out_shapes,
        scratch_shapes=[pltpu.VMEM((tile, D_FF), BF16),
                        pltpu.VMEM((2, halo + tile, FFN_SUB), F32),
                        pltpu.VMEM((halo, D_FF), F32)],
        compiler_params=_cparams("arbitrary", vmem_mb=LARGE_VMEM_MB),
        name="conv_ffn",
    )(x, cin, g2.reshape(1, -1), g3.reshape(1, -1), w_gate, w_up, w_down, conv_w, conv_b.reshape(1, -1), *cast_ins)
    return out.reshape(rows, D_MODEL), cout, tuple(cast)


def _ret_proj_kernel(x_ref, cos_ref, sin_ref, g_ref, wq_ref, wk_ref, wv_ref, q_ref, k_ref, v_ref):
    h = _rms(x_ref[...], g_ref[...]).astype(BF16)
    cos = cos_ref[...]
    sin = sin_ref[...]
    hw = RET_DK // 2
    for w_ref, dst, scale in ((wq_ref, q_ref, 1.0), (wk_ref, k_ref, RET_DK ** -0.5)):
        p = _dot(h, w_ref[...])
        for hd in range(RET_HEADS):
            lo = RET_DK * hd
            t1 = p[:, lo:lo + hw]
            t2 = p[:, lo + hw:lo + 2 * hw]
            dst[:, lo:lo + hw] = ((t1 * cos - t2 * sin) * scale).astype(BF16)
            dst[:, lo + hw:lo + 2 * hw] = ((t2 * cos + t1 * sin) * scale).astype(BF16)
    v_ref[...] = _dot(h, wv_ref[...]).astype(BF16)


def _ret_unit_stages(q_ref, k_ref, v_ref, sin_ref, o_ref, sout_ref, chunk, group):
    ri = lax.broadcasted_iota(jnp.int32, (chunk, chunk), 0)
    ci = lax.broadcasted_iota(jnp.int32, (chunk, chunk), 1)
    rel = (ri - ci).astype(F32)
    causal = ri >= ci
    idx = lax.broadcasted_iota(jnp.int32, (chunk, 1), 0).astype(F32)
    lgs = [math.log1p(-(2.0 ** (-5.0 - hd))) for hd in range(RET_HEADS)]
    units = [(g, hd) for g in range(group) for hd in range(RET_HEADS)]
    qsl = [slice(RET_DK * hd, RET_DK * (hd + 1)) for hd in range(RET_HEADS)]
    vsl = [slice(RET_DV * hd, RET_DV * (hd + 1)) for hd in range(RET_HEADS)]
    decay = [jnp.where(causal, jnp.exp(jnp.maximum(rel, 0.0) * lg), 0.0) for lg in lgs]
    scores = [(_dot_nt(q_ref[g, :, qsl[hd]], k_ref[g, :, qsl[hd]]) * decay[hd]).astype(BF16) for g, hd in units]
    yield
    for (g, hd), sc in zip(units, scores):
        q_decay = jnp.exp((idx + 1.0) * lgs[hd])
        inter = _dot(q_ref[g, :, qsl[hd]], sin_ref[g, hd].astype(BF16)) * q_decay
        o_ref[g, :, vsl[hd]] = _dot(sc, v_ref[g, :, vsl[hd]]) + inter
    yield
    for g, hd in units:
        k_decay = jnp.exp((chunk - 1.0 - idx) * lgs[hd])
        kd = (k_ref[g, :, qsl[hd]].astype(F32) * k_decay).astype(BF16)
        sout_ref[g, hd] = sin_ref[g, hd] * math.exp(chunk * lgs[hd]) + _dot_tn(kd, v_ref[g, :, vsl[hd]])


def _ret_chunk_kernel(q_ref, k_ref, v_ref, s0_ref, o_ref, sout_ref, *, chunk, group):
    @pl.when(pl.program_id(1) == 0)
    def _():
        sout_ref[...] = s0_ref[...]

    for _ in _ret_unit_stages(q_ref, k_ref, v_ref, sout_ref, o_ref, sout_ref, chunk, group):
        pass


def _ret_head_norm(oh, gate):
    mu = jnp.mean(oh, axis=-1, keepdims=True)
    ctr = oh - mu
    var = jnp.mean(ctr * ctr, axis=-1, keepdims=True)
    return (ctr * lax.rsqrt(var + NORM_EPS) * gate).astype(BF16)


def _ret_out_kernel(x_ref, o_ref, g0_ref, g1_ref, wg_ref, wo_ref, y_ref, on_sc):
    x = x_ref[...]
    h = _rms(x, g0_ref[...]).astype(BF16)
    gate = _silu(_dot(h, wg_ref[...]))
    for hd in range(RET_HEADS):
        sl = slice(RET_DV * hd, RET_DV * (hd + 1))
        on_sc[:, sl] = _ret_head_norm(o_ref[:, sl], gate[:, sl])
    y_ref[...] = x + _rms(_dot(on_sc[...], wo_ref[...]), g1_ref[...])


def _ret_mixer_kernel(x_ref, cos_ref, sin_ref, *rest, chunk, group, zero_state):
    s0_ref, rest = (None, rest) if zero_state else (rest[0], rest[1:])
    g0_ref, g1_ref, wq_ref, wk_ref, wv_ref, wg_ref, wo_ref, y_ref, sout_ref, q_sc, k_sc, v_sc, o_sc, on_sc = rest

    @pl.when(pl.program_id(1) == 0)
    def _():
        sout_ref[...] = jnp.zeros(sout_ref.shape, F32) if zero_state else s0_ref[...]

    n = group * chunk
    x = x_ref[...].reshape(n, D_MODEL)
    h = _rms(x, g0_ref[...]).astype(BF16)
    cos = cos_ref[...]
    sin = sin_ref[...]
    hw = RET_DK // 2
    for w_ref, dst, scale in ((wq_ref, q_sc, 1.0), (wk_ref, k_sc, RET_DK ** -0.5)):
        p = _dot(h, w_ref[...]).reshape(group, chunk, RET_HEADS * RET_DK)
        for hd in range(RET_HEADS):
            lo = RET_DK * hd
            t1 = p[:, :, lo:lo + hw]
            t2 = p[:, :, lo + hw:lo + 2 * hw]
            dst[:, :, lo:lo + hw] = ((t1 * cos - t2 * sin) * scale).astype(BF16)
            dst[:, :, lo + hw:lo + 2 * hw] = ((t2 * cos + t1 * sin) * scale).astype(BF16)
    v_sc[...] = _dot(h, wv_ref[...]).astype(BF16).reshape(group, chunk, RET_HEADS * RET_DV)
    stages = _ret_unit_stages(q_sc, k_sc, v_sc, sout_ref, o_sc, sout_ref, chunk, group)
    next(stages)
    next(stages)
    for hd in range(RET_HEADS):
        sl = slice(RET_DV * hd, RET_DV * (hd + 1))
        gate = _silu(_dot(h, wg_ref[:, sl]))
        on_sc[:, sl] = _ret_head_norm(o_sc[:, :, sl].reshape(n, RET_DV), gate)
    proj = _dot(on_sc[...], wo_ref[...])
    next(stages, None)
    y = x + _rms(proj, g1_ref[...])
    y_ref[...] = y.reshape(group, chunk, D_MODEL)


def _row_grid_call(kernel, row_ins, const_ins, outs, rows, tile, scratch=(), name=None, tab_ins=(),
                   vmem_mb=None):
    nt = rows // tile
    in_specs = [pl.BlockSpec((tile, a.shape[1]), lambda i: (i, 0)) for a in row_ins]
    for a in tab_ins:
        ntab = a.shape[0] // tile
        in_specs.append(pl.BlockSpec((tile, a.shape[1]), lambda i, ntab=ntab: (i % ntab, 0)))
    in_specs += [_const_spec(a.shape) for a in const_ins]
    return pl.pallas_call(
        kernel,
        grid=(nt,),
        in_specs=in_specs,
        out_specs=[pl.BlockSpec((tile, n), lambda i: (i, 0)) for n, _ in outs],
        out_shape=[jax.ShapeDtypeStruct((rows, n), dt) for n, dt in outs],
        scratch_shapes=list(scratch),
        compiler_params=_cparams("arbitrary", vmem_mb=vmem_mb),
        name=name,
    )(*row_ins, *tab_ins, *const_ins)


def _to_seq(a, bn, l, time_major):
    n = a.shape[1]
    return a.reshape(l, bn, n).transpose(1, 0, 2) if time_major else a.reshape(bn, l, n)


def _from_seq(a, bn, l, time_major):
    a = a.transpose(1, 0, 2) if time_major else a
    return a.reshape(bn * l, a.shape[2])


def _seq_group(bn, chunk, full_chunk, state_bytes):
    if chunk == full_chunk:
        return min(bn, LONG_CHUNK_GROUP)
    grp = max(1, min(bn, SEQ_GROUP_BYTES // state_bytes))
    while bn % grp:
        grp -= 1
    return grp


def _seq_chunking(l, chunk):
    if l % chunk == 0:
        return chunk
    assert l < chunk
    return l


def _rotary_tables(pos0, l):
    half = RET_DK // 2
    freq = 1.0 / (ROPE_BASE ** jnp.linspace(0.0, 1.0, half, dtype=F32))
    pos = jnp.arange(pos0, pos0 + l, dtype=jnp.int32).astype(F32)
    ang = pos[:, None] * freq[None, :]
    return jnp.cos(ang), jnp.sin(ang)


def _ret_proj_stage(x, bn, l, time_major, pos0, g0, wq, wk, wv):
    rows = x.shape[0]
    ptile = min(PROJ_ROW_TILE, rows)
    cos, sin = _rotary_tables(pos0, l)
    if time_major:
        cos, sin = jnp.repeat(cos, bn, axis=0), jnp.repeat(sin, bn, axis=0)
    elif l < ptile:
        cos, sin = jnp.tile(cos, (ptile // l, 1)), jnp.tile(sin, (ptile // l, 1))
    q, k, v = _row_grid_call(
        _ret_proj_kernel, [x], [g0.reshape(1, -1), wq.astype(BF16), wk.astype(BF16), wv.astype(BF16)],
        [(RET_HEADS * RET_DK, BF16), (RET_HEADS * RET_DK, BF16), (RET_HEADS * RET_DV, BF16)],
        rows, ptile, name="ret_proj", tab_ins=[cos, sin], vmem_mb=LARGE_VMEM_MB)
    return tuple(_to_seq(a, bn, l, time_major) for a in (q, k, v))


def _ret_chunk_stage(q3, k3, v3, s0):
    bn, l, _ = q3.shape
    chunk = _seq_chunking(l, RET_CHUNK)
    grp = _seq_group(bn, chunk, RET_CHUNK, RET_HEADS * RET_DK * RET_DV * 4)
    return pl.pallas_call(
        functools.partial(_ret_chunk_kernel, chunk=chunk, group=grp),
        grid=(bn // grp, l // chunk),
        in_specs=[
            pl.BlockSpec((grp, chunk, RET_HEADS * RET_DK), lambda b, i: (b, i, 0)),
            pl.BlockSpec((grp, chunk, RET_HEADS * RET_DK), lambda b, i: (b, i, 0)),
            pl.BlockSpec((grp, chunk, RET_HEADS * RET_DV), lambda b, i: (b, i, 0)),
            pl.BlockSpec((grp, RET_HEADS, RET_DK, RET_DV), lambda b, i: (b, 0, 0, 0)),
        ],
        out_specs=[
            pl.BlockSpec((grp, chunk, RET_HEADS * RET_DV), lambda b, i: (b, i, 0)),
            pl.BlockSpec((grp, RET_HEADS, RET_DK, RET_DV), lambda b, i: (b, 0, 0, 0)),
        ],
        out_shape=[jax.ShapeDtypeStruct((bn, l, RET_HEADS * RET_DV), F32),
                   jax.ShapeDtypeStruct((bn, RET_HEADS, RET_DK, RET_DV), F32)],
        compiler_params=_cparams("arbitrary", "arbitrary"),
        name="ret_chunk",
    )(q3, k3, v3, s0)


def _ret_mixer_stage(x, bn, l, pos0, s0, g0, g1, wq, wk, wv, wg, wo):
    grp = _seq_group(bn, RET_CHUNK, RET_CHUNK, 0)
    cos, sin = _rotary_tables(pos0, l)
    consts = [g0.reshape(1, -1), g1.reshape(1, -1)] + [a.astype(BF16) for a in (wq, wk, wv, wg, wo)]
    seq_spec = pl.BlockSpec((grp, RET_CHUNK, D_MODEL), lambda b, i: (b, i, 0))
    tab_spec = pl.BlockSpec((RET_CHUNK, RET_DK // 2), lambda b, i: (i, 0))
    state_spec = pl.BlockSpec((grp, RET_HEADS, RET_DK, RET_DV), lambda b, i: (b, 0, 0, 0))
    states = [] if s0 is None else [s0]
    y, s_new = pl.pallas_call(
        functools.partial(_ret_mixer_kernel, chunk=RET_CHUNK, group=grp, zero_state=s0 is None),
        grid=(bn // grp, l // RET_CHUNK),
        in_specs=[seq_spec, tab_spec, tab_spec] + [state_spec] * len(states) + [_const_spec(a.shape) for a in consts],
        out_specs=[seq_spec, state_spec],
        out_shape=[jax.ShapeDtypeStruct((bn, l, D_MODEL), F32),
                   jax.ShapeDtypeStruct((bn, RET_HEADS, RET_DK, RET_DV), F32)],
        scratch_shapes=[pltpu.VMEM((grp, RET_CHUNK, RET_HEADS * RET_DK), BF16),
                        pltpu.VMEM((grp, RET_CHUNK, RET_HEADS * RET_DK), BF16),
                        pltpu.VMEM((grp, RET_CHUNK, RET_HEADS * RET_DV), BF16),
                        pltpu.VMEM((grp, RET_CHUNK, RET_HEADS * RET_DV), F32),
                        pltpu.VMEM((grp * RET_CHUNK, RET_HEADS * RET_DV), BF16)],
        compiler_params=_cparams("arbitrary", "arbitrary", vmem_mb=LARGE_VMEM_MB),
        name="ret_mixer",
    )(x.reshape(bn, l, D_MODEL), cos, sin, *states, *consts)
    return y.reshape(bn * l, D_MODEL), s_new


def _ret_out_stage(x, o3, time_major, g0, g1, wg, wo):
    rows = x.shape[0]
    tile = min(ROW_TILE, rows)
    bn, l, _ = o3.shape
    o = _from_seq(o3, bn, l, time_major)
    (y,) = _row_grid_call(
        _ret_out_kernel, [x, o], [g0.reshape(1, -1), g1.reshape(1, -1), wg.astype(BF16), wo.astype(BF16)],
        [(D_MODEL, F32)], rows, tile, scratch=[pltpu.VMEM((tile, RET_HEADS * RET_DV), BF16)], name="ret_out")
    return y


def _gla_project(h, wq_ref, wk_ref, wv_ref, wa1_ref, wa2_ref, ba_ref):
    q = (_dot(h, wq_ref[...]) * (GLA_DK ** -0.5)).astype(BF16)
    k = _dot(h, wk_ref[...]).astype(BF16)
    v = _dot(h, wv_ref[...]).astype(BF16)
    low = _dot(h, wa1_ref[...]).astype(BF16)
    logit = _dot(low, wa2_ref[...]) + ba_ref[...]
    return q, k, v, _log_sigmoid(logit) / GLA_GATE_NORM


def _gla_proj_kernel(x_ref, g_ref, wq_ref, wk_ref, wv_ref, wa1_ref, wa2_ref, ba_ref,
                     q_ref, k_ref, v_ref, la_ref):
    h = _rms(x_ref[...], g_ref[...]).astype(BF16)
    q_ref[...], k_ref[...], v_ref[...], la_ref[...] = _gla_project(h, wq_ref, wk_ref, wv_ref, wa1_ref, wa2_ref,
                                                                 ba_ref)


def _gla_unit_stages(q_ref, k_ref, v_ref, la_ref, o_ref, st_sc, chunk, group):
    ri = lax.broadcasted_iota(jnp.int32, (chunk, chunk), 0)
    ci = lax.broadcasted_iota(jnp.int32, (chunk, chunk), 1)
    causal = ri >= ci
    tri = causal.astype(BF16)
    nk = GLA_HEADS * GLA_DK
    units = [(g, hd) for g in range(group) for hd in range(GLA_HEADS)]
    bcs = []
    for g in range(group):
        la = la_ref[g]
        p1 = la.astype(BF16)
        r1 = la - p1.astype(F32)
        p2 = r1.astype(BF16)
        p3 = (r1 - p2.astype(F32)).astype(BF16)
        acc = _dot(tri, jnp.concatenate([p1, p2, p3], axis=1))
        bcs.append(acc[:, :nk] + acc[:, nk:2 * nk] + acc[:, 2 * nk:])
    prep = []
    for g, hd in units:
        ksl = slice(GLA_DK * hd, GLA_DK * (hd + 1))
        bc = bcs[g][:, ksl]
        blast = bc[chunk - 1:chunk, :]
        ref = 0.5 * (bc[0:1, :] + blast)
        qh = q_ref[g, :, ksl].astype(F32)
        kh = k_ref[g, :, ksl].astype(F32)
        qt = (qh * jnp.exp(bc - ref)).astype(BF16)
        kt = (kh * jnp.exp(ref - bc)).astype(BF16)
        qg = (qh * jnp.exp(bc)).astype(BF16)
        kg = (kh * jnp.exp(blast - bc)).astype(BF16)
        row_decay = jnp.broadcast_to(jnp.exp(blast), (SUBLANES, GLA_DK)).T[:, :1]
        prep.append((qt, kt, qg, kg, row_decay))
    yield
    scores = [jnp.where(causal, _dot_nt(qt, kt), 0.0).astype(BF16) for qt, kt, _, _, _ in prep]
    for (g, hd), sc, (_, _, qg, kg, row_decay) in zip(units, scores, prep):
        vsl = slice(GLA_DV * hd, GLA_DV * (hd + 1))
        vh = v_ref[g, :, vsl]
        st = st_sc[g, hd]
        o_ref[g, :, vsl] = _dot(sc, vh) + _dot(qg, st.astype(BF16))
        st_sc[g, hd] = st * row_decay + _dot_tn(kg, vh)


def _gla_copy_states(src, dst, group):
    for g in range(group):
        for hd in range(GLA_HEADS):
            dst[g, hd] = src[g, hd]


def _gla_chunk_kernel(q_ref, k_ref, v_ref, la_ref, s0_ref, o_ref, sout_ref, st_sc, *, chunk, group):
    @pl.when(pl.program_id(1) == 0)
    def _():
        _gla_copy_states(s0_ref, st_sc, group)

    for _ in _gla_unit_stages(q_ref, k_ref, v_ref, la_ref, o_ref, st_sc, chunk, group):
        pass

    @pl.when(pl.program_id(1) == pl.num_programs(1) - 1)
    def _():
        _gla_copy_states(st_sc, sout_ref, group)


def _gla_head_norm(oh, norm_g, gate):
    on = oh * lax.rsqrt(jnp.mean(oh * oh, axis=-1, keepdims=True) + NORM_EPS) * norm_g
    return (on * gate).astype(BF16)


def _gla_mixer_kernel(x_ref, *rest, chunk, group, zero_state):
    s0_ref, rest = (None, rest) if zero_state else (rest[0], rest[1:])
    (g0_ref, wq_ref, wk_ref, wv_ref, wa1_ref, wa2_ref, ba_ref, g1_ref, ng_ref, wg_ref, wo_ref, y_ref, sout_ref,
     st_sc, q_sc, k_sc, v_sc, la_sc, o_sc, gate_sc, on_sc) = rest

    @pl.when(pl.program_id(1) == 0)
    def _():
        if zero_state:
            st_sc[...] = jnp.zeros(st_sc.shape, F32)
        else:
            _gla_copy_states(s0_ref, st_sc, group)

    n = group * chunk
    x = x_ref[...].reshape(n, D_MODEL)
    h = _rms(x, g0_ref[...]).astype(BF16)
    for dst, val in zip((q_sc, k_sc, v_sc, la_sc), _gla_project(h, wq_ref, wk_ref, wv_ref, wa1_ref, wa2_ref, ba_ref)):
        dst[...] = val.reshape(dst.shape)
    stages = _gla_unit_stages(q_sc, k_sc, v_sc, la_sc, o_sc, st_sc, chunk, group)
    next(stages)
    gate_sc[...] = _silu(_dot(h, wg_ref[...]))
    next(stages, None)
    for hd in range(GLA_HEADS):
        sl = slice(GLA_DV * hd, GLA_DV * (hd + 1))
        on_sc[:, sl] = _gla_head_norm(o_sc[:, :, sl].reshape(n, GLA_DV), ng_ref[...], gate_sc[:, sl])
    y = x + _rms(_dot(on_sc[...], wo_ref[...]), g1_ref[...])
    y_ref[...] = y.reshape(group, chunk, D_MODEL)

    @pl.when(pl.program_id(1) == pl.num_programs(1) - 1)
    def _():
        _gla_copy_states(st_sc, sout_ref, group)


def _gla_out_kernel(x_ref, o_ref, g0_ref, g1_ref, ng_ref, wg_ref, wo_ref, y_ref, on_sc):
    x = x_ref[...]
    h = _rms(x, g0_ref[...]).astype(BF16)
    gate = _silu(_dot(h, wg_ref[...]))
    for hd in range(GLA_HEADS):
        sl = slice(GLA_DV * hd, GLA_DV * (hd + 1))
        on_sc[:, sl] = _gla_head_norm(o_ref[:, sl], ng_ref[...], gate[:, sl])
    y_ref[...] = x + _rms(_dot(on_sc[...], wo_ref[...]), g1_ref[...])


def _gla_layer(x, bn, l, time_major, s0, g0, g1, wq, wk, wv, wg, wa1, wa2, ba, norm_g, wo):
    rows = x.shape[0]
    tile = min(PROJ_ROW_TILE, rows)
    lanes = 128
    wa1p = jnp.pad(wa1, ((0, 0), (0, lanes - GLA_GATE_RANK))).astype(BF16)
    wa2p = jnp.pad(wa2, ((0, lanes - GLA_GATE_RANK), (0, 0))).astype(BF16)
    proj_consts = [g0.reshape(1, -1), wq.astype(BF16), wk.astype(BF16), wv.astype(BF16), wa1p, wa2p,
                   ba.reshape(1, -1)]
    out_consts = [g0.reshape(1, -1), g1.reshape(1, -1), norm_g.reshape(1, -1), wg.astype(BF16), wo.astype(BF16)]
    chunk = _seq_chunking(l, GLA_CHUNK)
    nc = l // chunk
    grp = GLA_GROUP if bn % GLA_GROUP == 0 else 1
    nk, nv = GLA_HEADS * GLA_DK, GLA_HEADS * GLA_DV
    if not time_major and chunk == GLA_CHUNK:
        consts = proj_consts + out_consts[1:]
        seq_spec = pl.BlockSpec((grp, chunk, D_MODEL), lambda b, i: (b, i, 0))
        state_spec = pl.BlockSpec((grp, GLA_HEADS, GLA_DK, GLA_DV), lambda b, i: (b, 0, 0, 0))
        states = [] if s0 is None else [s0]
        y, s_new = pl.pallas_call(
            functools.partial(_gla_mixer_kernel, chunk=chunk, group=grp, zero_state=s0 is None),
            grid=(bn // grp, nc),
            in_specs=[seq_spec] + [state_spec] * len(states) + [_const_spec(a.shape) for a in consts],
            out_specs=[seq_spec, state_spec],
            out_shape=[jax.ShapeDtypeStruct((bn, l, D_MODEL), F32),
                       jax.ShapeDtypeStruct((bn, GLA_HEADS, GLA_DK, GLA_DV), F32)],
            scratch_shapes=[pltpu.VMEM((grp, GLA_HEADS, GLA_DK, GLA_DV), F32),
                            pltpu.VMEM((grp, chunk, nk), BF16), pltpu.VMEM((grp, chunk, nk), BF16),
                            pltpu.VMEM((grp, chunk, nv), BF16), pltpu.VMEM((grp, chunk, nk), F32),
                            pltpu.VMEM((grp, chunk, nv), F32),
                            pltpu.VMEM((grp * chunk, nv), F32),
                            pltpu.VMEM((grp * chunk, nv), BF16)],
            compiler_params=_cparams("arbitrary", "arbitrary"),
            name="gla_mixer",
        )(x.reshape(bn, l, D_MODEL), *states, *consts)
        return y.reshape(rows, D_MODEL), s_new
    if s0 is None:
        s0 = jnp.zeros((bn, GLA_HEADS, GLA_DK, GLA_DV), F32)
    q, k, v, la = _row_grid_call(
        _gla_proj_kernel, [x], proj_consts, [(nk, BF16), (nk, BF16), (nv, BF16), (nk, F32)],
        rows, tile, name="gla_proj", vmem_mb=LARGE_VMEM_MB)
    q3, k3, v3, la3 = (_to_seq(a, bn, l, time_major) for a in (q, k, v, la))
    o3, s_new = pl.pallas_call(
        functools.partial(_gla_chunk_kernel, chunk=chunk, group=grp),
        grid=(bn // grp, nc),
        in_specs=[
            pl.BlockSpec((grp, chunk, GLA_HEADS * GLA_DK), lambda b, i: (b, i, 0)),
            pl.BlockSpec((grp, chunk, GLA_HEADS * GLA_DK), lambda b, i: (b, i, 0)),
            pl.BlockSpec((grp, chunk, GLA_HEADS * GLA_DV), lambda b, i: (b, i, 0)),
            pl.BlockSpec((grp, chunk, GLA_HEADS * GLA_DK), lambda b, i: (b, i, 0)),
            pl.BlockSpec((grp, GLA_HEADS, GLA_DK, GLA_DV), lambda b, i: (b, 0, 0, 0)),
        ],
        out_specs=[
            pl.BlockSpec((grp, chunk, GLA_HEADS * GLA_DV), lambda b, i: (b, i, 0)),
            pl.BlockSpec((grp, GLA_HEADS, GLA_DK, GLA_DV), lambda b, i: (b, 0, 0, 0)),
        ],
        out_shape=[jax.ShapeDtypeStruct((bn, l, GLA_HEADS * GLA_DV), F32),
                   jax.ShapeDtypeStruct((bn, GLA_HEADS, GLA_DK, GLA_DV), F32)],
        scratch_shapes=[pltpu.VMEM((grp, GLA_HEADS, GLA_DK, GLA_DV), F32)],
        compiler_params=_cparams("arbitrary", "arbitrary"),
        name="gla_chunk",
    )(q3, k3, v3, la3, s0)
    o = _from_seq(o3, bn, l, time_major)
    (y,) = _row_grid_call(
        _gla_out_kernel, [x, o], out_consts,
        [(D_MODEL, F32)], rows, tile, scratch=[pltpu.VMEM((tile, GLA_HEADS * GLA_DV), BF16)], name="gla_out",
        vmem_mb=LARGE_VMEM_MB)
    return y, s_new


def _transpose_rows(x, a, b):
    return x.reshape(a, b, x.shape[1]).transpose(1, 0, 2).reshape(a * b, x.shape[1])


def _trunk(x, pos0, s5_re, s5_im, ret_s, gla_s, conv_buf, p, batch_major_mixers, bf16_w, cast_weights,
           defer_ret=False, host=None):
    bf16_w = dict(bf16_w)

    def w(name, idx):
        return bf16_w.get((name, idx), p[name][idx])
    bn, l, _ = x.shape
    depth = p['norm_g'].shape[0]
    assert not batch_major_mixers or bn == SUBLANES
    rows = x.reshape(bn * l, D_MODEL)
    time_major = False
    if not batch_major_mixers:
        rows, time_major = _transpose_rows(rows, bn, l), True
    new_re, new_im, new_ret, new_gla, new_conv = [], [], [], [], []
    hosted_o, hosted_s, hosted_n = [], None, 0
    for layer in range(depth):
        g = p['norm_g'][layer]
        kind = layer % N_MIXERS
        j = layer // N_MIXERS
        if kind == 0:
            todo_host = None
            if host is not None and hosted_n < host[0].shape[0]:
                todo_host = tuple(host) + (hosted_n, hosted_s)
            rows, hr, hi, hres = _s5_layer(rows, bn, None if s5_re is None else s5_re[j],
                                           None if s5_re is None else s5_im[j], g[0], g[1],
                                           p['s5_lambda_re'][j], p['s5_lambda_im'][j], p['s5_b_re'][j],
                                           p['s5_b_im'][j], p['s5_c_re'][j], p['s5_c_im'][j], p['s5_log_dt'][j],
                                           p['s5_d'][j], p['s5_w_glu'], j, x_bm=not time_major, host=todo_host)
            if hres:
                hosted_o.append(hres[0])
                hosted_s = hres[1]
                hosted_n += hres[0].shape[0]
            time_major = True
            new_re.append(hr)
            new_im.append(hi)
        elif kind == 1:
            s0 = None if ret_s is None else ret_s[j]
            if not defer_ret and not time_major and l % RET_CHUNK == 0:
                rows, s = _ret_mixer_stage(rows, bn, l, pos0, s0, g[0], g[1],
                                           *(w(n, j) for n in MIXER_WEIGHTS[kind]))
            else:
                if s0 is None:
                    s0 = jnp.zeros((bn, RET_HEADS, RET_DK, RET_DV), F32)
                q3, k3, v3 = _ret_proj_stage(rows, bn, l, time_major, pos0, g[0], w('ret_wq', j), w('ret_wk', j),
                                             w('ret_wv', j))
                if defer_ret:
                    o3, s, more = yield q3, k3, v3, s0
                    bf16_w.update(more)
                else:
                    o3, s = _ret_chunk_stage(q3, k3, v3, s0)
                rows = _ret_out_stage(rows, o3, time_major, g[0], g[1], w('ret_wg', j), w('ret_wo', j))
            new_ret.append(s)
        else:
            rows, s = _gla_layer(rows, bn, l, time_major, None if gla_s is None else gla_s[j], g[0], g[1],
                                 w('gla_wq', j), w('gla_wk', j),
                                 w('gla_wv', j), w('gla_wg', j), p['gla_wa1'][j], p['gla_wa2'][j], p['gla_ba'][j],
                                 p['gla_norm_g'][j], w('gla_wo', j))
            new_gla.append(s)
        next_is_s5 = layer + 1 < depth and (layer + 1) % N_MIXERS == 0
        out_bm = batch_major_mixers and not next_is_s5
        cin = conv_buf[layer].transpose(1, 0, 2).reshape((CONV_W - 1) * bn, D_FF)
        todo = []
        if cast_weights and layer + 1 < depth:
            mixer = MIXER_WEIGHTS[(layer + 1) % N_MIXERS]
            todo = [(n, layer + 1) for n in FFN_WEIGHTS] + [(n, (layer + 1) // N_MIXERS) for n in mixer]
            todo = [key for key in todo if key not in bf16_w]
        rows, cout, cast = _ffn_layer(rows, bn, cin, g[2], g[3], w('ffn_w_gate', layer), w('ffn_w_up', layer),
                                      p['ffn_conv_w'][layer], p['ffn_conv_b'][layer], w('ffn_w_down', layer),
                                      x_bm=not time_major, out_bm=out_bm,
                                      next_weights=[(p[n], idx) for n, idx in todo])
        bf16_w.update(zip(todo, cast))
        time_major = not out_bm
        new_conv.append(cout.reshape(CONV_W - 1, bn, D_FF).transpose(1, 0, 2))
    if time_major:
        rows = _transpose_rows(rows, l, bn)
    y = rows.reshape(bn, l, D_MODEL)
    hosted = None
    if host is not None:
        assert hosted_n == host[0].shape[0]
        hosted = (jnp.concatenate(hosted_o), hosted_s)
    outs = (y, jnp.stack(new_re), jnp.stack(new_im), jnp.stack(new_ret), jnp.stack(new_gla),
            jnp.stack(new_conv))
    return outs, bf16_w, hosted


def _finish(gen, reply=None):
    try:
        gen.send(reply)
    except StopIteration as done:
        return done.value
    raise AssertionError("unexpected request")


def kernel(x_prompt, x_sample, state_s5_re, state_s5_im, state_ret, state_gla, cache_ffn_conv,
           norm_g, s5_lambda_re, s5_lambda_im, s5_b_re, s5_b_im, s5_c_re, s5_c_im, s5_log_dt, s5_d, s5_w_glu,
           ret_wq, ret_wk, ret_wv, ret_wg, ret_wo,
           gla_wq, gla_wk, gla_wv, gla_wg, gla_wa1, gla_wa2, gla_ba, gla_norm_g, gla_wo,
           ffn_w_gate, ffn_w_up, ffn_conv_w, ffn_conv_b, ffn_w_down):
    p = dict(norm_g=norm_g, s5_lambda_re=s5_lambda_re, s5_lambda_im=s5_lambda_im, s5_b_re=s5_b_re,
             s5_b_im=s5_b_im, s5_c_re=s5_c_re, s5_c_im=s5_c_im, s5_log_dt=s5_log_dt, s5_d=s5_d,
             s5_w_glu=s5_w_glu.astype(BF16), ret_wq=ret_wq, ret_wk=ret_wk, ret_wv=ret_wv, ret_wg=ret_wg, ret_wo=ret_wo,
             gla_wq=gla_wq, gla_wk=gla_wk, gla_wv=gla_wv, gla_wg=gla_wg, gla_wa1=gla_wa1, gla_wa2=gla_wa2,
             gla_ba=gla_ba, gla_norm_g=gla_norm_g, gla_wo=gla_wo, ffn_w_gate=ffn_w_gate, ffn_w_up=ffn_w_up,
             ffn_conv_w=ffn_conv_w, ffn_conv_b=ffn_conv_b, ffn_w_down=ffn_w_down)
    bp = x_prompt.shape[0]
    z_conv = jnp.zeros((cache_ffn_conv.shape[0], bp) + cache_ffn_conv.shape[2:], x_prompt.dtype)
    first = {(n, 0): p[n][0].astype(BF16) for n in FFN_WEIGHTS}
    first.update({(n, j): p[n][j].astype(BF16) for n in ('ret_wq', 'ret_wk', 'ret_wv') for j in range(ret_wq.shape[0])})
    sample = _trunk(x_sample, PAST_LEN, state_s5_re, state_s5_im, state_ret, state_gla, cache_ffn_conv, p,
                    False, first, False, defer_ret=True)
    request = next(sample)
    outs_p, bf16_w, hosted = _finish(_trunk(x_prompt, 0, None, None, None, None, z_conv, p, True, first, True,
                                            host=request))
    outs_s, _, _ = _finish(sample, hosted + (bf16_w,))
    return (outs_p[0], outs_s[0]) + tuple(outs_p[1:]) + tuple(outs_s[1:])
```
